```python
import jax, jax.numpy as jnp
from jax import lax
import numpy as np

D_MODEL = 2048
BATCH = 2
SEQ = 4096
DEPTH = 1

MLSTM_HEADS = 4
MLSTM_DH = 256
MLSTM_W = MLSTM_HEADS * MLSTM_DH
MLSTM_CONV = 4
MLSTM_CHUNK = 64
MLA_HEADS = 8
MLA_NOPE = 128
MLA_ROPE = 64
MLA_V = 128
MLA_W = MLA_HEADS * MLA_V
Q_LORA = 512
KV_LORA = 256
ROPE_THETA = 10000.0
Q_BLOCK = 128
MIX_WIDTH = MLSTM_W + MLA_W
IN_SIZES = (MLSTM_W, MLSTM_W, MLSTM_W, MLSTM_HEADS, MLSTM_HEADS, Q_LORA, KV_LORA, MLA_ROPE)
IN_WIDTH = sum(IN_SIZES)
IN_OFFSETS = tuple(int(o) for o in np.cumsum(IN_SIZES)[:-1])
FFN_D = 5632
FFN_CONV = 3
EPS = 1e-6

kernel_name = 'hybrid_mlstm_mla_convffn_adaln'


def rms_norm(x, g):
    xf = x.astype(jnp.float32)
    y = xf * lax.rsqrt(jnp.mean(xf * xf, axis=-1, keepdims=True) + EPS)
    return (y * g.astype(jnp.float32)).astype(x.dtype)


def causal_dwconv(x, w, b):
    width = w.shape[0]
    seq = x.shape[1]
    xp = jnp.pad(x, ((0, 0), (width - 1, 0), (0, 0)))
    y = b
    for j in range(width):
        y = y + w[j] * xp[:, j:j + seq]
    return y


def rope(x, positions):
    half = x.shape[-1] // 2
    freqs = ROPE_THETA ** (-jnp.arange(half, dtype=jnp.float32) / half)
    ang = positions.astype(jnp.float32)[..., None] * freqs
    ang = ang.reshape(ang.shape[:2] + (1,) * (x.ndim - 3) + (half,))
    cos, sin = jnp.cos(ang), jnp.sin(ang)
    xf = x.astype(jnp.float32)
    x1, x2 = xf[..., :half], xf[..., half:]
    out = jnp.concatenate([x1 * cos - x2 * sin, x2 * cos + x1 * sin], axis=-1)
    return out.astype(x.dtype)


def mlstm_chunkwise(q, k, v, li, lf):
    B, H, S, dh = q.shape
    nc = S // MLSTM_CHUNK
    f32 = jnp.float32
    q = q.astype(f32).reshape(B, H, nc, MLSTM_CHUNK, dh)
    k = k.astype(f32).reshape(B, H, nc, MLSTM_CHUNK, dh)
    v = v.astype(f32).reshape(B, H, nc, MLSTM_CHUNK, dh)
    li = li.astype(f32).reshape(B, H, nc, MLSTM_CHUNK)
    lf = lf.astype(f32).reshape(B, H, nc, MLSTM_CHUNK)

    b = jnp.cumsum(lf, axis=-1)
    g = b[..., -1]
    a = g[..., None] - b + li
    m_loc = jnp.max(a, axis=-1)
    w = jnp.exp(a - m_loc[..., None])
    C_loc = jnp.einsum('bhcs,bhcsv,bhcsk->bhcvk', w, v, k)
    n_loc = jnp.einsum('bhcs,bhcsk->bhck', w, k)

    def step(carry, inp):
        C, n, m = carry
        g_c, m_l, C_l, n_l = inp
        m_new = jnp.maximum(g_c + m, m_l)
        s_prev = jnp.exp(g_c + m - m_new)
        s_loc = jnp.exp(m_l - m_new)
        C_new = s_prev[..., None, None] * C + s_loc[..., None, None] * C_l
        n_new = s_prev[..., None] * n + s_loc[..., None] * n_l
        return (C_new, n_new, m_new), (C, n, m)

    init = (jnp.zeros((B, H, dh, dh), f32), jnp.zeros((B, H, dh), f32), jnp.zeros((B, H), f32))
    xs = (jnp.moveaxis(g, 2, 0), jnp.moveaxis(m_loc, 2, 0),
          jnp.moveaxis(C_loc, 2, 0), jnp.moveaxis(n_loc, 2, 0))
    _, (C_prev, n_prev, m_prev) = lax.scan(step, init, xs)
    C_prev = jnp.moveaxis(C_prev, 0, 2)
    n_prev = jnp.moveaxis(n_prev, 0, 2)
    m_prev = jnp.moveaxis(m_prev, 0, 2)

    D = b[..., :, None] - b[..., None, :] + li[..., None, :]
    causal = jnp.tril(jnp.ones((MLSTM_CHUNK, MLSTM_CHUNK), dtype=bool))
    D = jnp.where(causal, D, -jnp.inf)
    m_t = jnp.maximum(b + m_prev[..., None], jnp.max(D, axis=-1))
    S_ts = jnp.einsum('bhctk,bhcsk->bhcts', q, k) * jnp.exp(D - m_t[..., None])
    inter = jnp.exp(b + m_prev[..., None] - m_t)
    num = (jnp.einsum('bhcts,bhcsv->bhctv', S_ts, v)
           + inter[..., None] * jnp.einsum('bhcvk,bhctk->bhctv', C_prev, q))
    den = jnp.sum(S_ts, axis=-1) + inter * jnp.einsum('bhck,bhctk->bhct', n_prev, q)
    h = num / jnp.maximum(jnp.abs(den), jnp.exp(-m_t))[..., None]
    return h.reshape(B, H, S, dh)


def mla_attention(q_nope, q_rope, k_nope, k_rope, v):
    B, S, H, _ = q_nope.shape
    nb = S // Q_BLOCK
    scale = (MLA_NOPE + MLA_ROPE) ** -0.5
    qn_b = q_nope.reshape(B, nb, Q_BLOCK, H, MLA_NOPE).transpose(1, 0, 2, 3, 4)
    qr_b = q_rope.reshape(B, nb, Q_BLOCK, H, MLA_ROPE).transpose(1, 0, 2, 3, 4)
    key_pos = jnp.arange(S)

    def block(args):
        qn, qr, i = args
        s = (jnp.einsum('bqhd,bkhd->bhqk', qn, k_nope)
             + jnp.einsum('bqhr,bkr->bhqk', qr, k_rope)).astype(jnp.float32) * scale
        q_pos = i * Q_BLOCK + jnp.arange(Q_BLOCK)
        mask = key_pos[None, :] <= q_pos[:, None]
        s = jnp.where(mask[None, None], s, -jnp.inf)
        p = jax.nn.softmax(s, axis=-1).astype(v.dtype)
        return jnp.einsum('bhqk,bkhv->bqhv', p, v)

    out = lax.map(block, (qn_b, qr_b, jnp.arange(nb)))
    return out.transpose(1, 0, 2, 3, 4).reshape(B, S, H, MLA_V)


def hybrid_mixer(h, positions, w_in, mlstm_conv_w, mlstm_conv_b, mlstm_wq, mlstm_wk,
                 mlstm_igate_b, mlstm_fgate_b, mla_q_norm_g, mla_w_uq, mla_kv_norm_g, mla_w_ukv,
                 mlstm_out_g, mla_out_g, w_out):
    B, S, _ = h.shape
    proj = h @ w_in
    u, v_m, o_m, ig, fg, cq, ckv, kr = jnp.split(proj, IN_OFFSETS, axis=-1)

    u = jax.nn.silu(causal_dwconv(u, mlstm_conv_w, mlstm_conv_b)).reshape(B, S, MLSTM_HEADS, MLSTM_DH)
    q_m = jnp.einsum('bshd,hde->bhse', u, mlstm_wq)
    k_m = jnp.einsum('bshd,hde->bhse', u, mlstm_wk) * (MLSTM_DH ** -0.5)
    v_m = v_m.reshape(B, S, MLSTM_HEADS, MLSTM_DH).transpose(0, 2, 1, 3)
    li = (ig + mlstm_igate_b).transpose(0, 2, 1)
    lf = jax.nn.log_sigmoid((fg + mlstm_fgate_b).astype(jnp.float32)).transpose(0, 2, 1)
    h_m = mlstm_chunkwise(q_m, k_m, v_m, li, lf).astype(h.dtype).transpose(0, 2, 1, 3)
    h_m = jax.nn.sigmoid(o_m).reshape(B, S, MLSTM_HEADS, MLSTM_DH) * h_m
    h_m = rms_norm(h_m, mlstm_out_g).reshape(B, S, MLSTM_W)

    q = (rms_norm(cq, mla_q_norm_g) @ mla_w_uq).reshape(B, S, MLA_HEADS, MLA_NOPE + MLA_ROPE)
    q_nope = q[..., :MLA_NOPE]
    q_rope = rope(q[..., MLA_NOPE:], positions)
    kv = (rms_norm(ckv, mla_kv_norm_g) @ mla_w_ukv).reshape(B, S, MLA_HEADS, MLA_NOPE + MLA_V)
    k_nope, v_a = kv[..., :MLA_NOPE], kv[..., MLA_NOPE:]
    k_rope = rope(kr, positions)
    h_a = mla_attention(q_nope, q_rope, k_nope, k_rope, v_a)
    h_a = rms_norm(h_a, mla_out_g).reshape(B, S, MLA_W)

    return jnp.concatenate([h_m, h_a], axis=-1) @ w_out


def conv_ffn(h, w_up, conv_w, conv_b, w_down):
    up = causal_dwconv(h @ w_up, conv_w, conv_b)
    val, gate = jnp.split(up, 2, axis=-1)
    return (jax.nn.silu(gate) * val) @ w_down


def setup_inputs(seed: int = 0) -> dict:
    key = jax.random.key(seed)
    ks = jax.random.split(key, 25)
    L = DEPTH

    def nrm(k, shape, s):
        return jax.random.normal(k, shape, jnp.float32) * s

    positions = jnp.broadcast_to(jnp.arange(SEQ, dtype=jnp.int32)[None, :], (BATCH, SEQ))
    return {
        'x': nrm(ks[0], (BATCH, SEQ, D_MODEL), 1.0),
        'c': nrm(ks[1], (BATCH, D_MODEL), 1.0),
        'positions': positions,
        'ada_w': nrm(ks[2], (L, D_MODEL, 6 * D_MODEL), 0.5 * D_MODEL ** -0.5),
        'ada_b': nrm(ks[3], (L, 6 * D_MODEL), 0.02),
        'attn_norm_g': 1.0 + nrm(ks[4], (L, D_MODEL), 0.02),
        'w_in': nrm(ks[5], (L, D_MODEL, IN_WIDTH), D_MODEL ** -0.5),
        'mlstm_conv_w': nrm(ks[6], (L, MLSTM_CONV, MLSTM_W), MLSTM_CONV ** -0.5),
        'mlstm_conv_b': nrm(ks[7], (L, MLSTM_W), 0.02),
        'mlstm_wq': nrm(ks[8], (L, MLSTM_HEADS, MLSTM_DH, MLSTM_DH), MLSTM_DH ** -0.5),
        'mlstm_wk': nrm(ks[9], (L, MLSTM_HEADS, MLSTM_DH, MLSTM_DH), MLSTM_DH ** -0.5),
        'mlstm_igate_b': nrm(ks[10], (L, MLSTM_HEADS), 0.1),
        'mlstm_fgate_b': jnp.linspace(3.0, 6.0, MLSTM_HEADS, dtype=jnp.float32)[None, :]
                         + nrm(ks[11], (L, MLSTM_HEADS), 0.1),
        'mla_q_norm_g': 1.0 + nrm(ks[12], (L, Q_LORA), 0.02),
        'mla_w_uq': nrm(ks[13], (L, Q_LORA, MLA_HEADS * (MLA_NOPE + MLA_ROPE)), Q_LORA ** -0.5),
        'mla_kv_norm_g': 1.0 + nrm(ks[14], (L, KV_LORA), 0.02),
        'mla_w_ukv': nrm(ks[15], (L, KV_LORA, MLA_HEADS * (MLA_NOPE + MLA_V)), KV_LORA ** -0.5),
        'mlstm_out_g': 1.0 + nrm(ks[16], (L, MLSTM_HEADS, MLSTM_DH), 0.02),
        'mla_out_g': 1.0 + nrm(ks[17], (L, MLA_HEADS, MLA_V), 0.02),
        'w_out': nrm(ks[18], (L, MIX_WIDTH, D_MODEL), MIX_WIDTH ** -0.5),
        'ffn_norm_g': 1.0 + nrm(ks[19], (L, D_MODEL), 0.02),
        'ffn_w_up': nrm(ks[20], (L, D_MODEL, 2 * FFN_D), D_MODEL ** -0.5),
        'ffn_conv_w': nrm(ks[21], (L, FFN_CONV, 2 * FFN_D), FFN_CONV ** -0.5),
        'ffn_conv_b': nrm(ks[22], (L, 2 * FFN_D), 0.02),
        'ffn_w_down': nrm(ks[23], (L, FFN_D, D_MODEL), FFN_D ** -0.5),
        'final_norm_g': 1.0 + nrm(ks[24], (D_MODEL,), 0.02),
    }


def reference(x, c, positions, ada_w, ada_b, attn_norm_g, w_in, mlstm_conv_w, mlstm_conv_b,
              mlstm_wq, mlstm_wk, mlstm_igate_b, mlstm_fgate_b, mla_q_norm_g, mla_w_uq,
              mla_kv_norm_g, mla_w_ukv, mlstm_out_g, mla_out_g, w_out, ffn_norm_g, ffn_w_up,
              ffn_conv_w, ffn_conv_b, ffn_w_down, final_norm_g):
    c_act = jax.nn.silu(c)
    for l in range(DEPTH):
        mod = c_act @ ada_w[l] + ada_b[l]
        sh1, sc1, g1, sh2, sc2, g2 = jnp.split(mod, 6, axis=-1)
        h = rms_norm(x, attn_norm_g[l]) * (1.0 + sc1[:, None]) + sh1[:, None]
        mix = hybrid_mixer(h, positions, w_in[l], mlstm_conv_w[l], mlstm_conv_b[l], mlstm_wq[l],
                           mlstm_wk[l], mlstm_igate_b[l], mlstm_fgate_b[l], mla_q_norm_g[l],
                           mla_w_uq[l], mla_kv_norm_g[l], mla_w_ukv[l], mlstm_out_g[l],
                           mla_out_g[l], w_out[l])
        x = x + g1[:, None] * mix
        h = rms_norm(x, ffn_norm_g[l]) * (1.0 + sc2[:, None]) + sh2[:, None]
        x = x + g2[:, None] * conv_ffn(h, ffn_w_up[l], ffn_conv_w[l], ffn_conv_b[l], ffn_w_down[l])
    return rms_norm(x, final_norm_g)
```

```python
import functools

import jax
import jax.numpy as jnp
import numpy as np
from jax import lax
from jax.experimental import pallas as pl
from jax.experimental.pallas import tpu as pltpu

F32 = jnp.float32
BF16 = jnp.bfloat16

EPS = 1e-6
ROPE_THETA = 10000.0
MLSTM_HEADS = 4
MLSTM_DH = 256
MLSTM_W = MLSTM_HEADS * MLSTM_DH
MLA_HEADS = 8
MLA_NOPE = 128
MLA_ROPE = 64
MLA_V = 128
MLA_QK = MLA_NOPE + MLA_ROPE
Q_LORA = 512
KV_LORA = 256
LANES = 128
MIB = 1024 * 1024

IN_PAD = 4096
IN_BLOCK = 1024
MISC_W = 256
MISC_OFF = IN_BLOCK - MISC_W


def _params(n_axes, vmem_mib):
    return pltpu.CompilerParams(
        dimension_semantics=("arbitrary",) * n_axes,
        vmem_limit_bytes=vmem_mib * MIB)


def _rms(x):
    return x * lax.rsqrt(jnp.mean(x * x, axis=-1, keepdims=True) + EPS)


def _silu(x):
    return x / (1.0 + jnp.exp(-x))


def _log_sigmoid(x):
    return jnp.minimum(x, 0.0) - jnp.log1p(jnp.exp(-jnp.abs(x)))


def _mod_kernel(c_ref, w_ref, b_ref, o_ref):
    ca = _silu(c_ref[...]).astype(BF16)
    o_ref[...] = jnp.dot(ca, w_ref[...].astype(BF16),
                         preferred_element_type=F32) + b_ref[...]


def _modulation(c, ada_w, ada_b):
    bsz, d = c.shape
    n = ada_w.shape[1]
    tn = 1024
    cp = jnp.zeros((8, d), F32).at[:bsz].set(c)
    out = pl.pallas_call(
        _mod_kernel,
        grid=(n // tn,),
        in_specs=[pl.BlockSpec((8, d), lambda j: (0, 0)),
                  pl.BlockSpec((d, tn), lambda j: (0, j)),
                  pl.BlockSpec((1, tn), lambda j: (0, j))],
        out_specs=pl.BlockSpec((8, tn), lambda j: (0, j)),
        out_shape=jax.ShapeDtypeStruct((8, n), F32),
        compiler_params=_params(1, 40),
        name="adaln_mod",
    )(cp, ada_w, ada_b.reshape(1, n))
    return out[:bsz].reshape(bsz, 6, d)


def _inproj_kernel(x_ref, mod_ref, g_ref, w_ref, o_ref, misc_ref, h_scr):
    j = pl.program_id(1)

    @pl.when(j == 0)
    def _():
        y = _rms(x_ref[...]) * g_ref[...]
        h_scr[...] = (y * (1.0 + mod_ref[0, 1:2, :]) + mod_ref[0, 0:1, :]).astype(BF16)

    acc = jnp.dot(h_scr[...], w_ref[...], preferred_element_type=F32)
    o_ref[...] = acc.astype(BF16)

    @pl.when(j == pl.num_programs(1) - 1)
    def _():
        misc_ref[...] = acc[:, MISC_OFF:]


def _in_projection(x2, mod, g, w_in_p, seq):
    t, d = x2.shape
    tm = 1024
    per_b = seq // tm
    return pl.pallas_call(
        _inproj_kernel,
        grid=(t // tm, IN_PAD // IN_BLOCK),
        in_specs=[pl.BlockSpec((tm, d), lambda i, j: (i, 0)),
                  pl.BlockSpec((1, 6, d), lambda i, j: (i // per_b, 0, 0)),
                  pl.BlockSpec((1, d), lambda i, j: (0, 0)),
                  pl.BlockSpec((d, IN_BLOCK), lambda i, j: (0, j))],
        out_specs=[pl.BlockSpec((tm, IN_BLOCK), lambda i, j: (i, j)),
                   pl.BlockSpec((tm, MISC_W), lambda i, j: (i, 0))],
        out_shape=[jax.ShapeDtypeStruct((t, IN_PAD), BF16),
                   jax.ShapeDtypeStruct((t, MISC_W), F32)],
        scratch_shapes=[pltpu.VMEM((tm, d), BF16)],
        compiler_params=_params(2, 48),
        name="norm_inproj",
    )(x2, mod, g.reshape(1, d), w_in_p)


def _cumsum_rows(x):
    n = x.shape[0]
    row = lax.broadcasted_iota(jnp.int32, x.shape, 0)
    sh = 1
    while sh < n:
        x = x + jnp.where(row >= sh, pltpu.roll(x, sh, axis=0), 0.0)
        sh *= 2
    return x


def _cumsum_lanes(x):
    n = x.shape[1]
    col = lax.broadcasted_iota(jnp.int32, x.shape, 1)
    sh = 1
    while sh < n:
        x = x + jnp.where(col >= sh, pltpu.roll(x, sh, axis=1), 0.0)
        sh *= 2
    return x


def _mlstm_kernel(u_ref, v_ref, o_ref, gt_ref, cw_ref, cb_ref, wqk_ref, gb_ref, og_ref,
                  out_ref, xs_ref, us_ref, ct_ref, n_ref, m_ref, *, ts, chunk):
    i = pl.program_id(1)
    nh, dh = MLSTM_HEADS, MLSTM_DH

    @pl.when(i == 0)
    def _():
        xs_ref[0:8, :] = jnp.zeros((8, MLSTM_W), F32)
        ct_ref[...] = jnp.zeros(ct_ref.shape, F32)
        n_ref[...] = jnp.zeros(n_ref.shape, F32)
        m_ref[...] = jnp.zeros(m_ref.shape, F32)

    kw = cw_ref.shape[0]
    xs_ref[8:8 + ts, :] = u_ref[...].astype(F32)
    acc = cb_ref[...] + cw_ref[kw - 1:kw, :] * xs_ref[8:8 + ts, :]
    for j in range(kw - 1):
        acc = acc + cw_ref[j:j + 1, :] * xs_ref[pl.ds(8 - (kw - 1) + j, ts), :]
    xs_ref[0:8, :] = xs_ref[ts:ts + 8, :]
    us_ref[...] = _silu(acc).astype(BF16)

    row = lax.broadcasted_iota(jnp.int32, (chunk, chunk), 0)
    col = lax.broadcasted_iota(jnp.int32, (chunk, chunk), 1)
    causal = col <= row
    gb = gb_ref[...]

    for c in range(ts // chunk):
        r0 = c * chunk
        gc = gt_ref[r0:r0 + chunk, MISC_W - LANES:] + gb
        bc = _cumsum_rows(_log_sigmoid(gc))
        gt = gc.T[0:8, :]
        bt = _cumsum_lanes(_log_sigmoid(gt))
        for h in range(nh):
            c0 = h * dh
            li_c = gc[:, h:h + 1]
            b_c = bc[:, nh + h:nh + h + 1]
            li_r = gt[h:h + 1, :]
            b_r = bt[nh + h:nh + h + 1, :]
            g_tot = b_r[:, chunk - 1:chunk]
            z_r = li_r - b_r
            z_c = li_c - b_c
            zmax = jnp.max(z_r, axis=-1, keepdims=True)
            m_prev = m_ref[h:h + 1, 0:1]

            uc = us_ref[r0:r0 + chunk, c0:c0 + dh]
            qk = jnp.dot(uc, wqk_ref[h], preferred_element_type=F32)
            q = qk[:, :dh]
            k = qk[:, dh:] * (dh ** -0.5)
            qb = q.astype(BF16)
            kb = k.astype(BF16)
            vc = v_ref[r0:r0 + chunk, c0:c0 + dh]

            dmat = jnp.where(causal, b_c + z_r, -jnp.inf)
            m_t = jnp.maximum(b_c + m_prev, jnp.max(dmat, axis=-1, keepdims=True))
            s = lax.dot_general(qb, kb, (((1,), (1,)), ((), ())),
                                preferred_element_type=F32) * jnp.exp(dmat - m_t)
            inter = jnp.exp(b_c + m_prev - m_t)
            ct = ct_ref[h]
            num = (jnp.dot(s.astype(BF16), vc, preferred_element_type=F32)
                   + inter * jnp.dot(qb, ct.astype(BF16), preferred_element_type=F32))
            den = (jnp.sum(s, axis=-1, keepdims=True)
                   + inter * jnp.sum(q * n_ref[h:h + 1, :], axis=-1, keepdims=True))
            hh = num / jnp.maximum(jnp.abs(den), jnp.exp(-m_t))

            w_c = jnp.exp(z_c - zmax)
            kw_ = k * w_c
            ct_loc = lax.dot_general(kw_.astype(BF16), vc, (((0,), (0,)), ((), ())),
                                     preferred_element_type=F32)
            n_loc = jnp.sum(kw_, axis=0, keepdims=True)
            mm = jnp.maximum(m_prev, zmax)
            s_prev = jnp.exp(m_prev - mm)
            s_loc = jnp.exp(zmax - mm)
            ct_ref[h] = s_prev * ct + s_loc * ct_loc
            n_ref[h:h + 1, :] = s_prev * n_ref[h:h + 1, :] + s_loc * n_loc
            m_ref[h:h + 1, :] = jnp.broadcast_to(g_tot + mm, (1, LANES))

            og = o_ref[r0:r0 + chunk, c0:c0 + dh].astype(F32)
            hh = hh / (1.0 + jnp.exp(-og))
            out_ref[r0:r0 + chunk, c0:c0 + dh] = (
                _rms(hh) * og_ref[:, c0:c0 + dh]).astype(BF16)


def _mlstm(proj, misc, conv_w, conv_b, wqk, gate_b, out_g, bsz, seq):
    ts, chunk = 512, 128
    nb = seq // ts
    t = bsz * seq
    kern = functools.partial(_mlstm_kernel, ts=ts, chunk=chunk)
    row = lambda b, i: b * nb + i
    return pl.pallas_call(
        kern,
        grid=(bsz, nb),
        in_specs=[pl.BlockSpec((ts, MLSTM_W), lambda b, i: (row(b, i), 0)),
                  pl.BlockSpec((ts, MLSTM_W), lambda b, i: (row(b, i), 1)),
                  pl.BlockSpec((ts, MLSTM_W), lambda b, i: (row(b, i), 2)),
                  pl.BlockSpec((ts, MISC_W), lambda b, i: (row(b, i), 0)),
                  pl.BlockSpec(conv_w.shape, lambda b, i: (0, 0)),
                  pl.BlockSpec((1, MLSTM_W), lambda b, i: (0, 0)),
                  pl.BlockSpec(wqk.shape, lambda b, i: (0, 0, 0)),
                  pl.BlockSpec((1, LANES), lambda b, i: (0, 0)),
                  pl.BlockSpec((1, MLSTM_W), lambda b, i: (0, 0))],
        out_specs=pl.BlockSpec((ts, MLSTM_W), lambda b, i: (row(b, i), 0)),
        out_shape=jax.ShapeDtypeStruct((t, MLSTM_W), BF16),
        scratch_shapes=[pltpu.VMEM((ts + 8, MLSTM_W), F32),
                        pltpu.VMEM((ts, MLSTM_W), BF16),
                        pltpu.VMEM((MLSTM_HEADS, MLSTM_DH, MLSTM_DH), F32),
                        pltpu.VMEM((8, MLSTM_DH), F32),
                        pltpu.VMEM((8, LANES), F32)],
        compiler_params=_params(2, 40),
        name="mlstm",
    )(proj, proj, proj, misc, conv_w, conv_b.reshape(1, MLSTM_W), wqk, gate_b,
      out_g.reshape(1, MLSTM_W))


def _qkv_kernel(cq_ref, ckv_ref, misc_ref, pos_ref, fr_ref, qg_ref, kvg_ref, wq_ref, wkv_ref,
                q_ref, k_ref, v_ref, cqn, ckvn, cos_s, sin_s, kr_s, *, scale):
    h = pl.program_id(1)

    @pl.when(h == 0)
    def _():
        cqn[...] = (_rms(cq_ref[...].astype(F32)) * qg_ref[...]).astype(BF16)
        ckvn[...] = (_rms(ckv_ref[...].astype(F32)) * kvg_ref[...]).astype(BF16)
        ang = pos_ref[...].astype(F32) * fr_ref[...]
        cs = jnp.cos(ang)
        sn = jnp.sin(ang)
        cos_s[...] = cs
        sin_s[...] = sn
        y = misc_ref[:, 0:LANES]
        kr_s[...] = y * cs + pltpu.roll(y, MLA_ROPE, axis=1) * sn

    mq = jnp.dot(cqn[...], wq_ref[0], preferred_element_type=F32)
    q_ref[0, 0, :, 0:MLA_NOPE] = (mq[:, 0:MLA_NOPE] * scale).astype(BF16)
    y = mq[:, MLA_NOPE:]
    r = (y * cos_s[...] + pltpu.roll(y, MLA_ROPE, axis=1) * sin_s[...]) * scale
    q_ref[0, 0, :, MLA_NOPE:MLA_QK] = r[:, 0:MLA_ROPE].astype(BF16)

    mkv = jnp.dot(ckvn[...], wkv_ref[0], preferred_element_type=F32)
    k_ref[0, 0, :, 0:MLA_NOPE] = mkv[:, 0:MLA_NOPE].astype(BF16)
    k_ref[0, 0, :, MLA_NOPE:MLA_QK] = kr_s[:, 0:MLA_ROPE].astype(BF16)
    v_ref[0, 0] = mkv[:, MLA_NOPE:].astype(BF16)


def _mla_qkv(proj, misc, pos, freqs, q_g, kv_g, wq_h, wkv_h, bsz, seq):
    tm = 1024
    t = bsz * seq
    nsb = seq // tm
    cq_blk = (3 * MLSTM_W) // Q_LORA
    ckv_blk = (3 * MLSTM_W + Q_LORA) // KV_LORA
    kern = functools.partial(_qkv_kernel, scale=MLA_QK ** -0.5)
    hm = lambda i, h: (i // nsb, h, i % nsb, 0)
    return pl.pallas_call(
        kern,
        grid=(t // tm, MLA_HEADS),
        in_specs=[pl.BlockSpec((tm, Q_LORA), lambda i, h: (i, cq_blk)),
                  pl.BlockSpec((tm, KV_LORA), lambda i, h: (i, ckv_blk)),
                  pl.BlockSpec((tm, MISC_W), lambda i, h: (i, 0)),
                  pl.BlockSpec((tm, 1), lambda i, h: (i, 0)),
                  pl.BlockSpec((1, LANES), lambda i, h: (0, 0)),
                  pl.BlockSpec((1, Q_LORA), lambda i, h: (0, 0)),
                  pl.BlockSpec((1, KV_LORA), lambda i, h: (0, 0)),
                  pl.BlockSpec((1, Q_LORA, 2 * LANES), lambda i, h: (h, 0, 0)),
                  pl.BlockSpec((1, KV_LORA, 2 * LANES), lambda i, h: (h, 0, 0))],
        out_specs=[pl.BlockSpec((1, 1, tm, MLA_QK), hm),
                   pl.BlockSpec((1, 1, tm, MLA_QK), hm),
                   pl.BlockSpec((1, 1, tm, MLA_V), hm)],
        out_shape=[jax.ShapeDtypeStruct((bsz, MLA_HEADS, seq, MLA_QK), BF16),
                   jax.ShapeDtypeStruct((bsz, MLA_HEADS, seq, MLA_QK), BF16),
                   jax.ShapeDtypeStruct((bsz, MLA_HEADS, seq, MLA_V), BF16)],
        scratch_shapes=[pltpu.VMEM((tm, Q_LORA), BF16),
                        pltpu.VMEM((tm, KV_LORA), BF16),
                        pltpu.VMEM((tm, LANES), F32),
                        pltpu.VMEM((tm, LANES), F32),
                        pltpu.VMEM((tm, LANES), F32)],
        compiler_params=_params(2, 32),
        name="mla_qkv",
    )(proj, proj, misc, pos, freqs, q_g.reshape(1, Q_LORA), kv_g.reshape(1, KV_LORA),
      wq_h, wkv_h)


def _flash_kernel(q_ref, k_ref, v_ref, g_ref, o_ref, m_ref, l_ref, acc_ref, *, tq, tk):
    qi = pl.program_id(2)
    q = q_ref[0, 0]
    m_ref[...] = jnp.full(m_ref.shape, -jnp.inf, F32)
    l_ref[...] = jnp.zeros(l_ref.shape, F32)
    acc_ref[...] = jnp.zeros(acc_ref.shape, F32)

    def step(kj, masked):
        ks = pl.multiple_of(kj * tk, tk)
        k = k_ref[0, 0, pl.ds(ks, tk), :]
        v = v_ref[0, 0, pl.ds(ks, tk), :]
        s = lax.dot_general(q, k, (((1,), (1,)), ((), ())), preferred_element_type=F32)
        if masked:
            row = qi * tq + lax.broadcasted_iota(jnp.int32, (tq, tk), 0)
            col = ks + lax.broadcasted_iota(jnp.int32, (tq, tk), 1)
            s = jnp.where(col <= row, s, -jnp.inf)
        m_prev = m_ref[...]
        m_new = jnp.maximum(m_prev, jnp.max(s, axis=-1, keepdims=True))
        alpha = jnp.exp(m_prev - m_new)
        p = jnp.exp(s - m_new)
        l_ref[...] = alpha * l_ref[...] + jnp.sum(p, axis=-1, keepdims=True)
        acc_ref[...] = alpha * acc_ref[...] + jnp.dot(p.astype(BF16), v,
                                                      preferred_element_type=F32)
        m_ref[...] = m_new

    n_full = qi * (tq // tk)

    def body(kj, carry):
        step(kj, False)
        return carry

    lax.fori_loop(0, n_full, body, 0)
    for d in range(tq // tk):
        step(n_full + d, True)

    o = acc_ref[...] / l_ref[...]
    o_ref[...] = (_rms(o) * g_ref[0]).astype(BF16)


def _mla_attention(q, k, v, out_g):
    bsz, nh, seq, _ = q.shape
    tq, tk = 1024, 512
    nq = seq // tq
    kern = functools.partial(_flash_kernel, tq=tq, tk=tk)
    return pl.pallas_call(
        kern,
        grid=(bsz, nh, nq),
        in_specs=[pl.BlockSpec((1, 1, tq, MLA_QK), lambda b, h, i: (b, h, i, 0)),
                  pl.BlockSpec((1, 1, seq, MLA_QK), lambda b, h, i: (b, h, 0, 0)),
                  pl.BlockSpec((1, 1, seq, MLA_V), lambda b, h, i: (b, h, 0, 0)),
                  pl.BlockSpec((1, 1, MLA_V), lambda b, h, i: (h, 0, 0))],
        out_specs=pl.BlockSpec((tq, MLA_V), lambda b, h, i: (b * nq + i, h)),
        out_shape=jax.ShapeDtypeStruct((bsz * seq, nh * MLA_V), BF16),
        scratch_shapes=[pltpu.VMEM((tq, 1), F32),
                        pltpu.VMEM((tq, 1), F32),
                        pltpu.VMEM((tq, MLA_V), F32)],
        compiler_params=_params(3, 40),
        name="mla_attention",
    )(q, k, v, out_g.reshape(nh, 1, MLA_V))


def _outproj_kernel(hm_ref, ha_ref, w_ref, x_ref, mod_ref, g_ref, x1_ref, h2_ref):
    km = hm_ref.shape[1]
    mix = (jnp.dot(hm_ref[...], w_ref[0:km, :], preferred_element_type=F32)
           + jnp.dot(ha_ref[...], w_ref[km:, :], preferred_element_type=F32))
    x1 = x_ref[...] + mod_ref[0, 2:3, :] * mix
    x1_ref[...] = x1
    y = _rms(x1) * g_ref[...]
    h2_ref[...] = (y * (1.0 + mod_ref[0, 4:5, :]) + mod_ref[0, 3:4, :]).astype(BF16)


def _out_projection(hm, ha, w_out, x2, mod, g, seq):
    t, d = x2.shape
    tm = 256
    per_b = seq // tm
    return pl.pallas_call(
        _outproj_kernel,
        grid=(t // tm,),
        in_specs=[pl.BlockSpec((tm, hm.shape[1]), lambda i: (i, 0)),
                  pl.BlockSpec((tm, ha.shape[1]), lambda i: (i, 0)),
                  pl.BlockSpec(w_out.shape, lambda i: (0, 0)),
                  pl.BlockSpec((tm, d), lambda i: (i, 0)),
                  pl.BlockSpec((1, 6, d), lambda i: (i // per_b, 0, 0)),
                  pl.BlockSpec((1, d), lambda i: (0, 0))],
        out_specs=[pl.BlockSpec((tm, d), lambda i: (i, 0)),
                   pl.BlockSpec((tm, d), lambda i: (i, 0))],
        out_shape=[jax.ShapeDtypeStruct((t, d), F32),
                   jax.ShapeDtypeStruct((t, d), BF16)],
        compiler_params=_params(1, 48),
        name="out_proj",
    )(hm, ha, w_out, x2, mod, g.reshape(1, d))


def _ffn_up_kernel(h_ref, wv_ref, wg_ref, cwv_ref, cwg_ref, cbv_ref, cbg_ref, o_ref,
                   xv_ref, xg_ref, *, tm, blocks_per_seq):
    i = pl.program_id(1)

    @pl.when(i % blocks_per_seq == 0)
    def _():
        xv_ref[0:8, :] = jnp.zeros((8, xv_ref.shape[1]), F32)
        xg_ref[0:8, :] = jnp.zeros((8, xg_ref.shape[1]), F32)

    h = h_ref[...]

    def conv(w_ref, cw_ref, cb_ref, xs_ref):
        kw = cw_ref.shape[0]
        up = jnp.dot(h, w_ref[...], preferred_element_type=F32)
        xs_ref[8:8 + tm, :] = up
        y = cb_ref[...] + cw_ref[kw - 1:kw, :] * up
        for j in range(kw - 1):
            y = y + cw_ref[j:j + 1, :] * xs_ref[pl.ds(8 - (kw - 1) + j, tm), :]
        xs_ref[0:8, :] = xs_ref[tm:tm + 8, :]
        return y

    val = conv(wv_ref, cwv_ref, cbv_ref, xv_ref)
    gate = conv(wg_ref, cwg_ref, cbg_ref, xg_ref)
    o_ref[...] = (_silu(gate) * val).astype(BF16)


def _ffn_up(h2, w_up, conv_w, conv_b, seq):
    t, d = h2.shape
    f = w_up.shape[1] // 2
    tm, tf = 1024, 512
    nf = f // tf
    kw = conv_w.shape[0]
    kern = functools.partial(_ffn_up_kernel, tm=tm, blocks_per_seq=seq // tm)
    cb = conv_b.reshape(1, 2 * f)
    return pl.pallas_call(
        kern,
        grid=(nf, t // tm),
        in_specs=[pl.BlockSpec((tm, d), lambda j, i: (i, 0)),
                  pl.BlockSpec((d, tf), lambda j, i: (0, j)),
                  pl.BlockSpec((d, tf), lambda j, i: (0, nf + j)),
                  pl.BlockSpec((kw, tf), lambda j, i: (0, j)),
                  pl.BlockSpec((kw, tf), lambda j, i: (0, nf + j)),
                  pl.BlockSpec((1, tf), lambda j, i: (0, j)),
                  pl.BlockSpec((1, tf), lambda j, i: (0, nf + j))],
        out_specs=pl.BlockSpec((tm, tf), lambda j, i: (i, j)),
        out_shape=jax.ShapeDtypeStruct((t, f), BF16),
        scratch_shapes=[pltpu.VMEM((tm + 8, tf), F32),
                        pltpu.VMEM((tm + 8, tf), F32)],
        compiler_params=_params(2, 48),
        name="ffn_up",
    )(h2, w_up, w_up, conv_w, conv_w, cb, cb)


def _ffn_down_kernel(a_ref, w_ref, x1_ref, mod_ref, g_ref, o_ref, acc_ref):
    k = pl.program_id(1)

    @pl.when(k == 0)
    def _():
        acc_ref[...] = jnp.zeros(acc_ref.shape, F32)

    acc_ref[...] += jnp.dot(a_ref[...], w_ref[...], preferred_element_type=F32)

    @pl.when(k == pl.num_programs(1) - 1)
    def _():
        x2 = x1_ref[...] + mod_ref[0, 5:6, :] * acc_ref[...]
        o_ref[...] = _rms(x2) * g_ref[...]


def _ffn_down(act, w_down, x1, mod, g, seq):
    t, f = act.shape
    d = w_down.shape[1]
    tm, tk = 512, 512
    per_b = seq // tm
    return pl.pallas_call(
        _ffn_down_kernel,
        grid=(t // tm, f // tk),
        in_specs=[pl.BlockSpec((tm, tk), lambda i, k: (i, k)),
                  pl.BlockSpec((tk, d), lambda i, k: (k, 0)),
                  pl.BlockSpec((tm, d), lambda i, k: (i, 0)),
                  pl.BlockSpec((1, 6, d), lambda i, k: (i // per_b, 0, 0)),
                  pl.BlockSpec((1, d), lambda i, k: (0, 0))],
        out_specs=pl.BlockSpec((tm, d), lambda i, k: (i, 0)),
        out_shape=jax.ShapeDtypeStruct((t, d), F32),
        scratch_shapes=[pltpu.VMEM((tm, d), F32)],
        compiler_params=_params(2, 40),
        name="ffn_down",
    )(act, w_down, x1, mod, g.reshape(1, d))


def _rot_cols(w):
    half = w.shape[-1] // 2
    return jnp.concatenate([-w[..., half:], w[..., :half]], axis=-1)


def _prep_w_in(w_in):
    d = w_in.shape[0]
    sizes = (MLSTM_W, MLSTM_W, MLSTM_W, MLSTM_HEADS, MLSTM_HEADS, Q_LORA, KV_LORA, MLA_ROPE)
    offs = np.cumsum(sizes)[:-1]
    u, v, o, ig, fg, cq, ckv, kr = jnp.split(w_in, offs, axis=1)
    used = 3 * MLSTM_W + Q_LORA + KV_LORA + 2 * MLA_ROPE + 2 * MLSTM_HEADS
    pad = jnp.zeros((d, IN_PAD - used), w_in.dtype)
    return jnp.concatenate([u, v, o, cq, ckv, kr, _rot_cols(kr), ig, fg, pad], axis=1).astype(BF16)


def kernel(x, c, positions, ada_w, ada_b, attn_norm_g, w_in, mlstm_conv_w, mlstm_conv_b, mlstm_wq, mlstm_wk, mlstm_igate_b, mlstm_fgate_b, mla_q_norm_g, mla_w_uq, mla_kv_norm_g, mla_w_ukv, mlstm_out_g, mla_out_g, w_out, ffn_norm_g, ffn_w_up, ffn_conv_w, ffn_conv_b, ffn_w_down, final_norm_g):
    bsz, seq, d = x.shape
    t = bsz * seq
    depth = ada_w.shape[0]
    xr = x.reshape(t, d)
    pos = positions.reshape(t, 1)
    half = MLA_ROPE // 2
    freqs = ROPE_THETA ** (-jnp.arange(half, dtype=F32) / half)
    freqs = jnp.tile(freqs, LANES // half).reshape(1, LANES)

    for l in range(depth):
        mod = _modulation(c, ada_w[l], ada_b[l])

        proj, misc = _in_projection(xr, mod, attn_norm_g[l], _prep_w_in(w_in[l]), seq)

        wqk = jnp.concatenate([mlstm_wq[l], mlstm_wk[l]], axis=-1).astype(BF16)
        gate_b = jnp.zeros((1, LANES), F32)
        gate_b = gate_b.at[0, :MLSTM_HEADS].set(mlstm_igate_b[l])
        gate_b = gate_b.at[0, MLSTM_HEADS:2 * MLSTM_HEADS].set(mlstm_fgate_b[l])
        hm = _mlstm(proj, misc, mlstm_conv_w[l], mlstm_conv_b[l], wqk, gate_b,
                    mlstm_out_g[l], bsz, seq)

        wq = mla_w_uq[l].reshape(Q_LORA, MLA_HEADS, MLA_QK)
        wq_r = wq[..., MLA_NOPE:]
        wq_h = jnp.concatenate([wq[..., :MLA_NOPE], wq_r, _rot_cols(wq_r)], axis=-1)
        wq_h = wq_h.transpose(1, 0, 2).astype(BF16)
        wkv_h = mla_w_ukv[l].reshape(KV_LORA, MLA_HEADS, MLA_NOPE + MLA_V)
        wkv_h = wkv_h.transpose(1, 0, 2).astype(BF16)
        q, k, v = _mla_qkv(proj, misc, pos, freqs, mla_q_norm_g[l], mla_kv_norm_g[l],
                           wq_h, wkv_h, bsz, seq)
        ha = _mla_attention(q, k, v, mla_out_g[l])

        x1, h2 = _out_projection(hm, ha, w_out[l].astype(BF16), xr, mod, ffn_norm_g[l], seq)

        act = _ffn_up(h2, ffn_w_up[l].astype(BF16), ffn_conv_w[l], ffn_conv_b[l], seq)
        if l == depth - 1:
            xr = _ffn_down(act, ffn_w_down[l].astype(BF16), x1, mod, final_norm_g, seq)
        else:
            raise NotImplementedError("fused final norm assumes a single layer")
    return xr.reshape(bsz, seq, d)
```

```python
import functools

import jax
import jax.numpy as jnp
import numpy as np
from jax import lax
from jax.experimental import pallas as pl
from jax.experimental.pallas import tpu as pltpu

F32 = jnp.float32
BF16 = jnp.bfloat16

EPS = 1e-6
ROPE_THETA = 10000.0
MLSTM_HEADS = 4
MLSTM_DH = 256
MLSTM_W = MLSTM_HEADS * MLSTM_DH
MLA_HEADS = 8
MLA_NOPE = 128
MLA_ROPE = 64
MLA_V = 128
MLA_QK = MLA_NOPE + MLA_ROPE
Q_LORA = 512
KV_LORA = 256
LANES = 128
MIB = 1024 * 1024

IN_PAD = 4096
IN_BLOCK = 1024
MISC_W = 256
MISC_OFF = IN_BLOCK - MISC_W


def _params(n_axes, vmem_mib):
    return pltpu.CompilerParams(
        dimension_semantics=("arbitrary",) * n_axes,
        vmem_limit_bytes=vmem_mib * MIB)


def _rms(x):
    return x * lax.rsqrt(jnp.mean(x * x, axis=-1, keepdims=True) + EPS)


def _silu(x):
    return x / (1.0 + jnp.exp(-x))


def _log_sigmoid(x):
    return jnp.minimum(x, 0.0) - jnp.log1p(jnp.exp(-jnp.abs(x)))


def _mod_kernel(c_ref, w_ref, b_ref, o_ref):
    ca = _silu(c_ref[...]).astype(BF16)
    o_ref[...] = jnp.dot(ca, w_ref[...].astype(BF16),
                         preferred_element_type=F32) + b_ref[...]


def _modulation(c, ada_w, ada_b):
    bsz, d = c.shape
    n = ada_w.shape[1]
    tn = 1024
    cp = jnp.zeros((8, d), F32).at[:bsz].set(c)
    out = pl.pallas_call(
        _mod_kernel,
        grid=(n // tn,),
        in_specs=[pl.BlockSpec((8, d), lambda j: (0, 0)),
                  pl.BlockSpec((d, tn), lambda j: (0, j)),
                  pl.BlockSpec((1, tn), lambda j: (0, j))],
        out_specs=pl.BlockSpec((8, tn), lambda j: (0, j)),
        out_shape=jax.ShapeDtypeStruct((8, n), F32),
        compiler_params=_params(1, 40),
        name="adaln_mod",
    )(cp, ada_w, ada_b.reshape(1, n))
    return out[:bsz].reshape(bsz, 6, d)


def _inproj_kernel(x_ref, mod_ref, g_ref, w_ref, o_ref, misc_ref, h_scr):
    j = pl.program_id(1)

    @pl.when(j == 0)
    def _():
        y = _rms(x_ref[...]) * g_ref[...]
        h_scr[...] = (y * (1.0 + mod_ref[0, 1:2, :]) + mod_ref[0, 0:1, :]).astype(BF16)

    acc = jnp.dot(h_scr[...], w_ref[...], preferred_element_type=F32)
    o_ref[...] = acc.astype(BF16)

    @pl.when(j == pl.num_programs(1) - 1)
    def _():
        misc_ref[...] = acc[:, MISC_OFF:]


def _in_projection(x2, mod, g, w_in_p, seq):
    t, d = x2.shape
    tm = 1024
    per_b = seq // tm
    return pl.pallas_call(
        _inproj_kernel,
        grid=(t // tm, IN_PAD // IN_BLOCK),
        in_specs=[pl.BlockSpec((tm, d), lambda i, j: (i, 0)),
                  pl.BlockSpec((1, 6, d), lambda i, j: (i // per_b, 0, 0)),
                  pl.BlockSpec((1, d), lambda i, j: (0, 0)),
                  pl.BlockSpec((d, IN_BLOCK), lambda i, j: (0, j))],
        out_specs=[pl.BlockSpec((tm, IN_BLOCK), lambda i, j: (i, j)),
                   pl.BlockSpec((tm, MISC_W), lambda i, j: (i, 0))],
        out_shape=[jax.ShapeDtypeStruct((t, IN_PAD), BF16),
                   jax.ShapeDtypeStruct((t, MISC_W), F32)],
        scratch_shapes=[pltpu.VMEM((tm, d), BF16)],
        compiler_params=_params(2, 48),
        name="norm_inproj",
    )(x2, mod, g.reshape(1, d), w_in_p)


def _cumsum_rows(x):
    n = x.shape[0]
    row = lax.broadcasted_iota(jnp.int32, x.shape, 0)
    sh = 1
    while sh < n:
        x = x + jnp.where(row >= sh, pltpu.roll(x, sh, axis=0), 0.0)
        sh *= 2
    return x


def _cumsum_lanes(x):
    n = x.shape[1]
    col = lax.broadcasted_iota(jnp.int32, x.shape, 1)
    sh = 1
    while sh < n:
        x = x + jnp.where(col >= sh, pltpu.roll(x, sh, axis=1), 0.0)
        sh *= 2
    return x


def _mlstm_kernel(u_ref, v_ref, o_ref, gt_ref, cw_ref, cb_ref, wqk_ref, gb_ref, og_ref,
                  out_ref, xs_ref, us_ref, ct_ref, n_ref, m_ref, *, ts, chunk):
    i = pl.program_id(1)
    nh, dh = MLSTM_HEADS, MLSTM_DH

    @pl.when(i == 0)
    def _():
        xs_ref[0:8, :] = jnp.zeros((8, MLSTM_W), F32)
        ct_ref[...] = jnp.zeros(ct_ref.shape, F32)
        n_ref[...] = jnp.zeros(n_ref.shape, F32)
        m_ref[...] = jnp.zeros(m_ref.shape, F32)

    kw = cw_ref.shape[0]
    xs_ref[8:8 + ts, :] = u_ref[...].astype(F32)
    acc = cb_ref[...] + cw_ref[kw - 1:kw, :] * xs_ref[8:8 + ts, :]
    for j in range(kw - 1):
        acc = acc + cw_ref[j:j + 1, :] * xs_ref[pl.ds(8 - (kw - 1) + j, ts), :]
    xs_ref[0:8, :] = xs_ref[ts:ts + 8, :]
    us_ref[...] = _silu(acc).astype(BF16)

    row = lax.broadcasted_iota(jnp.int32, (chunk, chunk), 0)
    col = lax.broadcasted_iota(jnp.int32, (chunk, chunk), 1)
    causal = col <= row
    gb = gb_ref[...]

    for c in range(ts // chunk):
        r0 = c * chunk
        gc = gt_ref[r0:r0 + chunk, MISC_W - LANES:] + gb
        bc = _cumsum_rows(_log_sigmoid(gc))
        gt = gc.T[0:8, :]
        bt = _cumsum_lanes(_log_sigmoid(gt))
        for h in range(nh):
            c0 = h * dh
            li_c = gc[:, h:h + 1]
            b_c = bc[:, nh + h:nh + h + 1]
            li_r = gt[h:h + 1, :]
            b_r = bt[nh + h:nh + h + 1, :]
            g_tot = b_r[:, chunk - 1:chunk]
            z_r = li_r - b_r
            z_c = li_c - b_c
            zmax = jnp.max(z_r, axis=-1, keepdims=True)
            m_prev = m_ref[h:h + 1, 0:1]

            uc = us_ref[r0:r0 + chunk, c0:c0 + dh]
            qk = jnp.dot(uc, wqk_ref[h], preferred_element_type=F32)
            q = qk[:, :dh]
            k = qk[:, dh:] * (dh ** -0.5)
            qb = q.astype(BF16)
            kb = k.astype(BF16)
            vc = v_ref[r0:r0 + chunk, c0:c0 + dh]

            dmat = jnp.where(causal, b_c + z_r, -jnp.inf)
            m_t = jnp.maximum(b_c + m_prev, jnp.max(dmat, axis=-1, keepdims=True))
            s = lax.dot_general(qb, kb, (((1,), (1,)), ((), ())),
                                preferred_element_type=F32) * jnp.exp(dmat - m_t)
            inter = jnp.exp(b_c + m_prev - m_t)
            ct = ct_ref[h]
            num = (jnp.dot(s.astype(BF16), vc, preferred_element_type=F32)
                   + inter * jnp.dot(qb, ct.astype(BF16), preferred_element_type=F32))
            den = (jnp.sum(s, axis=-1, keepdims=True)
                   + inter * jnp.sum(q * n_ref[h:h + 1, :], axis=-1, keepdims=True))
            hh = num / jnp.maximum(jnp.abs(den), jnp.exp(-m_t))

            w_c = jnp.exp(z_c - zmax)
            kw_ = k * w_c
            ct_loc = lax.dot_general(kw_.astype(BF16), vc, (((0,), (0,)), ((), ())),
                                     preferred_element_type=F32)
            n_loc = jnp.sum(kw_, axis=0, keepdims=True)
            mm = jnp.maximum(m_prev, zmax)
            s_prev = jnp.exp(m_prev - mm)
            s_loc = jnp.exp(zmax - mm)
            ct_ref[h] = s_prev * ct + s_loc * ct_loc
            n_ref[h:h + 1, :] = s_prev * n_ref[h:h + 1, :] + s_loc * n_loc
            m_ref[h:h + 1, :] = jnp.broadcast_to(g_tot + mm, (1, LANES))

            og = o_ref[r0:r0 + chunk, c0:c0 + dh].astype(F32)
            hh = hh / (1.0 + jnp.exp(-og))
            out_ref[r0:r0 + chunk, c0:c0 + dh] = (
                _rms(hh) * og_ref[:, c0:c0 + dh]).astype(BF16)


def _mlstm(proj, misc, conv_w, conv_b, wqk, gate_b, out_g, bsz, seq):
    ts, chunk = 512, 128
    nb = seq // ts
    t = bsz * seq
    kern = functools.partial(_mlstm_kernel, ts=ts, chunk=chunk)
    row = lambda b, i: b * nb + i
    return pl.pallas_call(
        kern,
        grid=(bsz, nb),
        in_specs=[pl.BlockSpec((ts, MLSTM_W), lambda b, i: (row(b, i), 0)),
                  pl.BlockSpec((ts, MLSTM_W), lambda b, i: (row(b, i), 1)),
                  pl.BlockSpec((ts, MLSTM_W), lambda b, i: (row(b, i), 2)),
                  pl.BlockSpec((ts, MISC_W), lambda b, i: (row(b, i), 0)),
                  pl.BlockSpec(conv_w.shape, lambda b, i: (0, 0)),
                  pl.BlockSpec((1, MLSTM_W), lambda b, i: (0, 0)),
                  pl.BlockSpec(wqk.shape, lambda b, i: (0, 0, 0)),
                  pl.BlockSpec((1, LANES), lambda b, i: (0, 0)),
                  pl.BlockSpec((1, MLSTM_W), lambda b, i: (0, 0))],
        out_specs=pl.BlockSpec((ts, MLSTM_W), lambda b, i: (row(b, i), 0)),
        out_shape=jax.ShapeDtypeStruct((t, MLSTM_W), BF16),
        scratch_shapes=[pltpu.VMEM((ts + 8, MLSTM_W), F32),
                        pltpu.VMEM((ts, MLSTM_W), BF16),
                        pltpu.VMEM((MLSTM_HEADS, MLSTM_DH, MLSTM_DH), F32),
                        pltpu.VMEM((8, MLSTM_DH), F32),
                        pltpu.VMEM((8, LANES), F32)],
        compiler_params=_params(2, 40),
        name="mlstm",
    )(proj, proj, proj, misc, conv_w, conv_b.reshape(1, MLSTM_W), wqk, gate_b,
      out_g.reshape(1, MLSTM_W))


ATT_TQ = 1024
ATT_TK = 512


def _qkv_kernel(cq_ref, ckv_ref, misc_ref, pos_ref, fr_ref, qg_ref, kvg_ref, wq_ref, wkv_ref,
                qt_ref, k_ref, vt_ref, cqn, ckvn, cos_s, sin_s, kr_s, *, scale, tk):
    h = pl.program_id(1)

    @pl.when(h == 0)
    def _():
        cqn[...] = (_rms(cq_ref[...].astype(F32)) * qg_ref[...]).astype(BF16)
        ckvn[...] = (_rms(ckv_ref[...].astype(F32)) * kvg_ref[...]).astype(BF16)
        ang = pos_ref[...].astype(F32) * fr_ref[...]
        cs = jnp.cos(ang)
        sn = jnp.sin(ang)
        cos_s[...] = cs
        sin_s[...] = sn
        y = misc_ref[:, 0:LANES]
        kr_s[...] = y * cs + pltpu.roll(y, MLA_ROPE, axis=1) * sn

    mq = jnp.dot(cqn[...], wq_ref[0], preferred_element_type=F32)
    qt_ref[0, 0, 0:MLA_NOPE, :] = (mq[:, 0:MLA_NOPE] * scale).T.astype(BF16)
    y = mq[:, MLA_NOPE:]
    r = (y * cos_s[...] + pltpu.roll(y, MLA_ROPE, axis=1) * sin_s[...]) * scale
    qt_ref[0, 0, MLA_NOPE:MLA_QK, :] = r.T[0:MLA_ROPE, :].astype(BF16)

    mkv = jnp.dot(ckvn[...], wkv_ref[0], preferred_element_type=F32)
    k_ref[0, 0, :, 0:MLA_NOPE] = mkv[:, 0:MLA_NOPE].astype(BF16)
    k_ref[0, 0, :, MLA_NOPE:MLA_QK] = kr_s[:, 0:MLA_ROPE].astype(BF16)
    vt = mkv[:, MLA_NOPE:].T.astype(BF16)
    for j in range(vt.shape[1] // tk):
        vt_ref[0, 0, j] = vt[:, j * tk:(j + 1) * tk]


def _mla_qkv(proj, misc, pos, freqs, q_g, kv_g, wq_h, wkv_h, bsz, seq):
    tm, tk = ATT_TQ, ATT_TK
    t = bsz * seq
    nsb = seq // tm
    cq_blk = (3 * MLSTM_W) // Q_LORA
    ckv_blk = (3 * MLSTM_W + Q_LORA) // KV_LORA
    kern = functools.partial(_qkv_kernel, scale=MLA_QK ** -0.5 * float(np.log2(np.e)), tk=tk)
    return pl.pallas_call(
        kern,
        grid=(t // tm, MLA_HEADS),
        in_specs=[pl.BlockSpec((tm, Q_LORA), lambda i, h: (i, cq_blk)),
                  pl.BlockSpec((tm, KV_LORA), lambda i, h: (i, ckv_blk)),
                  pl.BlockSpec((tm, MISC_W), lambda i, h: (i, 0)),
                  pl.BlockSpec((tm, 1), lambda i, h: (i, 0)),
                  pl.BlockSpec((1, LANES), lambda i, h: (0, 0)),
                  pl.BlockSpec((1, Q_LORA), lambda i, h: (0, 0)),
                  pl.BlockSpec((1, KV_LORA), lambda i, h: (0, 0)),
                  pl.BlockSpec((1, Q_LORA, 2 * LANES), lambda i, h: (h, 0, 0)),
                  pl.BlockSpec((1, KV_LORA, 2 * LANES), lambda i, h: (h, 0, 0))],
        out_specs=[pl.BlockSpec((1, 1, MLA_QK, tm), lambda i, h: (i // nsb, h, 0, i % nsb)),
                   pl.BlockSpec((1, 1, tm, MLA_QK), lambda i, h: (i // nsb, h, i % nsb, 0)),
                   pl.BlockSpec((1, 1, tm // tk, MLA_V, tk),
                                lambda i, h: (i // nsb, h, i % nsb, 0, 0))],
        out_shape=[jax.ShapeDtypeStruct((bsz, MLA_HEADS, MLA_QK, seq), BF16),
                   jax.ShapeDtypeStruct((bsz, MLA_HEADS, seq, MLA_QK), BF16),
                   jax.ShapeDtypeStruct((bsz, MLA_HEADS, seq // tk, MLA_V, tk), BF16)],
        scratch_shapes=[pltpu.VMEM((tm, Q_LORA), BF16),
                        pltpu.VMEM((tm, KV_LORA), BF16),
                        pltpu.VMEM((tm, LANES), F32),
                        pltpu.VMEM((tm, LANES), F32),
                        pltpu.VMEM((tm, LANES), F32)],
        compiler_params=_params(2, 32),
        name="mla_qkv",
    )(proj, proj, misc, pos, freqs, q_g.reshape(1, Q_LORA), kv_g.reshape(1, KV_LORA),
      wq_h, wkv_h)


def _flash_kernel(qt_ref, k_ref, vt_ref, g_ref, o_ref, m_ref, l_ref, acc_ref, bias_ref, *, tq, tk):
    qi = pl.program_id(2)
    m_ref[...] = jnp.full(m_ref.shape, -jnp.inf, F32)
    l_ref[...] = jnp.zeros(l_ref.shape, F32)
    acc_ref[...] = jnp.zeros(acc_ref.shape, F32)
    key = lax.broadcasted_iota(jnp.int32, (tk, tq), 0)
    qry = lax.broadcasted_iota(jnp.int32, (tk, tq), 1)
    bias_ref[...] = jnp.where(key <= qry, 0.0, -jnp.inf).astype(F32)

    def step(kj, c0, masked):
        k = k_ref[0, 0, pl.ds(pl.multiple_of(kj * tk, tk), tk), :]
        s = jnp.dot(k, qt_ref[0, 0, :, c0:], preferred_element_type=F32)
        if masked:
            s = s + bias_ref[:, 0:tq - c0]
        m_prev = m_ref[:, c0:]
        m_new = jnp.maximum(m_prev, jnp.max(s, axis=0, keepdims=True))
        alpha = jnp.exp2(m_prev - m_new)
        p = jnp.exp2(s - m_new)
        l_ref[:, c0:] = alpha * l_ref[:, c0:] + jnp.sum(p, axis=0, keepdims=True)
        acc_ref[:, c0:] = alpha * acc_ref[:, c0:] + jnp.dot(
            vt_ref[0, 0, kj], p.astype(BF16), preferred_element_type=F32)
        m_ref[:, c0:] = m_new

    n_full = qi * (tq // tk)

    def body(kj, carry):
        step(kj, 0, False)
        return carry

    lax.fori_loop(0, n_full, body, 0)
    for d in range(tq // tk):
        step(n_full + d, d * tk, True)

    o = acc_ref[...] / l_ref[...]
    o = o * lax.rsqrt(jnp.mean(o * o, axis=0, keepdims=True) + EPS) * g_ref[0]
    o_ref[...] = o.T.astype(BF16)


def _mla_attention(qt, k, vt, out_g):
    bsz, nh, seq, _ = k.shape
    tq, tk = ATT_TQ, ATT_TK
    nq = seq // tq
    kern = functools.partial(_flash_kernel, tq=tq, tk=tk)
    return pl.pallas_call(
        kern,
        grid=(bsz, nh, nq),
        in_specs=[pl.BlockSpec((1, 1, MLA_QK, tq), lambda b, h, i: (b, h, 0, i)),
                  pl.BlockSpec((1, 1, seq, MLA_QK), lambda b, h, i: (b, h, 0, 0)),
                  pl.BlockSpec((1, 1, seq // tk, MLA_V, tk), lambda b, h, i: (b, h, 0, 0, 0)),
                  pl.BlockSpec((1, MLA_V, 1), lambda b, h, i: (h, 0, 0))],
        out_specs=pl.BlockSpec((tq, MLA_V), lambda b, h, i: (b * nq + i, h)),
        out_shape=jax.ShapeDtypeStruct((bsz * seq, nh * MLA_V), BF16),
        scratch_shapes=[pltpu.VMEM((1, tq), F32),
                        pltpu.VMEM((1, tq), F32),
                        pltpu.VMEM((MLA_V, tq), F32),
                        pltpu.VMEM((tk, tq), F32)],
        compiler_params=_params(3, 40),
        name="mla_attention",
    )(qt, k, vt, out_g.reshape(nh, MLA_V, 1))


def _outproj_kernel(hm_ref, ha_ref, w_ref, x_ref, mod_ref, g_ref, x1_ref, h2_ref):
    km = hm_ref.shape[1]
    mix = (jnp.dot(hm_ref[...], w_ref[0:km, :], preferred_element_type=F32)
           + jnp.dot(ha_ref[...], w_ref[km:, :], preferred_element_type=F32))
    x1 = x_ref[...] + mod_ref[0, 2:3, :] * mix
    x1_ref[...] = x1
    y = _rms(x1) * g_ref[...]
    h2_ref[...] = (y * (1.0 + mod_ref[0, 4:5, :]) + mod_ref[0, 3:4, :]).astype(BF16)


def _out_projection(hm, ha, w_out, x2, mod, g, seq):
    t, d = x2.shape
    tm = 256
    per_b = seq // tm
    return pl.pallas_call(
        _outproj_kernel,
        grid=(t // tm,),
        in_specs=[pl.BlockSpec((tm, hm.shape[1]), lambda i: (i, 0)),
                  pl.BlockSpec((tm, ha.shape[1]), lambda i: (i, 0)),
                  pl.BlockSpec(w_out.shape, lambda i: (0, 0)),
                  pl.BlockSpec((tm, d), lambda i: (i, 0)),
                  pl.BlockSpec((1, 6, d), lambda i: (i // per_b, 0, 0)),
                  pl.BlockSpec((1, d), lambda i: (0, 0))],
        out_specs=[pl.BlockSpec((tm, d), lambda i: (i, 0)),
                   pl.BlockSpec((tm, d), lambda i: (i, 0))],
        out_shape=[jax.ShapeDtypeStruct((t, d), F32),
                   jax.ShapeDtypeStruct((t, d), BF16)],
        compiler_params=_params(1, 48),
        name="out_proj",
    )(hm, ha, w_out, x2, mod, g.reshape(1, d))


def _ffn_up_kernel(h_ref, wv_ref, wg_ref, cwv_ref, cwg_ref, cbv_ref, cbg_ref, o_ref,
                   xv_ref, xg_ref, *, tm, blocks_per_seq):
    i = pl.program_id(1)

    @pl.when(i % blocks_per_seq == 0)
    def _():
        xv_ref[0:8, :] = jnp.zeros((8, xv_ref.shape[1]), F32)
        xg_ref[0:8, :] = jnp.zeros((8, xg_ref.shape[1]), F32)

    h = h_ref[...]

    def conv(w_ref, cw_ref, cb_ref, xs_ref):
        kw = cw_ref.shape[0]
        up = jnp.dot(h, w_ref[...], preferred_element_type=F32)
        xs_ref[8:8 + tm, :] = up
        y = cb_ref[...] + cw_ref[kw - 1:kw, :] * up
        for j in range(kw - 1):
            y = y + cw_ref[j:j + 1, :] * xs_ref[pl.ds(8 - (kw - 1) + j, tm), :]
        xs_ref[0:8, :] = xs_ref[tm:tm + 8, :]
        return y

    val = conv(wv_ref, cwv_ref, cbv_ref, xv_ref)
    gate = conv(wg_ref, cwg_ref, cbg_ref, xg_ref)
    o_ref[...] = (_silu(gate) * val).astype(BF16)


def _ffn_up(h2, w_up, conv_w, conv_b, seq):
    t, d = h2.shape
    f = w_up.shape[1] // 2
    tm, tf = 1024, 512
    nf = f // tf
    kw = conv_w.shape[0]
    kern = functools.partial(_ffn_up_kernel, tm=tm, blocks_per_seq=seq // tm)
    cb = conv_b.reshape(1, 2 * f)
    return pl.pallas_call(
        kern,
        grid=(nf, t // tm),
        in_specs=[pl.BlockSpec((tm, d), lambda j, i: (i, 0)),
                  pl.BlockSpec((d, tf), lambda j, i: (0, j)),
                  pl.BlockSpec((d, tf), lambda j, i: (0, nf + j)),
                  pl.BlockSpec((kw, tf), lambda j, i: (0, j)),
                  pl.BlockSpec((kw, tf), lambda j, i: (0, nf + j)),
                  pl.BlockSpec((1, tf), lambda j, i: (0, j)),
                  pl.BlockSpec((1, tf), lambda j, i: (0, nf + j))],
        out_specs=pl.BlockSpec((tm, tf), lambda j, i: (i, j)),
        out_shape=jax.ShapeDtypeStruct((t, f), BF16),
        scratch_shapes=[pltpu.VMEM((tm + 8, tf), F32),
                        pltpu.VMEM((tm + 8, tf), F32)],
        compiler_params=_params(2, 48),
        name="ffn_up",
    )(h2, w_up, w_up, conv_w, conv_w, cb, cb)


def _ffn_down_kernel(a_ref, w_ref, x1_ref, mod_ref, g_ref, o_ref, acc_ref):
    k = pl.program_id(1)

    @pl.when(k == 0)
    def _():
        acc_ref[...] = jnp.zeros(acc_ref.shape, F32)

    acc_ref[...] += jnp.dot(a_ref[...], w_ref[...], preferred_element_type=F32)

    @pl.when(k == pl.num_programs(1) - 1)
    def _():
        x2 = x1_ref[...] + mod_ref[0, 5:6, :] * acc_ref[...]
        o_ref[...] = _rms(x2) * g_ref[...]


def _ffn_down(act, w_down, x1, mod, g, seq):
    t, f = act.shape
    d = w_down.shape[1]
    tm, tk = 512, 512
    per_b = seq // tm
    return pl.pallas_call(
        _ffn_down_kernel,
        grid=(t // tm, f // tk),
        in_specs=[pl.BlockSpec((tm, tk), lambda i, k: (i, k)),
                  pl.BlockSpec((tk, d), lambda i, k: (k, 0)),
                  pl.BlockSpec((tm, d), lambda i, k: (i, 0)),
                  pl.BlockSpec((1, 6, d), lambda i, k: (i // per_b, 0, 0)),
                  pl.BlockSpec((1, d), lambda i, k: (0, 0))],
        out_specs=pl.BlockSpec((tm, d), lambda i, k: (i, 0)),
        out_shape=jax.ShapeDtypeStruct((t, d), F32),
        scratch_shapes=[pltpu.VMEM((tm, d), F32)],
        compiler_params=_params(2, 40),
        name="ffn_down",
    )(act, w_down, x1, mod, g.reshape(1, d))


def _rot_cols(w):
    half = w.shape[-1] // 2
    return jnp.concatenate([-w[..., half:], w[..., :half]], axis=-1)


def _prep_w_in(w_in):
    d = w_in.shape[0]
    sizes = (MLSTM_W, MLSTM_W, MLSTM_W, MLSTM_HEADS, MLSTM_HEADS, Q_LORA, KV_LORA, MLA_ROPE)
    offs = np.cumsum(sizes)[:-1]
    u, v, o, ig, fg, cq, ckv, kr = jnp.split(w_in, offs, axis=1)
    used = 3 * MLSTM_W + Q_LORA + KV_LORA + 2 * MLA_ROPE + 2 * MLSTM_HEADS
    pad = jnp.zeros((d, IN_PAD - used), w_in.dtype)
    return jnp.concatenate([u, v, o, cq, ckv, kr, _rot_cols(kr), ig, fg, pad], axis=1).astype(BF16)


def kernel(x, c, positions, ada_w, ada_b, attn_norm_g, w_in, mlstm_conv_w, mlstm_conv_b, mlstm_wq, mlstm_wk, mlstm_igate_b, mlstm_fgate_b, mla_q_norm_g, mla_w_uq, mla_kv_norm_g, mla_w_ukv, mlstm_out_g, mla_out_g, w_out, ffn_norm_g, ffn_w_up, ffn_conv_w, ffn_conv_b, ffn_w_down, final_norm_g):
    bsz, seq, d = x.shape
    t = bsz * seq
    depth = ada_w.shape[0]
    xr = x.reshape(t, d)
    pos = positions.reshape(t, 1)
    half = MLA_ROPE // 2
    freqs = ROPE_THETA ** (-jnp.arange(half, dtype=F32) / half)
    freqs = jnp.tile(freqs, LANES // half).reshape(1, LANES)

    for l in range(depth):
        mod = _modulation(c, ada_w[l], ada_b[l])

        proj, misc = _in_projection(xr, mod, attn_norm_g[l], _prep_w_in(w_in[l]), seq)

        wqk = jnp.concatenate([mlstm_wq[l], mlstm_wk[l]], axis=-1).astype(BF16)
        gate_b = jnp.zeros((1, LANES), F32)
        gate_b = gate_b.at[0, :MLSTM_HEADS].set(mlstm_igate_b[l])
        gate_b = gate_b.at[0, MLSTM_HEADS:2 * MLSTM_HEADS].set(mlstm_fgate_b[l])
        hm = _mlstm(proj, misc, mlstm_conv_w[l], mlstm_conv_b[l], wqk, gate_b,
                    mlstm_out_g[l], bsz, seq)

        wq = mla_w_uq[l].reshape(Q_LORA, MLA_HEADS, MLA_QK)
        wq_r = wq[..., MLA_NOPE:]
        wq_h = jnp.concatenate([wq[..., :MLA_NOPE], wq_r, _rot_cols(wq_r)], axis=-1)
        wq_h = wq_h.transpose(1, 0, 2).astype(BF16)
        wkv_h = mla_w_ukv[l].reshape(KV_LORA, MLA_HEADS, MLA_NOPE + MLA_V)
        wkv_h = wkv_h.transpose(1, 0, 2).astype(BF16)
        qt, k, vt = _mla_qkv(proj, misc, pos, freqs, mla_q_norm_g[l], mla_kv_norm_g[l],
                             wq_h, wkv_h, bsz, seq)
        ha = _mla_attention(qt, k, vt, mla_out_g[l])

        x1, h2 = _out_projection(hm, ha, w_out[l].astype(BF16), xr, mod, ffn_norm_g[l], seq)

        act = _ffn_up(h2, ffn_w_up[l].astype(BF16), ffn_conv_w[l], ffn_conv_b[l], seq)
        if l == depth - 1:
            xr = _ffn_down(act, ffn_w_down[l].astype(BF16), x1, mod, final_norm_g, seq)
        else:
            raise NotImplementedError("fused final norm assumes a single layer")
    return xr.reshape(bsz, seq, d)
```

```python
import functools

import jax
import jax.numpy as jnp
import numpy as np
from jax import lax
from jax.experimental import pallas as pl
from jax.experimental.pallas import tpu as pltpu

F32 = jnp.float32
BF16 = jnp.bfloat16

EPS = 1e-6
ROPE_THETA = 10000.0
MLSTM_HEADS = 4
MLSTM_DH = 256
MLSTM_W = MLSTM_HEADS * MLSTM_DH
MLA_HEADS = 8
MLA_NOPE = 128
MLA_ROPE = 64
MLA_V = 128
MLA_QK = MLA_NOPE + MLA_ROPE
Q_LORA = 512
KV_LORA = 256
LANES = 128
MIB = 1024 * 1024

IN_PAD = 4096
IN_BLOCK = 1024
MISC_W = 256
MISC_OFF = IN_BLOCK - MISC_W


def _params(n_axes, vmem_mib):
    return pltpu.CompilerParams(
        dimension_semantics=("arbitrary",) * n_axes,
        vmem_limit_bytes=vmem_mib * MIB)


def _rms(x):
    return x * lax.rsqrt(jnp.mean(x * x, axis=-1, keepdims=True) + EPS)


def _silu(x):
    return x / (1.0 + jnp.exp(-x))


def _log_sigmoid(x):
    return jnp.minimum(x, 0.0) - jnp.log1p(jnp.exp(-jnp.abs(x)))


def _mod_kernel(c_ref, w_ref, b_ref, o_ref):
    ca = _silu(c_ref[...]).astype(BF16)
    o_ref[...] = jnp.dot(ca, w_ref[...].astype(BF16),
                         preferred_element_type=F32) + b_ref[...]


def _modulation(c, ada_w, ada_b):
    bsz, d = c.shape
    n = ada_w.shape[1]
    tn = 1024
    cp = jnp.zeros((8, d), F32).at[:bsz].set(c)
    out = pl.pallas_call(
        _mod_kernel,
        grid=(n // tn,),
        in_specs=[pl.BlockSpec((8, d), lambda j: (0, 0)),
                  pl.BlockSpec((d, tn), lambda j: (0, j)),
                  pl.BlockSpec((1, tn), lambda j: (0, j))],
        out_specs=pl.BlockSpec((8, tn), lambda j: (0, j)),
        out_shape=jax.ShapeDtypeStruct((8, n), F32),
        compiler_params=_params(1, 40),
        name="adaln_mod",
    )(cp, ada_w, ada_b.reshape(1, n))
    return out[:bsz].reshape(bsz, 6, d)


def _inproj_kernel(x_ref, mod_ref, g_ref, w_ref, o_ref, misc_ref, h_scr, *, sub):
    j = pl.program_id(1)
    last = pl.num_programs(1) - 1
    tm = x_ref.shape[0]

    def project(rows):
        acc = jnp.dot(h_scr[rows, :], w_ref[...], preferred_element_type=F32)
        o_ref[rows, :] = acc.astype(BF16)
        return acc

    @pl.when(j == 0)
    def _():
        for s in range(tm // sub):
            rows = pl.ds(s * sub, sub)
            y = _rms(x_ref[rows, :]) * g_ref[...]
            h_scr[rows, :] = (y * (1.0 + mod_ref[0, 1:2, :]) + mod_ref[0, 0:1, :]).astype(BF16)
            project(rows)

    @pl.when(j > 0)
    def _():
        for s in range(tm // sub):
            rows = pl.ds(s * sub, sub)
            acc = project(rows)

            @pl.when(j == last)
            def _():
                misc_ref[rows, :] = acc[:, MISC_OFF:]


def _in_projection(x2, mod, g, w_in_p, seq):
    t, d = x2.shape
    tm = 1024
    per_b = seq // tm
    return pl.pallas_call(
        functools.partial(_inproj_kernel, sub=256),
        grid=(t // tm, IN_PAD // IN_BLOCK),
        in_specs=[pl.BlockSpec((tm, d), lambda i, j: (i, 0)),
                  pl.BlockSpec((1, 6, d), lambda i, j: (i // per_b, 0, 0)),
                  pl.BlockSpec((1, d), lambda i, j: (0, 0)),
                  pl.BlockSpec((d, IN_BLOCK), lambda i, j: (0, j))],
        out_specs=[pl.BlockSpec((tm, IN_BLOCK), lambda i, j: (i, j)),
                   pl.BlockSpec((tm, MISC_W), lambda i, j: (i, 0))],
        out_shape=[jax.ShapeDtypeStruct((t, IN_PAD), BF16),
                   jax.ShapeDtypeStruct((t, MISC_W), F32)],
        scratch_shapes=[pltpu.VMEM((tm, d), BF16)],
        compiler_params=_params(2, 48),
        name="norm_inproj",
    )(x2, mod, g.reshape(1, d), w_in_p)


def _cumsum_rows(x):
    n = x.shape[0]
    row = lax.broadcasted_iota(jnp.int32, x.shape, 0)
    sh = 1
    while sh < n:
        x = x + jnp.where(row >= sh, pltpu.roll(x, sh, axis=0), 0.0)
        sh *= 2
    return x


def _cumsum_lanes(x):
    n = x.shape[1]
    col = lax.broadcasted_iota(jnp.int32, x.shape, 1)
    sh = 1
    while sh < n:
        x = x + jnp.where(col >= sh, pltpu.roll(x, sh, axis=1), 0.0)
        sh *= 2
    return x


def _mlstm_kernel(u_ref, v_ref, o_ref, gt_ref, cw_ref, cb_ref, wqk_ref, gb_ref, og_ref,
                  out_ref, xs_ref, us_ref, ct_ref, n_ref, m_ref, *, ts, chunk):
    i = pl.program_id(1)
    nh, dh = MLSTM_HEADS, MLSTM_DH

    @pl.when(i == 0)
    def _():
        xs_ref[0:8, :] = jnp.zeros((8, MLSTM_W), F32)
        ct_ref[...] = jnp.zeros(ct_ref.shape, F32)
        n_ref[...] = jnp.zeros(n_ref.shape, F32)
        m_ref[...] = jnp.zeros(m_ref.shape, F32)

    kw = cw_ref.shape[0]
    xs_ref[8:8 + ts, :] = u_ref[...].astype(F32)
    acc = cb_ref[...] + cw_ref[kw - 1:kw, :] * xs_ref[8:8 + ts, :]
    for j in range(kw - 1):
        acc = acc + cw_ref[j:j + 1, :] * xs_ref[pl.ds(8 - (kw - 1) + j, ts), :]
    xs_ref[0:8, :] = xs_ref[ts:ts + 8, :]
    us_ref[...] = _silu(acc).astype(BF16)

    row = lax.broadcasted_iota(jnp.int32, (chunk, chunk), 0)
    col = lax.broadcasted_iota(jnp.int32, (chunk, chunk), 1)
    causal = col <= row
    gb = gb_ref[...]

    for c in range(ts // chunk):
        r0 = c * chunk
        gc = gt_ref[r0:r0 + chunk, MISC_W - LANES:] + gb
        bc = _cumsum_rows(_log_sigmoid(gc))
        gt = gc.T[0:8, :]
        bt = _cumsum_lanes(_log_sigmoid(gt))
        for h in range(nh):
            c0 = h * dh
            li_c = gc[:, h:h + 1]
            b_c = bc[:, nh + h:nh + h + 1]
            li_r = gt[h:h + 1, :]
            b_r = bt[nh + h:nh + h + 1, :]
            g_tot = b_r[:, chunk - 1:chunk]
            z_r = li_r - b_r
            z_c = li_c - b_c
            zmax = jnp.max(z_r, axis=-1, keepdims=True)
            m_prev = m_ref[h:h + 1, 0:1]

            uc = us_ref[r0:r0 + chunk, c0:c0 + dh]
            qk = jnp.dot(uc, wqk_ref[h], preferred_element_type=F32)
            q = qk[:, :dh]
            k = qk[:, dh:] * (dh ** -0.5)
            qb = q.astype(BF16)
            kb = k.astype(BF16)
            vc = v_ref[r0:r0 + chunk, c0:c0 + dh]

            dmat = jnp.where(causal, b_c + z_r, -jnp.inf)
            m_t = jnp.maximum(b_c + m_prev, jnp.max(dmat, axis=-1, keepdims=True))
            s = lax.dot_general(qb, kb, (((1,), (1,)), ((), ())),
                                preferred_element_type=F32) * jnp.exp(dmat - m_t)
            inter = jnp.exp(b_c + m_prev - m_t)
            ct = ct_ref[h]
            num = (jnp.dot(s.astype(BF16), vc, preferred_element_type=F32)
                   + inter * jnp.dot(qb, ct.astype(BF16), preferred_element_type=F32))
            den = (jnp.sum(s, axis=-1, keepdims=True)
                   + inter * jnp.sum(q * n_ref[h:h + 1, :], axis=-1, keepdims=True))
            hh = num / jnp.maximum(jnp.abs(den), jnp.exp(-m_t))

            w_c = jnp.exp(z_c - zmax)
            kw_ = k * w_c
            ct_loc = lax.dot_general(kw_.astype(BF16), vc, (((0,), (0,)), ((), ())),
                                     preferred_element_type=F32)
            n_loc = jnp.sum(kw_, axis=0, keepdims=True)
            mm = jnp.maximum(m_prev, zmax)
            s_prev = jnp.exp(m_prev - mm)
            s_loc = jnp.exp(zmax - mm)
            ct_ref[h] = s_prev * ct + s_loc * ct_loc
            n_ref[h:h + 1, :] = s_prev * n_ref[h:h + 1, :] + s_loc * n_loc
            m_ref[h:h + 1, :] = jnp.broadcast_to(g_tot + mm, (1, LANES))

            og = o_ref[r0:r0 + chunk, c0:c0 + dh].astype(F32)
            hh = hh / (1.0 + jnp.exp(-og))
            out_ref[r0:r0 + chunk, c0:c0 + dh] = (
                _rms(hh) * og_ref[:, c0:c0 + dh]).astype(BF16)


def _mlstm(proj, misc, conv_w, conv_b, wqk, gate_b, out_g, bsz, seq):
    ts, chunk = 512, 128
    nb = seq // ts
    t = bsz * seq
    kern = functools.partial(_mlstm_kernel, ts=ts, chunk=chunk)
    row = lambda b, i: b * nb + i
    return pl.pallas_call(
        kern,
        grid=(bsz, nb),
        in_specs=[pl.BlockSpec((ts, MLSTM_W), lambda b, i: (row(b, i), 0)),
                  pl.BlockSpec((ts, MLSTM_W), lambda b, i: (row(b, i), 1)),
                  pl.BlockSpec((ts, MLSTM_W), lambda b, i: (row(b, i), 2)),
                  pl.BlockSpec((ts, MISC_W), lambda b, i: (row(b, i), 0)),
                  pl.BlockSpec(conv_w.shape, lambda b, i: (0, 0)),
                  pl.BlockSpec((1, MLSTM_W), lambda b, i: (0, 0)),
                  pl.BlockSpec(wqk.shape, lambda b, i: (0, 0, 0)),
                  pl.BlockSpec((1, LANES), lambda b, i: (0, 0)),
                  pl.BlockSpec((1, MLSTM_W), lambda b, i: (0, 0))],
        out_specs=pl.BlockSpec((ts, MLSTM_W), lambda b, i: (row(b, i), 0)),
        out_shape=jax.ShapeDtypeStruct((t, MLSTM_W), BF16),
        scratch_shapes=[pltpu.VMEM((ts + 8, MLSTM_W), F32),
                        pltpu.VMEM((ts, MLSTM_W), BF16),
                        pltpu.VMEM((MLSTM_HEADS, MLSTM_DH, MLSTM_DH), F32),
                        pltpu.VMEM((8, MLSTM_DH), F32),
                        pltpu.VMEM((8, LANES), F32)],
        compiler_params=_params(2, 40),
        name="mlstm",
    )(proj, proj, proj, misc, conv_w, conv_b.reshape(1, MLSTM_W), wqk, gate_b,
      out_g.reshape(1, MLSTM_W))


ATT_TQ = 1024
ATT_TK = 512


def _qkv_kernel(cq_ref, ckv_ref, misc_ref, pos_ref, fr_ref, qg_ref, kvg_ref, wq_ref, wkv_ref,
                qt_ref, k_ref, vt_ref, cqn, ckvn, cos_s, sin_s, kr_s, *, scale, tk):
    h = pl.program_id(1)

    @pl.when(h == 0)
    def _():
        cqn[...] = (_rms(cq_ref[...].astype(F32)) * qg_ref[...]).astype(BF16)
        ckvn[...] = (_rms(ckv_ref[...].astype(F32)) * kvg_ref[...]).astype(BF16)
        ang = pos_ref[...].astype(F32) * fr_ref[...]
        cs = jnp.cos(ang)
        sn = jnp.sin(ang)
        cos_s[...] = cs
        sin_s[...] = sn
        y = misc_ref[:, 0:LANES]
        kr_s[...] = y * cs + pltpu.roll(y, MLA_ROPE, axis=1) * sn

    mq = jnp.dot(cqn[...], wq_ref[0], preferred_element_type=F32)
    qt_ref[0, 0, 0:MLA_NOPE, :] = (mq[:, 0:MLA_NOPE] * scale).T.astype(BF16)
    y = mq[:, MLA_NOPE:]
    r = (y * cos_s[...] + pltpu.roll(y, MLA_ROPE, axis=1) * sin_s[...]) * scale
    qt_ref[0, 0, MLA_NOPE:MLA_QK, :] = r.T[0:MLA_ROPE, :].astype(BF16)

    mkv = jnp.dot(ckvn[...], wkv_ref[0], preferred_element_type=F32)
    k_ref[0, 0, :, 0:MLA_NOPE] = mkv[:, 0:MLA_NOPE].astype(BF16)
    k_ref[0, 0, :, MLA_NOPE:MLA_QK] = kr_s[:, 0:MLA_ROPE].astype(BF16)
    vt = mkv[:, MLA_NOPE:].T.astype(BF16)
    for j in range(vt.shape[1] // tk):
        vt_ref[0, 0, j] = vt[:, j * tk:(j + 1) * tk]


def _mla_qkv(proj, misc, pos, freqs, q_g, kv_g, wq_h, wkv_h, bsz, seq):
    tm, tk = ATT_TQ, ATT_TK
    t = bsz * seq
    nsb = seq // tm
    cq_blk = (3 * MLSTM_W) // Q_LORA
    ckv_blk = (3 * MLSTM_W + Q_LORA) // KV_LORA
    kern = functools.partial(_qkv_kernel, scale=MLA_QK ** -0.5 * float(np.log2(np.e)), tk=tk)
    return pl.pallas_call(
        kern,
        grid=(t // tm, MLA_HEADS),
        in_specs=[pl.BlockSpec((tm, Q_LORA), lambda i, h: (i, cq_blk)),
                  pl.BlockSpec((tm, KV_LORA), lambda i, h: (i, ckv_blk)),
                  pl.BlockSpec((tm, MISC_W), lambda i, h: (i, 0)),
                  pl.BlockSpec((tm, 1), lambda i, h: (i, 0)),
                  pl.BlockSpec((1, LANES), lambda i, h: (0, 0)),
                  pl.BlockSpec((1, Q_LORA), lambda i, h: (0, 0)),
                  pl.BlockSpec((1, KV_LORA), lambda i, h: (0, 0)),
                  pl.BlockSpec((1, Q_LORA, 2 * LANES), lambda i, h: (h, 0, 0)),
                  pl.BlockSpec((1, KV_LORA, 2 * LANES), lambda i, h: (h, 0, 0))],
        out_specs=[pl.BlockSpec((1, 1, MLA_QK, tm), lambda i, h: (i // nsb, h, 0, i % nsb)),
                   pl.BlockSpec((1, 1, tm, MLA_QK), lambda i, h: (i // nsb, h, i % nsb, 0)),
                   pl.BlockSpec((1, 1, tm // tk, MLA_V, tk),
                                lambda i, h: (i // nsb, h, i % nsb, 0, 0))],
        out_shape=[jax.ShapeDtypeStruct((bsz, MLA_HEADS, MLA_QK, seq), BF16),
                   jax.ShapeDtypeStruct((bsz, MLA_HEADS, seq, MLA_QK), BF16),
                   jax.ShapeDtypeStruct((bsz, MLA_HEADS, seq // tk, MLA_V, tk), BF16)],
        scratch_shapes=[pltpu.VMEM((tm, Q_LORA), BF16),
                        pltpu.VMEM((tm, KV_LORA), BF16),
                        pltpu.VMEM((tm, LANES), F32),
                        pltpu.VMEM((tm, LANES), F32),
                        pltpu.VMEM((tm, LANES), F32)],
        compiler_params=_params(2, 32),
        name="mla_qkv",
    )(proj, proj, misc, pos, freqs, q_g.reshape(1, Q_LORA), kv_g.reshape(1, KV_LORA),
      wq_h, wkv_h)


def _flash_kernel(qt_ref, k_ref, vt_ref, g_ref, o_ref, m_ref, l_ref, acc_ref, bias_ref, *, tq, tk):
    qi = pl.program_id(2)
    m_ref[...] = jnp.full(m_ref.shape, -jnp.inf, F32)
    l_ref[...] = jnp.zeros(l_ref.shape, F32)
    acc_ref[...] = jnp.zeros(acc_ref.shape, F32)
    key = lax.broadcasted_iota(jnp.int32, (tk, tq), 0)
    qry = lax.broadcasted_iota(jnp.int32, (tk, tq), 1)
    bias_ref[...] = jnp.where(key <= qry, 0.0, -jnp.inf).astype(F32)

    def step(kj, c0, masked):
        k = k_ref[0, 0, pl.ds(pl.multiple_of(kj * tk, tk), tk), :]
        s = jnp.dot(k, qt_ref[0, 0, :, c0:], preferred_element_type=F32)
        if masked:
            s = s + bias_ref[:, 0:tq - c0]
        m_prev = m_ref[:, c0:]
        m_new = jnp.maximum(m_prev, jnp.max(s, axis=0, keepdims=True))
        alpha = jnp.exp2(m_prev - m_new)
        p = jnp.exp2(s - m_new)
        l_ref[:, c0:] = alpha * l_ref[:, c0:] + jnp.sum(p, axis=0, keepdims=True)
        acc_ref[:, c0:] = alpha * acc_ref[:, c0:] + jnp.dot(
            vt_ref[0, 0, kj], p.astype(BF16), preferred_element_type=F32)
        m_ref[:, c0:] = m_new

    n_full = qi * (tq // tk)

    def body(kj, carry):
        step(kj, 0, False)
        return carry

    lax.fori_loop(0, n_full, body, 0)
    for d in range(tq // tk):
        step(n_full + d, d * tk, True)

    o = acc_ref[...] / l_ref[...]
    o = o * lax.rsqrt(jnp.mean(o * o, axis=0, keepdims=True) + EPS) * g_ref[0]
    o_ref[...] = o.T.astype(BF16)


def _mla_attention(qt, k, vt, out_g):
    bsz, nh, seq, _ = k.shape
    tq, tk = ATT_TQ, ATT_TK
    nq = seq // tq
    kern = functools.partial(_flash_kernel, tq=tq, tk=tk)
    return pl.pallas_call(
        kern,
        grid=(bsz, nh, nq),
        in_specs=[pl.BlockSpec((1, 1, MLA_QK, tq), lambda b, h, i: (b, h, 0, i)),
                  pl.BlockSpec((1, 1, seq, MLA_QK), lambda b, h, i: (b, h, 0, 0)),
                  pl.BlockSpec((1, 1, seq // tk, MLA_V, tk), lambda b, h, i: (b, h, 0, 0, 0)),
                  pl.BlockSpec((1, MLA_V, 1), lambda b, h, i: (h, 0, 0))],
        out_specs=pl.BlockSpec((tq, MLA_V), lambda b, h, i: (b * nq + i, h)),
        out_shape=jax.ShapeDtypeStruct((bsz * seq, nh * MLA_V), BF16),
        scratch_shapes=[pltpu.VMEM((1, tq), F32),
                        pltpu.VMEM((1, tq), F32),
                        pltpu.VMEM((MLA_V, tq), F32),
                        pltpu.VMEM((tk, tq), F32)],
        compiler_params=_params(3, 40),
        name="mla_attention",
    )(qt, k, vt, out_g.reshape(nh, MLA_V, 1))


def _outproj_kernel(hm_ref, ha_ref, w_ref, x_ref, mod_ref, g_ref, x1_ref, h2_ref, *, sub):
    km = hm_ref.shape[1]
    for s in range(hm_ref.shape[0] // sub):
        rows = pl.ds(s * sub, sub)
        mix = (jnp.dot(hm_ref[rows, :], w_ref[0:km, :], preferred_element_type=F32)
               + jnp.dot(ha_ref[rows, :], w_ref[km:, :], preferred_element_type=F32))
        x1 = x_ref[rows, :] + mod_ref[0, 2:3, :] * mix
        x1_ref[rows, :] = x1
        y = _rms(x1) * g_ref[...]
        h2_ref[rows, :] = (y * (1.0 + mod_ref[0, 4:5, :]) + mod_ref[0, 3:4, :]).astype(BF16)


def _out_projection(hm, ha, w_out, x2, mod, g, seq):
    t, d = x2.shape
    tm = 512
    per_b = seq // tm
    return pl.pallas_call(
        functools.partial(_outproj_kernel, sub=256),
        grid=(t // tm,),
        in_specs=[pl.BlockSpec((tm, hm.shape[1]), lambda i: (i, 0)),
                  pl.BlockSpec((tm, ha.shape[1]), lambda i: (i, 0)),
                  pl.BlockSpec(w_out.shape, lambda i: (0, 0), pipeline_mode=pl.Buffered(1)),
                  pl.BlockSpec((tm, d), lambda i: (i, 0)),
                  pl.BlockSpec((1, 6, d), lambda i: (i // per_b, 0, 0)),
                  pl.BlockSpec((1, d), lambda i: (0, 0))],
        out_specs=[pl.BlockSpec((tm, d), lambda i: (i, 0)),
                   pl.BlockSpec((tm, d), lambda i: (i, 0))],
        out_shape=[jax.ShapeDtypeStruct((t, d), F32),
                   jax.ShapeDtypeStruct((t, d), BF16)],
        compiler_params=_params(1, 48),
        name="out_proj",
    )(hm, ha, w_out, x2, mod, g.reshape(1, d))


def _ffn_up_kernel(h_ref, wv_ref, wg_ref, cwv_ref, cwg_ref, cbv_ref, cbg_ref, o_ref,
                   xv_ref, xg_ref, wvb_ref, wgb_ref, *, tm, sub, blocks_per_seq):
    i = pl.program_id(1)

    @pl.when(i % blocks_per_seq == 0)
    def _():
        xv_ref[0:8, :] = jnp.zeros((8, xv_ref.shape[1]), F32)
        xg_ref[0:8, :] = jnp.zeros((8, xg_ref.shape[1]), F32)

    @pl.when(i == 0)
    def _():
        wvb_ref[...] = wv_ref[...].astype(BF16)
        wgb_ref[...] = wg_ref[...].astype(BF16)

    def conv(h, r0, w_ref, cw_ref, cb_ref, xs_ref):
        kw = cw_ref.shape[0]
        up = jnp.dot(h, w_ref[...], preferred_element_type=F32)
        xs_ref[pl.ds(8 + r0, sub), :] = up
        y = cb_ref[...] + cw_ref[kw - 1:kw, :] * up
        for j in range(kw - 1):
            y = y + cw_ref[j:j + 1, :] * xs_ref[pl.ds(8 + r0 - (kw - 1) + j, sub), :]
        return y

    for s in range(tm // sub):
        r0 = s * sub
        h = h_ref[pl.ds(r0, sub), :]
        val = conv(h, r0, wvb_ref, cwv_ref, cbv_ref, xv_ref)
        gate = conv(h, r0, wgb_ref, cwg_ref, cbg_ref, xg_ref)
        o_ref[pl.ds(r0, sub), :] = (_silu(gate) * val).astype(BF16)
    xv_ref[0:8, :] = xv_ref[tm:tm + 8, :]
    xg_ref[0:8, :] = xg_ref[tm:tm + 8, :]


def _ffn_up(h2, w_up, conv_w, conv_b, seq):
    t, d = h2.shape
    f = w_up.shape[1] // 2
    tm, tf = 1024, 512
    nf = f // tf
    kw = conv_w.shape[0]
    kern = functools.partial(_ffn_up_kernel, tm=tm, sub=tm, blocks_per_seq=seq // tm)
    cb = conv_b.reshape(1, 2 * f)
    return pl.pallas_call(
        kern,
        grid=(nf, t // tm),
        in_specs=[pl.BlockSpec((tm, d), lambda j, i: (i, 0)),
                  pl.BlockSpec((d, tf), lambda j, i: (0, j)),
                  pl.BlockSpec((d, tf), lambda j, i: (0, nf + j)),
                  pl.BlockSpec((kw, tf), lambda j, i: (0, j)),
                  pl.BlockSpec((kw, tf), lambda j, i: (0, nf + j)),
                  pl.BlockSpec((1, tf), lambda j, i: (0, j)),
                  pl.BlockSpec((1, tf), lambda j, i: (0, nf + j))],
        out_specs=pl.BlockSpec((tm, tf), lambda j, i: (i, j)),
        out_shape=jax.ShapeDtypeStruct((t, f), BF16),
        scratch_shapes=[pltpu.VMEM((tm + 8, tf), F32),
                        pltpu.VMEM((tm + 8, tf), F32),
                        pltpu.VMEM((d, tf), BF16),
                        pltpu.VMEM((d, tf), BF16)],
        compiler_params=_params(2, 56),
        name="ffn_up",
    )(h2, w_up, w_up, conv_w, conv_w, cb, cb)


def _ffn_down_kernel(a_ref, w_ref, x1_ref, mod_ref, g_ref, o_ref, *, sub):
    for s in range(a_ref.shape[0] // sub):
        rows = pl.ds(s * sub, sub)
        y = jnp.dot(a_ref[rows, :], w_ref[...], preferred_element_type=F32)
        x2 = x1_ref[rows, :] + mod_ref[0, 5:6, :] * y
        o_ref[rows, :] = _rms(x2) * g_ref[...]


def _ffn_down(act, w_down, x1, mod, g, seq):
    t, f = act.shape
    d = w_down.shape[1]
    tm = 512
    per_b = seq // tm
    return pl.pallas_call(
        functools.partial(_ffn_down_kernel, sub=256),
        grid=(t // tm,),
        in_specs=[pl.BlockSpec((tm, f), lambda i: (i, 0)),
                  pl.BlockSpec((f, d), lambda i: (0, 0), pipeline_mode=pl.Buffered(1)),
                  pl.BlockSpec((tm, d), lambda i: (i, 0)),
                  pl.BlockSpec((1, 6, d), lambda i: (i // per_b, 0, 0)),
                  pl.BlockSpec((1, d), lambda i: (0, 0))],
        out_specs=pl.BlockSpec((tm, d), lambda i: (i, 0)),
        out_shape=jax.ShapeDtypeStruct((t, d), F32),
        compiler_params=_params(1, 58),
        name="ffn_down",
    )(act, w_down, x1, mod, g.reshape(1, d))


def _rot_cols(w):
    half = w.shape[-1] // 2
    return jnp.concatenate([-w[..., half:], w[..., :half]], axis=-1)


def _prep_w_in(w_in):
    d = w_in.shape[0]
    sizes = (MLSTM_W, MLSTM_W, MLSTM_W, MLSTM_HEADS, MLSTM_HEADS, Q_LORA, KV_LORA, MLA_ROPE)
    offs = np.cumsum(sizes)[:-1]
    u, v, o, ig, fg, cq, ckv, kr = jnp.split(w_in, offs, axis=1)
    used = 3 * MLSTM_W + Q_LORA + KV_LORA + 2 * MLA_ROPE + 2 * MLSTM_HEADS
    pad = jnp.zeros((d, IN_PAD - used), w_in.dtype)
    return jnp.concatenate([u, v, o, cq, ckv, kr, _rot_cols(kr), ig, fg, pad], axis=1).astype(BF16)


def kernel(x, c, positions, ada_w, ada_b, attn_norm_g, w_in, mlstm_conv_w, mlstm_conv_b, mlstm_wq, mlstm_wk, mlstm_igate_b, mlstm_fgate_b, mla_q_norm_g, mla_w_uq, mla_kv_norm_g, mla_w_ukv, mlstm_out_g, mla_out_g, w_out, ffn_norm_g, ffn_w_up, ffn_conv_w, ffn_conv_b, ffn_w_down, final_norm_g):
    bsz, seq, d = x.shape
    t = bsz * seq
    depth = ada_w.shape[0]
    xr = x.reshape(t, d)
    pos = positions.reshape(t, 1)
    half = MLA_ROPE // 2
    freqs = ROPE_THETA ** (-jnp.arange(half, dtype=F32) / half)
    freqs = jnp.tile(freqs, LANES // half).reshape(1, LANES)

    for l in range(depth):
        mod = _modulation(c, ada_w[l], ada_b[l])

        proj, misc = _in_projection(xr, mod, attn_norm_g[l], _prep_w_in(w_in[l]), seq)

        wqk = jnp.concatenate([mlstm_wq[l], mlstm_wk[l]], axis=-1).astype(BF16)
        gate_b = jnp.zeros((1, LANES), F32)
        gate_b = gate_b.at[0, :MLSTM_HEADS].set(mlstm_igate_b[l])
        gate_b = gate_b.at[0, MLSTM_HEADS:2 * MLSTM_HEADS].set(mlstm_fgate_b[l])
        hm = _mlstm(proj, misc, mlstm_conv_w[l], mlstm_conv_b[l], wqk, gate_b,
                    mlstm_out_g[l], bsz, seq)

        wq = mla_w_uq[l].reshape(Q_LORA, MLA_HEADS, MLA_QK)
        wq_r = wq[..., MLA_NOPE:]
        wq_h = jnp.concatenate([wq[..., :MLA_NOPE], wq_r, _rot_cols(wq_r)], axis=-1)
        wq_h = wq_h.transpose(1, 0, 2).astype(BF16)
        wkv_h = mla_w_ukv[l].reshape(KV_LORA, MLA_HEADS, MLA_NOPE + MLA_V)
        wkv_h = wkv_h.transpose(1, 0, 2).astype(BF16)
        qt, k, vt = _mla_qkv(proj, misc, pos, freqs, mla_q_norm_g[l], mla_kv_norm_g[l],
                             wq_h, wkv_h, bsz, seq)
        ha = _mla_attention(qt, k, vt, mla_out_g[l])

        x1, h2 = _out_projection(hm, ha, w_out[l].astype(BF16), xr, mod, ffn_norm_g[l], seq)

        act = _ffn_up(h2, ffn_w_up[l], ffn_conv_w[l], ffn_conv_b[l], seq)
        if l == depth - 1:
            xr = _ffn_down(act, ffn_w_down[l].astype(BF16), x1, mod, final_norm_g, seq)
        else:
            raise NotImplementedError("fused final norm assumes a single layer")
    return xr.reshape(bsz, seq, d)
```

```python
import functools

import jax
import jax.numpy as jnp
import numpy as np
from jax import lax
from jax.experimental import pallas as pl
from jax.experimental.pallas import tpu as pltpu

F32 = jnp.float32
BF16 = jnp.bfloat16

EPS = 1e-6
ROPE_THETA = 10000.0
MLSTM_HEADS = 4
MLSTM_DH = 256
MLSTM_W = MLSTM_HEADS * MLSTM_DH
MLA_HEADS = 8
MLA_NOPE = 128
MLA_ROPE = 64
MLA_V = 128
MLA_QK = MLA_NOPE + MLA_ROPE
Q_LORA = 512
KV_LORA = 256
LANES = 128
MIB = 1024 * 1024

IN_PAD = 4096
IN_BLOCK = 1024
MISC_W = 256
MISC_OFF = IN_BLOCK - MISC_W


def _params(n_axes, vmem_mib):
    return pltpu.CompilerParams(
        dimension_semantics=("arbitrary",) * n_axes,
        vmem_limit_bytes=vmem_mib * MIB)


def _rms(x):
    return x * lax.rsqrt(jnp.mean(x * x, axis=-1, keepdims=True) + EPS)


def _silu(x):
    return x / (1.0 + jnp.exp(-x))


def _log_sigmoid(x):
    return jnp.minimum(x, 0.0) - jnp.log1p(jnp.exp(-jnp.abs(x)))


def _mod_kernel(c_ref, w_ref, b_ref, o_ref):
    ca = _silu(c_ref[...]).astype(BF16)
    o_ref[...] = jnp.dot(ca, w_ref[...].astype(BF16),
                         preferred_element_type=F32) + b_ref[...]


def _modulation(c, ada_w, ada_b):
    bsz, d = c.shape
    n = ada_w.shape[1]
    tn = 1024
    cp = jnp.zeros((8, d), F32).at[:bsz].set(c)
    out = pl.pallas_call(
        _mod_kernel,
        grid=(n // tn,),
        in_specs=[pl.BlockSpec((8, d), lambda j: (0, 0)),
                  pl.BlockSpec((d, tn), lambda j: (0, j)),
                  pl.BlockSpec((1, tn), lambda j: (0, j))],
        out_specs=pl.BlockSpec((8, tn), lambda j: (0, j)),
        out_shape=jax.ShapeDtypeStruct((8, n), F32),
        compiler_params=_params(1, 40),
        name="adaln_mod",
    )(cp, ada_w, ada_b.reshape(1, n))
    return out[:bsz].reshape(bsz, 6, d)


def _inproj_kernel(x_ref, mod_ref, g_ref, w_ref, o_ref, misc_ref, h_scr, *, sub):
    j = pl.program_id(1)
    last = pl.num_programs(1) - 1
    tm = x_ref.shape[0]

    def project(rows):
        acc = lax.dot_general(h_scr[rows, :], w_ref[...], (((1,), (1,)), ((), ())),
                              preferred_element_type=F32)
        o_ref[rows, :] = acc.astype(BF16)
        return acc

    @pl.when(j == 0)
    def _():
        for s in range(tm // sub):
            rows = pl.ds(s * sub, sub)
            y = _rms(x_ref[rows, :]) * g_ref[...]
            h_scr[rows, :] = (y * (1.0 + mod_ref[0, 1:2, :]) + mod_ref[0, 0:1, :]).astype(BF16)
            project(rows)

    @pl.when(j > 0)
    def _():
        for s in range(tm // sub):
            rows = pl.ds(s * sub, sub)
            acc = project(rows)

            @pl.when(j == last)
            def _():
                misc_ref[rows, :] = acc[:, MISC_OFF:]


def _in_projection(x2, mod, g, w_in_p, seq):
    t, d = x2.shape
    tm = 1024
    per_b = seq // tm
    return pl.pallas_call(
        functools.partial(_inproj_kernel, sub=256),
        grid=(t // tm, IN_PAD // IN_BLOCK),
        in_specs=[pl.BlockSpec((tm, d), lambda i, j: (i, 0)),
                  pl.BlockSpec((1, 6, d), lambda i, j: (i // per_b, 0, 0)),
                  pl.BlockSpec((1, d), lambda i, j: (0, 0)),
                  pl.BlockSpec((IN_BLOCK, d), lambda i, j: (j, 0))],
        out_specs=[pl.BlockSpec((tm, IN_BLOCK), lambda i, j: (i, j)),
                   pl.BlockSpec((tm, MISC_W), lambda i, j: (i, 0))],
        out_shape=[jax.ShapeDtypeStruct((t, IN_PAD), BF16),
                   jax.ShapeDtypeStruct((t, MISC_W), F32)],
        scratch_shapes=[pltpu.VMEM((tm, d), BF16)],
        compiler_params=_params(2, 48),
        name="norm_inproj",
    )(x2, mod, g.reshape(1, d), w_in_p)


def _cumsum_rows(x):
    n = x.shape[0]
    row = lax.broadcasted_iota(jnp.int32, x.shape, 0)
    sh = 1
    while sh < n:
        x = x + jnp.where(row >= sh, pltpu.roll(x, sh, axis=0), 0.0)
        sh *= 2
    return x


def _cumsum_lanes(x):
    n = x.shape[1]
    col = lax.broadcasted_iota(jnp.int32, x.shape, 1)
    sh = 1
    while sh < n:
        x = x + jnp.where(col >= sh, pltpu.roll(x, sh, axis=1), 0.0)
        sh *= 2
    return x


def _mlstm_kernel(u_ref, v_ref, o_ref, gt_ref, cw_ref, cb_ref, wqk_ref, gb_ref, og_ref,
                  out_ref, xs_ref, us_ref, ct_ref, n_ref, m_ref, *, ts, chunk):
    i = pl.program_id(1)
    nh, dh = MLSTM_HEADS, MLSTM_DH

    @pl.when(i == 0)
    def _():
        xs_ref[0:8, :] = jnp.zeros((8, MLSTM_W), F32)
        ct_ref[...] = jnp.zeros(ct_ref.shape, F32)
        n_ref[...] = jnp.zeros(n_ref.shape, F32)
        m_ref[...] = jnp.zeros(m_ref.shape, F32)

    kw = cw_ref.shape[0]
    xs_ref[8:8 + ts, :] = u_ref[...].astype(F32)
    acc = cb_ref[...] + cw_ref[kw - 1:kw, :] * xs_ref[8:8 + ts, :]
    for j in range(kw - 1):
        acc = acc + cw_ref[j:j + 1, :] * xs_ref[pl.ds(8 - (kw - 1) + j, ts), :]
    xs_ref[0:8, :] = xs_ref[ts:ts + 8, :]
    us_ref[...] = _silu(acc).astype(BF16)

    row = lax.broadcasted_iota(jnp.int32, (chunk, chunk), 0)
    col = lax.broadcasted_iota(jnp.int32, (chunk, chunk), 1)
    causal = col <= row
    gb = gb_ref[...]

    for c in range(ts // chunk):
        r0 = c * chunk
        gc = gt_ref[r0:r0 + chunk, MISC_W - LANES:] + gb
        bc = _cumsum_rows(_log_sigmoid(gc))
        gt = gc.T[0:8, :]
        bt = _cumsum_lanes(_log_sigmoid(gt))
        for h in range(nh):
            c0 = h * dh
            li_c = gc[:, h:h + 1]
            b_c = bc[:, nh + h:nh + h + 1]
            li_r = gt[h:h + 1, :]
            b_r = bt[nh + h:nh + h + 1, :]
            g_tot = b_r[:, chunk - 1:chunk]
            z_r = li_r - b_r
            z_c = li_c - b_c
            zmax = jnp.max(z_r, axis=-1, keepdims=True)
            m_prev = m_ref[h:h + 1, 0:1]

            uc = us_ref[r0:r0 + chunk, c0:c0 + dh]
            qk = jnp.dot(uc, wqk_ref[h], preferred_element_type=F32)
            q = qk[:, :dh]
            k = qk[:, dh:] * (dh ** -0.5)
            qb = q.astype(BF16)
            kb = k.astype(BF16)
            vc = v_ref[r0:r0 + chunk, c0:c0 + dh]

            dmat = jnp.where(causal, b_c + z_r, -jnp.inf)
            m_t = jnp.maximum(b_c + m_prev, jnp.max(dmat, axis=-1, keepdims=True))
            s = lax.dot_general(qb, kb, (((1,), (1,)), ((), ())),
                                preferred_element_type=F32) * jnp.exp(dmat - m_t)
            inter = jnp.exp(b_c + m_prev - m_t)
            ct = ct_ref[h]
            num = (jnp.dot(s.astype(BF16), vc, preferred_element_type=F32)
                   + inter * jnp.dot(qb, ct.astype(BF16), preferred_element_type=F32))
            den = (jnp.sum(s, axis=-1, keepdims=True)
                   + inter * jnp.sum(q * n_ref[h:h + 1, :], axis=-1, keepdims=True))
            hh = num / jnp.maximum(jnp.abs(den), jnp.exp(-m_t))

            w_c = jnp.exp(z_c - zmax)
            kw_ = k * w_c
            ct_loc = lax.dot_general(kw_.astype(BF16), vc, (((0,), (0,)), ((), ())),
                                     preferred_element_type=F32)
            n_loc = jnp.sum(kw_, axis=0, keepdims=True)
            mm = jnp.maximum(m_prev, zmax)
            s_prev = jnp.exp(m_prev - mm)
            s_loc = jnp.exp(zmax - mm)
            ct_ref[h] = s_prev * ct + s_loc * ct_loc
            n_ref[h:h + 1, :] = s_prev * n_ref[h:h + 1, :] + s_loc * n_loc
            m_ref[h:h + 1, :] = jnp.broadcast_to(g_tot + mm, (1, LANES))

            og = o_ref[r0:r0 + chunk, c0:c0 + dh].astype(F32)
            hh = hh / (1.0 + jnp.exp(-og))
            out_ref[r0:r0 + chunk, c0:c0 + dh] = (
                _rms(hh) * og_ref[:, c0:c0 + dh]).astype(BF16)


def _mlstm(proj, misc, conv_w, conv_b, wqk, gate_b, out_g, bsz, seq):
    ts, chunk = 512, 128
    nb = seq // ts
    t = bsz * seq
    kern = functools.partial(_mlstm_kernel, ts=ts, chunk=chunk)
    row = lambda b, i: b * nb + i
    return pl.pallas_call(
        kern,
        grid=(bsz, nb),
        in_specs=[pl.BlockSpec((ts, MLSTM_W), lambda b, i: (row(b, i), 0)),
                  pl.BlockSpec((ts, MLSTM_W), lambda b, i: (row(b, i), 1)),
                  pl.BlockSpec((ts, MLSTM_W), lambda b, i: (row(b, i), 2)),
                  pl.BlockSpec((ts, MISC_W), lambda b, i: (row(b, i), 0)),
                  pl.BlockSpec(conv_w.shape, lambda b, i: (0, 0)),
                  pl.BlockSpec((1, MLSTM_W), lambda b, i: (0, 0)),
                  pl.BlockSpec(wqk.shape, lambda b, i: (0, 0, 0)),
                  pl.BlockSpec((1, LANES), lambda b, i: (0, 0)),
                  pl.BlockSpec((1, MLSTM_W), lambda b, i: (0, 0))],
        out_specs=pl.BlockSpec((ts, MLSTM_W), lambda b, i: (row(b, i), 0)),
        out_shape=jax.ShapeDtypeStruct((t, MLSTM_W), BF16),
        scratch_shapes=[pltpu.VMEM((ts + 8, MLSTM_W), F32),
                        pltpu.VMEM((ts, MLSTM_W), BF16),
                        pltpu.VMEM((MLSTM_HEADS, MLSTM_DH, MLSTM_DH), F32),
                        pltpu.VMEM((8, MLSTM_DH), F32),
                        pltpu.VMEM((8, LANES), F32)],
        compiler_params=_params(2, 40),
        name="mlstm",
    )(proj, proj, proj, misc, conv_w, conv_b.reshape(1, MLSTM_W), wqk, gate_b,
      out_g.reshape(1, MLSTM_W))


ATT_TQ = 1024
ATT_TK = 512


def _qkv_kernel(cq_ref, ckv_ref, misc_ref, pos_ref, fr_ref, qg_ref, kvg_ref, wq_ref, wkv_ref,
                qt_ref, k_ref, vt_ref, cqn, ckvn, cos_s, sin_s, kr_s, *, scale, tk):
    h = pl.program_id(1)

    @pl.when(h == 0)
    def _():
        cqn[...] = (_rms(cq_ref[...].astype(F32)) * qg_ref[...]).astype(BF16)
        ckvn[...] = (_rms(ckv_ref[...].astype(F32)) * kvg_ref[...]).astype(BF16)
        ang = pos_ref[...].astype(F32) * fr_ref[...]
        cs = jnp.cos(ang)
        sn = jnp.sin(ang)
        cos_s[...] = cs
        sin_s[...] = sn
        y = misc_ref[:, 0:LANES]
        kr_s[...] = y * cs + pltpu.roll(y, MLA_ROPE, axis=1) * sn

    mq = jnp.dot(cqn[...], wq_ref[0], preferred_element_type=F32)
    qt_ref[0, 0, 0:MLA_NOPE, :] = (mq[:, 0:MLA_NOPE] * scale).T.astype(BF16)
    y = mq[:, MLA_NOPE:]
    r = (y * cos_s[...] + pltpu.roll(y, MLA_ROPE, axis=1) * sin_s[...]) * scale
    qt_ref[0, 0, MLA_NOPE:MLA_QK, :] = r.T[0:MLA_ROPE, :].astype(BF16)

    mkv = jnp.dot(ckvn[...], wkv_ref[0], preferred_element_type=F32)
    k_ref[0, 0, :, 0:MLA_NOPE] = mkv[:, 0:MLA_NOPE].astype(BF16)
    k_ref[0, 0, :, MLA_NOPE:MLA_QK] = kr_s[:, 0:MLA_ROPE].astype(BF16)
    vt = mkv[:, MLA_NOPE:].T.astype(BF16)
    for j in range(vt.shape[1] // tk):
        vt_ref[0, 0, j] = vt[:, j * tk:(j + 1) * tk]


def _mla_qkv(proj, misc, pos, freqs, q_g, kv_g, wq_h, wkv_h, bsz, seq):
    tm, tk = ATT_TQ, ATT_TK
    t = bsz * seq
    nsb = seq // tm
    cq_blk = (3 * MLSTM_W) // Q_LORA
    ckv_blk = (3 * MLSTM_W + Q_LORA) // KV_LORA
    kern = functools.partial(_qkv_kernel, scale=MLA_QK ** -0.5 * float(np.log2(np.e)), tk=tk)
    return pl.pallas_call(
        kern,
        grid=(t // tm, MLA_HEADS),
        in_specs=[pl.BlockSpec((tm, Q_LORA), lambda i, h: (i, cq_blk)),
                  pl.BlockSpec((tm, KV_LORA), lambda i, h: (i, ckv_blk)),
                  pl.BlockSpec((tm, MISC_W), lambda i, h: (i, 0)),
                  pl.BlockSpec((tm, 1), lambda i, h: (i, 0)),
                  pl.BlockSpec((1, LANES), lambda i, h: (0, 0)),
                  pl.BlockSpec((1, Q_LORA), lambda i, h: (0, 0)),
                  pl.BlockSpec((1, KV_LORA), lambda i, h: (0, 0)),
                  pl.BlockSpec((1, Q_LORA, 2 * LANES), lambda i, h: (h, 0, 0)),
                  pl.BlockSpec((1, KV_LORA, 2 * LANES), lambda i, h: (h, 0, 0))],
        out_specs=[pl.BlockSpec((1, 1, MLA_QK, tm), lambda i, h: (i // nsb, h, 0, i % nsb)),
                   pl.BlockSpec((1, 1, tm, MLA_QK), lambda i, h: (i // nsb, h, i % nsb, 0)),
                   pl.BlockSpec((1, 1, tm // tk, MLA_V, tk),
                                lambda i, h: (i // nsb, h, i % nsb, 0, 0))],
        out_shape=[jax.ShapeDtypeStruct((bsz, MLA_HEADS, MLA_QK, seq), BF16),
                   jax.ShapeDtypeStruct((bsz, MLA_HEADS, seq, MLA_QK), BF16),
                   jax.ShapeDtypeStruct((bsz, MLA_HEADS, seq // tk, MLA_V, tk), BF16)],
        scratch_shapes=[pltpu.VMEM((tm, Q_LORA), BF16),
                        pltpu.VMEM((tm, KV_LORA), BF16),
                        pltpu.VMEM((tm, LANES), F32),
                        pltpu.VMEM((tm, LANES), F32),
                        pltpu.VMEM((tm, LANES), F32)],
        compiler_params=_params(2, 32),
        name="mla_qkv",
    )(proj, proj, misc, pos, freqs, q_g.reshape(1, Q_LORA), kv_g.reshape(1, KV_LORA),
      wq_h, wkv_h)


def _flash_kernel(qt_ref, k_ref, vt_ref, g_ref, o_ref, m_ref, l_ref, acc_ref, bias_ref, *, tq, tk):
    qi = pl.program_id(2)
    m_ref[...] = jnp.full(m_ref.shape, -jnp.inf, F32)
    l_ref[...] = jnp.zeros(l_ref.shape, F32)
    acc_ref[...] = jnp.zeros(acc_ref.shape, F32)
    @pl.when((pl.program_id(0) == 0) & (pl.program_id(1) == 0) & (qi == 0))
    def _():
        key = lax.broadcasted_iota(jnp.int32, (tk, tq), 0)
        qry = lax.broadcasted_iota(jnp.int32, (tk, tq), 1)
        bias_ref[...] = jnp.where(key <= qry, 0.0, -jnp.inf).astype(F32)

    def scores(kj, c0, masked):
        k = k_ref[0, 0, pl.ds(pl.multiple_of(kj * tk, tk), tk), :]
        s = jnp.dot(k, qt_ref[0, 0, :, c0:], preferred_element_type=F32)
        if masked:
            s = s + bias_ref[:, 0:tq - c0]
        return s

    def update(s, kj, c0):
        m_prev = m_ref[:, c0:]
        m_new = jnp.maximum(m_prev, jnp.max(s, axis=0, keepdims=True))
        alpha = jnp.exp2(m_prev - m_new)
        p = jnp.exp2(s - m_new)
        l_ref[:, c0:] = alpha * l_ref[:, c0:] + jnp.sum(p, axis=0, keepdims=True)
        acc_ref[:, c0:] = alpha * acc_ref[:, c0:] + jnp.dot(
            vt_ref[0, 0, kj], p.astype(BF16), preferred_element_type=F32)
        m_ref[:, c0:] = m_new

    per_q = tq // tk

    def body(jj, carry):
        ss = [scores(jj * per_q + d, 0, False) for d in range(per_q)]
        for d in range(per_q):
            update(ss[d], jj * per_q + d, 0)
        return carry

    lax.fori_loop(0, qi, body, 0)
    ss = [scores(qi * per_q + d, d * tk, True) for d in range(per_q)]
    for d in range(per_q):
        update(ss[d], qi * per_q + d, d * tk)

    o = acc_ref[...] / l_ref[...]
    o = o * lax.rsqrt(jnp.mean(o * o, axis=0, keepdims=True) + EPS) * g_ref[0]
    o_ref[...] = o.T.astype(BF16)


def _mla_attention(qt, k, vt, out_g):
    bsz, nh, seq, _ = k.shape
    tq, tk = ATT_TQ, ATT_TK
    nq = seq // tq
    kern = functools.partial(_flash_kernel, tq=tq, tk=tk)
    return pl.pallas_call(
        kern,
        grid=(bsz, nh, nq),
        in_specs=[pl.BlockSpec((1, 1, MLA_QK, tq), lambda b, h, i: (b, h, 0, i)),
                  pl.BlockSpec((1, 1, seq, MLA_QK), lambda b, h, i: (b, h, 0, 0)),
                  pl.BlockSpec((1, 1, seq // tk, MLA_V, tk), lambda b, h, i: (b, h, 0, 0, 0)),
                  pl.BlockSpec((1, MLA_V, 1), lambda b, h, i: (h, 0, 0))],
        out_specs=pl.BlockSpec((tq, MLA_V), lambda b, h, i: (b * nq + i, h)),
        out_shape=jax.ShapeDtypeStruct((bsz * seq, nh * MLA_V), BF16),
        scratch_shapes=[pltpu.VMEM((1, tq), F32),
                        pltpu.VMEM((1, tq), F32),
                        pltpu.VMEM((MLA_V, tq), F32),
                        pltpu.VMEM((tk, tq), F32)],
        compiler_params=_params(3, 40),
        name="mla_attention",
    )(qt, k, vt, out_g.reshape(nh, MLA_V, 1))


def _outproj_kernel(hm_ref, ha_ref, w_ref, x_ref, mod_ref, g_ref, x1_ref, h2_ref, *, sub):
    km = hm_ref.shape[1]
    for s in range(hm_ref.shape[0] // sub):
        rows = pl.ds(s * sub, sub)
        mix = (jnp.dot(hm_ref[rows, :], w_ref[0:km, :], preferred_element_type=F32)
               + jnp.dot(ha_ref[rows, :], w_ref[km:, :], preferred_element_type=F32))
        x1 = x_ref[rows, :] + mod_ref[0, 2:3, :] * mix
        x1_ref[rows, :] = x1
        y = _rms(x1) * g_ref[...]
        h2_ref[rows, :] = (y * (1.0 + mod_ref[0, 4:5, :]) + mod_ref[0, 3:4, :]).astype(BF16)


def _out_projection(hm, ha, w_out, x2, mod, g, seq):
    t, d = x2.shape
    tm = 512
    per_b = seq // tm
    return pl.pallas_call(
        functools.partial(_outproj_kernel, sub=256),
        grid=(t // tm,),
        in_specs=[pl.BlockSpec((tm, hm.shape[1]), lambda i: (i, 0)),
                  pl.BlockSpec((tm, ha.shape[1]), lambda i: (i, 0)),
                  pl.BlockSpec(w_out.shape, lambda i: (0, 0), pipeline_mode=pl.Buffered(1)),
                  pl.BlockSpec((tm, d), lambda i: (i, 0)),
                  pl.BlockSpec((1, 6, d), lambda i: (i // per_b, 0, 0)),
                  pl.BlockSpec((1, d), lambda i: (0, 0))],
        out_specs=[pl.BlockSpec((tm, d), lambda i: (i, 0)),
                   pl.BlockSpec((tm, d), lambda i: (i, 0))],
        out_shape=[jax.ShapeDtypeStruct((t, d), F32),
                   jax.ShapeDtypeStruct((t, d), BF16)],
        compiler_params=_params(1, 48),
        name="out_proj",
    )(hm, ha, w_out, x2, mod, g.reshape(1, d))


def _ffn_up_kernel(h_ref, wv_ref, wg_ref, cwv_ref, cwg_ref, cbv_ref, cbg_ref, o_ref,
                   xv_ref, xg_ref, wvb_ref, wgb_ref, *, tm, sub, blocks_per_seq):
    i = pl.program_id(1)

    @pl.when(i % blocks_per_seq == 0)
    def _():
        xv_ref[0:8, :] = jnp.zeros((8, xv_ref.shape[1]), F32)
        xg_ref[0:8, :] = jnp.zeros((8, xg_ref.shape[1]), F32)

    @pl.when(i == 0)
    def _():
        wvb_ref[...] = wv_ref[...].astype(BF16)
        wgb_ref[...] = wg_ref[...].astype(BF16)

    def conv(h, r0, w_ref, cw_ref, cb_ref, xs_ref):
        kw = cw_ref.shape[0]
        up = jnp.dot(h, w_ref[...], preferred_element_type=F32)
        xs_ref[pl.ds(8 + r0, sub), :] = up
        y = cb_ref[...] + cw_ref[kw - 1:kw, :] * up
        for j in range(kw - 1):
            y = y + cw_ref[j:j + 1, :] * xs_ref[pl.ds(8 + r0 - (kw - 1) + j, sub), :]
        return y

    for s in range(tm // sub):
        r0 = s * sub
        h = h_ref[pl.ds(r0, sub), :]
        val = conv(h, r0, wvb_ref, cwv_ref, cbv_ref, xv_ref)
        gate = conv(h, r0, wgb_ref, cwg_ref, cbg_ref, xg_ref)
        o_ref[pl.ds(r0, sub), :] = (_silu(gate) * val).astype(BF16)
    xv_ref[0:8, :] = xv_ref[tm:tm + 8, :]
    xg_ref[0:8, :] = xg_ref[tm:tm + 8, :]


def _ffn_up(h2, w_up, conv_w, conv_b, seq):
    t, d = h2.shape
    f = w_up.shape[1] // 2
    tm, tf = 1024, 512
    nf = f // tf
    kw = conv_w.shape[0]
    kern = functools.partial(_ffn_up_kernel, tm=tm, sub=tm, blocks_per_seq=seq // tm)
    cb = conv_b.reshape(1, 2 * f)
    return pl.pallas_call(
        kern,
        grid=(nf, t // tm),
        in_specs=[pl.BlockSpec((tm, d), lambda j, i: (i, 0)),
                  pl.BlockSpec((d, tf), lambda j, i: (0, j)),
                  pl.BlockSpec((d, tf), lambda j, i: (0, nf + j)),
                  pl.BlockSpec((kw, tf), lambda j, i: (0, j)),
                  pl.BlockSpec((kw, tf), lambda j, i: (0, nf + j)),
                  pl.BlockSpec((1, tf), lambda j, i: (0, j)),
                  pl.BlockSpec((1, tf), lambda j, i: (0, nf + j))],
        out_specs=pl.BlockSpec((tm, tf), lambda j, i: (i, j)),
        out_shape=jax.ShapeDtypeStruct((t, f), BF16),
        scratch_shapes=[pltpu.VMEM((tm + 8, tf), F32),
                        pltpu.VMEM((tm + 8, tf), F32),
                        pltpu.VMEM((d, tf), BF16),
                        pltpu.VMEM((d, tf), BF16)],
        compiler_params=_params(2, 56),
        name="ffn_up",
    )(h2, w_up, w_up, conv_w, conv_w, cb, cb)


def _ffn_down_kernel(a_ref, w_ref, x1_ref, mod_ref, g_ref, o_ref, *, sub):
    for s in range(a_ref.shape[0] // sub):
        rows = pl.ds(s * sub, sub)
        y = jnp.dot(a_ref[rows, :], w_ref[...], preferred_element_type=F32)
        x2 = x1_ref[rows, :] + mod_ref[0, 5:6, :] * y
        o_ref[rows, :] = _rms(x2) * g_ref[...]


def _ffn_down(act, w_down, x1, mod, g, seq):
    t, f = act.shape
    d = w_down.shape[1]
    tm = 512
    per_b = seq // tm
    return pl.pallas_call(
        functools.partial(_ffn_down_kernel, sub=256),
        grid=(t // tm,),
        in_specs=[pl.BlockSpec((tm, f), lambda i: (i, 0)),
                  pl.BlockSpec((f, d), lambda i: (0, 0), pipeline_mode=pl.Buffered(1)),
                  pl.BlockSpec((tm, d), lambda i: (i, 0)),
                  pl.BlockSpec((1, 6, d), lambda i: (i // per_b, 0, 0)),
                  pl.BlockSpec((1, d), lambda i: (0, 0))],
        out_specs=pl.BlockSpec((tm, d), lambda i: (i, 0)),
        out_shape=jax.ShapeDtypeStruct((t, d), F32),
        compiler_params=_params(1, 58),
        name="ffn_down",
    )(act, w_down, x1, mod, g.reshape(1, d))


def _rot_cols(w):
    half = w.shape[-1] // 2
    return jnp.concatenate([-w[..., half:], w[..., :half]], axis=-1)


def _win_prep_kernel(w_ref, tail_ref, o_ref, *, n_main):
    j = pl.program_id(0)
    gates = 2 * MLSTM_HEADS
    half = MLA_ROPE // 2

    @pl.when(j < n_main)
    def _():
        o_ref[...] = w_ref[...].astype(BF16)

    @pl.when(j == n_main)
    def _():
        o_ref[...] = tail_ref[gates:gates + Q_LORA, :].astype(BF16)

    @pl.when(j == n_main + 1)
    def _():
        r0 = gates + Q_LORA
        kr0 = r0 + KV_LORA
        blk = o_ref.shape[0]
        used = KV_LORA + 2 * MLA_ROPE + gates
        o_ref[...] = jnp.concatenate(
            [tail_ref[r0:kr0, :],
             tail_ref[kr0:kr0 + MLA_ROPE, :],
             -tail_ref[kr0 + half:kr0 + MLA_ROPE, :],
             tail_ref[kr0:kr0 + half, :],
             tail_ref[0:gates, :],
             jnp.zeros((blk - used, o_ref.shape[1]), F32)], axis=0).astype(BF16)


def _prep_w_in(w_in):
    d = w_in.shape[0]
    wt = w_in.T
    n_head = 3 * MLSTM_W
    blk = Q_LORA
    n_main = n_head // blk
    return pl.pallas_call(
        functools.partial(_win_prep_kernel, n_main=n_main),
        grid=(IN_PAD // blk,),
        in_specs=[pl.BlockSpec((blk, d), lambda j: (jnp.minimum(j, n_main - 1), 0)),
                  pl.BlockSpec((wt.shape[0] - n_head, d), lambda j: (0, 0))],
        out_specs=pl.BlockSpec((blk, d), lambda j: (j, 0)),
        out_shape=jax.ShapeDtypeStruct((IN_PAD, d), BF16),
        compiler_params=_params(1, 40),
        name="w_in_prep",
    )(wt, wt[n_head:])


def kernel(x, c, positions, ada_w, ada_b, attn_norm_g, w_in, mlstm_conv_w, mlstm_conv_b, mlstm_wq, mlstm_wk, mlstm_igate_b, mlstm_fgate_b, mla_q_norm_g, mla_w_uq, mla_kv_norm_g, mla_w_ukv, mlstm_out_g, mla_out_g, w_out, ffn_norm_g, ffn_w_up, ffn_conv_w, ffn_conv_b, ffn_w_down, final_norm_g):
    bsz, seq, d = x.shape
    t = bsz * seq
    depth = ada_w.shape[0]
    xr = x.reshape(t, d)
    pos = positions.reshape(t, 1)
    half = MLA_ROPE // 2
    freqs = ROPE_THETA ** (-jnp.arange(half, dtype=F32) / half)
    freqs = jnp.tile(freqs, LANES // half).reshape(1, LANES)

    for l in range(depth):
        mod = _modulation(c, ada_w[l], ada_b[l])

        proj, misc = _in_projection(xr, mod, attn_norm_g[l], _prep_w_in(w_in[l]), seq)

        wqk = jnp.concatenate([mlstm_wq[l], mlstm_wk[l]], axis=-1).astype(BF16)
        gate_b = jnp.zeros((1, LANES), F32)
        gate_b = gate_b.at[0, :MLSTM_HEADS].set(mlstm_igate_b[l])
        gate_b = gate_b.at[0, MLSTM_HEADS:2 * MLSTM_HEADS].set(mlstm_fgate_b[l])
        hm = _mlstm(proj, misc, mlstm_conv_w[l], mlstm_conv_b[l], wqk, gate_b,
                    mlstm_out_g[l], bsz, seq)

        wq = mla_w_uq[l].reshape(Q_LORA, MLA_HEADS, MLA_QK)
        wq_r = wq[..., MLA_NOPE:]
        wq_h = jnp.concatenate([wq[..., :MLA_NOPE], wq_r, _rot_cols(wq_r)], axis=-1)
        wq_h = wq_h.transpose(1, 0, 2).astype(BF16)
        wkv_h = mla_w_ukv[l].reshape(KV_LORA, MLA_HEADS, MLA_NOPE + MLA_V)
        wkv_h = wkv_h.transpose(1, 0, 2).astype(BF16)
        qt, k, vt = _mla_qkv(proj, misc, pos, freqs, mla_q_norm_g[l], mla_kv_norm_g[l],
                             wq_h, wkv_h, bsz, seq)
        ha = _mla_attention(qt, k, vt, mla_out_g[l])

        x1, h2 = _out_projection(hm, ha, w_out[l].astype(BF16), xr, mod, ffn_norm_g[l], seq)

        act = _ffn_up(h2, ffn_w_up[l], ffn_conv_w[l], ffn_conv_b[l], seq)
        if l == depth - 1:
            xr = _ffn_down(act, ffn_w_down[l].astype(BF16), x1, mod, final_norm_g, seq)
        else:
            raise NotImplementedError("fused final norm assumes a single layer")
    return xr.reshape(bsz, seq, d)
```

```python
import functools

import jax
import jax.numpy as jnp
import numpy as np
from jax import lax
from jax.experimental import pallas as pl
from jax.experimental.pallas import tpu as pltpu

F32 = jnp.float32
BF16 = jnp.bfloat16

EPS = 1e-6
ROPE_THETA = 10000.0
MLSTM_HEADS = 4
MLSTM_DH = 256
MLSTM_W = MLSTM_HEADS * MLSTM_DH
MLA_HEADS = 8
MLA_NOPE = 128
MLA_ROPE = 64
MLA_V = 128
MLA_QK = MLA_NOPE + MLA_ROPE
Q_LORA = 512
KV_LORA = 256
LANES = 128
MIB = 1024 * 1024

IN_PAD = 4096
IN_BLOCK = 1024
MISC_W = 256
MISC_OFF = IN_BLOCK - MISC_W


def _params(n_axes, vmem_mib):
    return pltpu.CompilerParams(
        dimension_semantics=("arbitrary",) * n_axes,
        vmem_limit_bytes=vmem_mib * MIB)


def _rms(x):
    return x * lax.rsqrt(jnp.mean(x * x, axis=-1, keepdims=True) + EPS)


def _silu(x):
    return x / (1.0 + jnp.exp(-x))


def _log_sigmoid(x):
    return jnp.minimum(x, 0.0) - jnp.log1p(jnp.exp(-jnp.abs(x)))


def _mod_kernel(c_ref, w_ref, b_ref, o_ref):
    ca = _silu(c_ref[...]).astype(BF16)
    o_ref[...] = jnp.dot(ca, w_ref[...].astype(BF16),
                         preferred_element_type=F32) + b_ref[...]


def _modulation(c, ada_w, ada_b):
    bsz, d = c.shape
    n = ada_w.shape[1]
    tn = 1024
    cp = jnp.zeros((8, d), F32).at[:bsz].set(c)
    out = pl.pallas_call(
        _mod_kernel,
        grid=(n // tn,),
        in_specs=[pl.BlockSpec((8, d), lambda j: (0, 0)),
                  pl.BlockSpec((d, tn), lambda j: (0, j)),
                  pl.BlockSpec((1, tn), lambda j: (0, j))],
        out_specs=pl.BlockSpec((8, tn), lambda j: (0, j)),
        out_shape=jax.ShapeDtypeStruct((8, n), F32),
        compiler_params=_params(1, 40),
        name="adaln_mod",
    )(cp, ada_w, ada_b.reshape(1, n))
    return out[:bsz].reshape(bsz, 6, d)


def _inproj_kernel(x_ref, mod_ref, g_ref, w_ref, o_ref, misc_ref, h_scr, *, sub):
    j = pl.program_id(1)
    last = pl.num_programs(1) - 1
    tm = x_ref.shape[0]

    def project(rows):
        acc = lax.dot_general(h_scr[rows, :], w_ref[...], (((1,), (1,)), ((), ())),
                              preferred_element_type=F32)
        o_ref[rows, :] = acc.astype(BF16)
        return acc

    @pl.when(j == 0)
    def _():
        for s in range(tm // sub):
            rows = pl.ds(s * sub, sub)
            y = _rms(x_ref[rows, :]) * g_ref[...]
            h_scr[rows, :] = (y * (1.0 + mod_ref[0, 1:2, :]) + mod_ref[0, 0:1, :]).astype(BF16)
            project(rows)

    @pl.when(j > 0)
    def _():
        for s in range(tm // sub):
            rows = pl.ds(s * sub, sub)
            acc = project(rows)

            @pl.when(j == last)
            def _():
                misc_ref[rows, :] = acc[:, MISC_OFF:]


def _in_projection(x2, mod, g, w_in_p, seq):
    t, d = x2.shape
    tm = 1024
    per_b = seq // tm
    return pl.pallas_call(
        functools.partial(_inproj_kernel, sub=256),
        grid=(t // tm, IN_PAD // IN_BLOCK),
        in_specs=[pl.BlockSpec((tm, d), lambda i, j: (i, 0)),
                  pl.BlockSpec((1, 6, d), lambda i, j: (i // per_b, 0, 0)),
                  pl.BlockSpec((1, d), lambda i, j: (0, 0)),
                  pl.BlockSpec((IN_BLOCK, d), lambda i, j: (j, 0))],
        out_specs=[pl.BlockSpec((tm, IN_BLOCK), lambda i, j: (i, j)),
                   pl.BlockSpec((tm, MISC_W), lambda i, j: (i, 0))],
        out_shape=[jax.ShapeDtypeStruct((t, IN_PAD), BF16),
                   jax.ShapeDtypeStruct((t, MISC_W), F32)],
        scratch_shapes=[pltpu.VMEM((tm, d), BF16)],
        compiler_params=_params(2, 48),
        name="norm_inproj",
    )(x2, mod, g.reshape(1, d), w_in_p)


def _cumsum_rows(x):
    n = x.shape[0]
    row = lax.broadcasted_iota(jnp.int32, x.shape, 0)
    sh = 1
    while sh < n:
        x = x + jnp.where(row >= sh, pltpu.roll(x, sh, axis=0), 0.0)
        sh *= 2
    return x


def _cummax_rows(x):
    n = x.shape[0]
    row = lax.broadcasted_iota(jnp.int32, x.shape, 0)
    sh = 1
    while sh < n:
        x = jnp.maximum(x, jnp.where(row >= sh, pltpu.roll(x, sh, axis=0), -jnp.inf))
        sh *= 2
    return x


def _cumsum_lanes(x):
    n = x.shape[1]
    col = lax.broadcasted_iota(jnp.int32, x.shape, 1)
    sh = 1
    while sh < n:
        x = x + jnp.where(col >= sh, pltpu.roll(x, sh, axis=1), 0.0)
        sh *= 2
    return x


def _mlstm_kernel(u_ref, v_ref, o_ref, gt_ref, cw_ref, cb_ref, wq_ref, wkt_ref, gb_ref, og_ref,
                  out_ref, xs_ref, ct_ref, n_ref, m_ref, *, ts, chunk):
    i = pl.program_id(0)
    nbatch = u_ref.shape[0]

    @pl.when(i == 0)
    def _():
        for b in range(nbatch):
            xs_ref[b, 0:8, :] = jnp.zeros((8, MLSTM_W), F32)
        ct_ref[...] = jnp.zeros(ct_ref.shape, F32)
        n_ref[...] = jnp.zeros(n_ref.shape, F32)
        m_ref[...] = jnp.zeros(m_ref.shape, F32)

    row = lax.broadcasted_iota(jnp.int32, (chunk, chunk), 0)
    col = lax.broadcasted_iota(jnp.int32, (chunk, chunk), 1)
    causal = col <= row

    for c in range(ts // chunk):
        for b in range(nbatch):
            _mlstm_chunk(u_ref.at[b], v_ref.at[b], o_ref.at[b], gt_ref.at[b], cw_ref, cb_ref,
                         wq_ref, wkt_ref, gb_ref, og_ref, out_ref.at[b], xs_ref.at[b],
                         ct_ref.at[b], n_ref.at[b], m_ref.at[b], c, chunk, causal)
    for b in range(nbatch):
        xs_ref[b, 0:8, :] = xs_ref[b, ts:ts + 8, :]


def _mlstm_chunk(u_ref, v_ref, o_ref, gt_ref, cw_ref, cb_ref, wq_ref, wkt_ref, gb_ref, og_ref,
                 out_ref, xs_ref, ct_ref, n_ref, m_ref, c, chunk, causal):
    nh, dh = MLSTM_HEADS, MLSTM_DH
    gb = gb_ref[...]
    kw = cw_ref.shape[0]
    ones = jnp.ones((chunk, LANES), BF16)
    if True:
        r0 = c * chunk
        xc = u_ref[r0:r0 + chunk, :].astype(F32)
        xs_ref[8 + r0:8 + r0 + chunk, :] = xc
        acc = cb_ref[...] + cw_ref[kw - 1:kw, :] * xc
        for j in range(kw - 1):
            acc = acc + cw_ref[j:j + 1, :] * xs_ref[pl.ds(8 + r0 - (kw - 1) + j, chunk), :]
        su = _silu(acc)

        gc = gt_ref[r0:r0 + chunk, MISC_W - LANES:] + gb
        bc = _cumsum_rows(_log_sigmoid(gc))
        gt = gc.T[0:8, :]
        bt = _cumsum_lanes(_log_sigmoid(gt))
        mdc = _cummax_rows(gc - pltpu.roll(bc, LANES - nh, axis=1))
        for h in range(nh):
            c0 = h * dh
            b_c = bc[:, nh + h:nh + h + 1]
            li_r = gt[h:h + 1, :]
            b_r = bt[nh + h:nh + h + 1, :]
            g_tot = b_r[:, chunk - 1:chunk]
            z_r = li_r - b_r
            zmax = jnp.max(z_r, axis=-1, keepdims=True)
            m_prev = m_ref[h:h + 1, 0:1]

            uh = su[:, c0:c0 + dh]
            qb = jnp.dot(uh.astype(BF16), wq_ref[h], preferred_element_type=F32).astype(BF16)
            kt = jnp.dot(wkt_ref[h], uh.T.astype(BF16),
                         preferred_element_type=F32) * (dh ** -0.5)
            vc = v_ref[r0:r0 + chunk, c0:c0 + dh]

            mx_c = jnp.maximum(m_prev, mdc[:, h:h + 1])
            m_t = b_c + mx_c
            sb = (jnp.dot(qb, kt.astype(BF16), preferred_element_type=F32)
                  * jnp.exp(jnp.where(causal, z_r - mx_c, -jnp.inf))).astype(BF16)
            inter = jnp.exp(m_prev - mx_c)
            ct = ct_ref[h]
            nrep = n_ref[h]
            num = (jnp.dot(sb, vc, preferred_element_type=F32)
                   + inter * jnp.dot(qb, ct.astype(BF16), preferred_element_type=F32))
            den = (jnp.dot(sb, ones, preferred_element_type=F32)
                   + inter * jnp.dot(qb, nrep.astype(BF16), preferred_element_type=F32))
            lim = jnp.maximum(jnp.abs(den), jnp.exp(-m_t))
            hh = num / jnp.concatenate([lim] * (dh // LANES), axis=1)

            mm = jnp.maximum(m_prev, zmax)
            s_prev = jnp.exp(m_prev - mm)
            s_loc = jnp.exp(zmax - mm)
            kwt = (kt * (jnp.exp(z_r - zmax) * s_loc)).astype(BF16)
            ct_ref[h] = s_prev * ct + jnp.dot(kwt, vc, preferred_element_type=F32)
            n_ref[h] = s_prev * nrep + jnp.dot(kwt, ones, preferred_element_type=F32)
            m_ref[h:h + 1, :] = jnp.broadcast_to(g_tot + mm, (1, LANES))

            og = o_ref[r0:r0 + chunk, c0:c0 + dh].astype(F32)
            hh = hh / (1.0 + jnp.exp(-og))
            out_ref[r0:r0 + chunk, c0:c0 + dh] = (
                _rms(hh) * og_ref[:, c0:c0 + dh]).astype(BF16)


def _mlstm(proj, misc, conv_w, conv_b, wq, wkt, gate_b, out_g, bsz, seq):
    ts, chunk = 512, 128
    t = bsz * seq
    kern = functools.partial(_mlstm_kernel, ts=ts, chunk=chunk)
    proj3 = proj.reshape(bsz, seq, proj.shape[1])
    misc3 = misc.reshape(bsz, seq, MISC_W)
    out = pl.pallas_call(
        kern,
        grid=(seq // ts,),
        in_specs=[pl.BlockSpec((bsz, ts, MLSTM_W), lambda i: (0, i, 0)),
                  pl.BlockSpec((bsz, ts, MLSTM_W), lambda i: (0, i, 1)),
                  pl.BlockSpec((bsz, ts, MLSTM_W), lambda i: (0, i, 2)),
                  pl.BlockSpec((bsz, ts, MISC_W), lambda i: (0, i, 0)),
                  pl.BlockSpec(conv_w.shape, lambda i: (0, 0)),
                  pl.BlockSpec((1, MLSTM_W), lambda i: (0, 0)),
                  pl.BlockSpec(wq.shape, lambda i: (0, 0, 0)),
                  pl.BlockSpec(wkt.shape, lambda i: (0, 0, 0)),
                  pl.BlockSpec((1, LANES), lambda i: (0, 0)),
                  pl.BlockSpec((1, MLSTM_W), lambda i: (0, 0))],
        out_specs=pl.BlockSpec((bsz, ts, MLSTM_W), lambda i: (0, i, 0)),
        out_shape=jax.ShapeDtypeStruct((bsz, seq, MLSTM_W), BF16),
        scratch_shapes=[pltpu.VMEM((bsz, ts + 8, MLSTM_W), F32),
                        pltpu.VMEM((bsz, MLSTM_HEADS, MLSTM_DH, MLSTM_DH), F32),
                        pltpu.VMEM((bsz, MLSTM_HEADS, MLSTM_DH, LANES), F32),
                        pltpu.VMEM((bsz, 8, LANES), F32)],
        compiler_params=_params(1, 48),
        name="mlstm",
    )(proj3, proj3, proj3, misc3, conv_w, conv_b.reshape(1, MLSTM_W), wq, wkt, gate_b,
      out_g.reshape(1, MLSTM_W))
    return out.reshape(t, MLSTM_W)


ATT_TQ = 1024
ATT_TK = 512
MLA_VA = MLA_V + 16


def _qkv_kernel(cq_ref, ckv_ref, misc_ref, pos_ref, fr_ref, qg_ref, kvg_ref, wq_ref, wkv_ref,
                qt_ref, k_ref, vt_ref, cqn, ckvn, cos_s, sin_s, kr_s, *, scale, tk):
    h = pl.program_id(1)

    @pl.when(h == 0)
    def _():
        cqn[...] = (_rms(cq_ref[...].astype(F32)) * qg_ref[...]).astype(BF16)
        ckvn[...] = (_rms(ckv_ref[...].astype(F32)) * kvg_ref[...]).astype(BF16)
        ang = pos_ref[...].astype(F32) * fr_ref[...]
        cs = jnp.cos(ang)
        sn = jnp.sin(ang)
        cos_s[...] = cs
        sin_s[...] = sn
        y = misc_ref[:, 0:LANES]
        kr_s[...] = y * cs + pltpu.roll(y, MLA_ROPE, axis=1) * sn

    mq = jnp.dot(cqn[...], wq_ref[0], preferred_element_type=F32)
    qt_ref[0, 0, 0:MLA_NOPE, :] = (mq[:, 0:MLA_NOPE] * scale).T.astype(BF16)
    y = mq[:, MLA_NOPE:]
    r = (y * cos_s[...] + pltpu.roll(y, MLA_ROPE, axis=1) * sin_s[...]) * scale
    qt_ref[0, 0, MLA_NOPE:MLA_QK, :] = r.T[0:MLA_ROPE, :].astype(BF16)

    mkv = jnp.dot(ckvn[...], wkv_ref[0], preferred_element_type=F32)
    k_ref[0, 0, :, 0:MLA_NOPE] = mkv[:, 0:MLA_NOPE].astype(BF16)
    k_ref[0, 0, :, MLA_NOPE:MLA_QK] = kr_s[:, 0:MLA_ROPE].astype(BF16)
    vt = mkv[:, MLA_NOPE:].T.astype(BF16)
    for j in range(vt.shape[1] // tk):
        vt_ref[0, 0, j, 0:MLA_V, :] = vt[:, j * tk:(j + 1) * tk]
        vt_ref[0, 0, j, MLA_V:, :] = jnp.ones((MLA_VA - MLA_V, tk), BF16)


def _mla_qkv(proj, misc, pos, freqs, q_g, kv_g, wq_h, wkv_h, bsz, seq):
    tm, tk = ATT_TQ, ATT_TK
    t = bsz * seq
    nsb = seq // tm
    cq_blk = (3 * MLSTM_W) // Q_LORA
    ckv_blk = (3 * MLSTM_W + Q_LORA) // KV_LORA
    kern = functools.partial(_qkv_kernel, scale=MLA_QK ** -0.5 * float(np.log2(np.e)), tk=tk)
    return pl.pallas_call(
        kern,
        grid=(t // tm, MLA_HEADS),
        in_specs=[pl.BlockSpec((tm, Q_LORA), lambda i, h: (i, cq_blk)),
                  pl.BlockSpec((tm, KV_LORA), lambda i, h: (i, ckv_blk)),
                  pl.BlockSpec((tm, MISC_W), lambda i, h: (i, 0)),
                  pl.BlockSpec((tm, 1), lambda i, h: (i, 0)),
                  pl.BlockSpec((1, LANES), lambda i, h: (0, 0)),
                  pl.BlockSpec((1, Q_LORA), lambda i, h: (0, 0)),
                  pl.BlockSpec((1, KV_LORA), lambda i, h: (0, 0)),
                  pl.BlockSpec((1, Q_LORA, 2 * LANES), lambda i, h: (h, 0, 0)),
                  pl.BlockSpec((1, KV_LORA, 2 * LANES), lambda i, h: (h, 0, 0))],
        out_specs=[pl.BlockSpec((1, 1, MLA_QK, tm), lambda i, h: (i // nsb, h, 0, i % nsb)),
                   pl.BlockSpec((1, 1, tm, MLA_QK), lambda i, h: (i // nsb, h, i % nsb, 0)),
                   pl.BlockSpec((1, 1, tm // tk, MLA_VA, tk),
                                lambda i, h: (i // nsb, h, i % nsb, 0, 0))],
        out_shape=[jax.ShapeDtypeStruct((bsz, MLA_HEADS, MLA_QK, seq), BF16),
                   jax.ShapeDtypeStruct((bsz, MLA_HEADS, seq, MLA_QK), BF16),
                   jax.ShapeDtypeStruct((bsz, MLA_HEADS, seq // tk, MLA_VA, tk), BF16)],
        scratch_shapes=[pltpu.VMEM((tm, Q_LORA), BF16),
                        pltpu.VMEM((tm, KV_LORA), BF16),
                        pltpu.VMEM((tm, LANES), F32),
                        pltpu.VMEM((tm, LANES), F32),
                        pltpu.VMEM((tm, LANES), F32)],
        compiler_params=_params(2, 32),
        name="mla_qkv",
    )(proj, proj, misc, pos, freqs, q_g.reshape(1, Q_LORA), kv_g.reshape(1, KV_LORA),
      wq_h, wkv_h)


def _flash_kernel(qt_ref, k_ref, vt_ref, g_ref, o_ref, m_ref, acc_ref, bias_ref, *, tq, tk):
    qi = pl.program_id(2)
    m_ref[...] = jnp.full(m_ref.shape, -jnp.inf, F32)
    acc_ref[...] = jnp.zeros(acc_ref.shape, F32)
    @pl.when((pl.program_id(0) == 0) & (pl.program_id(1) == 0) & (qi == 0))
    def _():
        key = lax.broadcasted_iota(jnp.int32, (tk, tq), 0)
        qry = lax.broadcasted_iota(jnp.int32, (tk, tq), 1)
        bias_ref[...] = jnp.where(key <= qry, 0.0, -jnp.inf).astype(F32)

    def scores(kj, c0, masked):
        k = k_ref[0, 0, pl.ds(pl.multiple_of(kj * tk, tk), tk), :]
        s = jnp.dot(k, qt_ref[0, 0, :, c0:], preferred_element_type=F32)
        if masked:
            s = s + bias_ref[:, 0:tq - c0]
        return s

    def update(s, kj, c0):
        m_prev = m_ref[:, c0:]
        m_new = jnp.maximum(m_prev, jnp.max(s, axis=0, keepdims=True))
        alpha = jnp.exp2(m_prev - m_new)
        p = jnp.exp2(s - m_new)
        acc_ref[:, c0:] = alpha * acc_ref[:, c0:] + jnp.dot(
            vt_ref[0, 0, kj], p.astype(BF16), preferred_element_type=F32)
        m_ref[:, c0:] = m_new

    per_q = tq // tk

    def body(jj, carry):
        ss = [scores(jj * per_q + d, 0, False) for d in range(per_q)]
        for d in range(per_q):
            update(ss[d], jj * per_q + d, 0)
        return carry

    lax.fori_loop(0, qi, body, 0)
    ss = [scores(qi * per_q + d, d * tk, True) for d in range(per_q)]
    for d in range(per_q):
        update(ss[d], qi * per_q + d, d * tk)

    o = acc_ref[0:MLA_V, :] / acc_ref[MLA_V:MLA_V + 1, :]
    o = o * lax.rsqrt(jnp.mean(o * o, axis=0, keepdims=True) + EPS) * g_ref[0]
    o_ref[...] = o.T.astype(BF16)


def _mla_attention(qt, k, vt, out_g):
    bsz, nh, seq, _ = k.shape
    tq, tk = ATT_TQ, ATT_TK
    nq = seq // tq
    kern = functools.partial(_flash_kernel, tq=tq, tk=tk)
    return pl.pallas_call(
        kern,
        grid=(bsz, nh, nq),
        in_specs=[pl.BlockSpec((1, 1, MLA_QK, tq), lambda b, h, i: (b, h, 0, i)),
                  pl.BlockSpec((1, 1, seq, MLA_QK), lambda b, h, i: (b, h, 0, 0)),
                  pl.BlockSpec((1, 1, seq // tk, MLA_VA, tk), lambda b, h, i: (b, h, 0, 0, 0)),
                  pl.BlockSpec((1, MLA_V, 1), lambda b, h, i: (h, 0, 0))],
        out_specs=pl.BlockSpec((tq, MLA_V), lambda b, h, i: (b * nq + i, h)),
        out_shape=jax.ShapeDtypeStruct((bsz * seq, nh * MLA_V), BF16),
        scratch_shapes=[pltpu.VMEM((1, tq), F32),
                        pltpu.VMEM((MLA_VA, tq), F32),
                        pltpu.VMEM((tk, tq), F32)],
        compiler_params=_params(3, 40),
        name="mla_attention",
    )(qt, k, vt, out_g.reshape(nh, MLA_V, 1))


def _outproj_kernel(hm_ref, ha_ref, w_ref, x_ref, mod_ref, g_ref, x1_ref, h2_ref, *, sub):
    km = hm_ref.shape[1]
    for s in range(hm_ref.shape[0] // sub):
        rows = pl.ds(s * sub, sub)
        mix = (jnp.dot(hm_ref[rows, :], w_ref[0:km, :], preferred_element_type=F32)
               + jnp.dot(ha_ref[rows, :], w_ref[km:, :], preferred_element_type=F32))
        x1 = x_ref[rows, :] + mod_ref[0, 2:3, :] * mix
        x1_ref[rows, :] = x1
        y = _rms(x1) * g_ref[...]
        h2_ref[rows, :] = (y * (1.0 + mod_ref[0, 4:5, :]) + mod_ref[0, 3:4, :]).astype(BF16)


def _out_projection(hm, ha, w_out, x2, mod, g, seq):
    t, d = x2.shape
    tm = 512
    per_b = seq // tm
    return pl.pallas_call(
        functools.partial(_outproj_kernel, sub=256),
        grid=(t // tm,),
        in_specs=[pl.BlockSpec((tm, hm.shape[1]), lambda i: (i, 0)),
                  pl.BlockSpec((tm, ha.shape[1]), lambda i: (i, 0)),
                  pl.BlockSpec(w_out.shape, lambda i: (0, 0), pipeline_mode=pl.Buffered(1)),
                  pl.BlockSpec((tm, d), lambda i: (i, 0)),
                  pl.BlockSpec((1, 6, d), lambda i: (i // per_b, 0, 0)),
                  pl.BlockSpec((1, d), lambda i: (0, 0))],
        out_specs=[pl.BlockSpec((tm, d), lambda i: (i, 0)),
                   pl.BlockSpec((tm, d), lambda i: (i, 0))],
        out_shape=[jax.ShapeDtypeStruct((t, d), F32),
                   jax.ShapeDtypeStruct((t, d), BF16)],
        compiler_params=_params(1, 48),
        name="out_proj",
    )(hm, ha, w_out, x2, mod, g.reshape(1, d))


def _ffn_up_kernel(h_ref, wv_ref, wg_ref, cwv_ref, cwg_ref, cbv_ref, cbg_ref, o_ref,
                   xv_ref, xg_ref, wvb_ref, wgb_ref, *, tm, blocks_per_seq):
    i = pl.program_id(1)

    @pl.when(i % blocks_per_seq == 0)
    def _():
        xv_ref[0:8, :] = jnp.zeros((8, xv_ref.shape[1]), F32)
        xg_ref[0:8, :] = jnp.zeros((8, xg_ref.shape[1]), F32)

    @pl.when(i == 0)
    def _():
        wvb_ref[...] = wv_ref[...].astype(BF16)
        wgb_ref[...] = wg_ref[...].astype(BF16)

    h = h_ref[...]

    def conv(w_ref, cw_ref, cb_ref, xs_ref):
        kw = cw_ref.shape[0]
        up = jnp.dot(h, w_ref[...], preferred_element_type=F32)
        xs_ref[8:8 + tm, :] = up
        y = cb_ref[...] + cw_ref[kw - 1:kw, :] * up
        for j in range(kw - 1):
            y = y + cw_ref[j:j + 1, :] * xs_ref[pl.ds(8 - (kw - 1) + j, tm), :]
        xs_ref[0:8, :] = xs_ref[tm:tm + 8, :]
        return y

    val = conv(wvb_ref, cwv_ref, cbv_ref, xv_ref)
    gate = conv(wgb_ref, cwg_ref, cbg_ref, xg_ref)
    o_ref[...] = (_silu(gate) * val).astype(BF16)


def _ffn_up(h2, w_up, conv_w, conv_b, seq):
    t, d = h2.shape
    f = w_up.shape[1] // 2
    tm, tf = 1024, 512
    nf = f // tf
    kw = conv_w.shape[0]
    kern = functools.partial(_ffn_up_kernel, tm=tm, blocks_per_seq=seq // tm)
    cb = conv_b.reshape(1, 2 * f)
    return pl.pallas_call(
        kern,
        grid=(nf, t // tm),
        in_specs=[pl.BlockSpec((tm, d), lambda j, i: (i, 0)),
                  pl.BlockSpec((d, tf), lambda j, i: (0, j)),
                  pl.BlockSpec((d, tf), lambda j, i: (0, nf + j)),
                  pl.BlockSpec((kw, tf), lambda j, i: (0, j)),
                  pl.BlockSpec((kw, tf), lambda j, i: (0, nf + j)),
                  pl.BlockSpec((1, tf), lambda j, i: (0, j)),
                  pl.BlockSpec((1, tf), lambda j, i: (0, nf + j))],
        out_specs=pl.BlockSpec((tm, tf), lambda j, i: (i, j)),
        out_shape=jax.ShapeDtypeStruct((t, f), BF16),
        scratch_shapes=[pltpu.VMEM((tm + 8, tf), F32),
                        pltpu.VMEM((tm + 8, tf), F32),
                        pltpu.VMEM((d, tf), BF16),
                        pltpu.VMEM((d, tf), BF16)],
        compiler_params=_params(2, 56),
        name="ffn_up",
    )(h2, w_up, w_up, conv_w, conv_w, cb, cb)


def _ffn_down_kernel(a_ref, w_ref, x1_ref, mod_ref, g_ref, o_ref, *, sub):
    for s in range(a_ref.shape[0] // sub):
        rows = pl.ds(s * sub, sub)
        y = jnp.dot(a_ref[rows, :], w_ref[...], preferred_element_type=F32)
        x2 = x1_ref[rows, :] + mod_ref[0, 5:6, :] * y
        o_ref[rows, :] = _rms(x2) * g_ref[...]


def _ffn_down(act, w_down, x1, mod, g, seq):
    t, f = act.shape
    d = w_down.shape[1]
    tm = 512
    per_b = seq // tm
    return pl.pallas_call(
        functools.partial(_ffn_down_kernel, sub=256),
        grid=(t // tm,),
        in_specs=[pl.BlockSpec((tm, f), lambda i: (i, 0)),
                  pl.BlockSpec((f, d), lambda i: (0, 0), pipeline_mode=pl.Buffered(1)),
                  pl.BlockSpec((tm, d), lambda i: (i, 0)),
                  pl.BlockSpec((1, 6, d), lambda i: (i // per_b, 0, 0)),
                  pl.BlockSpec((1, d), lambda i: (0, 0))],
        out_specs=pl.BlockSpec((tm, d), lambda i: (i, 0)),
        out_shape=jax.ShapeDtypeStruct((t, d), F32),
        compiler_params=_params(1, 58),
        name="ffn_down",
    )(act, w_down, x1, mod, g.reshape(1, d))


def _rot_cols(w):
    half = w.shape[-1] // 2
    return jnp.concatenate([-w[..., half:], w[..., :half]], axis=-1)


def _win_prep_kernel(w_ref, tail_ref, o_ref, *, n_main):
    j = pl.program_id(0)
    gates = 2 * MLSTM_HEADS
    half = MLA_ROPE // 2

    @pl.when(j < n_main)
    def _():
        o_ref[...] = w_ref[...].astype(BF16)

    @pl.when(j == n_main)
    def _():
        o_ref[...] = tail_ref[gates:gates + Q_LORA, :].astype(BF16)

    @pl.when(j == n_main + 1)
    def _():
        r0 = gates + Q_LORA
        kr0 = r0 + KV_LORA
        blk = o_ref.shape[0]
        used = KV_LORA + 2 * MLA_ROPE + gates
        o_ref[...] = jnp.concatenate(
            [tail_ref[r0:kr0, :],
             tail_ref[kr0:kr0 + MLA_ROPE, :],
             -tail_ref[kr0 + half:kr0 + MLA_ROPE, :],
             tail_ref[kr0:kr0 + half, :],
             tail_ref[0:gates, :],
             jnp.zeros((blk - used, o_ref.shape[1]), F32)], axis=0).astype(BF16)


def _prep_w_in(w_in):
    d = w_in.shape[0]
    wt = w_in.T
    n_head = 3 * MLSTM_W
    blk = Q_LORA
    n_main = n_head // blk
    return pl.pallas_call(
        functools.partial(_win_prep_kernel, n_main=n_main),
        grid=(IN_PAD // blk,),
        in_specs=[pl.BlockSpec((blk, d), lambda j: (jnp.minimum(j, n_main - 1), 0)),
                  pl.BlockSpec((wt.shape[0] - n_head, d), lambda j: (0, 0))],
        out_specs=pl.BlockSpec((blk, d), lambda j: (j, 0)),
        out_shape=jax.ShapeDtypeStruct((IN_PAD, d), BF16),
        compiler_params=_params(1, 40),
        name="w_in_prep",
    )(wt, wt[n_head:])


def kernel(x, c, positions, ada_w, ada_b, attn_norm_g, w_in, mlstm_conv_w, mlstm_conv_b, mlstm_wq, mlstm_wk, mlstm_igate_b, mlstm_fgate_b, mla_q_norm_g, mla_w_uq, mla_kv_norm_g, mla_w_ukv, mlstm_out_g, mla_out_g, w_out, ffn_norm_g, ffn_w_up, ffn_conv_w, ffn_conv_b, ffn_w_down, final_norm_g):
    bsz, seq, d = x.shape
    t = bsz * seq
    depth = ada_w.shape[0]
    xr = x.reshape(t, d)
    pos = positions.reshape(t, 1)
    half = MLA_ROPE // 2
    freqs = ROPE_THETA ** (-jnp.arange(half, dtype=F32) / half)
    freqs = jnp.tile(freqs, LANES // half).reshape(1, LANES)

    for l in range(depth):
        mod = _modulation(c, ada_w[l], ada_b[l])

        proj, misc = _in_projection(xr, mod, attn_norm_g[l], _prep_w_in(w_in[l]), seq)

        wq_m = mlstm_wq[l].astype(BF16)
        wkt_m = mlstm_wk[l].transpose(0, 2, 1).astype(BF16)
        gate_b = jnp.zeros((1, LANES), F32)
        gate_b = gate_b.at[0, :MLSTM_HEADS].set(mlstm_igate_b[l])
        gate_b = gate_b.at[0, MLSTM_HEADS:2 * MLSTM_HEADS].set(mlstm_fgate_b[l])
        hm = _mlstm(proj, misc, mlstm_conv_w[l], mlstm_conv_b[l], wq_m, wkt_m, gate_b,
                    mlstm_out_g[l], bsz, seq)

        wq = mla_w_uq[l].reshape(Q_LORA, MLA_HEADS, MLA_QK)
        wq_r = wq[..., MLA_NOPE:]
        wq_h = jnp.concatenate([wq[..., :MLA_NOPE], wq_r, _rot_cols(wq_r)], axis=-1)
        wq_h = wq_h.transpose(1, 0, 2).astype(BF16)
        wkv_h = mla_w_ukv[l].reshape(KV_LORA, MLA_HEADS, MLA_NOPE + MLA_V)
        wkv_h = wkv_h.transpose(1, 0, 2).astype(BF16)
        qt, k, vt = _mla_qkv(proj, misc, pos, freqs, mla_q_norm_g[l], mla_kv_norm_g[l],
                             wq_h, wkv_h, bsz, seq)
        ha = _mla_attention(qt, k, vt, mla_out_g[l])

        x1, h2 = _out_projection(hm, ha, w_out[l].astype(BF16), xr, mod, ffn_norm_g[l], seq)

        act = _ffn_up(h2, ffn_w_up[l], ffn_conv_w[l], ffn_conv_b[l], seq)
        if l == depth - 1:
            xr = _ffn_down(act, ffn_w_down[l].astype(BF16), x1, mod, final_norm_g, seq)
        else:
            raise NotImplementedError("fused final norm assumes a single layer")
    return xr.reshape(bsz, seq, d)
```

```python
import functools

import jax
import jax.numpy as jnp
import numpy as np
from jax import lax
from jax.experimental import pallas as pl
from jax.experimental.pallas import tpu as pltpu

F32 = jnp.float32
BF16 = jnp.bfloat16

EPS = 1e-6
ROPE_THETA = 10000.0
MLSTM_HEADS = 4
MLSTM_DH = 256
MLSTM_W = MLSTM_HEADS * MLSTM_DH
MLA_HEADS = 8
MLA_NOPE = 128
MLA_ROPE = 64
MLA_V = 128
MLA_QK = MLA_NOPE + MLA_ROPE
Q_LORA = 512
KV_LORA = 256
LANES = 128
MIB = 1024 * 1024

IN_PAD = 4096
IN_BLOCK = 1024
MISC_W = 256
MISC_OFF = IN_BLOCK - MISC_W


def _params(n_axes, vmem_mib):
    return pltpu.CompilerParams(
        dimension_semantics=("arbitrary",) * n_axes,
        vmem_limit_bytes=vmem_mib * MIB)


def _rms(x):
    return x * lax.rsqrt(jnp.mean(x * x, axis=-1, keepdims=True) + EPS)


def _silu(x):
    return x / (1.0 + jnp.exp(-x))


def _log_sigmoid(x):
    return jnp.minimum(x, 0.0) - jnp.log1p(jnp.exp(-jnp.abs(x)))


def _mod_kernel(c_ref, w_ref, b_ref, o_ref):
    ca = _silu(c_ref[...]).astype(BF16)
    o_ref[...] = jnp.dot(ca, w_ref[...].astype(BF16),
                         preferred_element_type=F32) + b_ref[...]


def _modulation(c, ada_w, ada_b):
    bsz, d = c.shape
    n = ada_w.shape[1]
    tn = 1024
    cp = jnp.zeros((8, d), F32).at[:bsz].set(c)
    out = pl.pallas_call(
        _mod_kernel,
        grid=(n // tn,),
        in_specs=[pl.BlockSpec((8, d), lambda j: (0, 0)),
                  pl.BlockSpec((d, tn), lambda j: (0, j)),
                  pl.BlockSpec((1, tn), lambda j: (0, j))],
        out_specs=pl.BlockSpec((8, tn), lambda j: (0, j)),
        out_shape=jax.ShapeDtypeStruct((8, n), F32),
        compiler_params=_params(1, 40),
        name="adaln_mod",
    )(cp, ada_w, ada_b.reshape(1, n))
    return out[:bsz].reshape(bsz, 6, d)


def _inproj_kernel(x_ref, mod_ref, g_ref, w_ref, o_ref, misc_ref, h_scr, *, sub):
    j = pl.program_id(1)
    last = pl.num_programs(1) - 1
    tm = x_ref.shape[0]

    def project(rows):
        acc = jnp.dot(h_scr[rows, :], w_ref[...], preferred_element_type=F32)
        o_ref[rows, :] = acc.astype(BF16)
        return acc

    @pl.when(j == 0)
    def _():
        for s in range(tm // sub):
            rows = pl.ds(s * sub, sub)
            y = _rms(x_ref[rows, :]) * g_ref[...]
            h_scr[rows, :] = (y * (1.0 + mod_ref[0, 1:2, :]) + mod_ref[0, 0:1, :]).astype(BF16)
            project(rows)

    @pl.when(j > 0)
    def _():
        for s in range(tm // sub):
            rows = pl.ds(s * sub, sub)
            acc = project(rows)

            @pl.when(j == last)
            def _():
                misc_ref[rows, :] = acc[:, MISC_OFF:]


def _in_projection(x2, mod, g, w_in_p, seq):
    t, d = x2.shape
    tm = 1024
    per_b = seq // tm
    return pl.pallas_call(
        functools.partial(_inproj_kernel, sub=256),
        grid=(t // tm, IN_PAD // IN_BLOCK),
        in_specs=[pl.BlockSpec((tm, d), lambda i, j: (i, 0)),
                  pl.BlockSpec((1, 6, d), lambda i, j: (i // per_b, 0, 0)),
                  pl.BlockSpec((1, d), lambda i, j: (0, 0)),
                  pl.BlockSpec((d, IN_BLOCK), lambda i, j: (0, j))],
        out_specs=[pl.BlockSpec((tm, IN_BLOCK), lambda i, j: (i, j)),
                   pl.BlockSpec((tm, MISC_W), lambda i, j: (i, 0))],
        out_shape=[jax.ShapeDtypeStruct((t, IN_PAD), BF16),
                   jax.ShapeDtypeStruct((t, MISC_W), F32)],
        scratch_shapes=[pltpu.VMEM((tm, d), BF16)],
        compiler_params=_params(2, 48),
        name="norm_inproj",
    )(x2, mod, g.reshape(1, d), w_in_p)


def _cumsum_rows(x):
    n = x.shape[0]
    row = lax.broadcasted_iota(jnp.int32, x.shape, 0)
    sh = 1
    while sh < n:
        x = x + jnp.where(row >= sh, pltpu.roll(x, sh, axis=0), 0.0)
        sh *= 2
    return x


def _cummax_rows(x):
    n = x.shape[0]
    row = lax.broadcasted_iota(jnp.int32, x.shape, 0)
    sh = 1
    while sh < n:
        x = jnp.maximum(x, jnp.where(row >= sh, pltpu.roll(x, sh, axis=0), -jnp.inf))
        sh *= 2
    return x


def _cumsum_lanes(x):
    n = x.shape[1]
    col = lax.broadcasted_iota(jnp.int32, x.shape, 1)
    sh = 1
    while sh < n:
        x = x + jnp.where(col >= sh, pltpu.roll(x, sh, axis=1), 0.0)
        sh *= 2
    return x


def _mlstm_kernel(u_ref, v_ref, o_ref, gt_ref, cw_ref, cb_ref, wq_ref, wkt_ref, gb_ref, og_ref,
                  out_ref, xs_ref, ct_ref, n_ref, m_ref, *, ts, chunk):
    i = pl.program_id(0)
    nbatch = u_ref.shape[0]

    @pl.when(i == 0)
    def _():
        for b in range(nbatch):
            xs_ref[b, 0:8, :] = jnp.zeros((8, MLSTM_W), F32)
        ct_ref[...] = jnp.zeros(ct_ref.shape, F32)
        n_ref[...] = jnp.zeros(n_ref.shape, F32)
        m_ref[...] = jnp.zeros(m_ref.shape, F32)

    row = lax.broadcasted_iota(jnp.int32, (chunk, chunk), 0)
    col = lax.broadcasted_iota(jnp.int32, (chunk, chunk), 1)
    causal = col <= row

    for c in range(ts // chunk):
        for b in range(nbatch):
            _mlstm_chunk(u_ref.at[b], v_ref.at[b], o_ref.at[b], gt_ref.at[b], cw_ref, cb_ref,
                         wq_ref, wkt_ref, gb_ref, og_ref, out_ref.at[b], xs_ref.at[b],
                         ct_ref.at[b], n_ref.at[b], m_ref.at[b], c, chunk, causal)
    for b in range(nbatch):
        xs_ref[b, 0:8, :] = xs_ref[b, ts:ts + 8, :]


def _mlstm_chunk(u_ref, v_ref, o_ref, gt_ref, cw_ref, cb_ref, wq_ref, wkt_ref, gb_ref, og_ref,
                 out_ref, xs_ref, ct_ref, n_ref, m_ref, c, chunk, causal):
    nh, dh = MLSTM_HEADS, MLSTM_DH
    gb = gb_ref[...]
    kw = cw_ref.shape[0]
    ones = jnp.ones((chunk, LANES), BF16)
    if True:
        r0 = c * chunk
        xc = u_ref[r0:r0 + chunk, :].astype(F32)
        xs_ref[8 + r0:8 + r0 + chunk, :] = xc
        acc = cb_ref[...] + cw_ref[kw - 1:kw, :] * xc
        for j in range(kw - 1):
            acc = acc + cw_ref[j:j + 1, :] * xs_ref[pl.ds(8 + r0 - (kw - 1) + j, chunk), :]
        su = _silu(acc)

        gc = gt_ref[r0:r0 + chunk, MISC_W - LANES:] + gb
        bc = _cumsum_rows(_log_sigmoid(gc))
        gt = gc.T[0:8, :]
        bt = _cumsum_lanes(_log_sigmoid(gt))
        mdc = _cummax_rows(gc - pltpu.roll(bc, LANES - nh, axis=1))
        for h in range(nh):
            c0 = h * dh
            b_c = bc[:, nh + h:nh + h + 1]
            li_r = gt[h:h + 1, :]
            b_r = bt[nh + h:nh + h + 1, :]
            g_tot = b_r[:, chunk - 1:chunk]
            z_r = li_r - b_r
            zmax = jnp.max(z_r, axis=-1, keepdims=True)
            m_prev = m_ref[h:h + 1, 0:1]

            uh = su[:, c0:c0 + dh]
            qb = jnp.dot(uh.astype(BF16), wq_ref[h], preferred_element_type=F32).astype(BF16)
            kt = jnp.dot(wkt_ref[h], uh.T.astype(BF16),
                         preferred_element_type=F32) * (dh ** -0.5)
            vc = v_ref[r0:r0 + chunk, c0:c0 + dh]

            mx_c = jnp.maximum(m_prev, mdc[:, h:h + 1])
            m_t = b_c + mx_c
            sb = (jnp.dot(qb, kt.astype(BF16), preferred_element_type=F32)
                  * jnp.exp(jnp.where(causal, z_r - mx_c, -jnp.inf))).astype(BF16)
            inter = jnp.exp(m_prev - mx_c)
            ct = ct_ref[h]
            nrep = n_ref[h]
            num = (jnp.dot(sb, vc, preferred_element_type=F32)
                   + inter * jnp.dot(qb, ct.astype(BF16), preferred_element_type=F32))
            den = (jnp.dot(sb, ones, preferred_element_type=F32)
                   + inter * jnp.dot(qb, nrep.astype(BF16), preferred_element_type=F32))
            lim = jnp.maximum(jnp.abs(den), jnp.exp(-m_t))
            hh = num / jnp.concatenate([lim] * (dh // LANES), axis=1)

            mm = jnp.maximum(m_prev, zmax)
            s_prev = jnp.exp(m_prev - mm)
            s_loc = jnp.exp(zmax - mm)
            kwt = (kt * (jnp.exp(z_r - zmax) * s_loc)).astype(BF16)
            ct_ref[h] = s_prev * ct + jnp.dot(kwt, vc, preferred_element_type=F32)
            n_ref[h] = s_prev * nrep + jnp.dot(kwt, ones, preferred_element_type=F32)
            m_ref[h:h + 1, :] = jnp.broadcast_to(g_tot + mm, (1, LANES))

            og = o_ref[r0:r0 + chunk, c0:c0 + dh].astype(F32)
            hh = hh / (1.0 + jnp.exp(-og))
            out_ref[r0:r0 + chunk, c0:c0 + dh] = (
                _rms(hh) * og_ref[:, c0:c0 + dh]).astype(BF16)


def _mlstm(proj, misc, conv_w, conv_b, wq, wkt, gate_b, out_g, bsz, seq):
    ts, chunk = 512, 128
    t = bsz * seq
    kern = functools.partial(_mlstm_kernel, ts=ts, chunk=chunk)
    proj3 = proj.reshape(bsz, seq, proj.shape[1])
    misc3 = misc.reshape(bsz, seq, MISC_W)
    out = pl.pallas_call(
        kern,
        grid=(seq // ts,),
        in_specs=[pl.BlockSpec((bsz, ts, MLSTM_W), lambda i: (0, i, 0)),
                  pl.BlockSpec((bsz, ts, MLSTM_W), lambda i: (0, i, 1)),
                  pl.BlockSpec((bsz, ts, MLSTM_W), lambda i: (0, i, 2)),
                  pl.BlockSpec((bsz, ts, MISC_W), lambda i: (0, i, 0)),
                  pl.BlockSpec(conv_w.shape, lambda i: (0, 0)),
                  pl.BlockSpec((1, MLSTM_W), lambda i: (0, 0)),
                  pl.BlockSpec(wq.shape, lambda i: (0, 0, 0)),
                  pl.BlockSpec(wkt.shape, lambda i: (0, 0, 0)),
                  pl.BlockSpec((1, LANES), lambda i: (0, 0)),
                  pl.BlockSpec((1, MLSTM_W), lambda i: (0, 0))],
        out_specs=pl.BlockSpec((bsz, ts, MLSTM_W), lambda i: (0, i, 0)),
        out_shape=jax.ShapeDtypeStruct((bsz, seq, MLSTM_W), BF16),
        scratch_shapes=[pltpu.VMEM((bsz, ts + 8, MLSTM_W), F32),
                        pltpu.VMEM((bsz, MLSTM_HEADS, MLSTM_DH, MLSTM_DH), F32),
                        pltpu.VMEM((bsz, MLSTM_HEADS, MLSTM_DH, LANES), F32),
                        pltpu.VMEM((bsz, 8, LANES), F32)],
        compiler_params=_params(1, 48),
        name="mlstm",
    )(proj3, proj3, proj3, misc3, conv_w, conv_b.reshape(1, MLSTM_W), wq, wkt, gate_b,
      out_g.reshape(1, MLSTM_W))
    return out.reshape(t, MLSTM_W)


ATT_TQ = 1024
ATT_TK = 512
MLA_VA = MLA_V + 16


def _qkv_kernel(cq_ref, ckv_ref, misc_ref, pos_ref, fr_ref, qg_ref, kvg_ref, wq_ref, wkv_ref,
                qt_ref, k_ref, vt_ref, cqn, ckvn, cos_s, sin_s, kr_s, *, scale, tk):
    h = pl.program_id(1)

    @pl.when(h == 0)
    def _():
        cqn[...] = (_rms(cq_ref[...].astype(F32)) * qg_ref[...]).astype(BF16)
        ckvn[...] = (_rms(ckv_ref[...].astype(F32)) * kvg_ref[...]).astype(BF16)
        ang = fr_ref[...] * pos_ref[0].astype(F32)
        reps = LANES // ang.shape[0]
        cs = jnp.concatenate([jnp.cos(ang)] * reps, axis=0).T
        sn = jnp.concatenate([jnp.sin(ang)] * reps, axis=0).T
        cos_s[...] = cs
        sin_s[...] = sn
        y = misc_ref[:, 0:LANES]
        kr_s[...] = y * cs + pltpu.roll(y, MLA_ROPE, axis=1) * sn

    mq = jnp.dot(cqn[...], wq_ref[0], preferred_element_type=F32)
    qt_ref[0, 0, 0:MLA_NOPE, :] = (mq[:, 0:MLA_NOPE] * scale).T.astype(BF16)
    y = mq[:, MLA_NOPE:]
    r = (y * cos_s[...] + pltpu.roll(y, MLA_ROPE, axis=1) * sin_s[...]) * scale
    qt_ref[0, 0, MLA_NOPE:MLA_QK, :] = r.T[0:MLA_ROPE, :].astype(BF16)

    mkv = jnp.dot(ckvn[...], wkv_ref[0], preferred_element_type=F32)
    k_ref[0, 0, :, 0:MLA_NOPE] = mkv[:, 0:MLA_NOPE].astype(BF16)
    k_ref[0, 0, :, MLA_NOPE:MLA_QK] = kr_s[:, 0:MLA_ROPE].astype(BF16)
    vt = mkv[:, MLA_NOPE:].T.astype(BF16)
    for j in range(vt.shape[1] // tk):
        vt_ref[0, 0, j, 0:MLA_V, :] = vt[:, j * tk:(j + 1) * tk]
        vt_ref[0, 0, j, MLA_V:, :] = jnp.ones((MLA_VA - MLA_V, tk), BF16)


def _mla_qkv(proj, misc, pos, freqs, q_g, kv_g, wq_h, wkv_h, bsz, seq):
    tm, tk = ATT_TQ, ATT_TK
    t = bsz * seq
    nsb = seq // tm
    cq_blk = (3 * MLSTM_W) // Q_LORA
    ckv_blk = (3 * MLSTM_W + Q_LORA) // KV_LORA
    kern = functools.partial(_qkv_kernel, scale=MLA_QK ** -0.5 * float(np.log2(np.e)), tk=tk)
    return pl.pallas_call(
        kern,
        grid=(t // tm, MLA_HEADS),
        in_specs=[pl.BlockSpec((tm, Q_LORA), lambda i, h: (i, cq_blk)),
                  pl.BlockSpec((tm, KV_LORA), lambda i, h: (i, ckv_blk)),
                  pl.BlockSpec((tm, MISC_W), lambda i, h: (i, 0)),
                  pl.BlockSpec((1, 1, tm), lambda i, h: (i, 0, 0)),
                  pl.BlockSpec(freqs.shape, lambda i, h: (0, 0)),
                  pl.BlockSpec((1, Q_LORA), lambda i, h: (0, 0)),
                  pl.BlockSpec((1, KV_LORA), lambda i, h: (0, 0)),
                  pl.BlockSpec((1, Q_LORA, 2 * LANES), lambda i, h: (h, 0, 0)),
                  pl.BlockSpec((1, KV_LORA, 2 * LANES), lambda i, h: (h, 0, 0))],
        out_specs=[pl.BlockSpec((1, 1, MLA_QK, tm), lambda i, h: (i // nsb, h, 0, i % nsb)),
                   pl.BlockSpec((1, 1, tm, MLA_QK), lambda i, h: (i // nsb, h, i % nsb, 0)),
                   pl.BlockSpec((1, 1, tm // tk, MLA_VA, tk),
                                lambda i, h: (i // nsb, h, i % nsb, 0, 0))],
        out_shape=[jax.ShapeDtypeStruct((bsz, MLA_HEADS, MLA_QK, seq), BF16),
                   jax.ShapeDtypeStruct((bsz, MLA_HEADS, seq, MLA_QK), BF16),
                   jax.ShapeDtypeStruct((bsz, MLA_HEADS, seq // tk, MLA_VA, tk), BF16)],
        scratch_shapes=[pltpu.VMEM((tm, Q_LORA), BF16),
                        pltpu.VMEM((tm, KV_LORA), BF16),
                        pltpu.VMEM((tm, LANES), F32),
                        pltpu.VMEM((tm, LANES), F32),
                        pltpu.VMEM((tm, LANES), F32)],
        compiler_params=_params(2, 32),
        name="mla_qkv",
    )(proj, proj, misc, pos, freqs, q_g.reshape(1, Q_LORA), kv_g.reshape(1, KV_LORA),
      wq_h, wkv_h)


def _flash_kernel(qt_ref, k_ref, vt_ref, g_ref, o_ref, m_ref, acc_ref, bias_ref, *, tq, tk):
    qi = pl.program_id(2)
    m_ref[...] = jnp.full(m_ref.shape, -jnp.inf, F32)
    acc_ref[...] = jnp.zeros(acc_ref.shape, F32)
    @pl.when((pl.program_id(0) == 0) & (pl.program_id(1) == 0) & (qi == 0))
    def _():
        key = lax.broadcasted_iota(jnp.int32, (tk, tq), 0)
        qry = lax.broadcasted_iota(jnp.int32, (tk, tq), 1)
        bias_ref[...] = jnp.where(key <= qry, 0.0, -jnp.inf).astype(F32)

    def scores(kj, c0, masked):
        k = k_ref[0, 0, pl.ds(pl.multiple_of(kj * tk, tk), tk), :]
        s = jnp.dot(k, qt_ref[0, 0, :, c0:], preferred_element_type=F32)
        if masked:
            s = s + bias_ref[:, 0:tq - c0]
        return s

    def update(s, kj, c0):
        m_prev = m_ref[:, c0:]
        m_new = jnp.maximum(m_prev, jnp.max(s, axis=0, keepdims=True))
        alpha = jnp.exp2(m_prev - m_new)
        p = jnp.exp2(s - m_new)
        acc_ref[:, c0:] = alpha * acc_ref[:, c0:] + jnp.dot(
            vt_ref[0, 0, kj], p.astype(BF16), preferred_element_type=F32)
        m_ref[:, c0:] = m_new

    per_q = tq // tk

    def body(jj, carry):
        ss = [scores(jj * per_q + d, 0, False) for d in range(per_q)]
        for d in range(per_q):
            update(ss[d], jj * per_q + d, 0)
        return carry

    lax.fori_loop(0, qi, body, 0)
    ss = [scores(qi * per_q + d, d * tk, True) for d in range(per_q)]
    for d in range(per_q):
        update(ss[d], qi * per_q + d, d * tk)

    o = acc_ref[0:MLA_V, :] / acc_ref[MLA_V:MLA_V + 1, :]
    o = o * lax.rsqrt(jnp.mean(o * o, axis=0, keepdims=True) + EPS) * g_ref[0]
    o_ref[...] = o.T.astype(BF16)


def _mla_attention(qt, k, vt, out_g):
    bsz, nh, seq, _ = k.shape
    tq, tk = ATT_TQ, ATT_TK
    nq = seq // tq
    kern = functools.partial(_flash_kernel, tq=tq, tk=tk)
    return pl.pallas_call(
        kern,
        grid=(bsz, nh, nq),
        in_specs=[pl.BlockSpec((1, 1, MLA_QK, tq), lambda b, h, i: (b, h, 0, i)),
                  pl.BlockSpec((1, 1, seq, MLA_QK), lambda b, h, i: (b, h, 0, 0)),
                  pl.BlockSpec((1, 1, seq // tk, MLA_VA, tk), lambda b, h, i: (b, h, 0, 0, 0)),
                  pl.BlockSpec((1, MLA_V, 1), lambda b, h, i: (h, 0, 0))],
        out_specs=pl.BlockSpec((tq, MLA_V), lambda b, h, i: (b * nq + i, h)),
        out_shape=jax.ShapeDtypeStruct((bsz * seq, nh * MLA_V), BF16),
        scratch_shapes=[pltpu.VMEM((1, tq), F32),
                        pltpu.VMEM((MLA_VA, tq), F32),
                        pltpu.VMEM((tk, tq), F32)],
        compiler_params=_params(3, 40),
        name="mla_attention",
    )(qt, k, vt, out_g.reshape(nh, MLA_V, 1))


def _outproj_kernel(hm_ref, ha_ref, w_ref, x_ref, mod_ref, g_ref, x1_ref, h2_ref, *, sub):
    km = hm_ref.shape[1]
    for s in range(hm_ref.shape[0] // sub):
        rows = pl.ds(s * sub, sub)
        mix = (jnp.dot(hm_ref[rows, :], w_ref[0:km, :], preferred_element_type=F32)
               + jnp.dot(ha_ref[rows, :], w_ref[km:, :], preferred_element_type=F32))
        x1 = x_ref[rows, :] + mod_ref[0, 2:3, :] * mix
        x1_ref[rows, :] = x1
        y = _rms(x1) * g_ref[...]
        h2_ref[rows, :] = (y * (1.0 + mod_ref[0, 4:5, :]) + mod_ref[0, 3:4, :]).astype(BF16)


def _out_projection(hm, ha, w_out, x2, mod, g, seq):
    t, d = x2.shape
    tm = 512
    per_b = seq // tm
    return pl.pallas_call(
        functools.partial(_outproj_kernel, sub=256),
        grid=(t // tm,),
        in_specs=[pl.BlockSpec((tm, hm.shape[1]), lambda i: (i, 0)),
                  pl.BlockSpec((tm, ha.shape[1]), lambda i: (i, 0)),
                  pl.BlockSpec(w_out.shape, lambda i: (0, 0), pipeline_mode=pl.Buffered(1)),
                  pl.BlockSpec((tm, d), lambda i: (i, 0)),
                  pl.BlockSpec((1, 6, d), lambda i: (i // per_b, 0, 0)),
                  pl.BlockSpec((1, d), lambda i: (0, 0))],
        out_specs=[pl.BlockSpec((tm, d), lambda i: (i, 0)),
                   pl.BlockSpec((tm, d), lambda i: (i, 0))],
        out_shape=[jax.ShapeDtypeStruct((t, d), F32),
                   jax.ShapeDtypeStruct((t, d), BF16)],
        compiler_params=_params(1, 48),
        name="out_proj",
    )(hm, ha, w_out, x2, mod, g.reshape(1, d))


def _ffn_up_kernel(h_ref, wv_ref, wg_ref, cwv_ref, cwg_ref, cbv_ref, cbg_ref, o_ref,
                   xv_ref, xg_ref, wvb_ref, wgb_ref, *, tm, blocks_per_seq):
    i = pl.program_id(1)

    @pl.when(i % blocks_per_seq == 0)
    def _():
        xv_ref[0:8, :] = jnp.zeros((8, xv_ref.shape[1]), F32)
        xg_ref[0:8, :] = jnp.zeros((8, xg_ref.shape[1]), F32)

    @pl.when(i == 0)
    def _():
        wvb_ref[...] = wv_ref[...].astype(BF16)
        wgb_ref[...] = wg_ref[...].astype(BF16)

    h = h_ref[...]

    def conv(w_ref, cw_ref, cb_ref, xs_ref):
        kw = cw_ref.shape[0]
        up = jnp.dot(h, w_ref[...], preferred_element_type=F32)
        xs_ref[8:8 + tm, :] = up
        y = cb_ref[...] + cw_ref[kw - 1:kw, :] * up
        for j in range(kw - 1):
            y = y + cw_ref[j:j + 1, :] * xs_ref[pl.ds(8 - (kw - 1) + j, tm), :]
        xs_ref[0:8, :] = xs_ref[tm:tm + 8, :]
        return y

    val = conv(wvb_ref, cwv_ref, cbv_ref, xv_ref)
    gate = conv(wgb_ref, cwg_ref, cbg_ref, xg_ref)
    o_ref[...] = (_silu(gate) * val).astype(BF16)


def _ffn_up(h2, w_up, conv_w, conv_b, seq):
    t, d = h2.shape
    f = w_up.shape[1] // 2
    tm, tf = 1024, 512
    nf = f // tf
    kw = conv_w.shape[0]
    kern = functools.partial(_ffn_up_kernel, tm=tm, blocks_per_seq=seq // tm)
    cb = conv_b.reshape(1, 2 * f)
    return pl.pallas_call(
        kern,
        grid=(nf, t // tm),
        in_specs=[pl.BlockSpec((tm, d), lambda j, i: (i, 0)),
                  pl.BlockSpec((d, tf), lambda j, i: (0, j)),
                  pl.BlockSpec((d, tf), lambda j, i: (0, nf + j)),
                  pl.BlockSpec((kw, tf), lambda j, i: (0, j)),
                  pl.BlockSpec((kw, tf), lambda j, i: (0, nf + j)),
                  pl.BlockSpec((1, tf), lambda j, i: (0, j)),
                  pl.BlockSpec((1, tf), lambda j, i: (0, nf + j))],
        out_specs=pl.BlockSpec((tm, tf), lambda j, i: (i, j)),
        out_shape=jax.ShapeDtypeStruct((t, f), BF16),
        scratch_shapes=[pltpu.VMEM((tm + 8, tf), F32),
                        pltpu.VMEM((tm + 8, tf), F32),
                        pltpu.VMEM((d, tf), BF16),
                        pltpu.VMEM((d, tf), BF16)],
        compiler_params=_params(2, 56),
        name="ffn_up",
    )(h2, w_up, w_up, conv_w, conv_w, cb, cb)


def _ffn_down_kernel(a_ref, w_ref, x1_ref, mod_ref, g_ref, o_ref, *, sub):
    for s in range(a_ref.shape[0] // sub):
        rows = pl.ds(s * sub, sub)
        y = jnp.dot(a_ref[rows, :], w_ref[...], preferred_element_type=F32)
        x2 = x1_ref[rows, :] + mod_ref[0, 5:6, :] * y
        o_ref[rows, :] = _rms(x2) * g_ref[...]


def _ffn_down(act, w_down, x1, mod, g, seq):
    t, f = act.shape
    d = w_down.shape[1]
    tm = 512
    per_b = seq // tm
    return pl.pallas_call(
        functools.partial(_ffn_down_kernel, sub=256),
        grid=(t // tm,),
        in_specs=[pl.BlockSpec((tm, f), lambda i: (i, 0)),
                  pl.BlockSpec((f, d), lambda i: (0, 0), pipeline_mode=pl.Buffered(1)),
                  pl.BlockSpec((tm, d), lambda i: (i, 0)),
                  pl.BlockSpec((1, 6, d), lambda i: (i // per_b, 0, 0)),
                  pl.BlockSpec((1, d), lambda i: (0, 0))],
        out_specs=pl.BlockSpec((tm, d), lambda i: (i, 0)),
        out_shape=jax.ShapeDtypeStruct((t, d), F32),
        compiler_params=_params(1, 58),
        name="ffn_down",
    )(act, w_down, x1, mod, g.reshape(1, d))


def _rot_cols(w):
    half = w.shape[-1] // 2
    return jnp.concatenate([-w[..., half:], w[..., :half]], axis=-1)


def _win_prep_kernel(w_ref, tail_ref, o_ref, *, n_main):
    j = pl.program_id(0)
    gates = 2 * MLSTM_HEADS
    half = MLA_ROPE // 2

    @pl.when(j < n_main)
    def _():
        o_ref[...] = w_ref[...].T.astype(BF16)

    @pl.when(j == n_main)
    def _():
        o_ref[...] = tail_ref[gates:gates + Q_LORA, :].T.astype(BF16)

    @pl.when(j == n_main + 1)
    def _():
        r0 = gates + Q_LORA
        kr0 = r0 + KV_LORA
        blk = o_ref.shape[1]
        used = KV_LORA + 2 * MLA_ROPE + gates
        o_ref[...] = jnp.concatenate(
            [tail_ref[r0:kr0, :],
             tail_ref[kr0:kr0 + MLA_ROPE, :],
             -tail_ref[kr0 + half:kr0 + MLA_ROPE, :],
             tail_ref[kr0:kr0 + half, :],
             tail_ref[0:gates, :],
             jnp.zeros((blk - used, o_ref.shape[0]), F32)], axis=0).T.astype(BF16)


def _prep_w_in(w_in):
    d = w_in.shape[0]
    wt = w_in.T
    n_head = 3 * MLSTM_W
    blk = Q_LORA
    n_main = n_head // blk
    return pl.pallas_call(
        functools.partial(_win_prep_kernel, n_main=n_main),
        grid=(IN_PAD // blk,),
        in_specs=[pl.BlockSpec((blk, d), lambda j: (jnp.minimum(j, n_main - 1), 0)),
                  pl.BlockSpec((wt.shape[0] - n_head, d), lambda j: (0, 0))],
        out_specs=pl.BlockSpec((d, blk), lambda j: (0, j)),
        out_shape=jax.ShapeDtypeStruct((d, IN_PAD), BF16),
        compiler_params=_params(1, 40),
        name="w_in_prep",
    )(wt, wt[n_head:])


def kernel(x, c, positions, ada_w, ada_b, attn_norm_g, w_in, mlstm_conv_w, mlstm_conv_b, mlstm_wq, mlstm_wk, mlstm_igate_b, mlstm_fgate_b, mla_q_norm_g, mla_w_uq, mla_kv_norm_g, mla_w_ukv, mlstm_out_g, mla_out_g, w_out, ffn_norm_g, ffn_w_up, ffn_conv_w, ffn_conv_b, ffn_w_down, final_norm_g):
    bsz, seq, d = x.shape
    t = bsz * seq
    depth = ada_w.shape[0]
    xr = x.reshape(t, d)
    pos = positions.reshape(t // ATT_TQ, 1, ATT_TQ)
    half = MLA_ROPE // 2
    freqs = (ROPE_THETA ** (-jnp.arange(half, dtype=F32) / half)).reshape(half, 1)

    for l in range(depth):
        mod = _modulation(c, ada_w[l], ada_b[l])

        proj, misc = _in_projection(xr, mod, attn_norm_g[l], _prep_w_in(w_in[l]), seq)

        wq_m = mlstm_wq[l].astype(BF16)
        wkt_m = mlstm_wk[l].transpose(0, 2, 1).astype(BF16)
        gate_b = jnp.zeros((1, LANES), F32)
        gate_b = gate_b.at[0, :MLSTM_HEADS].set(mlstm_igate_b[l])
        gate_b = gate_b.at[0, MLSTM_HEADS:2 * MLSTM_HEADS].set(mlstm_fgate_b[l])
        hm = _mlstm(proj, misc, mlstm_conv_w[l], mlstm_conv_b[l], wq_m, wkt_m, gate_b,
                    mlstm_out_g[l], bsz, seq)

        wq = mla_w_uq[l].reshape(Q_LORA, MLA_HEADS, MLA_QK)
        wq_r = wq[..., MLA_NOPE:]
        wq_h = jnp.concatenate([wq[..., :MLA_NOPE], wq_r, _rot_cols(wq_r)], axis=-1)
        wq_h = wq_h.transpose(1, 0, 2).astype(BF16)
        wkv_h = mla_w_ukv[l].reshape(KV_LORA, MLA_HEADS, MLA_NOPE + MLA_V)
        wkv_h = wkv_h.transpose(1, 0, 2).astype(BF16)
        qt, k, vt = _mla_qkv(proj, misc, pos, freqs, mla_q_norm_g[l], mla_kv_norm_g[l],
                             wq_h, wkv_h, bsz, seq)
        ha = _mla_attention(qt, k, vt, mla_out_g[l])

        x1, h2 = _out_projection(hm, ha, w_out[l].astype(BF16), xr, mod, ffn_norm_g[l], seq)

        act = _ffn_up(h2, ffn_w_up[l], ffn_conv_w[l], ffn_conv_b[l], seq)
        if l == depth - 1:
            xr = _ffn_down(act, ffn_w_down[l].astype(BF16), x1, mod, final_norm_g, seq)
        else:
            raise NotImplementedError("fused final norm assumes a single layer")
    return xr.reshape(bsz, seq, d)
```

```python
import functools

import jax
import jax.numpy as jnp
import numpy as np
from jax import lax
from jax.experimental import pallas as pl
from jax.experimental.pallas import tpu as pltpu

F32 = jnp.float32
BF16 = jnp.bfloat16

EPS = 1e-6
ROPE_THETA = 10000.0
MLSTM_HEADS = 4
MLSTM_DH = 256
MLSTM_W = MLSTM_HEADS * MLSTM_DH
MLA_HEADS = 8
MLA_NOPE = 128
MLA_ROPE = 64
MLA_V = 128
MLA_QK = MLA_NOPE + MLA_ROPE
Q_LORA = 512
KV_LORA = 256
LANES = 128
MIB = 1024 * 1024

IN_PAD = 4096
IN_BLOCK = 1024
MISC_W = 256
MISC_OFF = IN_BLOCK - MISC_W


def _params(n_axes, vmem_mib):
    return pltpu.CompilerParams(
        dimension_semantics=("arbitrary",) * n_axes,
        vmem_limit_bytes=vmem_mib * MIB)


def _rms(x):
    return x * lax.rsqrt(jnp.mean(x * x, axis=-1, keepdims=True) + EPS)


def _silu(x):
    return x / (1.0 + jnp.exp(-x))


def _log_sigmoid(x):
    return jnp.minimum(x, 0.0) - jnp.log1p(jnp.exp(-jnp.abs(x)))


def _mod_kernel(c_ref, w_ref, b_ref, o_ref):
    ca = _silu(c_ref[...]).astype(BF16)
    o_ref[...] = jnp.dot(ca, w_ref[...].astype(BF16),
                         preferred_element_type=F32) + b_ref[...]


def _modulation(c, ada_w, ada_b):
    bsz, d = c.shape
    n = ada_w.shape[1]
    tn = 1024
    cp = jnp.zeros((8, d), F32).at[:bsz].set(c)
    out = pl.pallas_call(
        _mod_kernel,
        grid=(n // tn,),
        in_specs=[pl.BlockSpec((8, d), lambda j: (0, 0)),
                  pl.BlockSpec((d, tn), lambda j: (0, j)),
                  pl.BlockSpec((1, tn), lambda j: (0, j))],
        out_specs=pl.BlockSpec((8, tn), lambda j: (0, j)),
        out_shape=jax.ShapeDtypeStruct((8, n), F32),
        compiler_params=_params(1, 40),
        name="adaln_mod",
    )(cp, ada_w, ada_b.reshape(1, n))
    return out[:bsz].reshape(bsz, 6, d)


def _inproj_kernel(x_ref, mod_ref, g_ref, w_ref, o_ref, misc_ref, h_scr, *, sub):
    j = pl.program_id(1)
    last = pl.num_programs(1) - 1
    tm = x_ref.shape[0]

    def project(rows):
        acc = jnp.dot(h_scr[rows, :], w_ref[...], preferred_element_type=F32)
        o_ref[rows, :] = acc.astype(BF16)
        return acc

    @pl.when(j == 0)
    def _():
        for s in range(tm // sub):
            rows = pl.ds(s * sub, sub)
            y = _rms(x_ref[rows, :]) * g_ref[...]
            h_scr[rows, :] = (y * (1.0 + mod_ref[0, 1:2, :]) + mod_ref[0, 0:1, :]).astype(BF16)
            project(rows)

    @pl.when(j > 0)
    def _():
        for s in range(tm // sub):
            rows = pl.ds(s * sub, sub)
            acc = project(rows)

            @pl.when(j == last)
            def _():
                misc_ref[rows, :] = acc[:, MISC_OFF:]


def _in_projection(x2, mod, g, w_in_p, seq):
    t, d = x2.shape
    tm = 1024
    per_b = seq // tm
    return pl.pallas_call(
        functools.partial(_inproj_kernel, sub=256),
        grid=(t // tm, IN_PAD // IN_BLOCK),
        in_specs=[pl.BlockSpec((tm, d), lambda i, j: (i, 0)),
                  pl.BlockSpec((1, 6, d), lambda i, j: (i // per_b, 0, 0)),
                  pl.BlockSpec((1, d), lambda i, j: (0, 0)),
                  pl.BlockSpec((d, IN_BLOCK), lambda i, j: (0, j))],
        out_specs=[pl.BlockSpec((tm, IN_BLOCK), lambda i, j: (i, j)),
                   pl.BlockSpec((tm, MISC_W), lambda i, j: (i, 0))],
        out_shape=[jax.ShapeDtypeStruct((t, IN_PAD), BF16),
                   jax.ShapeDtypeStruct((t, MISC_W), F32)],
        scratch_shapes=[pltpu.VMEM((tm, d), BF16)],
        compiler_params=_params(2, 48),
        name="norm_inproj",
    )(x2, mod, g.reshape(1, d), w_in_p)


def _cumsum_rows(x):
    n = x.shape[0]
    row = lax.broadcasted_iota(jnp.int32, x.shape, 0)
    sh = 1
    while sh < n:
        x = x + jnp.where(row >= sh, pltpu.roll(x, sh, axis=0), 0.0)
        sh *= 2
    return x


def _cummax_rows(x):
    n = x.shape[0]
    row = lax.broadcasted_iota(jnp.int32, x.shape, 0)
    sh = 1
    while sh < n:
        x = jnp.maximum(x, jnp.where(row >= sh, pltpu.roll(x, sh, axis=0), -jnp.inf))
        sh *= 2
    return x


def _cumsum_lanes(x):
    n = x.shape[1]
    col = lax.broadcasted_iota(jnp.int32, x.shape, 1)
    sh = 1
    while sh < n:
        x = x + jnp.where(col >= sh, pltpu.roll(x, sh, axis=1), 0.0)
        sh *= 2
    return x


def _mlstm_kernel(u_ref, v_ref, o_ref, gt_ref, cw_ref, cb_ref, wq_ref, wkt_ref, gb_ref, og_ref,
                  out_ref, xs_ref, ct_ref, n_ref, m_ref, *, ts, chunk):
    i = pl.program_id(0)
    nbatch = u_ref.shape[0]

    @pl.when(i == 0)
    def _():
        for b in range(nbatch):
            xs_ref[b, 0:8, :] = jnp.zeros((8, MLSTM_W), F32)
        ct_ref[...] = jnp.zeros(ct_ref.shape, F32)
        n_ref[...] = jnp.zeros(n_ref.shape, F32)
        m_ref[...] = jnp.zeros(m_ref.shape, F32)

    row = lax.broadcasted_iota(jnp.int32, (chunk, chunk), 0)
    col = lax.broadcasted_iota(jnp.int32, (chunk, chunk), 1)
    causal = col <= row

    for c in range(ts // chunk):
        for b in range(nbatch):
            _mlstm_chunk(u_ref.at[b], v_ref.at[b], o_ref.at[b], gt_ref.at[b], cw_ref, cb_ref,
                         wq_ref, wkt_ref, gb_ref, og_ref, out_ref.at[b], xs_ref.at[b],
                         ct_ref.at[b], n_ref.at[b], m_ref.at[b], c, chunk, causal)
    for b in range(nbatch):
        xs_ref[b, 0:8, :] = xs_ref[b, ts:ts + 8, :]


def _mlstm_chunk(u_ref, v_ref, o_ref, gt_ref, cw_ref, cb_ref, wq_ref, wkt_ref, gb_ref, og_ref,
                 out_ref, xs_ref, ct_ref, n_ref, m_ref, c, chunk, causal):
    nh, dh = MLSTM_HEADS, MLSTM_DH
    gb = gb_ref[...]
    kw = cw_ref.shape[0]
    ones = jnp.ones((chunk, LANES), BF16)
    if True:
        r0 = c * chunk
        xc = u_ref[r0:r0 + chunk, :].astype(F32)
        xs_ref[8 + r0:8 + r0 + chunk, :] = xc
        acc = cb_ref[...] + cw_ref[kw - 1:kw, :] * xc
        for j in range(kw - 1):
            acc = acc + cw_ref[j:j + 1, :] * xs_ref[pl.ds(8 + r0 - (kw - 1) + j, chunk), :]
        su = _silu(acc)

        gc = gt_ref[r0:r0 + chunk, MISC_W - LANES:] + gb
        bc = _cumsum_rows(_log_sigmoid(gc))
        gt = gc.T[0:8, :]
        bt = _cumsum_lanes(_log_sigmoid(gt))
        mdc = _cummax_rows(gc - pltpu.roll(bc, LANES - nh, axis=1))
        for h in range(nh):
            c0 = h * dh
            b_c = bc[:, nh + h:nh + h + 1]
            li_r = gt[h:h + 1, :]
            b_r = bt[nh + h:nh + h + 1, :]
            g_tot = b_r[:, chunk - 1:chunk]
            z_r = li_r - b_r
            zmax = jnp.max(z_r, axis=-1, keepdims=True)
            m_prev = m_ref[h:h + 1, 0:1]

            uh = su[:, c0:c0 + dh]
            qb = jnp.dot(uh.astype(BF16), wq_ref[h], preferred_element_type=F32).astype(BF16)
            kt = jnp.dot(wkt_ref[h], uh.T.astype(BF16),
                         preferred_element_type=F32) * (dh ** -0.5)
            vc = v_ref[r0:r0 + chunk, c0:c0 + dh]

            mx_c = jnp.maximum(m_prev, mdc[:, h:h + 1])
            m_t = b_c + mx_c
            sb = (jnp.dot(qb, kt.astype(BF16), preferred_element_type=F32)
                  * jnp.exp(jnp.where(causal, z_r - mx_c, -jnp.inf))).astype(BF16)
            inter = jnp.exp(m_prev - mx_c)
            ct = ct_ref[h]
            nrep = n_ref[h]
            num = (jnp.dot(sb, vc, preferred_element_type=F32)
                   + inter * jnp.dot(qb, ct.astype(BF16), preferred_element_type=F32))
            den = (jnp.dot(sb, ones, preferred_element_type=F32)
                   + inter * jnp.dot(qb, nrep.astype(BF16), preferred_element_type=F32))
            lim = jnp.maximum(jnp.abs(den), jnp.exp(-m_t))
            hh = num / jnp.concatenate([lim] * (dh // LANES), axis=1)

            mm = jnp.maximum(m_prev, zmax)
            s_prev = jnp.exp(m_prev - mm)
            s_loc = jnp.exp(zmax - mm)
            kwt = (kt * (jnp.exp(z_r - zmax) * s_loc)).astype(BF16)
            ct_ref[h] = s_prev * ct + jnp.dot(kwt, vc, preferred_element_type=F32)
            n_ref[h] = s_prev * nrep + jnp.dot(kwt, ones, preferred_element_type=F32)
            m_ref[h:h + 1, :] = jnp.broadcast_to(g_tot + mm, (1, LANES))

            og = o_ref[r0:r0 + chunk, c0:c0 + dh].astype(F32)
            hh = hh / (1.0 + jnp.exp(-og))
            out_ref[r0:r0 + chunk, c0:c0 + dh] = (
                _rms(hh) * og_ref[:, c0:c0 + dh]).astype(BF16)


def _mlstm(proj, misc, conv_w, conv_b, wq, wkt, gate_b, out_g, bsz, seq):
    ts, chunk = 512, 128
    t = bsz * seq
    kern = functools.partial(_mlstm_kernel, ts=ts, chunk=chunk)
    proj3 = proj.reshape(bsz, seq, proj.shape[1])
    misc3 = misc.reshape(bsz, seq, MISC_W)
    out = pl.pallas_call(
        kern,
        grid=(seq // ts,),
        in_specs=[pl.BlockSpec((bsz, ts, MLSTM_W), lambda i: (0, i, 0)),
                  pl.BlockSpec((bsz, ts, MLSTM_W), lambda i: (0, i, 1)),
                  pl.BlockSpec((bsz, ts, MLSTM_W), lambda i: (0, i, 2)),
                  pl.BlockSpec((bsz, ts, MISC_W), lambda i: (0, i, 0)),
                  pl.BlockSpec(conv_w.shape, lambda i: (0, 0)),
                  pl.BlockSpec((1, MLSTM_W), lambda i: (0, 0)),
                  pl.BlockSpec(wq.shape, lambda i: (0, 0, 0)),
                  pl.BlockSpec(wkt.shape, lambda i: (0, 0, 0)),
                  pl.BlockSpec((1, LANES), lambda i: (0, 0)),
                  pl.BlockSpec((1, MLSTM_W), lambda i: (0, 0))],
        out_specs=pl.BlockSpec((bsz, ts, MLSTM_W), lambda i: (0, i, 0)),
        out_shape=jax.ShapeDtypeStruct((bsz, seq, MLSTM_W), BF16),
        scratch_shapes=[pltpu.VMEM((bsz, ts + 8, MLSTM_W), F32),
                        pltpu.VMEM((bsz, MLSTM_HEADS, MLSTM_DH, MLSTM_DH), F32),
                        pltpu.VMEM((bsz, MLSTM_HEADS, MLSTM_DH, LANES), F32),
                        pltpu.VMEM((bsz, 8, LANES), F32)],
        compiler_params=_params(1, 48),
        name="mlstm",
    )(proj3, proj3, proj3, misc3, conv_w, conv_b.reshape(1, MLSTM_W), wq, wkt, gate_b,
      out_g.reshape(1, MLSTM_W))
    return out.reshape(t, MLSTM_W)


ATT_TQ = 1024
ATT_TK = 256
MLA_VA = MLA_V + 16


def _qkv_kernel(cq_ref, ckv_ref, misc_ref, pos_ref, fr_ref, qg_ref, kvg_ref, wq_ref, wkv_ref,
                qt_ref, k_ref, vt_ref, cqn, ckvn, cos_s, sin_s, kr_s, *, scale, tk):
    h = pl.program_id(1)

    @pl.when(h == 0)
    def _():
        cqn[...] = (_rms(cq_ref[...].astype(F32)) * qg_ref[...]).astype(BF16)
        ckvn[...] = (_rms(ckv_ref[...].astype(F32)) * kvg_ref[...]).astype(BF16)
        ang = fr_ref[...] * pos_ref[0].astype(F32)
        reps = LANES // ang.shape[0]
        cs = jnp.concatenate([jnp.cos(ang)] * reps, axis=0).T
        sn = jnp.concatenate([jnp.sin(ang)] * reps, axis=0).T
        cos_s[...] = cs
        sin_s[...] = sn
        y = misc_ref[:, 0:LANES]
        kr_s[...] = y * cs + pltpu.roll(y, MLA_ROPE, axis=1) * sn

    mq = jnp.dot(cqn[...], wq_ref[0], preferred_element_type=F32)
    qt_ref[0, 0, 0:MLA_NOPE, :] = (mq[:, 0:MLA_NOPE] * scale).T.astype(BF16)
    y = mq[:, MLA_NOPE:]
    r = (y * cos_s[...] + pltpu.roll(y, MLA_ROPE, axis=1) * sin_s[...]) * scale
    qt_ref[0, 0, MLA_NOPE:MLA_QK, :] = r.T[0:MLA_ROPE, :].astype(BF16)

    mkv = jnp.dot(ckvn[...], wkv_ref[0], preferred_element_type=F32)
    k_ref[0, 0, :, 0:MLA_NOPE] = mkv[:, 0:MLA_NOPE].astype(BF16)
    k_ref[0, 0, :, MLA_NOPE:MLA_QK] = kr_s[:, 0:MLA_ROPE].astype(BF16)
    vt = mkv[:, MLA_NOPE:].T.astype(BF16)
    for j in range(vt.shape[1] // tk):
        vt_ref[0, 0, j, 0:MLA_V, :] = vt[:, j * tk:(j + 1) * tk]
        vt_ref[0, 0, j, MLA_V:, :] = jnp.ones((MLA_VA - MLA_V, tk), BF16)


def _mla_qkv(proj, misc, pos, freqs, q_g, kv_g, wq_h, wkv_h, bsz, seq):
    tm, tk = ATT_TQ, ATT_TK
    t = bsz * seq
    nsb = seq // tm
    cq_blk = (3 * MLSTM_W) // Q_LORA
    ckv_blk = (3 * MLSTM_W + Q_LORA) // KV_LORA
    kern = functools.partial(_qkv_kernel, scale=MLA_QK ** -0.5 * float(np.log2(np.e)), tk=tk)
    return pl.pallas_call(
        kern,
        grid=(t // tm, MLA_HEADS),
        in_specs=[pl.BlockSpec((tm, Q_LORA), lambda i, h: (i, cq_blk)),
                  pl.BlockSpec((tm, KV_LORA), lambda i, h: (i, ckv_blk)),
                  pl.BlockSpec((tm, MISC_W), lambda i, h: (i, 0)),
                  pl.BlockSpec((1, 1, tm), lambda i, h: (i, 0, 0)),
                  pl.BlockSpec(freqs.shape, lambda i, h: (0, 0)),
                  pl.BlockSpec((1, Q_LORA), lambda i, h: (0, 0)),
                  pl.BlockSpec((1, KV_LORA), lambda i, h: (0, 0)),
                  pl.BlockSpec((1, Q_LORA, 2 * LANES), lambda i, h: (h, 0, 0)),
                  pl.BlockSpec((1, KV_LORA, 2 * LANES), lambda i, h: (h, 0, 0))],
        out_specs=[pl.BlockSpec((1, 1, MLA_QK, tm), lambda i, h: (i // nsb, h, 0, i % nsb)),
                   pl.BlockSpec((1, 1, tm, MLA_QK), lambda i, h: (i // nsb, h, i % nsb, 0)),
                   pl.BlockSpec((1, 1, tm // tk, MLA_VA, tk),
                                lambda i, h: (i // nsb, h, i % nsb, 0, 0))],
        out_shape=[jax.ShapeDtypeStruct((bsz, MLA_HEADS, MLA_QK, seq), BF16),
                   jax.ShapeDtypeStruct((bsz, MLA_HEADS, seq, MLA_QK), BF16),
                   jax.ShapeDtypeStruct((bsz, MLA_HEADS, seq // tk, MLA_VA, tk), BF16)],
        scratch_shapes=[pltpu.VMEM((tm, Q_LORA), BF16),
                        pltpu.VMEM((tm, KV_LORA), BF16),
                        pltpu.VMEM((tm, LANES), F32),
                        pltpu.VMEM((tm, LANES), F32),
                        pltpu.VMEM((tm, LANES), F32)],
        compiler_params=_params(2, 32),
        name="mla_qkv",
    )(proj, proj, misc, pos, freqs, q_g.reshape(1, Q_LORA), kv_g.reshape(1, KV_LORA),
      wq_h, wkv_h)


def _flash_kernel(qt_ref, k_ref, vt_ref, g_ref, wa_ref, wb_ref, o_ref, wab_ref, wbb_ref,
                  m_ref, acc_ref, bias_ref, *, tq, tk, nq, wa_slabs, wb_slabs):
    qi = pl.program_id(2)

    step = (pl.program_id(0) * pl.num_programs(1) + pl.program_id(1)) * nq + qi

    @pl.when(step < wa_slabs)
    def _():
        wab_ref[...] = wa_ref[...].astype(BF16)

    @pl.when(step < wb_slabs)
    def _():
        wbb_ref[...] = wb_ref[...].astype(BF16)

    m_ref[...] = jnp.full(m_ref.shape, -jnp.inf, F32)
    acc_ref[...] = jnp.zeros(acc_ref.shape, F32)
    @pl.when((pl.program_id(0) == 0) & (pl.program_id(1) == 0) & (qi == 0))
    def _():
        key = lax.broadcasted_iota(jnp.int32, (tk, tq), 0)
        qry = lax.broadcasted_iota(jnp.int32, (tk, tq), 1)
        bias_ref[...] = jnp.where(key <= qry, 0.0, -jnp.inf).astype(F32)

    def scores(kj, c0, masked):
        k = k_ref[0, 0, kj * tk:(kj + 1) * tk, :]
        s = jnp.dot(k, qt_ref[0, 0, :, c0:], preferred_element_type=F32)
        if masked:
            s = s + bias_ref[:, 0:tq - c0]
        return s

    def update(s, kj, c0):
        m_prev = m_ref[:, c0:]
        m_new = jnp.maximum(m_prev, jnp.max(s, axis=0, keepdims=True))
        alpha = jnp.exp2(m_prev - m_new)
        p = jnp.exp2(s - m_new)
        acc_ref[:, c0:] = alpha * acc_ref[:, c0:] + jnp.dot(
            vt_ref[0, 0, kj], p.astype(BF16), preferred_element_type=F32)
        m_ref[:, c0:] = m_new

    per_q = tq // tk

    def run(q):
        def place(b):
            d = b - q * per_q
            return (0, False) if d < 0 else (d * tk, True)

        nblk = (q + 1) * per_q
        s = scores(0, *place(0))
        for b in range(nblk):
            s_next = scores(b + 1, *place(b + 1)) if b + 1 < nblk else None
            update(s, b, place(b)[0])
            s = s_next

    for q in range(nq):
        pl.when(qi == q)(functools.partial(run, q))

    o = acc_ref[0:MLA_V, :] / acc_ref[MLA_V:MLA_V + 1, :]
    o = o * lax.rsqrt(jnp.mean(o * o, axis=0, keepdims=True) + EPS) * g_ref[0]
    o_ref[...] = o.T.astype(BF16)


def _cast_slab_rows(rows, n_steps):
    bf16_rows = 16
    for slab in range(bf16_rows, rows + 1, bf16_rows):
        if rows % slab == 0 and rows // slab <= n_steps:
            return slab
    raise ValueError("no slab size fits")


def _mla_attention(qt, k, vt, out_g, wa, wb):
    bsz, nh, seq, _ = k.shape
    tq, tk = ATT_TQ, ATT_TK
    nq = seq // tq
    n_steps = bsz * nh * nq
    sa = _cast_slab_rows(wa.shape[0], n_steps)
    sb = _cast_slab_rows(wb.shape[0], n_steps)
    na, nb = wa.shape[0] // sa, wb.shape[0] // sb
    kern = functools.partial(_flash_kernel, tq=tq, tk=tk, nq=nq, wa_slabs=na, wb_slabs=nb)
    step = lambda b, h, i: (b * nh + h) * nq + i
    wa_map = lambda b, h, i: (jnp.minimum(step(b, h, i), na - 1), 0)
    wb_map = lambda b, h, i: (jnp.minimum(step(b, h, i), nb - 1), 0)
    return pl.pallas_call(
        kern,
        grid=(bsz, nh, nq),
        in_specs=[pl.BlockSpec((1, 1, MLA_QK, tq), lambda b, h, i: (b, h, 0, i)),
                  pl.BlockSpec((1, 1, seq, MLA_QK), lambda b, h, i: (b, h, 0, 0)),
                  pl.BlockSpec((1, 1, seq // tk, MLA_VA, tk), lambda b, h, i: (b, h, 0, 0, 0)),
                  pl.BlockSpec((1, MLA_V, 1), lambda b, h, i: (h, 0, 0)),
                  pl.BlockSpec((sa, wa.shape[1]), wa_map),
                  pl.BlockSpec((sb, wb.shape[1]), wb_map)],
        out_specs=[pl.BlockSpec((tq, MLA_V), lambda b, h, i: (b * nq + i, h)),
                   pl.BlockSpec((sa, wa.shape[1]), wa_map),
                   pl.BlockSpec((sb, wb.shape[1]), wb_map)],
        out_shape=[jax.ShapeDtypeStruct((bsz * seq, nh * MLA_V), BF16),
                   jax.ShapeDtypeStruct(wa.shape, BF16),
                   jax.ShapeDtypeStruct(wb.shape, BF16)],
        scratch_shapes=[pltpu.VMEM((1, tq), F32),
                        pltpu.VMEM((MLA_VA, tq), F32),
                        pltpu.VMEM((tk, tq), F32)],
        compiler_params=_params(3, 40),
        name="mla_attention",
    )(qt, k, vt, out_g.reshape(nh, MLA_V, 1), wa, wb)


def _outproj_kernel(hm_ref, ha_ref, w_ref, x_ref, mod_ref, g_ref, x1_ref, h2_ref, *, sub):
    km = hm_ref.shape[1]
    for s in range(hm_ref.shape[0] // sub):
        rows = pl.ds(s * sub, sub)
        mix = (jnp.dot(hm_ref[rows, :], w_ref[0:km, :], preferred_element_type=F32)
               + jnp.dot(ha_ref[rows, :], w_ref[km:, :], preferred_element_type=F32))
        x1 = x_ref[rows, :] + mod_ref[0, 2:3, :] * mix
        x1_ref[rows, :] = x1
        y = _rms(x1) * g_ref[...]
        h2_ref[rows, :] = (y * (1.0 + mod_ref[0, 4:5, :]) + mod_ref[0, 3:4, :]).astype(BF16)


def _out_projection(hm, ha, w_out, x2, mod, g, seq):
    t, d = x2.shape
    tm = 512
    per_b = seq // tm
    return pl.pallas_call(
        functools.partial(_outproj_kernel, sub=256),
        grid=(t // tm,),
        in_specs=[pl.BlockSpec((tm, hm.shape[1]), lambda i: (i, 0)),
                  pl.BlockSpec((tm, ha.shape[1]), lambda i: (i, 0)),
                  pl.BlockSpec(w_out.shape, lambda i: (0, 0), pipeline_mode=pl.Buffered(1)),
                  pl.BlockSpec((tm, d), lambda i: (i, 0)),
                  pl.BlockSpec((1, 6, d), lambda i: (i // per_b, 0, 0)),
                  pl.BlockSpec((1, d), lambda i: (0, 0))],
        out_specs=[pl.BlockSpec((tm, d), lambda i: (i, 0)),
                   pl.BlockSpec((tm, d), lambda i: (i, 0))],
        out_shape=[jax.ShapeDtypeStruct((t, d), F32),
                   jax.ShapeDtypeStruct((t, d), BF16)],
        compiler_params=_params(1, 48),
        name="out_proj",
    )(hm, ha, w_out, x2, mod, g.reshape(1, d))


def _ffn_up_kernel(h_ref, wv_ref, wg_ref, cwv_ref, cwg_ref, cbv_ref, cbg_ref, o_ref,
                   xv_ref, xg_ref, wvb_ref, wgb_ref, *, tm, blocks_per_seq):
    i = pl.program_id(1)

    @pl.when(i % blocks_per_seq == 0)
    def _():
        xv_ref[0:8, :] = jnp.zeros((8, xv_ref.shape[1]), F32)
        xg_ref[0:8, :] = jnp.zeros((8, xg_ref.shape[1]), F32)

    @pl.when(i == 0)
    def _():
        wvb_ref[...] = wv_ref[...].astype(BF16)
        wgb_ref[...] = wg_ref[...].astype(BF16)

    h = h_ref[...]

    def conv(w_ref, cw_ref, cb_ref, xs_ref):
        kw = cw_ref.shape[0]
        up = jnp.dot(h, w_ref[...], preferred_element_type=F32)
        xs_ref[8:8 + tm, :] = up
        y = cb_ref[...] + cw_ref[kw - 1:kw, :] * up
        for j in range(kw - 1):
            y = y + cw_ref[j:j + 1, :] * xs_ref[pl.ds(8 - (kw - 1) + j, tm), :]
        xs_ref[0:8, :] = xs_ref[tm:tm + 8, :]
        return y

    val = conv(wvb_ref, cwv_ref, cbv_ref, xv_ref)
    gate = conv(wgb_ref, cwg_ref, cbg_ref, xg_ref)
    o_ref[...] = (_silu(gate) * val).astype(BF16)


def _ffn_up(h2, w_up, conv_w, conv_b, seq):
    t, d = h2.shape
    f = w_up.shape[1] // 2
    tm, tf = 1024, 512
    nf = f // tf
    kw = conv_w.shape[0]
    kern = functools.partial(_ffn_up_kernel, tm=tm, blocks_per_seq=seq // tm)
    cb = conv_b.reshape(1, 2 * f)
    return pl.pallas_call(
        kern,
        grid=(nf, t // tm),
        in_specs=[pl.BlockSpec((tm, d), lambda j, i: (i, 0)),
                  pl.BlockSpec((d, tf), lambda j, i: (0, j)),
                  pl.BlockSpec((d, tf), lambda j, i: (0, nf + j)),
                  pl.BlockSpec((kw, tf), lambda j, i: (0, j)),
                  pl.BlockSpec((kw, tf), lambda j, i: (0, nf + j)),
                  pl.BlockSpec((1, tf), lambda j, i: (0, j)),
                  pl.BlockSpec((1, tf), lambda j, i: (0, nf + j))],
        out_specs=pl.BlockSpec((tm, tf), lambda j, i: (i, j)),
        out_shape=jax.ShapeDtypeStruct((t, f), BF16),
        scratch_shapes=[pltpu.VMEM((tm + 8, tf), F32),
                        pltpu.VMEM((tm + 8, tf), F32),
                        pltpu.VMEM((d, tf), BF16),
                        pltpu.VMEM((d, tf), BF16)],
        compiler_params=_params(2, 56),
        name="ffn_up",
    )(h2, w_up, w_up, conv_w, conv_w, cb, cb)


def _ffn_down_kernel(a_ref, w_ref, x1_ref, mod_ref, g_ref, o_ref, *, sub):
    for s in range(a_ref.shape[0] // sub):
        rows = pl.ds(s * sub, sub)
        y = jnp.dot(a_ref[rows, :], w_ref[...], preferred_element_type=F32)
        x2 = x1_ref[rows, :] + mod_ref[0, 5:6, :] * y
        o_ref[rows, :] = _rms(x2) * g_ref[...]


def _ffn_down(act, w_down, x1, mod, g, seq):
    t, f = act.shape
    d = w_down.shape[1]
    tm = 512
    per_b = seq // tm
    return pl.pallas_call(
        functools.partial(_ffn_down_kernel, sub=256),
        grid=(t // tm,),
        in_specs=[pl.BlockSpec((tm, f), lambda i: (i, 0)),
                  pl.BlockSpec((f, d), lambda i: (0, 0), pipeline_mode=pl.Buffered(1)),
                  pl.BlockSpec((tm, d), lambda i: (i, 0)),
                  pl.BlockSpec((1, 6, d), lambda i: (i // per_b, 0, 0)),
                  pl.BlockSpec((1, d), lambda i: (0, 0))],
        out_specs=pl.BlockSpec((tm, d), lambda i: (i, 0)),
        out_shape=jax.ShapeDtypeStruct((t, d), F32),
        compiler_params=_params(1, 58),
        name="ffn_down",
    )(act, w_down, x1, mod, g.reshape(1, d))


def _rot_cols(w):
    half = w.shape[-1] // 2
    return jnp.concatenate([-w[..., half:], w[..., :half]], axis=-1)


def _win_prep_kernel(w_ref, tail_ref, o_ref, *, n_main):
    j = pl.program_id(0)
    gates = 2 * MLSTM_HEADS
    half = MLA_ROPE // 2

    @pl.when(j < n_main)
    def _():
        o_ref[...] = w_ref[...].T.astype(BF16)

    @pl.when(j == n_main)
    def _():
        o_ref[...] = tail_ref[gates:gates + Q_LORA, :].T.astype(BF16)

    @pl.when(j == n_main + 1)
    def _():
        r0 = gates + Q_LORA
        kr0 = r0 + KV_LORA
        blk = o_ref.shape[1]
        used = KV_LORA + 2 * MLA_ROPE + gates
        o_ref[...] = jnp.concatenate(
            [tail_ref[r0:kr0, :],
             tail_ref[kr0:kr0 + MLA_ROPE, :],
             -tail_ref[kr0 + half:kr0 + MLA_ROPE, :],
             tail_ref[kr0:kr0 + half, :],
             tail_ref[0:gates, :],
             jnp.zeros((blk - used, o_ref.shape[0]), F32)], axis=0).T.astype(BF16)


def _prep_w_in(w_in):
    d = w_in.shape[0]
    wt = w_in.T
    n_head = 3 * MLSTM_W
    blk = Q_LORA
    n_main = n_head // blk
    return pl.pallas_call(
        functools.partial(_win_prep_kernel, n_main=n_main),
        grid=(IN_PAD // blk,),
        in_specs=[pl.BlockSpec((blk, d), lambda j: (jnp.minimum(j, n_main - 1), 0)),
                  pl.BlockSpec((wt.shape[0] - n_head, d), lambda j: (0, 0))],
        out_specs=pl.BlockSpec((d, blk), lambda j: (0, j)),
        out_shape=jax.ShapeDtypeStruct((d, IN_PAD), BF16),
        compiler_params=_params(1, 40),
        name="w_in_prep",
    )(wt, wt[n_head:])


def kernel(x, c, positions, ada_w, ada_b, attn_norm_g, w_in, mlstm_conv_w, mlstm_conv_b, mlstm_wq, mlstm_wk, mlstm_igate_b, mlstm_fgate_b, mla_q_norm_g, mla_w_uq, mla_kv_norm_g, mla_w_ukv, mlstm_out_g, mla_out_g, w_out, ffn_norm_g, ffn_w_up, ffn_conv_w, ffn_conv_b, ffn_w_down, final_norm_g):
    bsz, seq, d = x.shape
    t = bsz * seq
    depth = ada_w.shape[0]
    assert depth == 1, "the final RMSNorm is fused into the single layer's down-projection"
    xr = x.reshape(t, d)
    pos = positions.reshape(t // ATT_TQ, 1, ATT_TQ)
    half = MLA_ROPE // 2
    freqs = (ROPE_THETA ** (-jnp.arange(half, dtype=F32) / half)).reshape(half, 1)

    for l in range(depth):
        mod = _modulation(c, ada_w[l], ada_b[l])

        proj, misc = _in_projection(xr, mod, attn_norm_g[l], _prep_w_in(w_in[l]), seq)

        wq_m = mlstm_wq[l].astype(BF16)
        wkt_m = mlstm_wk[l].transpose(0, 2, 1).astype(BF16)
        gate_b = jnp.zeros((1, LANES), F32)
        gate_b = gate_b.at[0, :MLSTM_HEADS].set(mlstm_igate_b[l])
        gate_b = gate_b.at[0, MLSTM_HEADS:2 * MLSTM_HEADS].set(mlstm_fgate_b[l])
        hm = _mlstm(proj, misc, mlstm_conv_w[l], mlstm_conv_b[l], wq_m, wkt_m, gate_b,
                    mlstm_out_g[l], bsz, seq)

        wq = mla_w_uq[l].reshape(Q_LORA, MLA_HEADS, MLA_QK)
        wq_r = wq[..., MLA_NOPE:]
        wq_h = jnp.concatenate([wq[..., :MLA_NOPE], wq_r, _rot_cols(wq_r)], axis=-1)
        wq_h = wq_h.transpose(1, 0, 2).astype(BF16)
        wkv_h = mla_w_ukv[l].reshape(KV_LORA, MLA_HEADS, MLA_NOPE + MLA_V)
        wkv_h = wkv_h.transpose(1, 0, 2).astype(BF16)
        qt, k, vt = _mla_qkv(proj, misc, pos, freqs, mla_q_norm_g[l], mla_kv_norm_g[l],
                             wq_h, wkv_h, bsz, seq)
        ha, w_down_b, w_out_b = _mla_attention(qt, k, vt, mla_out_g[l], ffn_w_down[l], w_out[l])

        x1, h2 = _out_projection(hm, ha, w_out_b, xr, mod, ffn_norm_g[l], seq)

        act = _ffn_up(h2, ffn_w_up[l], ffn_conv_w[l], ffn_conv_b[l], seq)
        xr = _ffn_down(act, w_down_b, x1, mod, final_norm_g, seq)
    return xr.reshape(bsz, seq, d)
```

```python
import functools

import jax
import jax.numpy as jnp
import numpy as np
from jax import lax
from jax.experimental import pallas as pl
from jax.experimental.pallas import tpu as pltpu

F32 = jnp.float32
BF16 = jnp.bfloat16

EPS = 1e-6
ROPE_THETA = 10000.0
MLSTM_HEADS = 4
MLSTM_DH = 256
MLSTM_W = MLSTM_HEADS * MLSTM_DH
MLA_HEADS = 8
MLA_NOPE = 128
MLA_ROPE = 64
MLA_V = 128
MLA_QK = MLA_NOPE + MLA_ROPE
Q_LORA = 512
KV_LORA = 256
LANES = 128
MIB = 1024 * 1024

IN_PAD = 4096
IN_BLOCK = 1024
MISC_W = 256
MISC_OFF = IN_BLOCK - MISC_W


def _params(n_axes, vmem_mib):
    return pltpu.CompilerParams(
        dimension_semantics=("arbitrary",) * n_axes,
        vmem_limit_bytes=vmem_mib * MIB)


def _rms(x):
    return x * lax.rsqrt(jnp.mean(x * x, axis=-1, keepdims=True) + EPS)


def _silu(x):
    return x / (1.0 + jnp.exp(-x))


def _log_sigmoid(x):
    return jnp.minimum(x, 0.0) - jnp.log1p(jnp.exp(-jnp.abs(x)))


def _mod_kernel(c_ref, w_ref, b_ref, o_ref):
    ca = _silu(c_ref[...]).astype(BF16)
    o_ref[...] = jnp.dot(ca, w_ref[...].astype(BF16),
                         preferred_element_type=F32) + b_ref[...]


def _modulation(c, ada_w, ada_b):
    bsz, d = c.shape
    n = ada_w.shape[1]
    tn = 1024
    cp = jnp.zeros((8, d), F32).at[:bsz].set(c)
    out = pl.pallas_call(
        _mod_kernel,
        grid=(n // tn,),
        in_specs=[pl.BlockSpec((8, d), lambda j: (0, 0)),
                  pl.BlockSpec((d, tn), lambda j: (0, j)),
                  pl.BlockSpec((1, tn), lambda j: (0, j))],
        out_specs=pl.BlockSpec((8, tn), lambda j: (0, j)),
        out_shape=jax.ShapeDtypeStruct((8, n), F32),
        compiler_params=_params(1, 40),
        name="adaln_mod",
    )(cp, ada_w, ada_b.reshape(1, n))
    return out[:bsz].reshape(bsz, 6, d)


def _inproj_kernel(x_ref, mod_ref, g_ref, w_ref, o_ref, misc_ref, h_scr, *, sub):
    j = pl.program_id(1)
    last = pl.num_programs(1) - 1
    tm = x_ref.shape[0]

    def project(rows):
        acc = jnp.dot(h_scr[rows, :], w_ref[...], preferred_element_type=F32)
        o_ref[rows, :] = acc.astype(BF16)
        return acc

    @pl.when(j == 0)
    def _():
        for s in range(tm // sub):
            rows = pl.ds(s * sub, sub)
            y = _rms(x_ref[rows, :]) * g_ref[...]
            h_scr[rows, :] = (y * (1.0 + mod_ref[0, 1:2, :]) + mod_ref[0, 0:1, :]).astype(BF16)
            project(rows)

    @pl.when(j > 0)
    def _():
        for s in range(tm // sub):
            rows = pl.ds(s * sub, sub)
            acc = project(rows)

            @pl.when(j == last)
            def _():
                misc_ref[rows, :] = acc[:, MISC_OFF:]


def _in_projection(x2, mod, g, w_in_p, seq):
    t, d = x2.shape
    tm = 1024
    per_b = seq // tm
    return pl.pallas_call(
        functools.partial(_inproj_kernel, sub=512),
        grid=(t // tm, IN_PAD // IN_BLOCK),
        in_specs=[pl.BlockSpec((tm, d), lambda i, j: (i, 0)),
                  pl.BlockSpec((1, 6, d), lambda i, j: (i // per_b, 0, 0)),
                  pl.BlockSpec((1, d), lambda i, j: (0, 0)),
                  pl.BlockSpec((d, IN_BLOCK), lambda i, j: (0, j))],
        out_specs=[pl.BlockSpec((tm, IN_BLOCK), lambda i, j: (i, j)),
                   pl.BlockSpec((tm, MISC_W), lambda i, j: (i, 0))],
        out_shape=[jax.ShapeDtypeStruct((t, IN_PAD), BF16),
                   jax.ShapeDtypeStruct((t, MISC_W), F32)],
        scratch_shapes=[pltpu.VMEM((tm, d), BF16)],
        compiler_params=_params(2, 48),
        name="norm_inproj",
    )(x2, mod, g.reshape(1, d), w_in_p)


def _cumsum_rows(x):
    n = x.shape[0]
    row = lax.broadcasted_iota(jnp.int32, x.shape, 0)
    sh = 1
    while sh < n:
        x = x + jnp.where(row >= sh, pltpu.roll(x, sh, axis=0), 0.0)
        sh *= 2
    return x


def _cummax_rows(x):
    n = x.shape[0]
    row = lax.broadcasted_iota(jnp.int32, x.shape, 0)
    sh = 1
    while sh < n:
        x = jnp.maximum(x, jnp.where(row >= sh, pltpu.roll(x, sh, axis=0), -jnp.inf))
        sh *= 2
    return x


def _cumsum_lanes(x):
    n = x.shape[1]
    col = lax.broadcasted_iota(jnp.int32, x.shape, 1)
    sh = 1
    while sh < n:
        x = x + jnp.where(col >= sh, pltpu.roll(x, sh, axis=1), 0.0)
        sh *= 2
    return x


def _mlstm_kernel(u_ref, v_ref, o_ref, gt_ref, cw_ref, cb_ref, wq_ref, wkt_ref, gb_ref, og_ref,
                  out_ref, xs_ref, ct_ref, n_ref, m_ref, *, ts, chunk):
    i = pl.program_id(0)
    nbatch = u_ref.shape[0]

    @pl.when(i == 0)
    def _():
        for b in range(nbatch):
            xs_ref[b, 0:8, :] = jnp.zeros((8, MLSTM_W), F32)
        ct_ref[...] = jnp.zeros(ct_ref.shape, F32)
        n_ref[...] = jnp.zeros(n_ref.shape, F32)
        m_ref[...] = jnp.zeros(m_ref.shape, F32)

    row = lax.broadcasted_iota(jnp.int32, (chunk, chunk), 0)
    col = lax.broadcasted_iota(jnp.int32, (chunk, chunk), 1)
    causal = col <= row

    for c in range(ts // chunk):
        for b in range(nbatch):
            _mlstm_chunk(u_ref.at[b], v_ref.at[b], o_ref.at[b], gt_ref.at[b], cw_ref, cb_ref,
                         wq_ref, wkt_ref, gb_ref, og_ref, out_ref.at[b], xs_ref.at[b],
                         ct_ref.at[b], n_ref.at[b], m_ref.at[b], c, chunk, causal)
    for b in range(nbatch):
        xs_ref[b, 0:8, :] = xs_ref[b, ts:ts + 8, :]


def _mlstm_chunk(u_ref, v_ref, o_ref, gt_ref, cw_ref, cb_ref, wq_ref, wkt_ref, gb_ref, og_ref,
                 out_ref, xs_ref, ct_ref, n_ref, m_ref, c, chunk, causal):
    nh, dh = MLSTM_HEADS, MLSTM_DH
    gb = gb_ref[...]
    kw = cw_ref.shape[0]
    ones = jnp.ones((chunk, LANES), BF16)
    if True:
        r0 = c * chunk
        xc = u_ref[r0:r0 + chunk, :].astype(F32)
        xs_ref[8 + r0:8 + r0 + chunk, :] = xc
        acc = cb_ref[...] + cw_ref[kw - 1:kw, :] * xc
        for j in range(kw - 1):
            acc = acc + cw_ref[j:j + 1, :] * xs_ref[pl.ds(8 + r0 - (kw - 1) + j, chunk), :]
        su = _silu(acc)

        gc = gt_ref[r0:r0 + chunk, MISC_W - LANES:] + gb
        bc = _cumsum_rows(_log_sigmoid(gc))
        gt = gc.T[0:8, :]
        bt = _cumsum_lanes(_log_sigmoid(gt))
        mdc = _cummax_rows(gc - pltpu.roll(bc, LANES - nh, axis=1))
        for h in range(nh):
            c0 = h * dh
            b_c = bc[:, nh + h:nh + h + 1]
            li_r = gt[h:h + 1, :]
            b_r = bt[nh + h:nh + h + 1, :]
            g_tot = b_r[:, chunk - 1:chunk]
            z_r = li_r - b_r
            zmax = jnp.max(z_r, axis=-1, keepdims=True)
            m_prev = m_ref[h:h + 1, 0:1]

            uh = su[:, c0:c0 + dh]
            qb = jnp.dot(uh.astype(BF16), wq_ref[h], preferred_element_type=F32).astype(BF16)
            kt = jnp.dot(wkt_ref[h], uh.T.astype(BF16),
                         preferred_element_type=F32) * (dh ** -0.5)
            vc = v_ref[r0:r0 + chunk, c0:c0 + dh]

            mx_c = jnp.maximum(m_prev, mdc[:, h:h + 1])
            m_t = b_c + mx_c
            sb = (jnp.dot(qb, kt.astype(BF16), preferred_element_type=F32)
                  * jnp.exp(jnp.where(causal, z_r - mx_c, -jnp.inf))).astype(BF16)
            inter = jnp.exp(m_prev - mx_c)
            ct = ct_ref[h]
            nrep = n_ref[h]
            num = (jnp.dot(sb, vc, preferred_element_type=F32)
                   + inter * jnp.dot(qb, ct.astype(BF16), preferred_element_type=F32))
            den = (jnp.dot(sb, ones, preferred_element_type=F32)
                   + inter * jnp.dot(qb, nrep.astype(BF16), preferred_element_type=F32))
            lim = jnp.maximum(jnp.abs(den), jnp.exp(-m_t))
            hh = num / jnp.concatenate([lim] * (dh // LANES), axis=1)

            mm = jnp.maximum(m_prev, zmax)
            s_prev = jnp.exp(m_prev - mm)
            s_loc = jnp.exp(zmax - mm)
            kwt = (kt * (jnp.exp(z_r - zmax) * s_loc)).astype(BF16)
            ct_ref[h] = s_prev * ct + jnp.dot(kwt, vc, preferred_element_type=F32)
            n_ref[h] = s_prev * nrep + jnp.dot(kwt, ones, preferred_element_type=F32)
            m_ref[h:h + 1, :] = jnp.broadcast_to(g_tot + mm, (1, LANES))

            og = o_ref[r0:r0 + chunk, c0:c0 + dh].astype(F32)
            hh = hh / (1.0 + jnp.exp(-og))
            out_ref[r0:r0 + chunk, c0:c0 + dh] = (
                _rms(hh) * og_ref[:, c0:c0 + dh]).astype(BF16)


def _mlstm(proj, misc, conv_w, conv_b, wq, wkt, gate_b, out_g, bsz, seq):
    ts, chunk = 512, 256
    t = bsz * seq
    kern = functools.partial(_mlstm_kernel, ts=ts, chunk=chunk)
    proj3 = proj.reshape(bsz, seq, proj.shape[1])
    misc3 = misc.reshape(bsz, seq, MISC_W)
    out = pl.pallas_call(
        kern,
        grid=(seq // ts,),
        in_specs=[pl.BlockSpec((bsz, ts, MLSTM_W), lambda i: (0, i, 0)),
                  pl.BlockSpec((bsz, ts, MLSTM_W), lambda i: (0, i, 1)),
                  pl.BlockSpec((bsz, ts, MLSTM_W), lambda i: (0, i, 2)),
                  pl.BlockSpec((bsz, ts, MISC_W), lambda i: (0, i, 0)),
                  pl.BlockSpec(conv_w.shape, lambda i: (0, 0)),
                  pl.BlockSpec((1, MLSTM_W), lambda i: (0, 0)),
                  pl.BlockSpec(wq.shape, lambda i: (0, 0, 0)),
                  pl.BlockSpec(wkt.shape, lambda i: (0, 0, 0)),
                  pl.BlockSpec((1, LANES), lambda i: (0, 0)),
                  pl.BlockSpec((1, MLSTM_W), lambda i: (0, 0))],
        out_specs=pl.BlockSpec((bsz, ts, MLSTM_W), lambda i: (0, i, 0)),
        out_shape=jax.ShapeDtypeStruct((bsz, seq, MLSTM_W), BF16),
        scratch_shapes=[pltpu.VMEM((bsz, ts + 8, MLSTM_W), F32),
                        pltpu.VMEM((bsz, MLSTM_HEADS, MLSTM_DH, MLSTM_DH), F32),
                        pltpu.VMEM((bsz, MLSTM_HEADS, MLSTM_DH, LANES), F32),
                        pltpu.VMEM((bsz, 8, LANES), F32)],
        compiler_params=_params(1, 48),
        name="mlstm",
    )(proj3, proj3, proj3, misc3, conv_w, conv_b.reshape(1, MLSTM_W), wq, wkt, gate_b,
      out_g.reshape(1, MLSTM_W))
    return out.reshape(t, MLSTM_W)


ATT_TQ = 1024
ATT_TK = 256
MLA_VA = MLA_V + 16


def _qkv_kernel(cq_ref, ckv_ref, misc_ref, pos_ref, fr_ref, qg_ref, kvg_ref, wq_ref, wkv_ref,
                qt_ref, k_ref, vt_ref, cqn, ckvn, cos_s, sin_s, kr_s, *, scale, tk):
    h = pl.program_id(1)

    @pl.when(h == 0)
    def _():
        cqn[...] = (_rms(cq_ref[...].astype(F32)) * qg_ref[...]).astype(BF16)
        ckvn[...] = (_rms(ckv_ref[...].astype(F32)) * kvg_ref[...]).astype(BF16)
        ang = fr_ref[...] * pos_ref[0].astype(F32)
        reps = LANES // ang.shape[0]
        cs = jnp.concatenate([jnp.cos(ang)] * reps, axis=0).T
        sn = jnp.concatenate([jnp.sin(ang)] * reps, axis=0).T
        cos_s[...] = cs
        sin_s[...] = sn
        y = misc_ref[:, 0:LANES]
        kr_s[...] = y * cs + pltpu.roll(y, MLA_ROPE, axis=1) * sn

    mq = jnp.dot(cqn[...], wq_ref[0], preferred_element_type=F32)
    qt_ref[0, 0, 0:MLA_NOPE, :] = (mq[:, 0:MLA_NOPE] * scale).T.astype(BF16)
    y = mq[:, MLA_NOPE:]
    r = (y * cos_s[...] + pltpu.roll(y, MLA_ROPE, axis=1) * sin_s[...]) * scale
    qt_ref[0, 0, MLA_NOPE:MLA_QK, :] = r.T[0:MLA_ROPE, :].astype(BF16)

    mkv = jnp.dot(ckvn[...], wkv_ref[0], preferred_element_type=F32)
    k_ref[0, 0, :, 0:MLA_NOPE] = mkv[:, 0:MLA_NOPE].astype(BF16)
    k_ref[0, 0, :, MLA_NOPE:MLA_QK] = kr_s[:, 0:MLA_ROPE].astype(BF16)
    vt = mkv[:, MLA_NOPE:].T.astype(BF16)
    for j in range(vt.shape[1] // tk):
        vt_ref[0, 0, j, 0:MLA_V, :] = vt[:, j * tk:(j + 1) * tk]
        vt_ref[0, 0, j, MLA_V:, :] = jnp.ones((MLA_VA - MLA_V, tk), BF16)


def _mla_qkv(proj, misc, pos, freqs, q_g, kv_g, wq_h, wkv_h, bsz, seq):
    tm, tk = ATT_TQ, ATT_TK
    t = bsz * seq
    nsb = seq // tm
    cq_blk = (3 * MLSTM_W) // Q_LORA
    ckv_blk = (3 * MLSTM_W + Q_LORA) // KV_LORA
    kern = functools.partial(_qkv_kernel, scale=MLA_QK ** -0.5 * float(np.log2(np.e)), tk=tk)
    return pl.pallas_call(
        kern,
        grid=(t // tm, MLA_HEADS),
        in_specs=[pl.BlockSpec((tm, Q_LORA), lambda i, h: (i, cq_blk)),
                  pl.BlockSpec((tm, KV_LORA), lambda i, h: (i, ckv_blk)),
                  pl.BlockSpec((tm, MISC_W), lambda i, h: (i, 0)),
                  pl.BlockSpec((1, 1, tm), lambda i, h: (i, 0, 0)),
                  pl.BlockSpec(freqs.shape, lambda i, h: (0, 0)),
                  pl.BlockSpec((1, Q_LORA), lambda i, h: (0, 0)),
                  pl.BlockSpec((1, KV_LORA), lambda i, h: (0, 0)),
                  pl.BlockSpec((1, Q_LORA, 2 * LANES), lambda i, h: (h, 0, 0)),
                  pl.BlockSpec((1, KV_LORA, 2 * LANES), lambda i, h: (h, 0, 0))],
        out_specs=[pl.BlockSpec((1, 1, MLA_QK, tm), lambda i, h: (i // nsb, h, 0, i % nsb)),
                   pl.BlockSpec((1, 1, tm, MLA_QK), lambda i, h: (i // nsb, h, i % nsb, 0)),
                   pl.BlockSpec((1, 1, tm // tk, MLA_VA, tk),
                                lambda i, h: (i // nsb, h, i % nsb, 0, 0))],
        out_shape=[jax.ShapeDtypeStruct((bsz, MLA_HEADS, MLA_QK, seq), BF16),
                   jax.ShapeDtypeStruct((bsz, MLA_HEADS, seq, MLA_QK), BF16),
                   jax.ShapeDtypeStruct((bsz, MLA_HEADS, seq // tk, MLA_VA, tk), BF16)],
        scratch_shapes=[pltpu.VMEM((tm, Q_LORA), BF16),
                        pltpu.VMEM((tm, KV_LORA), BF16),
                        pltpu.VMEM((tm, LANES), F32),
                        pltpu.VMEM((tm, LANES), F32),
                        pltpu.VMEM((tm, LANES), F32)],
        compiler_params=_params(2, 32),
        name="mla_qkv",
    )(proj, proj, misc, pos, freqs, q_g.reshape(1, Q_LORA), kv_g.reshape(1, KV_LORA),
      wq_h, wkv_h)


def _flash_kernel(qt_ref, k_ref, vt_ref, g_ref, wa_ref, wb_ref, o_ref, wab_ref, wbb_ref,
                  m_ref, acc_ref, bias_ref, *, tq, tk, nq, wa_slabs, wb_slabs):
    qi = pl.program_id(2)

    step = (pl.program_id(0) * pl.num_programs(1) + pl.program_id(1)) * nq + qi

    @pl.when(step < wa_slabs)
    def _():
        wab_ref[...] = wa_ref[...].astype(BF16)

    @pl.when(step < wb_slabs)
    def _():
        wbb_ref[...] = wb_ref[...].astype(BF16)

    m_ref[...] = jnp.full(m_ref.shape, -jnp.inf, F32)
    acc_ref[...] = jnp.zeros(acc_ref.shape, F32)
    @pl.when((pl.program_id(0) == 0) & (pl.program_id(1) == 0) & (qi == 0))
    def _():
        key = lax.broadcasted_iota(jnp.int32, (tk, tq), 0)
        qry = lax.broadcasted_iota(jnp.int32, (tk, tq), 1)
        bias_ref[...] = jnp.where(key <= qry, 0.0, -jnp.inf).astype(F32)

    def scores(kj, c0, masked):
        k = k_ref[0, 0, kj * tk:(kj + 1) * tk, :]
        s = jnp.dot(k, qt_ref[0, 0, :, c0:], preferred_element_type=F32)
        if masked:
            s = s + bias_ref[:, 0:tq - c0]
        return s

    def update(s, kj, c0):
        m_prev = m_ref[:, c0:]
        m_new = jnp.maximum(m_prev, jnp.max(s, axis=0, keepdims=True))
        alpha = jnp.exp2(m_prev - m_new)
        p = jnp.exp2(s - m_new)
        acc_ref[:, c0:] = alpha * acc_ref[:, c0:] + jnp.dot(
            vt_ref[0, 0, kj], p.astype(BF16), preferred_element_type=F32)
        m_ref[:, c0:] = m_new

    per_q = tq // tk

    def run(q):
        def place(b):
            d = b - q * per_q
            return (0, False) if d < 0 else (d * tk, True)

        nblk = (q + 1) * per_q
        s = scores(0, *place(0))
        for b in range(nblk):
            s_next = scores(b + 1, *place(b + 1)) if b + 1 < nblk else None
            update(s, b, place(b)[0])
            s = s_next

    for q in range(nq):
        pl.when(qi == q)(functools.partial(run, q))

    o = acc_ref[0:MLA_V, :] / acc_ref[MLA_V:MLA_V + 1, :]
    o = o * lax.rsqrt(jnp.mean(o * o, axis=0, keepdims=True) + EPS) * g_ref[0]
    o_ref[...] = o.T.astype(BF16)


def _cast_slab_rows(rows, n_steps):
    bf16_rows = 16
    for slab in range(bf16_rows, rows + 1, bf16_rows):
        if rows % slab == 0 and rows // slab <= n_steps:
            return slab
    raise ValueError("no slab size fits")


def _mla_attention(qt, k, vt, out_g, wa, wb):
    bsz, nh, seq, _ = k.shape
    tq, tk = ATT_TQ, ATT_TK
    nq = seq // tq
    n_steps = bsz * nh * nq
    sa = _cast_slab_rows(wa.shape[0], n_steps)
    sb = _cast_slab_rows(wb.shape[0], n_steps)
    na, nb = wa.shape[0] // sa, wb.shape[0] // sb
    kern = functools.partial(_flash_kernel, tq=tq, tk=tk, nq=nq, wa_slabs=na, wb_slabs=nb)
    step = lambda b, h, i: (b * nh + h) * nq + i
    wa_map = lambda b, h, i: (jnp.minimum(step(b, h, i), na - 1), 0)
    wb_map = lambda b, h, i: (jnp.minimum(step(b, h, i), nb - 1), 0)
    return pl.pallas_call(
        kern,
        grid=(bsz, nh, nq),
        in_specs=[pl.BlockSpec((1, 1, MLA_QK, tq), lambda b, h, i: (b, h, 0, i)),
                  pl.BlockSpec((1, 1, seq, MLA_QK), lambda b, h, i: (b, h, 0, 0)),
                  pl.BlockSpec((1, 1, seq // tk, MLA_VA, tk), lambda b, h, i: (b, h, 0, 0, 0)),
                  pl.BlockSpec((1, MLA_V, 1), lambda b, h, i: (h, 0, 0)),
                  pl.BlockSpec((sa, wa.shape[1]), wa_map),
                  pl.BlockSpec((sb, wb.shape[1]), wb_map)],
        out_specs=[pl.BlockSpec((tq, MLA_V), lambda b, h, i: (b * nq + i, h)),
                   pl.BlockSpec((sa, wa.shape[1]), wa_map),
                   pl.BlockSpec((sb, wb.shape[1]), wb_map)],
        out_shape=[jax.ShapeDtypeStruct((bsz * seq, nh * MLA_V), BF16),
                   jax.ShapeDtypeStruct(wa.shape, BF16),
                   jax.ShapeDtypeStruct(wb.shape, BF16)],
        scratch_shapes=[pltpu.VMEM((1, tq), F32),
                        pltpu.VMEM((MLA_VA, tq), F32),
                        pltpu.VMEM((tk, tq), F32)],
        compiler_params=_params(3, 40),
        name="mla_attention",
    )(qt, k, vt, out_g.reshape(nh, MLA_V, 1), wa, wb)


def _outproj_kernel(hm_ref, ha_ref, w_ref, x_ref, mod_ref, g_ref, x1_ref, h2_ref, *, sub):
    km = hm_ref.shape[1]
    for s in range(hm_ref.shape[0] // sub):
        rows = pl.ds(s * sub, sub)
        mix = (jnp.dot(hm_ref[rows, :], w_ref[0:km, :], preferred_element_type=F32)
               + jnp.dot(ha_ref[rows, :], w_ref[km:, :], preferred_element_type=F32))
        x1 = x_ref[rows, :] + mod_ref[0, 2:3, :] * mix
        x1_ref[rows, :] = x1
        y = _rms(x1) * g_ref[...]
        h2_ref[rows, :] = (y * (1.0 + mod_ref[0, 4:5, :]) + mod_ref[0, 3:4, :]).astype(BF16)


def _out_projection(hm, ha, w_out, x2, mod, g, seq):
    t, d = x2.shape
    tm = 512
    per_b = seq // tm
    return pl.pallas_call(
        functools.partial(_outproj_kernel, sub=256),
        grid=(t // tm,),
        in_specs=[pl.BlockSpec((tm, hm.shape[1]), lambda i: (i, 0)),
                  pl.BlockSpec((tm, ha.shape[1]), lambda i: (i, 0)),
                  pl.BlockSpec(w_out.shape, lambda i: (0, 0), pipeline_mode=pl.Buffered(1)),
                  pl.BlockSpec((tm, d), lambda i: (i, 0)),
                  pl.BlockSpec((1, 6, d), lambda i: (i // per_b, 0, 0)),
                  pl.BlockSpec((1, d), lambda i: (0, 0))],
        out_specs=[pl.BlockSpec((tm, d), lambda i: (i, 0)),
                   pl.BlockSpec((tm, d), lambda i: (i, 0))],
        out_shape=[jax.ShapeDtypeStruct((t, d), F32),
                   jax.ShapeDtypeStruct((t, d), BF16)],
        compiler_params=_params(1, 48),
        name="out_proj",
    )(hm, ha, w_out, x2, mod, g.reshape(1, d))


def _ffn_up_kernel(h_ref, wv_ref, wg_ref, cwv_ref, cwg_ref, cbv_ref, cbg_ref, o_ref,
                   xv_ref, xg_ref, wvb_ref, wgb_ref, *, tm, blocks_per_seq):
    i = pl.program_id(1)

    @pl.when(i % blocks_per_seq == 0)
    def _():
        xv_ref[0:8, :] = jnp.zeros((8, xv_ref.shape[1]), F32)
        xg_ref[0:8, :] = jnp.zeros((8, xg_ref.shape[1]), F32)

    @pl.when(i == 0)
    def _():
        wvb_ref[...] = wv_ref[...].astype(BF16)
        wgb_ref[...] = wg_ref[...].astype(BF16)

    h = h_ref[...]

    def conv(w_ref, cw_ref, cb_ref, xs_ref):
        kw = cw_ref.shape[0]
        up = jnp.dot(h, w_ref[...], preferred_element_type=F32)
        xs_ref[8:8 + tm, :] = up
        y = cb_ref[...] + cw_ref[kw - 1:kw, :] * up
        for j in range(kw - 1):
            y = y + cw_ref[j:j + 1, :] * xs_ref[pl.ds(8 - (kw - 1) + j, tm), :]
        xs_ref[0:8, :] = xs_ref[tm:tm + 8, :]
        return y

    val = conv(wvb_ref, cwv_ref, cbv_ref, xv_ref)
    gate = conv(wgb_ref, cwg_ref, cbg_ref, xg_ref)
    o_ref[...] = (_silu(gate) * val).astype(BF16)


def _ffn_up(h2, w_up, conv_w, conv_b, seq):
    t, d = h2.shape
    f = w_up.shape[1] // 2
    tm, tf = 1024, 512
    nf = f // tf
    kw = conv_w.shape[0]
    kern = functools.partial(_ffn_up_kernel, tm=tm, blocks_per_seq=seq // tm)
    cb = conv_b.reshape(1, 2 * f)
    return pl.pallas_call(
        kern,
        grid=(nf, t // tm),
        in_specs=[pl.BlockSpec((tm, d), lambda j, i: (i, 0)),
                  pl.BlockSpec((d, tf), lambda j, i: (0, j)),
                  pl.BlockSpec((d, tf), lambda j, i: (0, nf + j)),
                  pl.BlockSpec((kw, tf), lambda j, i: (0, j)),
                  pl.BlockSpec((kw, tf), lambda j, i: (0, nf + j)),
                  pl.BlockSpec((1, tf), lambda j, i: (0, j)),
                  pl.BlockSpec((1, tf), lambda j, i: (0, nf + j))],
        out_specs=pl.BlockSpec((tm, tf), lambda j, i: (i, j)),
        out_shape=jax.ShapeDtypeStruct((t, f), BF16),
        scratch_shapes=[pltpu.VMEM((tm + 8, tf), F32),
                        pltpu.VMEM((tm + 8, tf), F32),
                        pltpu.VMEM((d, tf), BF16),
                        pltpu.VMEM((d, tf), BF16)],
        compiler_params=_params(2, 56),
        name="ffn_up",
    )(h2, w_up, w_up, conv_w, conv_w, cb, cb)


def _ffn_down_kernel(a_ref, w_ref, x1_ref, mod_ref, g_ref, o_ref, *, sub):
    for s in range(a_ref.shape[0] // sub):
        rows = pl.ds(s * sub, sub)
        y = jnp.dot(a_ref[rows, :], w_ref[...], preferred_element_type=F32)
        x2 = x1_ref[rows, :] + mod_ref[0, 5:6, :] * y
        o_ref[rows, :] = _rms(x2) * g_ref[...]


def _ffn_down(act, w_down, x1, mod, g, seq):
    t, f = act.shape
    d = w_down.shape[1]
    tm = 512
    per_b = seq // tm
    return pl.pallas_call(
        functools.partial(_ffn_down_kernel, sub=256),
        grid=(t // tm,),
        in_specs=[pl.BlockSpec((tm, f), lambda i: (i, 0)),
                  pl.BlockSpec((f, d), lambda i: (0, 0), pipeline_mode=pl.Buffered(1)),
                  pl.BlockSpec((tm, d), lambda i: (i, 0)),
                  pl.BlockSpec((1, 6, d), lambda i: (i // per_b, 0, 0)),
                  pl.BlockSpec((1, d), lambda i: (0, 0))],
        out_specs=pl.BlockSpec((tm, d), lambda i: (i, 0)),
        out_shape=jax.ShapeDtypeStruct((t, d), F32),
        compiler_params=_params(1, 58),
        name="ffn_down",
    )(act, w_down, x1, mod, g.reshape(1, d))


def _rot_cols(w):
    half = w.shape[-1] // 2
    return jnp.concatenate([-w[..., half:], w[..., :half]], axis=-1)


def _win_prep_kernel(w_ref, tail_ref, o_ref, *, n_main):
    j = pl.program_id(0)
    gates = 2 * MLSTM_HEADS
    half = MLA_ROPE // 2

    @pl.when(j < n_main)
    def _():
        o_ref[...] = w_ref[...].T.astype(BF16)

    @pl.when(j == n_main)
    def _():
        o_ref[...] = tail_ref[gates:gates + Q_LORA, :].T.astype(BF16)

    @pl.when(j == n_main + 1)
    def _():
        r0 = gates + Q_LORA
        kr0 = r0 + KV_LORA
        blk = o_ref.shape[1]
        used = KV_LORA + 2 * MLA_ROPE + gates
        o_ref[...] = jnp.concatenate(
            [tail_ref[r0:kr0, :],
             tail_ref[kr0:kr0 + MLA_ROPE, :],
             -tail_ref[kr0 + half:kr0 + MLA_ROPE, :],
             tail_ref[kr0:kr0 + half, :],
             tail_ref[0:gates, :],
             jnp.zeros((blk - used, o_ref.shape[0]), F32)], axis=0).T.astype(BF16)


def _prep_w_in(w_in):
    d = w_in.shape[0]
    wt = w_in.T
    n_head = 3 * MLSTM_W
    blk = Q_LORA
    n_main = n_head // blk
    return pl.pallas_call(
        functools.partial(_win_prep_kernel, n_main=n_main),
        grid=(IN_PAD // blk,),
        in_specs=[pl.BlockSpec((blk, d), lambda j: (jnp.minimum(j, n_main - 1), 0)),
                  pl.BlockSpec((wt.shape[0] - n_head, d), lambda j: (0, 0))],
        out_specs=pl.BlockSpec((d, blk), lambda j: (0, j)),
        out_shape=jax.ShapeDtypeStruct((d, IN_PAD), BF16),
        compiler_params=_params(1, 40),
        name="w_in_prep",
    )(wt, wt[n_head:])


def kernel(x, c, positions, ada_w, ada_b, attn_norm_g, w_in, mlstm_conv_w, mlstm_conv_b, mlstm_wq, mlstm_wk, mlstm_igate_b, mlstm_fgate_b, mla_q_norm_g, mla_w_uq, mla_kv_norm_g, mla_w_ukv, mlstm_out_g, mla_out_g, w_out, ffn_norm_g, ffn_w_up, ffn_conv_w, ffn_conv_b, ffn_w_down, final_norm_g):
    bsz, seq, d = x.shape
    t = bsz * seq
    depth = ada_w.shape[0]
    assert depth == 1, "the final RMSNorm is fused into the single layer's down-projection"
    xr = x.reshape(t, d)
    pos = positions.reshape(t // ATT_TQ, 1, ATT_TQ)
    half = MLA_ROPE // 2
    freqs = (ROPE_THETA ** (-jnp.arange(half, dtype=F32) / half)).reshape(half, 1)

    for l in range(depth):
        mod = _modulation(c, ada_w[l], ada_b[l])

        proj, misc = _in_projection(xr, mod, attn_norm_g[l], _prep_w_in(w_in[l]), seq)

        wq_m = mlstm_wq[l].astype(BF16)
        wkt_m = mlstm_wk[l].transpose(0, 2, 1).astype(BF16)
        gate_b = jnp.zeros((1, LANES), F32)
        gate_b = gate_b.at[0, :MLSTM_HEADS].set(mlstm_igate_b[l])
        gate_b = gate_b.at[0, MLSTM_HEADS:2 * MLSTM_HEADS].set(mlstm_fgate_b[l])
        hm = _mlstm(proj, misc, mlstm_conv_w[l], mlstm_conv_b[l], wq_m, wkt_m, gate_b,
                    mlstm_out_g[l], bsz, seq)

        wq = mla_w_uq[l].reshape(Q_LORA, MLA_HEADS, MLA_QK)
        wq_r = wq[..., MLA_NOPE:]
        wq_h = jnp.concatenate([wq[..., :MLA_NOPE], wq_r, _rot_cols(wq_r)], axis=-1)
        wq_h = wq_h.transpose(1, 0, 2).astype(BF16)
        wkv_h = mla_w_ukv[l].reshape(KV_LORA, MLA_HEADS, MLA_NOPE + MLA_V)
        wkv_h = wkv_h.transpose(1, 0, 2).astype(BF16)
        qt, k, vt = _mla_qkv(proj, misc, pos, freqs, mla_q_norm_g[l], mla_kv_norm_g[l],
                             wq_h, wkv_h, bsz, seq)
        ha, w_down_b, w_out_b = _mla_attention(qt, k, vt, mla_out_g[l], ffn_w_down[l], w_out[l])

        x1, h2 = _out_projection(hm, ha, w_out_b, xr, mod, ffn_norm_g[l], seq)

        act = _ffn_up(h2, ffn_w_up[l], ffn_conv_w[l], ffn_conv_b[l], seq)
        xr = _ffn_down(act, w_down_b, x1, mod, final_norm_g, seq)
    return xr.reshape(bsz, seq, d)
```

```python
import functools

import jax
import jax.numpy as jnp
import numpy as np
from jax import lax
from jax.experimental import pallas as pl
from jax.experimental.pallas import tpu as pltpu

F32 = jnp.float32
BF16 = jnp.bfloat16

EPS = 1e-6
ROPE_THETA = 10000.0
MLSTM_HEADS = 4
MLSTM_DH = 256
MLSTM_W = MLSTM_HEADS * MLSTM_DH
MLA_HEADS = 8
MLA_NOPE = 128
MLA_ROPE = 64
MLA_V = 128
MLA_QK = MLA_NOPE + MLA_ROPE
Q_LORA = 512
KV_LORA = 256
LANES = 128
MIB = 1024 * 1024

IN_PAD = 4096
IN_BLOCK = 1024
MISC_W = 256
MISC_OFF = IN_BLOCK - MISC_W


def _params(n_axes, vmem_mib):
    return pltpu.CompilerParams(
        dimension_semantics=("arbitrary",) * n_axes,
        vmem_limit_bytes=vmem_mib * MIB)


def _rms(x):
    return x * lax.rsqrt(jnp.mean(x * x, axis=-1, keepdims=True) + EPS)


def _silu(x):
    return x / (1.0 + jnp.exp(-x))


def _log_sigmoid(x):
    return jnp.minimum(x, 0.0) - jnp.log1p(jnp.exp(-jnp.abs(x)))


def _mod_kernel(c_ref, w_ref, b_ref, o_ref):
    ca = _silu(c_ref[...]).astype(BF16)
    o_ref[...] = jnp.dot(ca, w_ref[...].astype(BF16),
                         preferred_element_type=F32) + b_ref[...]


def _modulation(c, ada_w, ada_b):
    bsz, d = c.shape
    n = ada_w.shape[1]
    tn = 1024
    cp = jnp.zeros((8, d), F32).at[:bsz].set(c)
    out = pl.pallas_call(
        _mod_kernel,
        grid=(n // tn,),
        in_specs=[pl.BlockSpec((8, d), lambda j: (0, 0)),
                  pl.BlockSpec((d, tn), lambda j: (0, j)),
                  pl.BlockSpec((1, tn), lambda j: (0, j))],
        out_specs=pl.BlockSpec((8, tn), lambda j: (0, j)),
        out_shape=jax.ShapeDtypeStruct((8, n), F32),
        compiler_params=_params(1, 40),
        name="adaln_mod",
    )(cp, ada_w, ada_b.reshape(1, n))
    return out[:bsz].reshape(bsz, 6, d)


def _inproj_kernel(x_ref, mod_ref, g_ref, w_ref, o_ref, misc_ref, h_scr, *, sub):
    j = pl.program_id(1)
    last = pl.num_programs(1) - 1
    tm = x_ref.shape[0]

    def project(rows):
        acc = jnp.dot(h_scr[rows, :], w_ref[...], preferred_element_type=F32)
        o_ref[rows, :] = acc.astype(BF16)
        return acc

    @pl.when(j == 0)
    def _():
        for s in range(tm // sub):
            rows = pl.ds(s * sub, sub)
            y = _rms(x_ref[rows, :]) * g_ref[...]
            h_scr[rows, :] = (y * (1.0 + mod_ref[0, 1:2, :]) + mod_ref[0, 0:1, :]).astype(BF16)
            project(rows)

    @pl.when(j > 0)
    def _():
        for s in range(tm // sub):
            rows = pl.ds(s * sub, sub)
            acc = project(rows)

            @pl.when(j == last)
            def _():
                misc_ref[rows, :] = acc[:, MISC_OFF:]


def _in_projection(x2, mod, g, w_in_p, seq):
    t, d = x2.shape
    tm = 1024
    per_b = seq // tm
    return pl.pallas_call(
        functools.partial(_inproj_kernel, sub=512),
        grid=(t // tm, IN_PAD // IN_BLOCK),
        in_specs=[pl.BlockSpec((tm, d), lambda i, j: (i, 0)),
                  pl.BlockSpec((1, 6, d), lambda i, j: (i // per_b, 0, 0)),
                  pl.BlockSpec((1, d), lambda i, j: (0, 0)),
                  pl.BlockSpec((d, IN_BLOCK), lambda i, j: (0, j))],
        out_specs=[pl.BlockSpec((tm, IN_BLOCK), lambda i, j: (i, j)),
                   pl.BlockSpec((tm, MISC_W), lambda i, j: (i, 0))],
        out_shape=[jax.ShapeDtypeStruct((t, IN_PAD), BF16),
                   jax.ShapeDtypeStruct((t, MISC_W), F32)],
        scratch_shapes=[pltpu.VMEM((tm, d), BF16)],
        compiler_params=_params(2, 48),
        name="norm_inproj",
    )(x2, mod, g.reshape(1, d), w_in_p)


def _cumsum_rows(x):
    n = x.shape[0]
    row = lax.broadcasted_iota(jnp.int32, x.shape, 0)
    sh = 1
    while sh < n:
        x = x + jnp.where(row >= sh, pltpu.roll(x, sh, axis=0), 0.0)
        sh *= 2
    return x


def _cummax_rows(x):
    n = x.shape[0]
    row = lax.broadcasted_iota(jnp.int32, x.shape, 0)
    sh = 1
    while sh < n:
        x = jnp.maximum(x, jnp.where(row >= sh, pltpu.roll(x, sh, axis=0), -jnp.inf))
        sh *= 2
    return x


def _cumsum_lanes(x):
    n = x.shape[1]
    col = lax.broadcasted_iota(jnp.int32, x.shape, 1)
    sh = 1
    while sh < n:
        x = x + jnp.where(col >= sh, pltpu.roll(x, sh, axis=1), 0.0)
        sh *= 2
    return x


def _mlstm_kernel(u_ref, v_ref, o_ref, gt_ref, cw_ref, cb_ref, wq_ref, wkt_ref, gb_ref, og_ref,
                  out_ref, xs_ref, ct_ref, n_ref, m_ref, *, ts, chunk):
    i = pl.program_id(0)
    nbatch = u_ref.shape[0]

    @pl.when(i == 0)
    def _():
        for b in range(nbatch):
            xs_ref[b, 0:8, :] = jnp.zeros((8, MLSTM_W), F32)
        ct_ref[...] = jnp.zeros(ct_ref.shape, F32)
        n_ref[...] = jnp.zeros(n_ref.shape, F32)
        m_ref[...] = jnp.zeros(m_ref.shape, F32)

    row = lax.broadcasted_iota(jnp.int32, (chunk, chunk), 0)
    col = lax.broadcasted_iota(jnp.int32, (chunk, chunk), 1)
    causal = col <= row

    for c in range(ts // chunk):
        for b in range(nbatch):
            _mlstm_chunk(u_ref.at[b], v_ref.at[b], o_ref.at[b], gt_ref.at[b], cw_ref, cb_ref,
                         wq_ref, wkt_ref, gb_ref, og_ref, out_ref.at[b], xs_ref.at[b],
                         ct_ref.at[b], n_ref.at[b], m_ref.at[b], c, chunk, causal)
    for b in range(nbatch):
        xs_ref[b, 0:8, :] = xs_ref[b, ts:ts + 8, :]


def _mlstm_chunk(u_ref, v_ref, o_ref, gt_ref, cw_ref, cb_ref, wq_ref, wkt_ref, gb_ref, og_ref,
                 out_ref, xs_ref, ct_ref, n_ref, m_ref, c, chunk, causal):
    nh, dh = MLSTM_HEADS, MLSTM_DH
    gb = gb_ref[...]
    kw = cw_ref.shape[0]
    ones = jnp.ones((chunk, LANES), BF16)
    if True:
        r0 = c * chunk
        xc = u_ref[r0:r0 + chunk, :].astype(F32)
        xs_ref[8 + r0:8 + r0 + chunk, :] = xc
        acc = cb_ref[...] + cw_ref[kw - 1:kw, :] * xc
        for j in range(kw - 1):
            acc = acc + cw_ref[j:j + 1, :] * xs_ref[pl.ds(8 + r0 - (kw - 1) + j, chunk), :]
        su = _silu(acc)

        gc = gt_ref[r0:r0 + chunk, MISC_W - LANES:] + gb
        bc = _cumsum_rows(_log_sigmoid(gc))
        gt = gc.T[0:8, :]
        bt = _cumsum_lanes(_log_sigmoid(gt))
        mdc = _cummax_rows(gc - pltpu.roll(bc, LANES - nh, axis=1))
        for h in range(nh):
            c0 = h * dh
            b_c = bc[:, nh + h:nh + h + 1]
            li_r = gt[h:h + 1, :]
            b_r = bt[nh + h:nh + h + 1, :]
            g_tot = b_r[:, chunk - 1:chunk]
            z_r = li_r - b_r
            zmax = jnp.max(z_r, axis=-1, keepdims=True)
            m_prev = m_ref[h:h + 1, 0:1]

            uh = su[:, c0:c0 + dh]
            qb = jnp.dot(uh.astype(BF16), wq_ref[h], preferred_element_type=F32).astype(BF16)
            kt = jnp.dot(wkt_ref[h], uh.T.astype(BF16),
                         preferred_element_type=F32) * (dh ** -0.5)
            vc = v_ref[r0:r0 + chunk, c0:c0 + dh]

            mx_c = jnp.maximum(m_prev, mdc[:, h:h + 1])
            m_t = b_c + mx_c
            sb = (jnp.dot(qb, kt.astype(BF16), preferred_element_type=F32)
                  * jnp.exp(jnp.where(causal, z_r - mx_c, -jnp.inf))).astype(BF16)
            inter = jnp.exp(m_prev - mx_c)
            ct = ct_ref[h]
            nrep = n_ref[h]
            num = (jnp.dot(sb, vc, preferred_element_type=F32)
                   + inter * jnp.dot(qb, ct.astype(BF16), preferred_element_type=F32))
            den = (jnp.dot(sb, ones, preferred_element_type=F32)
                   + inter * jnp.dot(qb, nrep.astype(BF16), preferred_element_type=F32))
            lim = jnp.maximum(jnp.abs(den), jnp.exp(-m_t))
            hh = num / jnp.concatenate([lim] * (dh // LANES), axis=1)

            mm = jnp.maximum(m_prev, zmax)
            s_prev = jnp.exp(m_prev - mm)
            s_loc = jnp.exp(zmax - mm)
            kwt = (kt * (jnp.exp(z_r - zmax) * s_loc)).astype(BF16)
            ct_ref[h] = s_prev * ct + jnp.dot(kwt, vc, preferred_element_type=F32)
            n_ref[h] = s_prev * nrep + jnp.dot(kwt, ones, preferred_element_type=F32)
            m_ref[h:h + 1, :] = jnp.broadcast_to(g_tot + mm, (1, LANES))

            og = o_ref[r0:r0 + chunk, c0:c0 + dh].astype(F32)
            hh = hh / (1.0 + jnp.exp(-og))
            out_ref[r0:r0 + chunk, c0:c0 + dh] = (
                _rms(hh) * og_ref[:, c0:c0 + dh]).astype(BF16)


def _mlstm(proj, misc, conv_w, conv_b, wq, wkt, gate_b, out_g, bsz, seq):
    ts, chunk = 512, 256
    t = bsz * seq
    kern = functools.partial(_mlstm_kernel, ts=ts, chunk=chunk)
    proj3 = proj.reshape(bsz, seq, proj.shape[1])
    misc3 = misc.reshape(bsz, seq, MISC_W)
    out = pl.pallas_call(
        kern,
        grid=(seq // ts,),
        in_specs=[pl.BlockSpec((bsz, ts, MLSTM_W), lambda i: (0, i, 0)),
                  pl.BlockSpec((bsz, ts, MLSTM_W), lambda i: (0, i, 1)),
                  pl.BlockSpec((bsz, ts, MLSTM_W), lambda i: (0, i, 2)),
                  pl.BlockSpec((bsz, ts, MISC_W), lambda i: (0, i, 0)),
                  pl.BlockSpec(conv_w.shape, lambda i: (0, 0)),
                  pl.BlockSpec((1, MLSTM_W), lambda i: (0, 0)),
                  pl.BlockSpec(wq.shape, lambda i: (0, 0, 0)),
                  pl.BlockSpec(wkt.shape, lambda i: (0, 0, 0)),
                  pl.BlockSpec((1, LANES), lambda i: (0, 0)),
                  pl.BlockSpec((1, MLSTM_W), lambda i: (0, 0))],
        out_specs=pl.BlockSpec((bsz, ts, MLSTM_W), lambda i: (0, i, 0)),
        out_shape=jax.ShapeDtypeStruct((bsz, seq, MLSTM_W), BF16),
        scratch_shapes=[pltpu.VMEM((bsz, ts + 8, MLSTM_W), F32),
                        pltpu.VMEM((bsz, MLSTM_HEADS, MLSTM_DH, MLSTM_DH), F32),
                        pltpu.VMEM((bsz, MLSTM_HEADS, MLSTM_DH, LANES), F32),
                        pltpu.VMEM((bsz, 8, LANES), F32)],
        compiler_params=_params(1, 48),
        name="mlstm",
    )(proj3, proj3, proj3, misc3, conv_w, conv_b.reshape(1, MLSTM_W), wq, wkt, gate_b,
      out_g.reshape(1, MLSTM_W))
    return out.reshape(t, MLSTM_W)


ATT_TQ = 1024
ATT_TK = 256
MLA_VA = MLA_V + 16


def _qkv_kernel(cq_ref, ckv_ref, misc_ref, pos_ref, fr_ref, qg_ref, kvg_ref, wq_ref, wkv_ref,
                qt_ref, k_ref, vt_ref, cqn, ckvn, cos_s, sin_s, kr_s, *, scale, tk):
    h = pl.program_id(1)

    @pl.when(h == 0)
    def _():
        cqn[...] = (_rms(cq_ref[...].astype(F32)) * qg_ref[...]).astype(BF16)
        ckvn[...] = (_rms(ckv_ref[...].astype(F32)) * kvg_ref[...]).astype(BF16)
        ang = fr_ref[...] * pos_ref[0].astype(F32)
        reps = LANES // ang.shape[0]
        cs = jnp.concatenate([jnp.cos(ang)] * reps, axis=0).T
        sn = jnp.concatenate([jnp.sin(ang)] * reps, axis=0).T
        cos_s[...] = cs
        sin_s[...] = sn
        y = misc_ref[:, 0:LANES]
        kr_s[...] = y * cs + pltpu.roll(y, MLA_ROPE, axis=1) * sn

    for hh in range(wq_ref.shape[0]):
        mq = jnp.dot(cqn[...], wq_ref[hh], preferred_element_type=F32)
        qt_ref[0, hh, 0:MLA_NOPE, :] = (mq[:, 0:MLA_NOPE] * scale).T.astype(BF16)
        y = mq[:, MLA_NOPE:]
        r = (y * cos_s[...] + pltpu.roll(y, MLA_ROPE, axis=1) * sin_s[...]) * scale
        qt_ref[0, hh, MLA_NOPE:MLA_QK, :] = r.T[0:MLA_ROPE, :].astype(BF16)

        mkv = jnp.dot(ckvn[...], wkv_ref[hh], preferred_element_type=F32)
        k_ref[0, hh, :, 0:MLA_NOPE] = mkv[:, 0:MLA_NOPE].astype(BF16)
        k_ref[0, hh, :, MLA_NOPE:MLA_QK] = kr_s[:, 0:MLA_ROPE].astype(BF16)
        vt = mkv[:, MLA_NOPE:].T.astype(BF16)
        for j in range(vt.shape[1] // tk):
            vt_ref[0, hh, j, 0:MLA_V, :] = vt[:, j * tk:(j + 1) * tk]
            vt_ref[0, hh, j, MLA_V:, :] = jnp.ones((MLA_VA - MLA_V, tk), BF16)


def _mla_qkv(proj, misc, pos, freqs, q_g, kv_g, wq_h, wkv_h, bsz, seq):
    tm, tk = ATT_TQ, ATT_TK
    hps = MLA_HEADS
    t = bsz * seq
    nsb = seq // tm
    cq_blk = (3 * MLSTM_W) // Q_LORA
    ckv_blk = (3 * MLSTM_W + Q_LORA) // KV_LORA
    kern = functools.partial(_qkv_kernel, scale=MLA_QK ** -0.5 * float(np.log2(np.e)), tk=tk)
    return pl.pallas_call(
        kern,
        grid=(t // tm, MLA_HEADS // hps),
        in_specs=[pl.BlockSpec((tm, Q_LORA), lambda i, h: (i, cq_blk)),
                  pl.BlockSpec((tm, KV_LORA), lambda i, h: (i, ckv_blk)),
                  pl.BlockSpec((tm, MISC_W), lambda i, h: (i, 0)),
                  pl.BlockSpec((1, 1, tm), lambda i, h: (i, 0, 0)),
                  pl.BlockSpec(freqs.shape, lambda i, h: (0, 0)),
                  pl.BlockSpec((1, Q_LORA), lambda i, h: (0, 0)),
                  pl.BlockSpec((1, KV_LORA), lambda i, h: (0, 0)),
                  pl.BlockSpec((hps, Q_LORA, 2 * LANES), lambda i, h: (h, 0, 0)),
                  pl.BlockSpec((hps, KV_LORA, 2 * LANES), lambda i, h: (h, 0, 0))],
        out_specs=[pl.BlockSpec((1, hps, MLA_QK, tm), lambda i, h: (i // nsb, h, 0, i % nsb)),
                   pl.BlockSpec((1, hps, tm, MLA_QK), lambda i, h: (i // nsb, h, i % nsb, 0)),
                   pl.BlockSpec((1, hps, tm // tk, MLA_VA, tk),
                                lambda i, h: (i // nsb, h, i % nsb, 0, 0))],
        out_shape=[jax.ShapeDtypeStruct((bsz, MLA_HEADS, MLA_QK, seq), BF16),
                   jax.ShapeDtypeStruct((bsz, MLA_HEADS, seq, MLA_QK), BF16),
                   jax.ShapeDtypeStruct((bsz, MLA_HEADS, seq // tk, MLA_VA, tk), BF16)],
        scratch_shapes=[pltpu.VMEM((tm, Q_LORA), BF16),
                        pltpu.VMEM((tm, KV_LORA), BF16),
                        pltpu.VMEM((tm, LANES), F32),
                        pltpu.VMEM((tm, LANES), F32),
                        pltpu.VMEM((tm, LANES), F32)],
        compiler_params=_params(2, 40),
        name="mla_qkv",
    )(proj, proj, misc, pos, freqs, q_g.reshape(1, Q_LORA), kv_g.reshape(1, KV_LORA),
      wq_h, wkv_h)


def _flash_kernel(qt_ref, k_ref, vt_ref, g_ref, wa_ref, wb_ref, o_ref, wab_ref, wbb_ref,
                  m_ref, acc_ref, bias_ref, *, tq, tk, nq, wa_slabs, wb_slabs):
    qi = pl.program_id(2)

    step = (pl.program_id(0) * pl.num_programs(1) + pl.program_id(1)) * nq + qi

    @pl.when(step < wa_slabs)
    def _():
        wab_ref[...] = wa_ref[...].astype(BF16)

    @pl.when(step < wb_slabs)
    def _():
        wbb_ref[...] = wb_ref[...].astype(BF16)

    m_ref[...] = jnp.full(m_ref.shape, -jnp.inf, F32)
    acc_ref[...] = jnp.zeros(acc_ref.shape, F32)
    @pl.when((pl.program_id(0) == 0) & (pl.program_id(1) == 0) & (qi == 0))
    def _():
        key = lax.broadcasted_iota(jnp.int32, (tk, tq), 0)
        qry = lax.broadcasted_iota(jnp.int32, (tk, tq), 1)
        bias_ref[...] = jnp.where(key <= qry, 0.0, -jnp.inf).astype(F32)

    def scores(kj, c0, masked):
        k = k_ref[0, 0, kj * tk:(kj + 1) * tk, :]
        s = jnp.dot(k, qt_ref[0, 0, :, c0:], preferred_element_type=F32)
        if masked:
            s = s + bias_ref[:, 0:tq - c0]
        return s

    def update(s, kj, c0):
        m_prev = m_ref[:, c0:]
        m_new = jnp.maximum(m_prev, jnp.max(s, axis=0, keepdims=True))
        alpha = jnp.exp2(m_prev - m_new)
        p = jnp.exp2(s - m_new)
        acc_ref[:, c0:] = alpha * acc_ref[:, c0:] + jnp.dot(
            vt_ref[0, 0, kj], p.astype(BF16), preferred_element_type=F32)
        m_ref[:, c0:] = m_new

    per_q = tq // tk

    def run(q):
        def place(b):
            d = b - q * per_q
            return (0, False) if d < 0 else (d * tk, True)

        nblk = (q + 1) * per_q
        s = scores(0, *place(0))
        for b in range(nblk):
            s_next = scores(b + 1, *place(b + 1)) if b + 1 < nblk else None
            update(s, b, place(b)[0])
            s = s_next

    for q in range(nq):
        pl.when(qi == q)(functools.partial(run, q))

    o = acc_ref[0:MLA_V, :] / acc_ref[MLA_V:MLA_V + 1, :]
    o = o * lax.rsqrt(jnp.mean(o * o, axis=0, keepdims=True) + EPS) * g_ref[0]
    o_ref[...] = o.T.astype(BF16)


def _cast_slab_rows(rows, n_steps):
    bf16_rows = 16
    for slab in range(bf16_rows, rows + 1, bf16_rows):
        if rows % slab == 0 and rows // slab <= n_steps:
            return slab
    raise ValueError("no slab size fits")


def _mla_attention(qt, k, vt, out_g, wa, wb):
    bsz, nh, seq, _ = k.shape
    tq, tk = ATT_TQ, ATT_TK
    nq = seq // tq
    n_steps = bsz * nh * nq
    sa = _cast_slab_rows(wa.shape[0], n_steps)
    sb = _cast_slab_rows(wb.shape[0], n_steps)
    na, nb = wa.shape[0] // sa, wb.shape[0] // sb
    kern = functools.partial(_flash_kernel, tq=tq, tk=tk, nq=nq, wa_slabs=na, wb_slabs=nb)
    step = lambda b, h, i: (b * nh + h) * nq + i
    wa_map = lambda b, h, i: (jnp.minimum(step(b, h, i), na - 1), 0)
    wb_map = lambda b, h, i: (jnp.minimum(step(b, h, i), nb - 1), 0)
    return pl.pallas_call(
        kern,
        grid=(bsz, nh, nq),
        in_specs=[pl.BlockSpec((1, 1, MLA_QK, tq), lambda b, h, i: (b, h, 0, i)),
                  pl.BlockSpec((1, 1, seq, MLA_QK), lambda b, h, i: (b, h, 0, 0)),
                  pl.BlockSpec((1, 1, seq // tk, MLA_VA, tk), lambda b, h, i: (b, h, 0, 0, 0)),
                  pl.BlockSpec((1, MLA_V, 1), lambda b, h, i: (h, 0, 0)),
                  pl.BlockSpec((sa, wa.shape[1]), wa_map),
                  pl.BlockSpec((sb, wb.shape[1]), wb_map)],
        out_specs=[pl.BlockSpec((tq, MLA_V), lambda b, h, i: (b * nq + i, h)),
                   pl.BlockSpec((sa, wa.shape[1]), wa_map),
                   pl.BlockSpec((sb, wb.shape[1]), wb_map)],
        out_shape=[jax.ShapeDtypeStruct((bsz * seq, nh * MLA_V), BF16),
                   jax.ShapeDtypeStruct(wa.shape, BF16),
                   jax.ShapeDtypeStruct(wb.shape, BF16)],
        scratch_shapes=[pltpu.VMEM((1, tq), F32),
                        pltpu.VMEM((MLA_VA, tq), F32),
                        pltpu.VMEM((tk, tq), F32)],
        compiler_params=_params(3, 40),
        name="mla_attention",
    )(qt, k, vt, out_g.reshape(nh, MLA_V, 1), wa, wb)


def _outproj_kernel(hm_ref, ha_ref, w_ref, x_ref, mod_ref, g_ref, x1_ref, h2_ref, *, sub):
    km = hm_ref.shape[1]
    for s in range(hm_ref.shape[0] // sub):
        rows = pl.ds(s * sub, sub)
        mix = (jnp.dot(hm_ref[rows, :], w_ref[0:km, :], preferred_element_type=F32)
               + jnp.dot(ha_ref[rows, :], w_ref[km:, :], preferred_element_type=F32))
        x1 = x_ref[rows, :] + mod_ref[0, 2:3, :] * mix
        x1_ref[rows, :] = x1
        y = _rms(x1) * g_ref[...]
        h2_ref[rows, :] = (y * (1.0 + mod_ref[0, 4:5, :]) + mod_ref[0, 3:4, :]).astype(BF16)


def _out_projection(hm, ha, w_out, x2, mod, g, seq):
    t, d = x2.shape
    tm = 512
    per_b = seq // tm
    return pl.pallas_call(
        functools.partial(_outproj_kernel, sub=256),
        grid=(t // tm,),
        in_specs=[pl.BlockSpec((tm, hm.shape[1]), lambda i: (i, 0)),
                  pl.BlockSpec((tm, ha.shape[1]), lambda i: (i, 0)),
                  pl.BlockSpec(w_out.shape, lambda i: (0, 0), pipeline_mode=pl.Buffered(1)),
                  pl.BlockSpec((tm, d), lambda i: (i, 0)),
                  pl.BlockSpec((1, 6, d), lambda i: (i // per_b, 0, 0)),
                  pl.BlockSpec((1, d), lambda i: (0, 0))],
        out_specs=[pl.BlockSpec((tm, d), lambda i: (i, 0)),
                   pl.BlockSpec((tm, d), lambda i: (i, 0))],
        out_shape=[jax.ShapeDtypeStruct((t, d), F32),
                   jax.ShapeDtypeStruct((t, d), BF16)],
        compiler_params=_params(1, 48),
        name="out_proj",
    )(hm, ha, w_out, x2, mod, g.reshape(1, d))


def _ffn_up_kernel(h_ref, wv_ref, wg_ref, cwv_ref, cwg_ref, cbv_ref, cbg_ref, o_ref,
                   xv_ref, xg_ref, wvb_ref, wgb_ref, *, tm, blocks_per_seq):
    i = pl.program_id(1)

    @pl.when(i % blocks_per_seq == 0)
    def _():
        xv_ref[0:8, :] = jnp.zeros((8, xv_ref.shape[1]), F32)
        xg_ref[0:8, :] = jnp.zeros((8, xg_ref.shape[1]), F32)

    @pl.when(i == 0)
    def _():
        wvb_ref[...] = wv_ref[...].astype(BF16)
        wgb_ref[...] = wg_ref[...].astype(BF16)

    h = h_ref[...]

    def conv(w_ref, cw_ref, cb_ref, xs_ref):
        kw = cw_ref.shape[0]
        up = jnp.dot(h, w_ref[...], preferred_element_type=F32)
        xs_ref[8:8 + tm, :] = up
        y = cb_ref[...] + cw_ref[kw - 1:kw, :] * up
        for j in range(kw - 1):
            y = y + cw_ref[j:j + 1, :] * xs_ref[pl.ds(8 - (kw - 1) + j, tm), :]
        xs_ref[0:8, :] = xs_ref[tm:tm + 8, :]
        return y

    val = conv(wvb_ref, cwv_ref, cbv_ref, xv_ref)
    gate = conv(wgb_ref, cwg_ref, cbg_ref, xg_ref)
    o_ref[...] = (_silu(gate) * val).astype(BF16)


def _ffn_up(h2, w_up, conv_w, conv_b, seq):
    t, d = h2.shape
    f = w_up.shape[1] // 2
    tm, tf = 1024, 512
    nf = f // tf
    kw = conv_w.shape[0]
    kern = functools.partial(_ffn_up_kernel, tm=tm, blocks_per_seq=seq // tm)
    cb = conv_b.reshape(1, 2 * f)
    return pl.pallas_call(
        kern,
        grid=(nf, t // tm),
        in_specs=[pl.BlockSpec((tm, d), lambda j, i: (i, 0)),
                  pl.BlockSpec((d, tf), lambda j, i: (0, j)),
                  pl.BlockSpec((d, tf), lambda j, i: (0, nf + j)),
                  pl.BlockSpec((kw, tf), lambda j, i: (0, j)),
                  pl.BlockSpec((kw, tf), lambda j, i: (0, nf + j)),
                  pl.BlockSpec((1, tf), lambda j, i: (0, j)),
                  pl.BlockSpec((1, tf), lambda j, i: (0, nf + j))],
        out_specs=pl.BlockSpec((tm, tf), lambda j, i: (i, j)),
        out_shape=jax.ShapeDtypeStruct((t, f), BF16),
        scratch_shapes=[pltpu.VMEM((tm + 8, tf), F32),
                        pltpu.VMEM((tm + 8, tf), F32),
                        pltpu.VMEM((d, tf), BF16),
                        pltpu.VMEM((d, tf), BF16)],
        compiler_params=_params(2, 56),
        name="ffn_up",
    )(h2, w_up, w_up, conv_w, conv_w, cb, cb)


def _ffn_down_kernel(a_ref, w_ref, x1_ref, mod_ref, g_ref, o_ref, *, sub):
    for s in range(a_ref.shape[0] // sub):
        rows = pl.ds(s * sub, sub)
        y = jnp.dot(a_ref[rows, :], w_ref[...], preferred_element_type=F32)
        x2 = x1_ref[rows, :] + mod_ref[0, 5:6, :] * y
        o_ref[rows, :] = _rms(x2) * g_ref[...]


def _ffn_down(act, w_down, x1, mod, g, seq):
    t, f = act.shape
    d = w_down.shape[1]
    tm = 512
    per_b = seq // tm
    return pl.pallas_call(
        functools.partial(_ffn_down_kernel, sub=256),
        grid=(t // tm,),
        in_specs=[pl.BlockSpec((tm, f), lambda i: (i, 0)),
                  pl.BlockSpec((f, d), lambda i: (0, 0), pipeline_mode=pl.Buffered(1)),
                  pl.BlockSpec((tm, d), lambda i: (i, 0)),
                  pl.BlockSpec((1, 6, d), lambda i: (i // per_b, 0, 0)),
                  pl.BlockSpec((1, d), lambda i: (0, 0))],
        out_specs=pl.BlockSpec((tm, d), lambda i: (i, 0)),
        out_shape=jax.ShapeDtypeStruct((t, d), F32),
        compiler_params=_params(1, 58),
        name="ffn_down",
    )(act, w_down, x1, mod, g.reshape(1, d))


def _rot_cols(w):
    half = w.shape[-1] // 2
    return jnp.concatenate([-w[..., half:], w[..., :half]], axis=-1)


def _win_prep_kernel(w_ref, tail_ref, o_ref, *, n_main):
    j = pl.program_id(0)
    gates = 2 * MLSTM_HEADS
    half = MLA_ROPE // 2

    @pl.when(j < n_main)
    def _():
        o_ref[...] = w_ref[...].T.astype(BF16)

    @pl.when(j == n_main)
    def _():
        o_ref[...] = tail_ref[gates:gates + Q_LORA, :].T.astype(BF16)

    @pl.when(j == n_main + 1)
    def _():
        r0 = gates + Q_LORA
        kr0 = r0 + KV_LORA
        blk = o_ref.shape[1]
        used = KV_LORA + 2 * MLA_ROPE + gates
        o_ref[...] = jnp.concatenate(
            [tail_ref[r0:kr0, :],
             tail_ref[kr0:kr0 + MLA_ROPE, :],
             -tail_ref[kr0 + half:kr0 + MLA_ROPE, :],
             tail_ref[kr0:kr0 + half, :],
             tail_ref[0:gates, :],
             jnp.zeros((blk - used, o_ref.shape[0]), F32)], axis=0).T.astype(BF16)


def _prep_w_in(w_in):
    d = w_in.shape[0]
    wt = w_in.T
    n_head = 3 * MLSTM_W
    blk = Q_LORA
    n_main = n_head // blk
    return pl.pallas_call(
        functools.partial(_win_prep_kernel, n_main=n_main),
        grid=(IN_PAD // blk,),
        in_specs=[pl.BlockSpec((blk, d), lambda j: (jnp.minimum(j, n_main - 1), 0)),
                  pl.BlockSpec((wt.shape[0] - n_head, d), lambda j: (0, 0))],
        out_specs=pl.BlockSpec((d, blk), lambda j: (0, j)),
        out_shape=jax.ShapeDtypeStruct((d, IN_PAD), BF16),
        compiler_params=_params(1, 40),
        name="w_in_prep",
    )(wt, wt[n_head:])


def kernel(x, c, positions, ada_w, ada_b, attn_norm_g, w_in, mlstm_conv_w, mlstm_conv_b, mlstm_wq, mlstm_wk, mlstm_igate_b, mlstm_fgate_b, mla_q_norm_g, mla_w_uq, mla_kv_norm_g, mla_w_ukv, mlstm_out_g, mla_out_g, w_out, ffn_norm_g, ffn_w_up, ffn_conv_w, ffn_conv_b, ffn_w_down, final_norm_g):
    bsz, seq, d = x.shape
    t = bsz * seq
    depth = ada_w.shape[0]
    assert depth == 1, "the final RMSNorm is fused into the single layer's down-projection"
    xr = x.reshape(t, d)
    pos = positions.reshape(t // ATT_TQ, 1, ATT_TQ)
    half = MLA_ROPE // 2
    freqs = (ROPE_THETA ** (-jnp.arange(half, dtype=F32) / half)).reshape(half, 1)

    for l in range(depth):
        mod = _modulation(c, ada_w[l], ada_b[l])

        proj, misc = _in_projection(xr, mod, attn_norm_g[l], _prep_w_in(w_in[l]), seq)

        wq_m = mlstm_wq[l].astype(BF16)
        wkt_m = mlstm_wk[l].transpose(0, 2, 1).astype(BF16)
        gate_b = jnp.zeros((1, LANES), F32)
        gate_b = gate_b.at[0, :MLSTM_HEADS].set(mlstm_igate_b[l])
        gate_b = gate_b.at[0, MLSTM_HEADS:2 * MLSTM_HEADS].set(mlstm_fgate_b[l])
        hm = _mlstm(proj, misc, mlstm_conv_w[l], mlstm_conv_b[l], wq_m, wkt_m, gate_b,
                    mlstm_out_g[l], bsz, seq)

        wq = mla_w_uq[l].reshape(Q_LORA, MLA_HEADS, MLA_QK)
        wq_r = wq[..., MLA_NOPE:]
        wq_h = jnp.concatenate([wq[..., :MLA_NOPE], wq_r, _rot_cols(wq_r)], axis=-1)
        wq_h = wq_h.transpose(1, 0, 2).astype(BF16)
        wkv_h = mla_w_ukv[l].reshape(KV_LORA, MLA_HEADS, MLA_NOPE + MLA_V)
        wkv_h = wkv_h.transpose(1, 0, 2).astype(BF16)
        qt, k, vt = _mla_qkv(proj, misc, pos, freqs, mla_q_norm_g[l], mla_kv_norm_g[l],
                             wq_h, wkv_h, bsz, seq)
        ha, w_down_b, w_out_b = _mla_attention(qt, k, vt, mla_out_g[l], ffn_w_down[l], w_out[l])

        x1, h2 = _out_projection(hm, ha, w_out_b, xr, mod, ffn_norm_g[l], seq)

        act = _ffn_up(h2, ffn_w_up[l], ffn_conv_w[l], ffn_conv_b[l], seq)
        xr = _ffn_down(act, w_down_b, x1, mod, final_norm_g, seq)
    return xr.reshape(bsz, seq, d)
```

```python
import functools

import jax
import jax.numpy as jnp
import numpy as np
from jax import lax
from jax.experimental import pallas as pl
from jax.experimental.pallas import tpu as pltpu

F32 = jnp.float32
BF16 = jnp.bfloat16

EPS = 1e-6
ROPE_THETA = 10000.0
MLSTM_HEADS = 4
MLSTM_DH = 256
MLSTM_W = MLSTM_HEADS * MLSTM_DH
MLA_HEADS = 8
MLA_NOPE = 128
MLA_ROPE = 64
MLA_V = 128
MLA_QK = MLA_NOPE + MLA_ROPE
Q_LORA = 512
KV_LORA = 256
LANES = 128
MIB = 1024 * 1024

IN_PAD = 4096
IN_BLOCK = 1024
MISC_W = 256
MISC_OFF = IN_BLOCK - MISC_W


def _params(n_axes, vmem_mib):
    return pltpu.CompilerParams(
        dimension_semantics=("arbitrary",) * n_axes,
        vmem_limit_bytes=vmem_mib * MIB)


def _rms(x):
    return x * lax.rsqrt(jnp.mean(x * x, axis=-1, keepdims=True) + EPS)


def _silu(x):
    return x / (1.0 + jnp.exp(-x))


def _log_sigmoid(x):
    return jnp.minimum(x, 0.0) - jnp.log1p(jnp.exp(-jnp.abs(x)))


def _mod_kernel(c_ref, w_ref, b_ref, o_ref):
    ca = _silu(c_ref[...]).astype(BF16)
    o_ref[...] = jnp.dot(ca, w_ref[...].astype(BF16),
                         preferred_element_type=F32) + b_ref[...]


def _modulation(c, ada_w, ada_b):
    bsz, d = c.shape
    n = ada_w.shape[1]
    tn = 1024
    cp = jnp.zeros((8, d), F32).at[:bsz].set(c)
    out = pl.pallas_call(
        _mod_kernel,
        grid=(n // tn,),
        in_specs=[pl.BlockSpec((8, d), lambda j: (0, 0)),
                  pl.BlockSpec((d, tn), lambda j: (0, j)),
                  pl.BlockSpec((1, tn), lambda j: (0, j))],
        out_specs=pl.BlockSpec((8, tn), lambda j: (0, j)),
        out_shape=jax.ShapeDtypeStruct((8, n), F32),
        compiler_params=_params(1, 40),
        name="adaln_mod",
    )(cp, ada_w, ada_b.reshape(1, n))
    return out[:bsz].reshape(bsz, 6, d)


def _inproj_kernel(x_ref, mod_ref, g_ref, w_ref, o_ref, misc_ref, h_scr, *, sub):
    j = pl.program_id(1)
    last = pl.num_programs(1) - 1
    tm = x_ref.shape[0]

    def project(rows):
        acc = jnp.dot(h_scr[rows, :], w_ref[...], preferred_element_type=F32)
        o_ref[rows, :] = acc.astype(BF16)
        return acc

    @pl.when(j == 0)
    def _():
        for s in range(tm // sub):
            rows = pl.ds(s * sub, sub)
            y = _rms(x_ref[rows, :]) * g_ref[...]
            h_scr[rows, :] = (y * (1.0 + mod_ref[0, 1:2, :]) + mod_ref[0, 0:1, :]).astype(BF16)
            project(rows)

    @pl.when(j > 0)
    def _():
        for s in range(tm // sub):
            rows = pl.ds(s * sub, sub)
            acc = project(rows)

            @pl.when(j == last)
            def _():
                misc_ref[rows, :] = acc[:, MISC_OFF:]


def _in_projection(x2, mod, g, w_in_p, seq):
    t, d = x2.shape
    tm = 1024
    per_b = seq // tm
    return pl.pallas_call(
        functools.partial(_inproj_kernel, sub=512),
        grid=(t // tm, IN_PAD // IN_BLOCK),
        in_specs=[pl.BlockSpec((tm, d), lambda i, j: (i, 0)),
                  pl.BlockSpec((1, 6, d), lambda i, j: (i // per_b, 0, 0)),
                  pl.BlockSpec((1, d), lambda i, j: (0, 0)),
                  pl.BlockSpec((d, IN_BLOCK), lambda i, j: (0, j))],
        out_specs=[pl.BlockSpec((tm, IN_BLOCK), lambda i, j: (i, j)),
                   pl.BlockSpec((tm, MISC_W), lambda i, j: (i, 0))],
        out_shape=[jax.ShapeDtypeStruct((t, IN_PAD), BF16),
                   jax.ShapeDtypeStruct((t, MISC_W), F32)],
        scratch_shapes=[pltpu.VMEM((tm, d), BF16)],
        compiler_params=_params(2, 48),
        name="norm_inproj",
    )(x2, mod, g.reshape(1, d), w_in_p)


def _cumsum_rows(x):
    n = x.shape[0]
    row = lax.broadcasted_iota(jnp.int32, x.shape, 0)
    sh = 1
    while sh < n:
        x = x + jnp.where(row >= sh, pltpu.roll(x, sh, axis=0), 0.0)
        sh *= 2
    return x


def _cummax_rows(x):
    n = x.shape[0]
    row = lax.broadcasted_iota(jnp.int32, x.shape, 0)
    sh = 1
    while sh < n:
        x = jnp.maximum(x, jnp.where(row >= sh, pltpu.roll(x, sh, axis=0), -jnp.inf))
        sh *= 2
    return x


def _cumsum_lanes(x):
    n = x.shape[1]
    col = lax.broadcasted_iota(jnp.int32, x.shape, 1)
    sh = 1
    while sh < n:
        x = x + jnp.where(col >= sh, pltpu.roll(x, sh, axis=1), 0.0)
        sh *= 2
    return x


def _mlstm_kernel(u_ref, v_ref, o_ref, gt_ref, cw_ref, cb_ref, wq_ref, wkt_ref, gb_ref, og_ref,
                  out_ref, xs_ref, ct_ref, n_ref, m_ref, *, ts, chunk):
    i = pl.program_id(0)
    nbatch = u_ref.shape[0]

    @pl.when(i == 0)
    def _():
        for b in range(nbatch):
            xs_ref[b, 0:8, :] = jnp.zeros((8, MLSTM_W), F32)
        ct_ref[...] = jnp.zeros(ct_ref.shape, F32)
        n_ref[...] = jnp.zeros(n_ref.shape, F32)
        m_ref[...] = jnp.zeros(m_ref.shape, F32)

    row = lax.broadcasted_iota(jnp.int32, (chunk, chunk), 0)
    col = lax.broadcasted_iota(jnp.int32, (chunk, chunk), 1)
    causal = col <= row

    for c in range(ts // chunk):
        for b in range(nbatch):
            _mlstm_chunk(u_ref.at[b], v_ref.at[b], o_ref.at[b], gt_ref.at[b], cw_ref, cb_ref,
                         wq_ref, wkt_ref, gb_ref, og_ref, out_ref.at[b], xs_ref.at[b],
                         ct_ref.at[b], n_ref.at[b], m_ref.at[b], c, chunk, causal)
    for b in range(nbatch):
        xs_ref[b, 0:8, :] = xs_ref[b, ts:ts + 8, :]


def _mlstm_chunk(u_ref, v_ref, o_ref, gt_ref, cw_ref, cb_ref, wq_ref, wkt_ref, gb_ref, og_ref,
                 out_ref, xs_ref, ct_ref, n_ref, m_ref, c, chunk, causal):
    nh, dh = MLSTM_HEADS, MLSTM_DH
    gb = gb_ref[...]
    kw = cw_ref.shape[0]
    ones = jnp.ones((chunk, LANES), BF16)
    if True:
        r0 = c * chunk
        xc = u_ref[r0:r0 + chunk, :].astype(F32)
        xs_ref[8 + r0:8 + r0 + chunk, :] = xc
        acc = cb_ref[...] + cw_ref[kw - 1:kw, :] * xc
        for j in range(kw - 1):
            acc = acc + cw_ref[j:j + 1, :] * xs_ref[pl.ds(8 + r0 - (kw - 1) + j, chunk), :]
        su = _silu(acc)

        gc = gt_ref[r0:r0 + chunk, MISC_W - LANES:] + gb
        bc = _cumsum_rows(_log_sigmoid(gc))
        gt = gc.T[0:8, :]
        bt = _cumsum_lanes(_log_sigmoid(gt))
        mdc = _cummax_rows(gc - pltpu.roll(bc, LANES - nh, axis=1))
        for h in range(nh):
            c0 = h * dh
            b_c = bc[:, nh + h:nh + h + 1]
            li_r = gt[h:h + 1, :]
            b_r = bt[nh + h:nh + h + 1, :]
            g_tot = b_r[:, chunk - 1:chunk]
            z_r = li_r - b_r
            zmax = jnp.max(z_r, axis=-1, keepdims=True)
            m_prev = m_ref[h:h + 1, 0:1]

            uh = su[:, c0:c0 + dh]
            qb = jnp.dot(uh.astype(BF16), wq_ref[h], preferred_element_type=F32).astype(BF16)
            kt = jnp.dot(wkt_ref[h], uh.T.astype(BF16),
                         preferred_element_type=F32) * (dh ** -0.5)
            vc = v_ref[r0:r0 + chunk, c0:c0 + dh]

            mx_c = jnp.maximum(m_prev, mdc[:, h:h + 1])
            m_t = b_c + mx_c
            sb = (jnp.dot(qb, kt.astype(BF16), preferred_element_type=F32)
                  * jnp.exp(jnp.where(causal, z_r - mx_c, -jnp.inf))).astype(BF16)
            inter = jnp.exp(m_prev - mx_c)
            ct = ct_ref[h]
            nrep = n_ref[h]
            num = (jnp.dot(sb, vc, preferred_element_type=F32)
                   + inter * jnp.dot(qb, ct.astype(BF16), preferred_element_type=F32))
            den = (jnp.dot(sb, ones, preferred_element_type=F32)
                   + inter * jnp.dot(qb, nrep.astype(BF16), preferred_element_type=F32))
            lim = jnp.maximum(jnp.abs(den), jnp.exp(-m_t))
            hh = num / jnp.concatenate([lim] * (dh // LANES), axis=1)

            mm = jnp.maximum(m_prev, zmax)
            s_prev = jnp.exp(m_prev - mm)
            s_loc = jnp.exp(zmax - mm)
            kwt = (kt * (jnp.exp(z_r - zmax) * s_loc)).astype(BF16)
            ct_ref[h] = s_prev * ct + jnp.dot(kwt, vc, preferred_element_type=F32)
            n_ref[h] = s_prev * nrep + jnp.dot(kwt, ones, preferred_element_type=F32)
            m_ref[h:h + 1, :] = jnp.broadcast_to(g_tot + mm, (1, LANES))

            og = o_ref[r0:r0 + chunk, c0:c0 + dh].astype(F32)
            hh = hh / (1.0 + jnp.exp(-og))
            out_ref[r0:r0 + chunk, c0:c0 + dh] = (
                _rms(hh) * og_ref[:, c0:c0 + dh]).astype(BF16)


def _mlstm(proj, misc, conv_w, conv_b, wq, wkt, gate_b, out_g, bsz, seq):
    ts, chunk = 512, 256
    t = bsz * seq
    kern = functools.partial(_mlstm_kernel, ts=ts, chunk=chunk)
    proj3 = proj.reshape(bsz, seq, proj.shape[1])
    misc3 = misc.reshape(bsz, seq, MISC_W)
    out = pl.pallas_call(
        kern,
        grid=(seq // ts,),
        in_specs=[pl.BlockSpec((bsz, ts, MLSTM_W), lambda i: (0, i, 0)),
                  pl.BlockSpec((bsz, ts, MLSTM_W), lambda i: (0, i, 1)),
                  pl.BlockSpec((bsz, ts, MLSTM_W), lambda i: (0, i, 2)),
                  pl.BlockSpec((bsz, ts, MISC_W), lambda i: (0, i, 0)),
                  pl.BlockSpec(conv_w.shape, lambda i: (0, 0)),
                  pl.BlockSpec((1, MLSTM_W), lambda i: (0, 0)),
                  pl.BlockSpec(wq.shape, lambda i: (0, 0, 0)),
                  pl.BlockSpec(wkt.shape, lambda i: (0, 0, 0)),
                  pl.BlockSpec((1, LANES), lambda i: (0, 0)),
                  pl.BlockSpec((1, MLSTM_W), lambda i: (0, 0))],
        out_specs=pl.BlockSpec((bsz, ts, MLSTM_W), lambda i: (0, i, 0)),
        out_shape=jax.ShapeDtypeStruct((bsz, seq, MLSTM_W), BF16),
        scratch_shapes=[pltpu.VMEM((bsz, ts + 8, MLSTM_W), F32),
                        pltpu.VMEM((bsz, MLSTM_HEADS, MLSTM_DH, MLSTM_DH), F32),
                        pltpu.VMEM((bsz, MLSTM_HEADS, MLSTM_DH, LANES), F32),
                        pltpu.VMEM((bsz, 8, LANES), F32)],
        compiler_params=_params(1, 48),
        name="mlstm",
    )(proj3, proj3, proj3, misc3, conv_w, conv_b.reshape(1, MLSTM_W), wq, wkt, gate_b,
      out_g.reshape(1, MLSTM_W))
    return out.reshape(t, MLSTM_W)


ATT_TQ = 1024
ATT_TK = 256
MLA_VA = MLA_V + 16


def _qkv_kernel(cq_ref, ckv_ref, misc_ref, pos_ref, fr_ref, qg_ref, kvg_ref, wq_ref, wkv_ref,
                qt_ref, k_ref, vt_ref, cqn, ckvn, cos_s, sin_s, kr_s, *, scale, tk):
    h = pl.program_id(1)

    @pl.when(h == 0)
    def _():
        cqn[...] = (_rms(cq_ref[...].astype(F32)) * qg_ref[...]).astype(BF16)
        ckvn[...] = (_rms(ckv_ref[...].astype(F32)) * kvg_ref[...]).astype(BF16)
        ang = fr_ref[...] * pos_ref[0].astype(F32)
        reps = LANES // ang.shape[0]
        cs = jnp.concatenate([jnp.cos(ang)] * reps, axis=0).T
        sn = jnp.concatenate([jnp.sin(ang)] * reps, axis=0).T
        cos_s[...] = cs
        sin_s[...] = sn
        y = misc_ref[:, 0:LANES]
        kr_s[...] = y * cs + pltpu.roll(y, MLA_ROPE, axis=1) * sn

    for hh in range(wq_ref.shape[0]):
        mq = jnp.dot(cqn[...], wq_ref[hh], preferred_element_type=F32)
        qt_ref[0, hh, 0:MLA_NOPE, :] = (mq[:, 0:MLA_NOPE] * scale).T.astype(BF16)
        y = mq[:, MLA_NOPE:]
        r = (y * cos_s[...] + pltpu.roll(y, MLA_ROPE, axis=1) * sin_s[...]) * scale
        qt_ref[0, hh, MLA_NOPE:MLA_QK, :] = r.T[0:MLA_ROPE, :].astype(BF16)

        mkv = jnp.dot(ckvn[...], wkv_ref[hh], preferred_element_type=F32)
        k_ref[0, hh, :, 0:MLA_NOPE] = mkv[:, 0:MLA_NOPE].astype(BF16)
        k_ref[0, hh, :, MLA_NOPE:MLA_QK] = kr_s[:, 0:MLA_ROPE].astype(BF16)
        vt = mkv[:, MLA_NOPE:].T.astype(BF16)
        for j in range(vt.shape[1] // tk):
            vt_ref[0, hh, j, 0:MLA_V, :] = vt[:, j * tk:(j + 1) * tk]
            vt_ref[0, hh, j, MLA_V:, :] = jnp.ones((MLA_VA - MLA_V, tk), BF16)


def _mla_qkv(proj, misc, pos, freqs, q_g, kv_g, wq_h, wkv_h, bsz, seq):
    tm, tk = ATT_TQ, ATT_TK
    hps = MLA_HEADS
    t = bsz * seq
    nsb = seq // tm
    cq_blk = (3 * MLSTM_W) // Q_LORA
    ckv_blk = (3 * MLSTM_W + Q_LORA) // KV_LORA
    kern = functools.partial(_qkv_kernel, scale=MLA_QK ** -0.5 * float(np.log2(np.e)), tk=tk)
    return pl.pallas_call(
        kern,
        grid=(t // tm, MLA_HEADS // hps),
        in_specs=[pl.BlockSpec((tm, Q_LORA), lambda i, h: (i, cq_blk)),
                  pl.BlockSpec((tm, KV_LORA), lambda i, h: (i, ckv_blk)),
                  pl.BlockSpec((tm, MISC_W), lambda i, h: (i, 0)),
                  pl.BlockSpec((1, 1, tm), lambda i, h: (i, 0, 0)),
                  pl.BlockSpec(freqs.shape, lambda i, h: (0, 0)),
                  pl.BlockSpec((1, Q_LORA), lambda i, h: (0, 0)),
                  pl.BlockSpec((1, KV_LORA), lambda i, h: (0, 0)),
                  pl.BlockSpec((hps, Q_LORA, 2 * LANES), lambda i, h: (h, 0, 0)),
                  pl.BlockSpec((hps, KV_LORA, 2 * LANES), lambda i, h: (h, 0, 0))],
        out_specs=[pl.BlockSpec((1, hps, MLA_QK, tm), lambda i, h: (i // nsb, h, 0, i % nsb)),
                   pl.BlockSpec((1, hps, tm, MLA_QK), lambda i, h: (i // nsb, h, i % nsb, 0)),
                   pl.BlockSpec((1, hps, tm // tk, MLA_VA, tk),
                                lambda i, h: (i // nsb, h, i % nsb, 0, 0))],
        out_shape=[jax.ShapeDtypeStruct((bsz, MLA_HEADS, MLA_QK, seq), BF16),
                   jax.ShapeDtypeStruct((bsz, MLA_HEADS, seq, MLA_QK), BF16),
                   jax.ShapeDtypeStruct((bsz, MLA_HEADS, seq // tk, MLA_VA, tk), BF16)],
        scratch_shapes=[pltpu.VMEM((tm, Q_LORA), BF16),
                        pltpu.VMEM((tm, KV_LORA), BF16),
                        pltpu.VMEM((tm, LANES), F32),
                        pltpu.VMEM((tm, LANES), F32),
                        pltpu.VMEM((tm, LANES), F32)],
        compiler_params=_params(2, 40),
        name="mla_qkv",
    )(proj, proj, misc, pos, freqs, q_g.reshape(1, Q_LORA), kv_g.reshape(1, KV_LORA),
      wq_h, wkv_h)


def _flash_kernel(qt_ref, k_ref, vt_ref, g_ref, wa_ref, wb_ref, o_ref, wab_ref, wbb_ref,
                  m_ref, acc_ref, bias_ref, *, tq, tk, nq, wa_slabs, wb_slabs):
    qi = pl.program_id(2)

    step = (pl.program_id(0) * pl.num_programs(1) + pl.program_id(1)) * nq + qi

    @pl.when(step < wa_slabs)
    def _():
        wab_ref[...] = wa_ref[...].astype(BF16)

    @pl.when(step < wb_slabs)
    def _():
        wbb_ref[...] = wb_ref[...].astype(BF16)

    m_ref[...] = jnp.full(m_ref.shape, -jnp.inf, F32)
    acc_ref[...] = jnp.zeros(acc_ref.shape, F32)
    @pl.when((pl.program_id(0) == 0) & (pl.program_id(1) == 0) & (qi == 0))
    def _():
        key = lax.broadcasted_iota(jnp.int32, (tk, tq), 0)
        qry = lax.broadcasted_iota(jnp.int32, (tk, tq), 1)
        bias_ref[...] = jnp.where(key <= qry, 0.0, -jnp.inf).astype(F32)

    hp = qt_ref.shape[1]

    def scores(hh, kj, c0, masked):
        k = k_ref[0, hh, kj * tk:(kj + 1) * tk, :]
        s = jnp.dot(k, qt_ref[0, hh, :, c0:], preferred_element_type=F32)
        if masked:
            s = s + bias_ref[:, 0:tq - c0]
        return s

    def update(hh, s, kj, c0):
        m_prev = m_ref[hh, :, c0:]
        m_new = jnp.maximum(m_prev, jnp.max(s, axis=0, keepdims=True))
        alpha = jnp.exp2(m_prev - m_new)
        p = jnp.exp2(s - m_new)
        acc_ref[hh, :, c0:] = alpha * acc_ref[hh, :, c0:] + jnp.dot(
            vt_ref[0, hh, kj], p.astype(BF16), preferred_element_type=F32)
        m_ref[hh, :, c0:] = m_new

    per_q = tq // tk

    def run(q):
        def place(b):
            d = b - q * per_q
            return (0, False) if d < 0 else (d * tk, True)

        nblk = (q + 1) * per_q
        s = [scores(hh, 0, *place(0)) for hh in range(hp)]
        for b in range(nblk):
            s_next = ([scores(hh, b + 1, *place(b + 1)) for hh in range(hp)]
                      if b + 1 < nblk else None)
            for hh in range(hp):
                update(hh, s[hh], b, place(b)[0])
            s = s_next

    for q in range(nq):
        pl.when(qi == q)(functools.partial(run, q))

    for hh in range(hp):
        o = acc_ref[hh, 0:MLA_V, :] / acc_ref[hh, MLA_V:MLA_V + 1, :]
        o = o * lax.rsqrt(jnp.mean(o * o, axis=0, keepdims=True) + EPS) * g_ref[hh]
        o_ref[:, hh * MLA_V:(hh + 1) * MLA_V] = o.T.astype(BF16)


def _cast_slab_rows(rows, n_steps):
    bf16_rows = 16
    for slab in range(bf16_rows, rows + 1, bf16_rows):
        if rows % slab == 0 and rows // slab <= n_steps:
            return slab
    raise ValueError("no slab size fits")


def _mla_attention(qt, k, vt, out_g, wa, wb):
    bsz, nh, seq, _ = k.shape
    tq, tk = ATT_TQ, ATT_TK
    nq = seq // tq
    hp = 2
    ng = nh // hp
    n_steps = bsz * ng * nq
    sa = _cast_slab_rows(wa.shape[0], n_steps)
    sb = _cast_slab_rows(wb.shape[0], n_steps)
    na, nb = wa.shape[0] // sa, wb.shape[0] // sb
    kern = functools.partial(_flash_kernel, tq=tq, tk=tk, nq=nq, wa_slabs=na, wb_slabs=nb)
    step = lambda b, h, i: (b * ng + h) * nq + i
    wa_map = lambda b, h, i: (jnp.minimum(step(b, h, i), na - 1), 0)
    wb_map = lambda b, h, i: (jnp.minimum(step(b, h, i), nb - 1), 0)
    return pl.pallas_call(
        kern,
        grid=(bsz, ng, nq),
        in_specs=[pl.BlockSpec((1, hp, MLA_QK, tq), lambda b, h, i: (b, h, 0, i)),
                  pl.BlockSpec((1, hp, seq, MLA_QK), lambda b, h, i: (b, h, 0, 0)),
                  pl.BlockSpec((1, hp, seq // tk, MLA_VA, tk), lambda b, h, i: (b, h, 0, 0, 0)),
                  pl.BlockSpec((hp, MLA_V, 1), lambda b, h, i: (h, 0, 0)),
                  pl.BlockSpec((sa, wa.shape[1]), wa_map),
                  pl.BlockSpec((sb, wb.shape[1]), wb_map)],
        out_specs=[pl.BlockSpec((tq, hp * MLA_V), lambda b, h, i: (b * nq + i, h)),
                   pl.BlockSpec((sa, wa.shape[1]), wa_map),
                   pl.BlockSpec((sb, wb.shape[1]), wb_map)],
        out_shape=[jax.ShapeDtypeStruct((bsz * seq, nh * MLA_V), BF16),
                   jax.ShapeDtypeStruct(wa.shape, BF16),
                   jax.ShapeDtypeStruct(wb.shape, BF16)],
        scratch_shapes=[pltpu.VMEM((hp, 1, tq), F32),
                        pltpu.VMEM((hp, MLA_VA, tq), F32),
                        pltpu.VMEM((tk, tq), F32)],
        compiler_params=_params(3, 48),
        name="mla_attention",
    )(qt, k, vt, out_g.reshape(nh, MLA_V, 1), wa, wb)


def _outproj_kernel(hm_ref, ha_ref, w_ref, x_ref, mod_ref, g_ref, x1_ref, h2_ref, *, sub):
    km = hm_ref.shape[1]
    for s in range(hm_ref.shape[0] // sub):
        rows = pl.ds(s * sub, sub)
        mix = (jnp.dot(hm_ref[rows, :], w_ref[0:km, :], preferred_element_type=F32)
               + jnp.dot(ha_ref[rows, :], w_ref[km:, :], preferred_element_type=F32))
        x1 = x_ref[rows, :] + mod_ref[0, 2:3, :] * mix
        x1_ref[rows, :] = x1
        y = _rms(x1) * g_ref[...]
        h2_ref[rows, :] = (y * (1.0 + mod_ref[0, 4:5, :]) + mod_ref[0, 3:4, :]).astype(BF16)


def _out_projection(hm, ha, w_out, x2, mod, g, seq):
    t, d = x2.shape
    tm = 512
    per_b = seq // tm
    return pl.pallas_call(
        functools.partial(_outproj_kernel, sub=256),
        grid=(t // tm,),
        in_specs=[pl.BlockSpec((tm, hm.shape[1]), lambda i: (i, 0)),
                  pl.BlockSpec((tm, ha.shape[1]), lambda i: (i, 0)),
                  pl.BlockSpec(w_out.shape, lambda i: (0, 0), pipeline_mode=pl.Buffered(1)),
                  pl.BlockSpec((tm, d), lambda i: (i, 0)),
                  pl.BlockSpec((1, 6, d), lambda i: (i // per_b, 0, 0)),
                  pl.BlockSpec((1, d), lambda i: (0, 0))],
        out_specs=[pl.BlockSpec((tm, d), lambda i: (i, 0)),
                   pl.BlockSpec((tm, d), lambda i: (i, 0))],
        out_shape=[jax.ShapeDtypeStruct((t, d), F32),
                   jax.ShapeDtypeStruct((t, d), BF16)],
        compiler_params=_params(1, 48),
        name="out_proj",
    )(hm, ha, w_out, x2, mod, g.reshape(1, d))


def _ffn_up_kernel(h_ref, wv_ref, wg_ref, cwv_ref, cwg_ref, cbv_ref, cbg_ref, o_ref,
                   xv_ref, xg_ref, wvb_ref, wgb_ref, *, tm, blocks_per_seq):
    i = pl.program_id(1)

    @pl.when(i % blocks_per_seq == 0)
    def _():
        xv_ref[0:8, :] = jnp.zeros((8, xv_ref.shape[1]), F32)
        xg_ref[0:8, :] = jnp.zeros((8, xg_ref.shape[1]), F32)

    @pl.when(i == 0)
    def _():
        wvb_ref[...] = wv_ref[...].astype(BF16)
        wgb_ref[...] = wg_ref[...].astype(BF16)

    h = h_ref[...]

    def conv(w_ref, cw_ref, cb_ref, xs_ref):
        kw = cw_ref.shape[0]
        up = jnp.dot(h, w_ref[...], preferred_element_type=F32)
        xs_ref[8:8 + tm, :] = up
        y = cb_ref[...] + cw_ref[kw - 1:kw, :] * up
        for j in range(kw - 1):
            y = y + cw_ref[j:j + 1, :] * xs_ref[pl.ds(8 - (kw - 1) + j, tm), :]
        xs_ref[0:8, :] = xs_ref[tm:tm + 8, :]
        return y

    val = conv(wvb_ref, cwv_ref, cbv_ref, xv_ref)
    gate = conv(wgb_ref, cwg_ref, cbg_ref, xg_ref)
    o_ref[...] = (_silu(gate) * val).astype(BF16)


def _ffn_up(h2, w_up, conv_w, conv_b, seq):
    t, d = h2.shape
    f = w_up.shape[1] // 2
    tm, tf = 1024, 512
    nf = f // tf
    kw = conv_w.shape[0]
    kern = functools.partial(_ffn_up_kernel, tm=tm, blocks_per_seq=seq // tm)
    cb = conv_b.reshape(1, 2 * f)
    return pl.pallas_call(
        kern,
        grid=(nf, t // tm),
        in_specs=[pl.BlockSpec((tm, d), lambda j, i: (i, 0)),
                  pl.BlockSpec((d, tf), lambda j, i: (0, j)),
                  pl.BlockSpec((d, tf), lambda j, i: (0, nf + j)),
                  pl.BlockSpec((kw, tf), lambda j, i: (0, j)),
                  pl.BlockSpec((kw, tf), lambda j, i: (0, nf + j)),
                  pl.BlockSpec((1, tf), lambda j, i: (0, j)),
                  pl.BlockSpec((1, tf), lambda j, i: (0, nf + j))],
        out_specs=pl.BlockSpec((tm, tf), lambda j, i: (i, j)),
        out_shape=jax.ShapeDtypeStruct((t, f), BF16),
        scratch_shapes=[pltpu.VMEM((tm + 8, tf), F32),
                        pltpu.VMEM((tm + 8, tf), F32),
                        pltpu.VMEM((d, tf), BF16),
                        pltpu.VMEM((d, tf), BF16)],
        compiler_params=_params(2, 56),
        name="ffn_up",
    )(h2, w_up, w_up, conv_w, conv_w, cb, cb)


def _ffn_down_kernel(a_ref, w_ref, x1_ref, mod_ref, g_ref, o_ref, *, sub):
    for s in range(a_ref.shape[0] // sub):
        rows = pl.ds(s * sub, sub)
        y = jnp.dot(a_ref[rows, :], w_ref[...], preferred_element_type=F32)
        x2 = x1_ref[rows, :] + mod_ref[0, 5:6, :] * y
        o_ref[rows, :] = _rms(x2) * g_ref[...]


def _ffn_down(act, w_down, x1, mod, g, seq):
    t, f = act.shape
    d = w_down.shape[1]
    tm = 512
    per_b = seq // tm
    return pl.pallas_call(
        functools.partial(_ffn_down_kernel, sub=256),
        grid=(t // tm,),
        in_specs=[pl.BlockSpec((tm, f), lambda i: (i, 0)),
                  pl.BlockSpec((f, d), lambda i: (0, 0), pipeline_mode=pl.Buffered(1)),
                  pl.BlockSpec((tm, d), lambda i: (i, 0)),
                  pl.BlockSpec((1, 6, d), lambda i: (i // per_b, 0, 0)),
                  pl.BlockSpec((1, d), lambda i: (0, 0))],
        out_specs=pl.BlockSpec((tm, d), lambda i: (i, 0)),
        out_shape=jax.ShapeDtypeStruct((t, d), F32),
        compiler_params=_params(1, 58),
        name="ffn_down",
    )(act, w_down, x1, mod, g.reshape(1, d))


def _rot_cols(w):
    half = w.shape[-1] // 2
    return jnp.concatenate([-w[..., half:], w[..., :half]], axis=-1)


def _win_prep_kernel(w_ref, tail_ref, o_ref, *, n_main):
    j = pl.program_id(0)
    gates = 2 * MLSTM_HEADS
    half = MLA_ROPE // 2

    @pl.when(j < n_main)
    def _():
        o_ref[...] = w_ref[...].T.astype(BF16)

    @pl.when(j == n_main)
    def _():
        o_ref[...] = tail_ref[gates:gates + Q_LORA, :].T.astype(BF16)

    @pl.when(j == n_main + 1)
    def _():
        r0 = gates + Q_LORA
        kr0 = r0 + KV_LORA
        blk = o_ref.shape[1]
        used = KV_LORA + 2 * MLA_ROPE + gates
        o_ref[...] = jnp.concatenate(
            [tail_ref[r0:kr0, :],
             tail_ref[kr0:kr0 + MLA_ROPE, :],
             -tail_ref[kr0 + half:kr0 + MLA_ROPE, :],
             tail_ref[kr0:kr0 + half, :],
             tail_ref[0:gates, :],
             jnp.zeros((blk - used, o_ref.shape[0]), F32)], axis=0).T.astype(BF16)


def _prep_w_in(w_in):
    d = w_in.shape[0]
    wt = w_in.T
    n_head = 3 * MLSTM_W
    blk = Q_LORA
    n_main = n_head // blk
    return pl.pallas_call(
        functools.partial(_win_prep_kernel, n_main=n_main),
        grid=(IN_PAD // blk,),
        in_specs=[pl.BlockSpec((blk, d), lambda j: (jnp.minimum(j, n_main - 1), 0)),
                  pl.BlockSpec((wt.shape[0] - n_head, d), lambda j: (0, 0))],
        out_specs=pl.BlockSpec((d, blk), lambda j: (0, j)),
        out_shape=jax.ShapeDtypeStruct((d, IN_PAD), BF16),
        compiler_params=_params(1, 40),
        name="w_in_prep",
    )(wt, wt[n_head:])


def kernel(x, c, positions, ada_w, ada_b, attn_norm_g, w_in, mlstm_conv_w, mlstm_conv_b, mlstm_wq, mlstm_wk, mlstm_igate_b, mlstm_fgate_b, mla_q_norm_g, mla_w_uq, mla_kv_norm_g, mla_w_ukv, mlstm_out_g, mla_out_g, w_out, ffn_norm_g, ffn_w_up, ffn_conv_w, ffn_conv_b, ffn_w_down, final_norm_g):
    bsz, seq, d = x.shape
    t = bsz * seq
    depth = ada_w.shape[0]
    assert depth == 1, "the final RMSNorm is fused into the single layer's down-projection"
    xr = x.reshape(t, d)
    pos = positions.reshape(t // ATT_TQ, 1, ATT_TQ)
    half = MLA_ROPE // 2
    freqs = (ROPE_THETA ** (-jnp.arange(half, dtype=F32) / half)).reshape(half, 1)

    for l in range(depth):
        mod = _modulation(c, ada_w[l], ada_b[l])

        proj, misc = _in_projection(xr, mod, attn_norm_g[l], _prep_w_in(w_in[l]), seq)

        wq_m = mlstm_wq[l].astype(BF16)
        wkt_m = mlstm_wk[l].transpose(0, 2, 1).astype(BF16)
        gate_b = jnp.zeros((1, LANES), F32)
        gate_b = gate_b.at[0, :MLSTM_HEADS].set(mlstm_igate_b[l])
        gate_b = gate_b.at[0, MLSTM_HEADS:2 * MLSTM_HEADS].set(mlstm_fgate_b[l])
        hm = _mlstm(proj, misc, mlstm_conv_w[l], mlstm_conv_b[l], wq_m, wkt_m, gate_b,
                    mlstm_out_g[l], bsz, seq)

        wq = mla_w_uq[l].reshape(Q_LORA, MLA_HEADS, MLA_QK)
        wq_r = wq[..., MLA_NOPE:]
        wq_h = jnp.concatenate([wq[..., :MLA_NOPE], wq_r, _rot_cols(wq_r)], axis=-1)
        wq_h = wq_h.transpose(1, 0, 2).astype(BF16)
        wkv_h = mla_w_ukv[l].reshape(KV_LORA, MLA_HEADS, MLA_NOPE + MLA_V)
        wkv_h = wkv_h.transpose(1, 0, 2).astype(BF16)
        qt, k, vt = _mla_qkv(proj, misc, pos, freqs, mla_q_norm_g[l], mla_kv_norm_g[l],
                             wq_h, wkv_h, bsz, seq)
        ha, w_down_b, w_out_b = _mla_attention(qt, k, vt, mla_out_g[l], ffn_w_down[l], w_out[l])

        x1, h2 = _out_projection(hm, ha, w_out_b, xr, mod, ffn_norm_g[l], seq)

        act = _ffn_up(h2, ffn_w_up[l], ffn_conv_w[l], ffn_conv_b[l], seq)
        xr = _ffn_down(act, w_down_b, x1, mod, final_norm_g, seq)
    return xr.reshape(bsz, seq, d)
```

```python
import functools

import jax
import jax.numpy as jnp
import numpy as np
from jax import lax
from jax.experimental import pallas as pl
from jax.experimental.pallas import tpu as pltpu

F32 = jnp.float32
BF16 = jnp.bfloat16

EPS = 1e-6
ROPE_THETA = 10000.0
MLSTM_HEADS = 4
MLSTM_DH = 256
MLSTM_W = MLSTM_HEADS * MLSTM_DH
MLA_HEADS = 8
MLA_NOPE = 128
MLA_ROPE = 64
MLA_V = 128
MLA_QK = MLA_NOPE + MLA_ROPE
Q_LORA = 512
KV_LORA = 256
LANES = 128
MIB = 1024 * 1024

IN_PAD = 4096
IN_BLOCK = 1024
MISC_W = 256
MISC_OFF = IN_BLOCK - MISC_W


def _params(n_axes, vmem_mib):
    return pltpu.CompilerParams(
        dimension_semantics=("arbitrary",) * n_axes,
        vmem_limit_bytes=vmem_mib * MIB)


def _rms(x):
    return x * lax.rsqrt(jnp.mean(x * x, axis=-1, keepdims=True) + EPS)


def _silu(x):
    return x / (1.0 + jnp.exp(-x))


def _log_sigmoid(x):
    return jnp.minimum(x, 0.0) - jnp.log1p(jnp.exp(-jnp.abs(x)))


def _mod_kernel(c_ref, w_ref, b_ref, o_ref):
    ca = _silu(c_ref[...]).astype(BF16)
    o_ref[...] = jnp.dot(ca, w_ref[...].astype(BF16),
                         preferred_element_type=F32) + b_ref[...]


def _modulation(c, ada_w, ada_b):
    bsz, d = c.shape
    n = ada_w.shape[1]
    tn = 1024
    cp = jnp.zeros((8, d), F32).at[:bsz].set(c)
    out = pl.pallas_call(
        _mod_kernel,
        grid=(n // tn,),
        in_specs=[pl.BlockSpec((8, d), lambda j: (0, 0)),
                  pl.BlockSpec((d, tn), lambda j: (0, j)),
                  pl.BlockSpec((1, tn), lambda j: (0, j))],
        out_specs=pl.BlockSpec((8, tn), lambda j: (0, j)),
        out_shape=jax.ShapeDtypeStruct((8, n), F32),
        compiler_params=_params(1, 40),
        name="adaln_mod",
    )(cp, ada_w, ada_b.reshape(1, n))
    return out[:bsz].reshape(bsz, 6, d)


def _inproj_kernel(x_ref, mod_ref, g_ref, w_ref, o_ref, misc_ref, *, sub):
    tm = x_ref.shape[0]
    nblk = w_ref.shape[1] // IN_BLOCK
    for s in range(tm // sub):
        rows = pl.ds(s * sub, sub)
        y = _rms(x_ref[rows, :]) * g_ref[...]
        h = (y * (1.0 + mod_ref[0, 1:2, :]) + mod_ref[0, 0:1, :]).astype(BF16)
        for j in range(nblk):
            cols = slice(j * IN_BLOCK, (j + 1) * IN_BLOCK)
            acc = jnp.dot(h, w_ref[:, cols], preferred_element_type=F32)
            o_ref[rows, cols] = acc.astype(BF16)
            if j == nblk - 1:
                misc_ref[rows, :] = acc[:, MISC_OFF:]


def _in_projection(x2, mod, g, w_in_p, seq):
    t, d = x2.shape
    tm = 512
    per_b = seq // tm
    return pl.pallas_call(
        functools.partial(_inproj_kernel, sub=256),
        grid=(t // tm,),
        in_specs=[pl.BlockSpec((tm, d), lambda i: (i, 0)),
                  pl.BlockSpec((1, 6, d), lambda i: (i // per_b, 0, 0)),
                  pl.BlockSpec((1, d), lambda i: (0, 0)),
                  pl.BlockSpec((d, IN_PAD), lambda i: (0, 0), pipeline_mode=pl.Buffered(1))],
        out_specs=[pl.BlockSpec((tm, IN_PAD), lambda i: (i, 0)),
                   pl.BlockSpec((tm, MISC_W), lambda i: (i, 0))],
        out_shape=[jax.ShapeDtypeStruct((t, IN_PAD), BF16),
                   jax.ShapeDtypeStruct((t, MISC_W), F32)],
        compiler_params=_params(1, 48),
        name="norm_inproj",
    )(x2, mod, g.reshape(1, d), w_in_p)


def _cumsum_rows(x):
    n = x.shape[0]
    row = lax.broadcasted_iota(jnp.int32, x.shape, 0)
    sh = 1
    while sh < n:
        x = x + jnp.where(row >= sh, pltpu.roll(x, sh, axis=0), 0.0)
        sh *= 2
    return x


def _cummax_rows(x):
    n = x.shape[0]
    row = lax.broadcasted_iota(jnp.int32, x.shape, 0)
    sh = 1
    while sh < n:
        x = jnp.maximum(x, jnp.where(row >= sh, pltpu.roll(x, sh, axis=0), -jnp.inf))
        sh *= 2
    return x


def _cumsum_lanes(x):
    n = x.shape[1]
    col = lax.broadcasted_iota(jnp.int32, x.shape, 1)
    sh = 1
    while sh < n:
        x = x + jnp.where(col >= sh, pltpu.roll(x, sh, axis=1), 0.0)
        sh *= 2
    return x


def _mlstm_kernel(u_ref, v_ref, o_ref, gt_ref, cw_ref, cb_ref, wq_ref, wkt_ref, gb_ref, og_ref,
                  out_ref, xs_ref, ct_ref, n_ref, m_ref, *, ts, chunk):
    i = pl.program_id(0)
    nbatch = u_ref.shape[0]

    @pl.when(i == 0)
    def _():
        for b in range(nbatch):
            xs_ref[b, 0:8, :] = jnp.zeros((8, MLSTM_W), F32)
        ct_ref[...] = jnp.zeros(ct_ref.shape, F32)
        n_ref[...] = jnp.zeros(n_ref.shape, F32)
        m_ref[...] = jnp.zeros(m_ref.shape, F32)

    row = lax.broadcasted_iota(jnp.int32, (chunk, chunk), 0)
    col = lax.broadcasted_iota(jnp.int32, (chunk, chunk), 1)
    causal = col <= row

    for c in range(ts // chunk):
        for b in range(nbatch):
            _mlstm_chunk(u_ref.at[b], v_ref.at[b], o_ref.at[b], gt_ref.at[b], cw_ref, cb_ref,
                         wq_ref, wkt_ref, gb_ref, og_ref, out_ref.at[b], xs_ref.at[b],
                         ct_ref.at[b], n_ref.at[b], m_ref.at[b], c, chunk, causal)
    for b in range(nbatch):
        xs_ref[b, 0:8, :] = xs_ref[b, ts:ts + 8, :]


def _mlstm_chunk(u_ref, v_ref, o_ref, gt_ref, cw_ref, cb_ref, wq_ref, wkt_ref, gb_ref, og_ref,
                 out_ref, xs_ref, ct_ref, n_ref, m_ref, c, chunk, causal):
    nh, dh = MLSTM_HEADS, MLSTM_DH
    gb = gb_ref[...]
    kw = cw_ref.shape[0]
    ones = jnp.ones((chunk, LANES), BF16)
    if True:
        r0 = c * chunk
        xc = u_ref[r0:r0 + chunk, :].astype(F32)
        xs_ref[8 + r0:8 + r0 + chunk, :] = xc
        acc = cb_ref[...] + cw_ref[kw - 1:kw, :] * xc
        for j in range(kw - 1):
            acc = acc + cw_ref[j:j + 1, :] * xs_ref[pl.ds(8 + r0 - (kw - 1) + j, chunk), :]
        su = _silu(acc)

        gc = gt_ref[r0:r0 + chunk, MISC_W - LANES:] + gb
        bc = _cumsum_rows(_log_sigmoid(gc))
        gt = gc.T[0:8, :]
        bt = _cumsum_lanes(_log_sigmoid(gt))
        mdc = _cummax_rows(gc - pltpu.roll(bc, LANES - nh, axis=1))
        for h in range(nh):
            c0 = h * dh
            b_c = bc[:, nh + h:nh + h + 1]
            li_r = gt[h:h + 1, :]
            b_r = bt[nh + h:nh + h + 1, :]
            g_tot = b_r[:, chunk - 1:chunk]
            z_r = li_r - b_r
            zmax = jnp.max(z_r, axis=-1, keepdims=True)
            m_prev = m_ref[h:h + 1, 0:1]

            uh = su[:, c0:c0 + dh]
            qb = jnp.dot(uh.astype(BF16), wq_ref[h], preferred_element_type=F32).astype(BF16)
            kt = jnp.dot(wkt_ref[h], uh.T.astype(BF16),
                         preferred_element_type=F32) * (dh ** -0.5)
            vc = v_ref[r0:r0 + chunk, c0:c0 + dh]

            mx_c = jnp.maximum(m_prev, mdc[:, h:h + 1])
            m_t = b_c + mx_c
            sb = (jnp.dot(qb, kt.astype(BF16), preferred_element_type=F32)
                  * jnp.exp(jnp.where(causal, z_r - mx_c, -jnp.inf))).astype(BF16)
            inter = jnp.exp(m_prev - mx_c)
            ct = ct_ref[h]
            nrep = n_ref[h]
            num = (jnp.dot(sb, vc, preferred_element_type=F32)
                   + inter * jnp.dot(qb, ct.astype(BF16), preferred_element_type=F32))
            den = (jnp.dot(sb, ones, preferred_element_type=F32)
                   + inter * jnp.dot(qb, nrep.astype(BF16), preferred_element_type=F32))
            lim = jnp.maximum(jnp.abs(den), jnp.exp(-m_t))
            hh = num / jnp.concatenate([lim] * (dh // LANES), axis=1)

            mm = jnp.maximum(m_prev, zmax)
            s_prev = jnp.exp(m_prev - mm)
            s_loc = jnp.exp(zmax - mm)
            kwt = (kt * (jnp.exp(z_r - zmax) * s_loc)).astype(BF16)
            ct_ref[h] = s_prev * ct + jnp.dot(kwt, vc, preferred_element_type=F32)
            n_ref[h] = s_prev * nrep + jnp.dot(kwt, ones, preferred_element_type=F32)
            m_ref[h:h + 1, :] = jnp.broadcast_to(g_tot + mm, (1, LANES))

            og = o_ref[r0:r0 + chunk, c0:c0 + dh].astype(F32)
            hh = hh / (1.0 + jnp.exp(-og))
            out_ref[r0:r0 + chunk, c0:c0 + dh] = (
                _rms(hh) * og_ref[:, c0:c0 + dh]).astype(BF16)


def _mlstm(proj, misc, conv_w, conv_b, wq, wkt, gate_b, out_g, bsz, seq):
    ts, chunk = 512, 256
    t = bsz * seq
    kern = functools.partial(_mlstm_kernel, ts=ts, chunk=chunk)
    proj3 = proj.reshape(bsz, seq, proj.shape[1])
    misc3 = misc.reshape(bsz, seq, MISC_W)
    out = pl.pallas_call(
        kern,
        grid=(seq // ts,),
        in_specs=[pl.BlockSpec((bsz, ts, MLSTM_W), lambda i: (0, i, 0)),
                  pl.BlockSpec((bsz, ts, MLSTM_W), lambda i: (0, i, 1)),
                  pl.BlockSpec((bsz, ts, MLSTM_W), lambda i: (0, i, 2)),
                  pl.BlockSpec((bsz, ts, MISC_W), lambda i: (0, i, 0)),
                  pl.BlockSpec(conv_w.shape, lambda i: (0, 0)),
                  pl.BlockSpec((1, MLSTM_W), lambda i: (0, 0)),
                  pl.BlockSpec(wq.shape, lambda i: (0, 0, 0)),
                  pl.BlockSpec(wkt.shape, lambda i: (0, 0, 0)),
                  pl.BlockSpec((1, LANES), lambda i: (0, 0)),
                  pl.BlockSpec((1, MLSTM_W), lambda i: (0, 0))],
        out_specs=pl.BlockSpec((bsz, ts, MLSTM_W), lambda i: (0, i, 0)),
        out_shape=jax.ShapeDtypeStruct((bsz, seq, MLSTM_W), BF16),
        scratch_shapes=[pltpu.VMEM((bsz, ts + 8, MLSTM_W), F32),
                        pltpu.VMEM((bsz, MLSTM_HEADS, MLSTM_DH, MLSTM_DH), F32),
                        pltpu.VMEM((bsz, MLSTM_HEADS, MLSTM_DH, LANES), F32),
                        pltpu.VMEM((bsz, 8, LANES), F32)],
        compiler_params=_params(1, 48),
        name="mlstm",
    )(proj3, proj3, proj3, misc3, conv_w, conv_b.reshape(1, MLSTM_W), wq, wkt, gate_b,
      out_g.reshape(1, MLSTM_W))
    return out.reshape(t, MLSTM_W)


ATT_TQ = 1024
ATT_TK = 256
MLA_VA = MLA_V + 16


def _qkv_kernel(cq_ref, ckv_ref, misc_ref, pos_ref, fr_ref, qg_ref, kvg_ref, wq_ref, wkv_ref,
                qt_ref, k_ref, vt_ref, cqn, ckvn, cos_s, sin_s, kr_s, *, scale, tk):
    h = pl.program_id(1)

    @pl.when(h == 0)
    def _():
        cqn[...] = (_rms(cq_ref[...].astype(F32)) * qg_ref[...]).astype(BF16)
        ckvn[...] = (_rms(ckv_ref[...].astype(F32)) * kvg_ref[...]).astype(BF16)
        ang = fr_ref[...] * pos_ref[0].astype(F32)
        reps = LANES // ang.shape[0]
        cs = jnp.concatenate([jnp.cos(ang)] * reps, axis=0).T
        sn = jnp.concatenate([jnp.sin(ang)] * reps, axis=0).T
        cos_s[...] = cs
        sin_s[...] = sn
        y = misc_ref[:, 0:LANES]
        kr_s[...] = y * cs + pltpu.roll(y, MLA_ROPE, axis=1) * sn

    for hh in range(wq_ref.shape[0]):
        mq = jnp.dot(cqn[...], wq_ref[hh], preferred_element_type=F32)
        qt_ref[0, hh, 0:MLA_NOPE, :] = (mq[:, 0:MLA_NOPE] * scale).T.astype(BF16)
        y = mq[:, MLA_NOPE:]
        r = (y * cos_s[...] + pltpu.roll(y, MLA_ROPE, axis=1) * sin_s[...]) * scale
        qt_ref[0, hh, MLA_NOPE:MLA_QK, :] = r.T[0:MLA_ROPE, :].astype(BF16)

        mkv = jnp.dot(ckvn[...], wkv_ref[hh], preferred_element_type=F32)
        k_ref[0, hh, :, 0:MLA_NOPE] = mkv[:, 0:MLA_NOPE].astype(BF16)
        k_ref[0, hh, :, MLA_NOPE:MLA_QK] = kr_s[:, 0:MLA_ROPE].astype(BF16)
        vt = mkv[:, MLA_NOPE:].T.astype(BF16)
        for j in range(vt.shape[1] // tk):
            vt_ref[0, hh, j, 0:MLA_V, :] = vt[:, j * tk:(j + 1) * tk]
            vt_ref[0, hh, j, MLA_V:, :] = jnp.ones((MLA_VA - MLA_V, tk), BF16)


def _mla_qkv(proj, misc, pos, freqs, q_g, kv_g, wq_h, wkv_h, bsz, seq):
    tm, tk = ATT_TQ, ATT_TK
    hps = MLA_HEADS
    t = bsz * seq
    nsb = seq // tm
    cq_blk = (3 * MLSTM_W) // Q_LORA
    ckv_blk = (3 * MLSTM_W + Q_LORA) // KV_LORA
    kern = functools.partial(_qkv_kernel, scale=MLA_QK ** -0.5 * float(np.log2(np.e)), tk=tk)
    return pl.pallas_call(
        kern,
        grid=(t // tm, MLA_HEADS // hps),
        in_specs=[pl.BlockSpec((tm, Q_LORA), lambda i, h: (i, cq_blk)),
                  pl.BlockSpec((tm, KV_LORA), lambda i, h: (i, ckv_blk)),
                  pl.BlockSpec((tm, MISC_W), lambda i, h: (i, 0)),
                  pl.BlockSpec((1, 1, tm), lambda i, h: (i, 0, 0)),
                  pl.BlockSpec(freqs.shape, lambda i, h: (0, 0)),
                  pl.BlockSpec((1, Q_LORA), lambda i, h: (0, 0)),
                  pl.BlockSpec((1, KV_LORA), lambda i, h: (0, 0)),
                  pl.BlockSpec((hps, Q_LORA, 2 * LANES), lambda i, h: (h, 0, 0)),
                  pl.BlockSpec((hps, KV_LORA, 2 * LANES), lambda i, h: (h, 0, 0))],
        out_specs=[pl.BlockSpec((1, hps, MLA_QK, tm), lambda i, h: (i // nsb, h, 0, i % nsb)),
                   pl.BlockSpec((1, hps, tm, MLA_QK), lambda i, h: (i // nsb, h, i % nsb, 0)),
                   pl.BlockSpec((1, hps, tm // tk, MLA_VA, tk),
                                lambda i, h: (i // nsb, h, i % nsb, 0, 0))],
        out_shape=[jax.ShapeDtypeStruct((bsz, MLA_HEADS, MLA_QK, seq), BF16),
                   jax.ShapeDtypeStruct((bsz, MLA_HEADS, seq, MLA_QK), BF16),
                   jax.ShapeDtypeStruct((bsz, MLA_HEADS, seq // tk, MLA_VA, tk), BF16)],
        scratch_shapes=[pltpu.VMEM((tm, Q_LORA), BF16),
                        pltpu.VMEM((tm, KV_LORA), BF16),
                        pltpu.VMEM((tm, LANES), F32),
                        pltpu.VMEM((tm, LANES), F32),
                        pltpu.VMEM((tm, LANES), F32)],
        compiler_params=_params(2, 40),
        name="mla_qkv",
    )(proj, proj, misc, pos, freqs, q_g.reshape(1, Q_LORA), kv_g.reshape(1, KV_LORA),
      wq_h, wkv_h)


def _flash_kernel(qt_ref, k_ref, vt_ref, g_ref, wa_ref, wb_ref, o_ref, wab_ref, wbb_ref,
                  m_ref, acc_ref, bias_ref, *, tq, tk, nq, wa_slabs, wb_slabs):
    qi = pl.program_id(2)

    step = (pl.program_id(0) * pl.num_programs(1) + pl.program_id(1)) * nq + qi

    @pl.when(step < wa_slabs)
    def _():
        wab_ref[...] = wa_ref[...].astype(BF16)

    @pl.when(step < wb_slabs)
    def _():
        wbb_ref[...] = wb_ref[...].astype(BF16)

    m_ref[...] = jnp.full(m_ref.shape, -jnp.inf, F32)
    acc_ref[...] = jnp.zeros(acc_ref.shape, F32)
    @pl.when((pl.program_id(0) == 0) & (pl.program_id(1) == 0) & (qi == 0))
    def _():
        key = lax.broadcasted_iota(jnp.int32, (tk, tq), 0)
        qry = lax.broadcasted_iota(jnp.int32, (tk, tq), 1)
        bias_ref[...] = jnp.where(key <= qry, 0.0, -jnp.inf).astype(F32)

    hp = qt_ref.shape[1]

    def scores(hh, kj, c0, masked):
        k = k_ref[0, hh, kj * tk:(kj + 1) * tk, :]
        s = jnp.dot(k, qt_ref[0, hh, :, c0:], preferred_element_type=F32)
        if masked:
            s = s + bias_ref[:, 0:tq - c0]
        return s

    def update(hh, s, kj, c0):
        m_prev = m_ref[hh, :, c0:]
        m_new = jnp.maximum(m_prev, jnp.max(s, axis=0, keepdims=True))
        alpha = jnp.exp2(m_prev - m_new)
        p = jnp.exp2(s - m_new)
        acc_ref[hh, :, c0:] = alpha * acc_ref[hh, :, c0:] + jnp.dot(
            vt_ref[0, hh, kj], p.astype(BF16), preferred_element_type=F32)
        m_ref[hh, :, c0:] = m_new

    per_q = tq // tk

    def run(q):
        def place(b):
            d = b - q * per_q
            return (0, False) if d < 0 else (d * tk, True)

        nblk = (q + 1) * per_q
        s = [scores(hh, 0, *place(0)) for hh in range(hp)]
        for b in range(nblk):
            s_next = ([scores(hh, b + 1, *place(b + 1)) for hh in range(hp)]
                      if b + 1 < nblk else None)
            for hh in range(hp):
                update(hh, s[hh], b, place(b)[0])
            s = s_next

    for q in range(nq):
        pl.when(qi == q)(functools.partial(run, q))

    for hh in range(hp):
        o = acc_ref[hh, 0:MLA_V, :] / acc_ref[hh, MLA_V:MLA_V + 1, :]
        o = o * lax.rsqrt(jnp.mean(o * o, axis=0, keepdims=True) + EPS) * g_ref[hh]
        o_ref[:, hh * MLA_V:(hh + 1) * MLA_V] = o.T.astype(BF16)


def _cast_slab_rows(rows, n_steps):
    bf16_rows = 16
    for slab in range(bf16_rows, rows + 1, bf16_rows):
        if rows % slab == 0 and rows // slab <= n_steps:
            return slab
    raise ValueError("no slab size fits")


def _mla_attention(qt, k, vt, out_g, wa, wb):
    bsz, nh, seq, _ = k.shape
    tq, tk = ATT_TQ, ATT_TK
    nq = seq // tq
    hp = 2
    ng = nh // hp
    n_steps = bsz * ng * nq
    sa = _cast_slab_rows(wa.shape[0], n_steps)
    sb = _cast_slab_rows(wb.shape[0], n_steps)
    na, nb = wa.shape[0] // sa, wb.shape[0] // sb
    kern = functools.partial(_flash_kernel, tq=tq, tk=tk, nq=nq, wa_slabs=na, wb_slabs=nb)
    step = lambda b, h, i: (b * ng + h) * nq + i
    wa_map = lambda b, h, i: (jnp.minimum(step(b, h, i), na - 1), 0)
    wb_map = lambda b, h, i: (jnp.minimum(step(b, h, i), nb - 1), 0)
    return pl.pallas_call(
        kern,
        grid=(bsz, ng, nq),
        in_specs=[pl.BlockSpec((1, hp, MLA_QK, tq), lambda b, h, i: (b, h, 0, i)),
                  pl.BlockSpec((1, hp, seq, MLA_QK), lambda b, h, i: (b, h, 0, 0)),
                  pl.BlockSpec((1, hp, seq // tk, MLA_VA, tk), lambda b, h, i: (b, h, 0, 0, 0)),
                  pl.BlockSpec((hp, MLA_V, 1), lambda b, h, i: (h, 0, 0)),
                  pl.BlockSpec((sa, wa.shape[1]), wa_map),
                  pl.BlockSpec((sb, wb.shape[1]), wb_map)],
        out_specs=[pl.BlockSpec((tq, hp * MLA_V), lambda b, h, i: (b * nq + i, h)),
                   pl.BlockSpec((sa, wa.shape[1]), wa_map),
                   pl.BlockSpec((sb, wb.shape[1]), wb_map)],
        out_shape=[jax.ShapeDtypeStruct((bsz * seq, nh * MLA_V), BF16),
                   jax.ShapeDtypeStruct(wa.shape, BF16),
                   jax.ShapeDtypeStruct(wb.shape, BF16)],
        scratch_shapes=[pltpu.VMEM((hp, 1, tq), F32),
                        pltpu.VMEM((hp, MLA_VA, tq), F32),
                        pltpu.VMEM((tk, tq), F32)],
        compiler_params=_params(3, 48),
        name="mla_attention",
    )(qt, k, vt, out_g.reshape(nh, MLA_V, 1), wa, wb)


def _outproj_kernel(hm_ref, ha_ref, w_ref, x_ref, mod_ref, g_ref, x1_ref, h2_ref, *, sub):
    km = hm_ref.shape[1]
    for s in range(hm_ref.shape[0] // sub):
        rows = pl.ds(s * sub, sub)
        mix = (jnp.dot(hm_ref[rows, :], w_ref[0:km, :], preferred_element_type=F32)
               + jnp.dot(ha_ref[rows, :], w_ref[km:, :], preferred_element_type=F32))
        x1 = x_ref[rows, :] + mod_ref[0, 2:3, :] * mix
        x1_ref[rows, :] = x1
        y = _rms(x1) * g_ref[...]
        h2_ref[rows, :] = (y * (1.0 + mod_ref[0, 4:5, :]) + mod_ref[0, 3:4, :]).astype(BF16)


def _out_projection(hm, ha, w_out, x2, mod, g, seq):
    t, d = x2.shape
    tm = 512
    per_b = seq // tm
    return pl.pallas_call(
        functools.partial(_outproj_kernel, sub=256),
        grid=(t // tm,),
        in_specs=[pl.BlockSpec((tm, hm.shape[1]), lambda i: (i, 0)),
                  pl.BlockSpec((tm, ha.shape[1]), lambda i: (i, 0)),
                  pl.BlockSpec(w_out.shape, lambda i: (0, 0), pipeline_mode=pl.Buffered(1)),
                  pl.BlockSpec((tm, d), lambda i: (i, 0)),
                  pl.BlockSpec((1, 6, d), lambda i: (i // per_b, 0, 0)),
                  pl.BlockSpec((1, d), lambda i: (0, 0))],
        out_specs=[pl.BlockSpec((tm, d), lambda i: (i, 0)),
                   pl.BlockSpec((tm, d), lambda i: (i, 0))],
        out_shape=[jax.ShapeDtypeStruct((t, d), F32),
                   jax.ShapeDtypeStruct((t, d), BF16)],
        compiler_params=_params(1, 48),
        name="out_proj",
    )(hm, ha, w_out, x2, mod, g.reshape(1, d))


def _ffn_up_kernel(h_ref, wv_ref, wg_ref, cwv_ref, cwg_ref, cbv_ref, cbg_ref, o_ref,
                   xv_ref, xg_ref, wvb_ref, wgb_ref, *, tm, blocks_per_seq):
    i = pl.program_id(1)

    @pl.when(i % blocks_per_seq == 0)
    def _():
        xv_ref[0:8, :] = jnp.zeros((8, xv_ref.shape[1]), F32)
        xg_ref[0:8, :] = jnp.zeros((8, xg_ref.shape[1]), F32)

    @pl.when(i == 0)
    def _():
        wvb_ref[...] = wv_ref[...].astype(BF16)
        wgb_ref[...] = wg_ref[...].astype(BF16)

    h = h_ref[...]

    def conv(w_ref, cw_ref, cb_ref, xs_ref):
        kw = cw_ref.shape[0]
        up = jnp.dot(h, w_ref[...], preferred_element_type=F32)
        xs_ref[8:8 + tm, :] = up
        y = cb_ref[...] + cw_ref[kw - 1:kw, :] * up
        for j in range(kw - 1):
            y = y + cw_ref[j:j + 1, :] * xs_ref[pl.ds(8 - (kw - 1) + j, tm), :]
        xs_ref[0:8, :] = xs_ref[tm:tm + 8, :]
        return y

    val = conv(wvb_ref, cwv_ref, cbv_ref, xv_ref)
    gate = conv(wgb_ref, cwg_ref, cbg_ref, xg_ref)
    o_ref[...] = (_silu(gate) * val).astype(BF16)


def _ffn_up(h2, w_up, conv_w, conv_b, seq):
    t, d = h2.shape
    f = w_up.shape[1] // 2
    tm, tf = 1024, 512
    nf = f // tf
    kw = conv_w.shape[0]
    kern = functools.partial(_ffn_up_kernel, tm=tm, blocks_per_seq=seq // tm)
    cb = conv_b.reshape(1, 2 * f)
    return pl.pallas_call(
        kern,
        grid=(nf, t // tm),
        in_specs=[pl.BlockSpec((tm, d), lambda j, i: (i, 0)),
                  pl.BlockSpec((d, tf), lambda j, i: (0, j)),
                  pl.BlockSpec((d, tf), lambda j, i: (0, nf + j)),
                  pl.BlockSpec((kw, tf), lambda j, i: (0, j)),
                  pl.BlockSpec((kw, tf), lambda j, i: (0, nf + j)),
                  pl.BlockSpec((1, tf), lambda j, i: (0, j)),
                  pl.BlockSpec((1, tf), lambda j, i: (0, nf + j))],
        out_specs=pl.BlockSpec((tm, tf), lambda j, i: (i, j)),
        out_shape=jax.ShapeDtypeStruct((t, f), BF16),
        scratch_shapes=[pltpu.VMEM((tm + 8, tf), F32),
                        pltpu.VMEM((tm + 8, tf), F32),
                        pltpu.VMEM((d, tf), BF16),
                        pltpu.VMEM((d, tf), BF16)],
        compiler_params=_params(2, 56),
        name="ffn_up",
    )(h2, w_up, w_up, conv_w, conv_w, cb, cb)


def _ffn_down_kernel(a_ref, w_ref, x1_ref, mod_ref, g_ref, o_ref, *, sub):
    for s in range(a_ref.shape[0] // sub):
        rows = pl.ds(s * sub, sub)
        y = jnp.dot(a_ref[rows, :], w_ref[...], preferred_element_type=F32)
        x2 = x1_ref[rows, :] + mod_ref[0, 5:6, :] * y
        o_ref[rows, :] = _rms(x2) * g_ref[...]


def _ffn_down(act, w_down, x1, mod, g, seq):
    t, f = act.shape
    d = w_down.shape[1]
    tm = 512
    per_b = seq // tm
    return pl.pallas_call(
        functools.partial(_ffn_down_kernel, sub=256),
        grid=(t // tm,),
        in_specs=[pl.BlockSpec((tm, f), lambda i: (i, 0)),
                  pl.BlockSpec((f, d), lambda i: (0, 0), pipeline_mode=pl.Buffered(1)),
                  pl.BlockSpec((tm, d), lambda i: (i, 0)),
                  pl.BlockSpec((1, 6, d), lambda i: (i // per_b, 0, 0)),
                  pl.BlockSpec((1, d), lambda i: (0, 0))],
        out_specs=pl.BlockSpec((tm, d), lambda i: (i, 0)),
        out_shape=jax.ShapeDtypeStruct((t, d), F32),
        compiler_params=_params(1, 58),
        name="ffn_down",
    )(act, w_down, x1, mod, g.reshape(1, d))


def _rot_cols(w):
    half = w.shape[-1] // 2
    return jnp.concatenate([-w[..., half:], w[..., :half]], axis=-1)


def _win_prep_kernel(w_ref, tail_ref, o_ref, *, n_main):
    j = pl.program_id(0)
    gates = 2 * MLSTM_HEADS
    half = MLA_ROPE // 2

    @pl.when(j < n_main)
    def _():
        o_ref[...] = w_ref[...].T.astype(BF16)

    @pl.when(j == n_main)
    def _():
        o_ref[...] = tail_ref[gates:gates + Q_LORA, :].T.astype(BF16)

    @pl.when(j == n_main + 1)
    def _():
        r0 = gates + Q_LORA
        kr0 = r0 + KV_LORA
        blk = o_ref.shape[1]
        used = KV_LORA + 2 * MLA_ROPE + gates
        o_ref[...] = jnp.concatenate(
            [tail_ref[r0:kr0, :],
             tail_ref[kr0:kr0 + MLA_ROPE, :],
             -tail_ref[kr0 + half:kr0 + MLA_ROPE, :],
             tail_ref[kr0:kr0 + half, :],
             tail_ref[0:gates, :],
             jnp.zeros((blk - used, o_ref.shape[0]), F32)], axis=0).T.astype(BF16)


def _prep_w_in(w_in):
    d = w_in.shape[0]
    wt = w_in.T
    n_head = 3 * MLSTM_W
    blk = Q_LORA
    n_main = n_head // blk
    return pl.pallas_call(
        functools.partial(_win_prep_kernel, n_main=n_main),
        grid=(IN_PAD // blk,),
        in_specs=[pl.BlockSpec((blk, d), lambda j: (jnp.minimum(j, n_main - 1), 0)),
                  pl.BlockSpec((wt.shape[0] - n_head, d), lambda j: (0, 0))],
        out_specs=pl.BlockSpec((d, blk), lambda j: (0, j)),
        out_shape=jax.ShapeDtypeStruct((d, IN_PAD), BF16),
        compiler_params=_params(1, 40),
        name="w_in_prep",
    )(wt, wt[n_head:])


def kernel(x, c, positions, ada_w, ada_b, attn_norm_g, w_in, mlstm_conv_w, mlstm_conv_b, mlstm_wq, mlstm_wk, mlstm_igate_b, mlstm_fgate_b, mla_q_norm_g, mla_w_uq, mla_kv_norm_g, mla_w_ukv, mlstm_out_g, mla_out_g, w_out, ffn_norm_g, ffn_w_up, ffn_conv_w, ffn_conv_b, ffn_w_down, final_norm_g):
    bsz, seq, d = x.shape
    t = bsz * seq
    depth = ada_w.shape[0]
    assert depth == 1, "the final RMSNorm is fused into the single layer's down-projection"
    xr = x.reshape(t, d)
    pos = positions.reshape(t // ATT_TQ, 1, ATT_TQ)
    half = MLA_ROPE // 2
    freqs = (ROPE_THETA ** (-jnp.arange(half, dtype=F32) / half)).reshape(half, 1)

    for l in range(depth):
        mod = _modulation(c, ada_w[l], ada_b[l])

        proj, misc = _in_projection(xr, mod, attn_norm_g[l], _prep_w_in(w_in[l]), seq)

        wq_m = mlstm_wq[l].astype(BF16)
        wkt_m = mlstm_wk[l].transpose(0, 2, 1).astype(BF16)
        gate_b = jnp.zeros((1, LANES), F32)
        gate_b = gate_b.at[0, :MLSTM_HEADS].set(mlstm_igate_b[l])
        gate_b = gate_b.at[0, MLSTM_HEADS:2 * MLSTM_HEADS].set(mlstm_fgate_b[l])
        hm = _mlstm(proj, misc, mlstm_conv_w[l], mlstm_conv_b[l], wq_m, wkt_m, gate_b,
                    mlstm_out_g[l], bsz, seq)

        wq = mla_w_uq[l].reshape(Q_LORA, MLA_HEADS, MLA_QK)
        wq_r = wq[..., MLA_NOPE:]
        wq_h = jnp.concatenate([wq[..., :MLA_NOPE], wq_r, _rot_cols(wq_r)], axis=-1)
        wq_h = wq_h.transpose(1, 0, 2).astype(BF16)
        wkv_h = mla_w_ukv[l].reshape(KV_LORA, MLA_HEADS, MLA_NOPE + MLA_V)
        wkv_h = wkv_h.transpose(1, 0, 2).astype(BF16)
        qt, k, vt = _mla_qkv(proj, misc, pos, freqs, mla_q_norm_g[l], mla_kv_norm_g[l],
                             wq_h, wkv_h, bsz, seq)
        ha, w_down_b, w_out_b = _mla_attention(qt, k, vt, mla_out_g[l], ffn_w_down[l], w_out[l])

        x1, h2 = _out_projection(hm, ha, w_out_b, xr, mod, ffn_norm_g[l], seq)

        act = _ffn_up(h2, ffn_w_up[l], ffn_conv_w[l], ffn_conv_b[l], seq)
        xr = _ffn_down(act, w_down_b, x1, mod, final_norm_g, seq)
    return xr.reshape(bsz, seq, d)
```

```python
import functools

import jax
import jax.numpy as jnp
import numpy as np
from jax import lax
from jax.experimental import pallas as pl
from jax.experimental.pallas import tpu as pltpu

F32 = jnp.float32
BF16 = jnp.bfloat16

EPS = 1e-6
ROPE_THETA = 10000.0
MLSTM_HEADS = 4
MLSTM_DH = 256
MLSTM_W = MLSTM_HEADS * MLSTM_DH
MLA_HEADS = 8
MLA_NOPE = 128
MLA_ROPE = 64
MLA_V = 128
MLA_QK = MLA_NOPE + MLA_ROPE
Q_LORA = 512
KV_LORA = 256
LANES = 128
MIB = 1024 * 1024

IN_PAD = 4096
IN_BLOCK = 1024
MISC_W = 256
MISC_OFF = IN_BLOCK - MISC_W


def _params(n_axes, vmem_mib):
    return pltpu.CompilerParams(
        dimension_semantics=("arbitrary",) * n_axes,
        vmem_limit_bytes=vmem_mib * MIB)


def _rms(x):
    return x * lax.rsqrt(jnp.mean(x * x, axis=-1, keepdims=True) + EPS)


def _silu(x):
    return x / (1.0 + jnp.exp(-x))


def _log_sigmoid(x):
    return jnp.minimum(x, 0.0) - jnp.log1p(jnp.exp(-jnp.abs(x)))


def _mod_kernel(c_ref, w_ref, b_ref, o_ref):
    ca = _silu(c_ref[...]).astype(BF16)
    o_ref[...] = jnp.dot(ca, w_ref[...].astype(BF16),
                         preferred_element_type=F32) + b_ref[...]


def _modulation(c, ada_w, ada_b):
    bsz, d = c.shape
    n = ada_w.shape[1]
    tn = 1024
    cp = jnp.pad(c, ((0, 8 - bsz), (0, 0)))
    out = pl.pallas_call(
        _mod_kernel,
        grid=(n // tn,),
        in_specs=[pl.BlockSpec((8, d), lambda j: (0, 0)),
                  pl.BlockSpec((d, tn), lambda j: (0, j)),
                  pl.BlockSpec((1, tn), lambda j: (0, j))],
        out_specs=pl.BlockSpec((8, tn), lambda j: (0, j)),
        out_shape=jax.ShapeDtypeStruct((8, n), F32),
        compiler_params=_params(1, 40),
        name="adaln_mod",
    )(cp, ada_w, ada_b.reshape(1, n))
    return out[:bsz].reshape(bsz, 6, d)


def _inproj_kernel(x_ref, mod_ref, g_ref, w_ref, o_ref, misc_ref, *, sub):
    tm = x_ref.shape[0]
    nblk = w_ref.shape[1] // IN_BLOCK
    for s in range(tm // sub):
        rows = pl.ds(s * sub, sub)
        y = _rms(x_ref[rows, :]) * g_ref[...]
        h = (y * (1.0 + mod_ref[0, 1:2, :]) + mod_ref[0, 0:1, :]).astype(BF16)
        for j in range(nblk):
            cols = slice(j * IN_BLOCK, (j + 1) * IN_BLOCK)
            acc = jnp.dot(h, w_ref[:, cols], preferred_element_type=F32)
            o_ref[rows, cols] = acc.astype(BF16)
            if j == nblk - 1:
                misc_ref[rows, :] = acc[:, MISC_OFF:]


def _in_projection(x2, mod, g, w_in_p, seq):
    t, d = x2.shape
    tm = 512
    per_b = seq // tm
    return pl.pallas_call(
        functools.partial(_inproj_kernel, sub=256),
        grid=(t // tm,),
        in_specs=[pl.BlockSpec((tm, d), lambda i: (i, 0)),
                  pl.BlockSpec((1, 6, d), lambda i: (i // per_b, 0, 0)),
                  pl.BlockSpec((1, d), lambda i: (0, 0)),
                  pl.BlockSpec((d, IN_PAD), lambda i: (0, 0), pipeline_mode=pl.Buffered(1))],
        out_specs=[pl.BlockSpec((tm, IN_PAD), lambda i: (i, 0)),
                   pl.BlockSpec((tm, MISC_W), lambda i: (i, 0))],
        out_shape=[jax.ShapeDtypeStruct((t, IN_PAD), BF16),
                   jax.ShapeDtypeStruct((t, MISC_W), F32)],
        compiler_params=_params(1, 48),
        name="norm_inproj",
    )(x2, mod, g.reshape(1, d), w_in_p)


def _cumsum_rows(x):
    n = x.shape[0]
    row = lax.broadcasted_iota(jnp.int32, x.shape, 0)
    sh = 1
    while sh < n:
        x = x + jnp.where(row >= sh, pltpu.roll(x, sh, axis=0), 0.0)
        sh *= 2
    return x


def _cummax_rows(x):
    n = x.shape[0]
    row = lax.broadcasted_iota(jnp.int32, x.shape, 0)
    sh = 1
    while sh < n:
        x = jnp.maximum(x, jnp.where(row >= sh, pltpu.roll(x, sh, axis=0), -jnp.inf))
        sh *= 2
    return x


def _cumsum_lanes(x):
    n = x.shape[1]
    col = lax.broadcasted_iota(jnp.int32, x.shape, 1)
    sh = 1
    while sh < n:
        x = x + jnp.where(col >= sh, pltpu.roll(x, sh, axis=1), 0.0)
        sh *= 2
    return x


def _mlstm_kernel(u_ref, v_ref, o_ref, gt_ref, cw_ref, cb_ref, wq_ref, wkt_ref, gb_ref, og_ref,
                  out_ref, xs_ref, ct_ref, n_ref, m_ref, *, ts, chunk):
    i = pl.program_id(0)
    nbatch = u_ref.shape[0]

    @pl.when(i == 0)
    def _():
        for b in range(nbatch):
            xs_ref[b, 0:8, :] = jnp.zeros((8, MLSTM_W), F32)
        ct_ref[...] = jnp.zeros(ct_ref.shape, F32)
        n_ref[...] = jnp.zeros(n_ref.shape, F32)
        m_ref[...] = jnp.zeros(m_ref.shape, F32)

    row = lax.broadcasted_iota(jnp.int32, (chunk, chunk), 0)
    col = lax.broadcasted_iota(jnp.int32, (chunk, chunk), 1)
    causal = col <= row

    for c in range(ts // chunk):
        for b in range(nbatch):
            _mlstm_chunk(u_ref.at[b], v_ref.at[b], o_ref.at[b], gt_ref.at[b], cw_ref, cb_ref,
                         wq_ref, wkt_ref, gb_ref, og_ref, out_ref.at[b], xs_ref.at[b],
                         ct_ref.at[b], n_ref.at[b], m_ref.at[b], c, chunk, causal)
    for b in range(nbatch):
        xs_ref[b, 0:8, :] = xs_ref[b, ts:ts + 8, :]


def _mlstm_chunk(u_ref, v_ref, o_ref, gt_ref, cw_ref, cb_ref, wq_ref, wkt_ref, gb_ref, og_ref,
                 out_ref, xs_ref, ct_ref, n_ref, m_ref, c, chunk, causal):
    nh, dh = MLSTM_HEADS, MLSTM_DH
    gb = gb_ref[...]
    kw = cw_ref.shape[0]
    ones = jnp.ones((chunk, LANES), BF16)
    if True:
        r0 = c * chunk
        xc = u_ref[r0:r0 + chunk, :].astype(F32)
        xs_ref[8 + r0:8 + r0 + chunk, :] = xc
        acc = cb_ref[...] + cw_ref[kw - 1:kw, :] * xc
        for j in range(kw - 1):
            acc = acc + cw_ref[j:j + 1, :] * xs_ref[pl.ds(8 + r0 - (kw - 1) + j, chunk), :]
        su = _silu(acc)

        gc = gt_ref[r0:r0 + chunk, MISC_W - LANES:] + gb
        bc = _cumsum_rows(_log_sigmoid(gc))
        gt = gc.T[0:8, :]
        bt = _cumsum_lanes(_log_sigmoid(gt))
        mdc = _cummax_rows(gc - pltpu.roll(bc, LANES - nh, axis=1))
        for h in range(nh):
            c0 = h * dh
            b_c = bc[:, nh + h:nh + h + 1]
            li_r = gt[h:h + 1, :]
            b_r = bt[nh + h:nh + h + 1, :]
            g_tot = b_r[:, chunk - 1:chunk]
            z_r = li_r - b_r
            zmax = jnp.max(z_r, axis=-1, keepdims=True)
            m_prev = m_ref[h:h + 1, 0:1]

            ub = su[:, c0:c0 + dh].astype(BF16)
            qb = jnp.dot(ub, wq_ref[h], preferred_element_type=F32).astype(BF16)
            kt = jnp.dot(wkt_ref[h], ub.T,
                         preferred_element_type=F32) * (dh ** -0.5)
            vc = v_ref[r0:r0 + chunk, c0:c0 + dh]

            mx_c = jnp.maximum(m_prev, mdc[:, h:h + 1])
            m_t = b_c + mx_c
            sb = (jnp.dot(qb, kt.astype(BF16), preferred_element_type=F32)
                  * jnp.exp(jnp.where(causal, z_r - mx_c, -jnp.inf))).astype(BF16)
            inter = jnp.exp(m_prev - mx_c)
            ct = ct_ref[h]
            nrep = n_ref[h]
            num = (jnp.dot(sb, vc, preferred_element_type=F32)
                   + inter * jnp.dot(qb, ct.astype(BF16), preferred_element_type=F32))
            den = (jnp.dot(sb, ones, preferred_element_type=F32)
                   + inter * jnp.dot(qb, nrep.astype(BF16), preferred_element_type=F32))
            lim = jnp.maximum(jnp.abs(den), jnp.exp(-m_t))
            hh = num / jnp.concatenate([lim] * (dh // LANES), axis=1)

            mm = jnp.maximum(m_prev, zmax)
            s_prev = jnp.exp(m_prev - mm)
            s_loc = jnp.exp(zmax - mm)
            kwt = (kt * (jnp.exp(z_r - zmax) * s_loc)).astype(BF16)
            ct_ref[h] = s_prev * ct + jnp.dot(kwt, vc, preferred_element_type=F32)
            n_ref[h] = s_prev * nrep + jnp.dot(kwt, ones, preferred_element_type=F32)
            m_ref[h:h + 1, :] = jnp.broadcast_to(g_tot + mm, (1, LANES))

            og = o_ref[r0:r0 + chunk, c0:c0 + dh].astype(F32)
            hh = hh / (1.0 + jnp.exp(-og))
            out_ref[r0:r0 + chunk, c0:c0 + dh] = (
                _rms(hh) * og_ref[:, c0:c0 + dh]).astype(BF16)


def _mlstm(proj, misc, conv_w, conv_b, wq, wkt, gate_b, out_g, bsz, seq):
    ts, chunk = 512, 256
    t = bsz * seq
    kern = functools.partial(_mlstm_kernel, ts=ts, chunk=chunk)
    proj3 = proj.reshape(bsz, seq, proj.shape[1])
    misc3 = misc.reshape(bsz, seq, MISC_W)
    out = pl.pallas_call(
        kern,
        grid=(seq // ts,),
        in_specs=[pl.BlockSpec((bsz, ts, MLSTM_W), lambda i: (0, i, 0)),
                  pl.BlockSpec((bsz, ts, MLSTM_W), lambda i: (0, i, 1)),
                  pl.BlockSpec((bsz, ts, MLSTM_W), lambda i: (0, i, 2)),
                  pl.BlockSpec((bsz, ts, MISC_W), lambda i: (0, i, 0)),
                  pl.BlockSpec(conv_w.shape, lambda i: (0, 0)),
                  pl.BlockSpec((1, MLSTM_W), lambda i: (0, 0)),
                  pl.BlockSpec(wq.shape, lambda i: (0, 0, 0)),
                  pl.BlockSpec(wkt.shape, lambda i: (0, 0, 0)),
                  pl.BlockSpec((1, LANES), lambda i: (0, 0)),
                  pl.BlockSpec((1, MLSTM_W), lambda i: (0, 0))],
        out_specs=pl.BlockSpec((bsz, ts, MLSTM_W), lambda i: (0, i, 0)),
        out_shape=jax.ShapeDtypeStruct((bsz, seq, MLSTM_W), BF16),
        scratch_shapes=[pltpu.VMEM((bsz, ts + 8, MLSTM_W), F32),
                        pltpu.VMEM((bsz, MLSTM_HEADS, MLSTM_DH, MLSTM_DH), F32),
                        pltpu.VMEM((bsz, MLSTM_HEADS, MLSTM_DH, LANES), F32),
                        pltpu.VMEM((bsz, 8, LANES), F32)],
        compiler_params=_params(1, 48),
        name="mlstm",
    )(proj3, proj3, proj3, misc3, conv_w, conv_b.reshape(1, MLSTM_W), wq, wkt, gate_b,
      out_g.reshape(1, MLSTM_W))
    return out.reshape(t, MLSTM_W)


ATT_TQ = 1024
QKV_TM = 1024
ATT_TK = 256
MLA_VA = MLA_V + 16


def _qkv_kernel(cq_ref, ckv_ref, misc_ref, pos_ref, fr_ref, qg_ref, kvg_ref, wq_ref, wkv_ref,
                qt_ref, k_ref, vt_ref, cqn, ckvn, cos_s, sin_s, kr_s, *, scale, tk):
    h = pl.program_id(1)

    @pl.when(h == 0)
    def _():
        cqn[...] = (_rms(cq_ref[...].astype(F32)) * qg_ref[...]).astype(BF16)
        ckvn[...] = (_rms(ckv_ref[...].astype(F32)) * kvg_ref[...]).astype(BF16)
        ang = fr_ref[...] * pos_ref[0].astype(F32)
        reps = LANES // ang.shape[0]
        cs = jnp.concatenate([jnp.cos(ang)] * reps, axis=0).T
        sn = jnp.concatenate([jnp.sin(ang)] * reps, axis=0).T
        cos_s[...] = cs
        sin_s[...] = sn
        y = misc_ref[:, 0:LANES]
        kr_s[...] = y * cs + pltpu.roll(y, MLA_ROPE, axis=1) * sn

    for hh in range(wq_ref.shape[0]):
        mq = jnp.dot(cqn[...], wq_ref[hh], preferred_element_type=F32)
        qt_ref[0, hh, 0:MLA_NOPE, :] = (mq[:, 0:MLA_NOPE] * scale).T.astype(BF16)
        y = mq[:, MLA_NOPE:]
        r = (y * cos_s[...] + pltpu.roll(y, MLA_ROPE, axis=1) * sin_s[...]) * scale
        qt_ref[0, hh, MLA_NOPE:MLA_QK, :] = r.T[0:MLA_ROPE, :].astype(BF16)

        mkv = jnp.dot(ckvn[...], wkv_ref[hh], preferred_element_type=F32)
        k_ref[0, hh, :, 0:MLA_NOPE] = mkv[:, 0:MLA_NOPE].astype(BF16)
        k_ref[0, hh, :, MLA_NOPE:MLA_QK] = kr_s[:, 0:MLA_ROPE].astype(BF16)
        vt = mkv[:, MLA_NOPE:].T.astype(BF16)
        for j in range(vt.shape[1] // tk):
            vt_ref[0, hh, j, 0:MLA_V, :] = vt[:, j * tk:(j + 1) * tk]
            vt_ref[0, hh, j, MLA_V:, :] = jnp.ones((MLA_VA - MLA_V, tk), BF16)


def _mla_qkv(proj, misc, pos, freqs, q_g, kv_g, wq_h, wkv_h, bsz, seq):
    tm, tk = QKV_TM, ATT_TK
    hps = MLA_HEADS
    t = bsz * seq
    nsb = seq // tm
    cq_blk = (3 * MLSTM_W) // Q_LORA
    ckv_blk = (3 * MLSTM_W + Q_LORA) // KV_LORA
    kern = functools.partial(_qkv_kernel, scale=MLA_QK ** -0.5 * float(np.log2(np.e)), tk=tk)
    return pl.pallas_call(
        kern,
        grid=(t // tm, MLA_HEADS // hps),
        in_specs=[pl.BlockSpec((tm, Q_LORA), lambda i, h: (i, cq_blk)),
                  pl.BlockSpec((tm, KV_LORA), lambda i, h: (i, ckv_blk)),
                  pl.BlockSpec((tm, MISC_W), lambda i, h: (i, 0)),
                  pl.BlockSpec((1, 1, tm), lambda i, h: (i, 0, 0)),
                  pl.BlockSpec(freqs.shape, lambda i, h: (0, 0)),
                  pl.BlockSpec((1, Q_LORA), lambda i, h: (0, 0)),
                  pl.BlockSpec((1, KV_LORA), lambda i, h: (0, 0)),
                  pl.BlockSpec((hps, Q_LORA, 2 * LANES), lambda i, h: (h, 0, 0)),
                  pl.BlockSpec((hps, KV_LORA, 2 * LANES), lambda i, h: (h, 0, 0))],
        out_specs=[pl.BlockSpec((1, hps, MLA_QK, tm), lambda i, h: (i // nsb, h, 0, i % nsb)),
                   pl.BlockSpec((1, hps, tm, MLA_QK), lambda i, h: (i // nsb, h, i % nsb, 0)),
                   pl.BlockSpec((1, hps, tm // tk, MLA_VA, tk),
                                lambda i, h: (i // nsb, h, i % nsb, 0, 0))],
        out_shape=[jax.ShapeDtypeStruct((bsz, MLA_HEADS, MLA_QK, seq), BF16),
                   jax.ShapeDtypeStruct((bsz, MLA_HEADS, seq, MLA_QK), BF16),
                   jax.ShapeDtypeStruct((bsz, MLA_HEADS, seq // tk, MLA_VA, tk), BF16)],
        scratch_shapes=[pltpu.VMEM((tm, Q_LORA), BF16),
                        pltpu.VMEM((tm, KV_LORA), BF16),
                        pltpu.VMEM((tm, LANES), F32),
                        pltpu.VMEM((tm, LANES), F32),
                        pltpu.VMEM((tm, LANES), F32)],
        compiler_params=_params(2, 40),
        name="mla_qkv",
    )(proj, proj, misc, pos, freqs, q_g.reshape(1, Q_LORA), kv_g.reshape(1, KV_LORA),
      wq_h, wkv_h)


def _flash_kernel(qt_ref, k_ref, vt_ref, g_ref, wa_ref, wb_ref, o_ref, wab_ref, wbb_ref,
                  m_ref, acc_ref, bias_ref, *, tq, tk, nq, wa_slabs, wb_slabs):
    qi = pl.program_id(2)

    step = (pl.program_id(0) * pl.num_programs(1) + pl.program_id(1)) * nq + qi

    @pl.when(step < wa_slabs)
    def _():
        wab_ref[...] = wa_ref[...].astype(BF16)

    @pl.when(step < wb_slabs)
    def _():
        wbb_ref[...] = wb_ref[...].astype(BF16)

    m_ref[...] = jnp.full(m_ref.shape, -jnp.inf, F32)
    acc_ref[...] = jnp.zeros(acc_ref.shape, F32)
    @pl.when((pl.program_id(0) == 0) & (pl.program_id(1) == 0) & (qi == 0))
    def _():
        key = lax.broadcasted_iota(jnp.int32, (tk, tq), 0)
        qry = lax.broadcasted_iota(jnp.int32, (tk, tq), 1)
        bias_ref[...] = jnp.where(key <= qry, 0.0, -jnp.inf).astype(F32)

    hp = qt_ref.shape[1]

    def scores(hh, kj, c0, masked):
        k = k_ref[0, hh, kj * tk:(kj + 1) * tk, :]
        s = jnp.dot(k, qt_ref[0, hh, :, c0:], preferred_element_type=F32)
        if masked:
            s = s + bias_ref[:, 0:tq - c0]
        return s

    def update(hh, s, kj, c0):
        m_prev = m_ref[hh, :, c0:]
        m_new = jnp.maximum(m_prev, jnp.max(s, axis=0, keepdims=True))
        alpha = jnp.exp2(m_prev - m_new)
        p = jnp.exp2(s - m_new)
        acc_ref[hh, :, c0:] = alpha * acc_ref[hh, :, c0:] + jnp.dot(
            vt_ref[0, hh, kj], p.astype(BF16), preferred_element_type=F32)
        m_ref[hh, :, c0:] = m_new

    per_q = tq // tk

    def run(q):
        def place(b):
            d = b - q * per_q
            return (0, False) if d < 0 else (d * tk, True)

        nblk = (q + 1) * per_q
        s = [scores(hh, 0, *place(0)) for hh in range(hp)]
        for b in range(nblk):
            s_next = ([scores(hh, b + 1, *place(b + 1)) for hh in range(hp)]
                      if b + 1 < nblk else None)
            for hh in range(hp):
                update(hh, s[hh], b, place(b)[0])
            s = s_next

    for q in range(nq):
        pl.when(qi == q)(functools.partial(run, q))

    for hh in range(hp):
        o = acc_ref[hh, 0:MLA_V, :] / acc_ref[hh, MLA_V:MLA_V + 1, :]
        o = o * lax.rsqrt(jnp.mean(o * o, axis=0, keepdims=True) + EPS) * g_ref[hh]
        o_ref[:, hh * MLA_V:(hh + 1) * MLA_V] = o.T.astype(BF16)


def _cast_slab_rows(rows, n_steps):
    bf16_rows = 16
    for slab in range(bf16_rows, rows + 1, bf16_rows):
        if rows % slab == 0 and rows // slab <= n_steps:
            return slab
    raise ValueError("no slab size fits")


def _mla_attention(qt, k, vt, out_g, wa, wb):
    bsz, nh, seq, _ = k.shape
    tq, tk = ATT_TQ, ATT_TK
    nq = seq // tq
    hp = 2
    ng = nh // hp
    n_steps = bsz * ng * nq
    sa = _cast_slab_rows(wa.shape[0], n_steps)
    sb = _cast_slab_rows(wb.shape[0], n_steps)
    na, nb = wa.shape[0] // sa, wb.shape[0] // sb
    kern = functools.partial(_flash_kernel, tq=tq, tk=tk, nq=nq, wa_slabs=na, wb_slabs=nb)
    step = lambda b, h, i: (b * ng + h) * nq + i
    wa_map = lambda b, h, i: (jnp.minimum(step(b, h, i), na - 1), 0)
    wb_map = lambda b, h, i: (jnp.minimum(step(b, h, i), nb - 1), 0)
    return pl.pallas_call(
        kern,
        grid=(bsz, ng, nq),
        in_specs=[pl.BlockSpec((1, hp, MLA_QK, tq), lambda b, h, i: (b, h, 0, i)),
                  pl.BlockSpec((1, hp, seq, MLA_QK), lambda b, h, i: (b, h, 0, 0)),
                  pl.BlockSpec((1, hp, seq // tk, MLA_VA, tk), lambda b, h, i: (b, h, 0, 0, 0)),
                  pl.BlockSpec((hp, MLA_V, 1), lambda b, h, i: (h, 0, 0)),
                  pl.BlockSpec((sa, wa.shape[1]), wa_map),
                  pl.BlockSpec((sb, wb.shape[1]), wb_map)],
        out_specs=[pl.BlockSpec((tq, hp * MLA_V), lambda b, h, i: (b * nq + i, h)),
                   pl.BlockSpec((sa, wa.shape[1]), wa_map),
                   pl.BlockSpec((sb, wb.shape[1]), wb_map)],
        out_shape=[jax.ShapeDtypeStruct((bsz * seq, nh * MLA_V), BF16),
                   jax.ShapeDtypeStruct(wa.shape, BF16),
                   jax.ShapeDtypeStruct(wb.shape, BF16)],
        scratch_shapes=[pltpu.VMEM((hp, 1, tq), F32),
                        pltpu.VMEM((hp, MLA_VA, tq), F32),
                        pltpu.VMEM((tk, tq), F32)],
        compiler_params=_params(3, 48),
        name="mla_attention",
    )(qt, k, vt, out_g.reshape(nh, MLA_V, 1), wa, wb)


def _outproj_kernel(hm_ref, ha_ref, w_ref, x_ref, mod_ref, g_ref, x1_ref, h2_ref, *, sub):
    km = hm_ref.shape[1]
    for s in range(hm_ref.shape[0] // sub):
        rows = pl.ds(s * sub, sub)
        mix = (jnp.dot(hm_ref[rows, :], w_ref[0:km, :], preferred_element_type=F32)
               + jnp.dot(ha_ref[rows, :], w_ref[km:, :], preferred_element_type=F32))
        x1 = x_ref[rows, :] + mod_ref[0, 2:3, :] * mix
        x1_ref[rows, :] = x1
        y = _rms(x1) * g_ref[...]
        h2_ref[rows, :] = (y * (1.0 + mod_ref[0, 4:5, :]) + mod_ref[0, 3:4, :]).astype(BF16)


def _out_projection(hm, ha, w_out, x2, mod, g, seq):
    t, d = x2.shape
    tm = 512
    per_b = seq // tm
    return pl.pallas_call(
        functools.partial(_outproj_kernel, sub=256),
        grid=(t // tm,),
        in_specs=[pl.BlockSpec((tm, hm.shape[1]), lambda i: (i, 0)),
                  pl.BlockSpec((tm, ha.shape[1]), lambda i: (i, 0)),
                  pl.BlockSpec(w_out.shape, lambda i: (0, 0), pipeline_mode=pl.Buffered(1)),
                  pl.BlockSpec((tm, d), lambda i: (i, 0)),
                  pl.BlockSpec((1, 6, d), lambda i: (i // per_b, 0, 0)),
                  pl.BlockSpec((1, d), lambda i: (0, 0))],
        out_specs=[pl.BlockSpec((tm, d), lambda i: (i, 0)),
                   pl.BlockSpec((tm, d), lambda i: (i, 0))],
        out_shape=[jax.ShapeDtypeStruct((t, d), F32),
                   jax.ShapeDtypeStruct((t, d), BF16)],
        compiler_params=_params(1, 48),
        name="out_proj",
    )(hm, ha, w_out, x2, mod, g.reshape(1, d))


def _ffn_up_kernel(h_ref, wv_ref, wg_ref, cwv_ref, cwg_ref, cbv_ref, cbg_ref, o_ref,
                   xv_ref, xg_ref, wvb_ref, wgb_ref, *, tm, blocks_per_seq):
    i = pl.program_id(1)

    @pl.when(i % blocks_per_seq == 0)
    def _():
        xv_ref[0:8, :] = jnp.zeros((8, xv_ref.shape[1]), F32)
        xg_ref[0:8, :] = jnp.zeros((8, xg_ref.shape[1]), F32)

    @pl.when(i == 0)
    def _():
        wvb_ref[...] = wv_ref[...].astype(BF16)
        wgb_ref[...] = wg_ref[...].astype(BF16)

    h = h_ref[...]

    def conv(w_ref, cw_ref, cb_ref, xs_ref):
        kw = cw_ref.shape[0]
        up = jnp.dot(h, w_ref[...], preferred_element_type=F32)
        xs_ref[8:8 + tm, :] = up
        y = cb_ref[...] + cw_ref[kw - 1:kw, :] * up
        for j in range(kw - 1):
            y = y + cw_ref[j:j + 1, :] * xs_ref[pl.ds(8 - (kw - 1) + j, tm), :]
        xs_ref[0:8, :] = xs_ref[tm:tm + 8, :]
        return y

    val = conv(wvb_ref, cwv_ref, cbv_ref, xv_ref)
    gate = conv(wgb_ref, cwg_ref, cbg_ref, xg_ref)
    o_ref[...] = (_silu(gate) * val).astype(BF16)


def _ffn_up(h2, w_up, conv_w, conv_b, seq):
    t, d = h2.shape
    f = w_up.shape[1] // 2
    tm, tf = 1024, 512
    nf = f // tf
    kw = conv_w.shape[0]
    kern = functools.partial(_ffn_up_kernel, tm=tm, blocks_per_seq=seq // tm)
    cb = conv_b.reshape(1, 2 * f)
    return pl.pallas_call(
        kern,
        grid=(nf, t // tm),
        in_specs=[pl.BlockSpec((tm, d), lambda j, i: (i, 0)),
                  pl.BlockSpec((d, tf), lambda j, i: (0, j)),
                  pl.BlockSpec((d, tf), lambda j, i: (0, nf + j)),
                  pl.BlockSpec((kw, tf), lambda j, i: (0, j)),
                  pl.BlockSpec((kw, tf), lambda j, i: (0, nf + j)),
                  pl.BlockSpec((1, tf), lambda j, i: (0, j)),
                  pl.BlockSpec((1, tf), lambda j, i: (0, nf + j))],
        out_specs=pl.BlockSpec((tm, tf), lambda j, i: (i, j)),
        out_shape=jax.ShapeDtypeStruct((t, f), BF16),
        scratch_shapes=[pltpu.VMEM((tm + 8, tf), F32),
                        pltpu.VMEM((tm + 8, tf), F32),
                        pltpu.VMEM((d, tf), BF16),
                        pltpu.VMEM((d, tf), BF16)],
        compiler_params=_params(2, 56),
        name="ffn_up",
    )(h2, w_up, w_up, conv_w, conv_w, cb, cb)


def _ffn_down_kernel(a_ref, w_ref, x1_ref, mod_ref, g_ref, o_ref, *, sub):
    for s in range(a_ref.shape[0] // sub):
        rows = pl.ds(s * sub, sub)
        y = jnp.dot(a_ref[rows, :], w_ref[...], preferred_element_type=F32)
        x2 = x1_ref[rows, :] + mod_ref[0, 5:6, :] * y
        o_ref[rows, :] = _rms(x2) * g_ref[...]


def _ffn_down(act, w_down, x1, mod, g, seq):
    t, f = act.shape
    d = w_down.shape[1]
    tm = 512
    per_b = seq // tm
    return pl.pallas_call(
        functools.partial(_ffn_down_kernel, sub=256),
        grid=(t // tm,),
        in_specs=[pl.BlockSpec((tm, f), lambda i: (i, 0)),
                  pl.BlockSpec((f, d), lambda i: (0, 0), pipeline_mode=pl.Buffered(1)),
                  pl.BlockSpec((tm, d), lambda i: (i, 0)),
                  pl.BlockSpec((1, 6, d), lambda i: (i // per_b, 0, 0)),
                  pl.BlockSpec((1, d), lambda i: (0, 0))],
        out_specs=pl.BlockSpec((tm, d), lambda i: (i, 0)),
        out_shape=jax.ShapeDtypeStruct((t, d), F32),
        compiler_params=_params(1, 58),
        name="ffn_down",
    )(act, w_down, x1, mod, g.reshape(1, d))


def _rot_cols(w):
    half = w.shape[-1] // 2
    return jnp.concatenate([-w[..., half:], w[..., :half]], axis=-1)


def _win_prep_kernel(w_ref, tail_ref, o_ref, *, n_main):
    j = pl.program_id(0)
    gates = 2 * MLSTM_HEADS
    half = MLA_ROPE // 2

    @pl.when(j < n_main)
    def _():
        o_ref[...] = w_ref[...].T.astype(BF16)

    @pl.when(j == n_main)
    def _():
        o_ref[...] = tail_ref[gates:gates + Q_LORA, :].T.astype(BF16)

    @pl.when(j == n_main + 1)
    def _():
        r0 = gates + Q_LORA
        kr0 = r0 + KV_LORA
        blk = o_ref.shape[1]
        used = KV_LORA + 2 * MLA_ROPE + gates
        o_ref[...] = jnp.concatenate(
            [tail_ref[r0:kr0, :],
             tail_ref[kr0:kr0 + MLA_ROPE, :],
             -tail_ref[kr0 + half:kr0 + MLA_ROPE, :],
             tail_ref[kr0:kr0 + half, :],
             tail_ref[0:gates, :],
             jnp.zeros((blk - used, o_ref.shape[0]), F32)], axis=0).T.astype(BF16)


def _prep_w_in(w_in):
    d = w_in.shape[0]
    wt = w_in.T
    n_head = 3 * MLSTM_W
    blk = Q_LORA
    n_main = n_head // blk
    return pl.pallas_call(
        functools.partial(_win_prep_kernel, n_main=n_main),
        grid=(IN_PAD // blk,),
        in_specs=[pl.BlockSpec((blk, d), lambda j: (jnp.minimum(j, n_main - 1), 0)),
                  pl.BlockSpec((IN_PAD - n_head, d), lambda j: (n_head // (IN_PAD - n_head), 0))],
        out_specs=pl.BlockSpec((d, blk), lambda j: (0, j)),
        out_shape=jax.ShapeDtypeStruct((d, IN_PAD), BF16),
        compiler_params=_params(1, 40),
        name="w_in_prep",
    )(wt, wt)


def kernel(x, c, positions, ada_w, ada_b, attn_norm_g, w_in, mlstm_conv_w, mlstm_conv_b, mlstm_wq, mlstm_wk, mlstm_igate_b, mlstm_fgate_b, mla_q_norm_g, mla_w_uq, mla_kv_norm_g, mla_w_ukv, mlstm_out_g, mla_out_g, w_out, ffn_norm_g, ffn_w_up, ffn_conv_w, ffn_conv_b, ffn_w_down, final_norm_g):
    bsz, seq, d = x.shape
    t = bsz * seq
    depth = ada_w.shape[0]
    assert depth == 1, "the final RMSNorm is fused into the single layer's down-projection"
    xr = x.reshape(t, d)
    pos = positions.reshape(t // QKV_TM, 1, QKV_TM)
    half = MLA_ROPE // 2
    freqs = (ROPE_THETA ** (-jnp.arange(half, dtype=F32) / half)).reshape(half, 1)

    for l in range(depth):
        mod = _modulation(c, ada_w[l], ada_b[l])

        proj, misc = _in_projection(xr, mod, attn_norm_g[l], _prep_w_in(w_in[l]), seq)

        wq_m = mlstm_wq[l].astype(BF16)
        wkt_m = mlstm_wk[l].transpose(0, 2, 1).astype(BF16)
        gate_b = jnp.concatenate(
            [mlstm_igate_b[l], mlstm_fgate_b[l],
             jnp.zeros((LANES - 2 * MLSTM_HEADS,), F32)]).reshape(1, LANES)
        hm = _mlstm(proj, misc, mlstm_conv_w[l], mlstm_conv_b[l], wq_m, wkt_m, gate_b,
                    mlstm_out_g[l], bsz, seq)

        wq = mla_w_uq[l].reshape(Q_LORA, MLA_HEADS, MLA_QK)
        wq_r = wq[..., MLA_NOPE:]
        wq_h = jnp.concatenate([wq[..., :MLA_NOPE], wq_r, _rot_cols(wq_r)], axis=-1)
        wq_h = wq_h.transpose(1, 0, 2).astype(BF16)
        wkv_h = mla_w_ukv[l].reshape(KV_LORA, MLA_HEADS, MLA_NOPE + MLA_V)
        wkv_h = wkv_h.transpose(1, 0, 2).astype(BF16)
        qt, k, vt = _mla_qkv(proj, misc, pos, freqs, mla_q_norm_g[l], mla_kv_norm_g[l],
                             wq_h, wkv_h, bsz, seq)
        ha, w_down_b, w_out_b = _mla_attention(qt, k, vt, mla_out_g[l], ffn_w_down[l], w_out[l])

        x1, h2 = _out_projection(hm, ha, w_out_b, xr, mod, ffn_norm_g[l], seq)

        act = _ffn_up(h2, ffn_w_up[l], ffn_conv_w[l], ffn_conv_b[l], seq)
        xr = _ffn_down(act, w_down_b, x1, mod, final_norm_g, seq)
    return xr.reshape(bsz, seq, d)
```

```python
import functools

import jax
import jax.numpy as jnp
import numpy as np
from jax import lax
from jax.experimental import pallas as pl
from jax.experimental.pallas import tpu as pltpu

F32 = jnp.float32
BF16 = jnp.bfloat16

EPS = 1e-6
ROPE_THETA = 10000.0
MLSTM_HEADS = 4
MLSTM_DH = 256
MLSTM_W = MLSTM_HEADS * MLSTM_DH
MLA_HEADS = 8
MLA_NOPE = 128
MLA_ROPE = 64
MLA_V = 128
MLA_QK = MLA_NOPE + MLA_ROPE
Q_LORA = 512
KV_LORA = 256
LANES = 128
MIB = 1024 * 1024

IN_PAD = 4096
IN_BLOCK = 1024
MISC_W = 256
MISC_OFF = IN_BLOCK - MISC_W

MOD_TN = 1024
INPROJ_TM, INPROJ_SUB = 512, 256
MLSTM_TS, MLSTM_CHUNK = 512, 512
QKV_TM = 1024
ATT_TQ = 1024
ATT_TK = 256
ATT_HP = 2
OUTPROJ_TM, OUTPROJ_SUB = 512, 256
FFN_UP_TM, FFN_UP_TF = 1024, 512
FFN_DOWN_TM, FFN_DOWN_SUB = 512, 256


def _params(n_axes, vmem_mib):
    return pltpu.CompilerParams(
        dimension_semantics=("arbitrary",) * n_axes,
        vmem_limit_bytes=vmem_mib * MIB)


def _rms(x):
    return x * lax.rsqrt(jnp.mean(x * x, axis=-1, keepdims=True) + EPS)


def _silu(x):
    return x / (1.0 + jnp.exp(-x))


def _log_sigmoid(x):
    return jnp.minimum(x, 0.0) - jnp.log1p(jnp.exp(-jnp.abs(x)))


def _mod_kernel(c_ref, w_ref, b_ref, o_ref):
    ca = _silu(c_ref[...]).astype(BF16)
    o_ref[...] = jnp.dot(ca, w_ref[...].astype(BF16),
                         preferred_element_type=F32) + b_ref[...]


def _modulation(c, ada_w, ada_b):
    bsz, d = c.shape
    n = ada_w.shape[1]
    tn = MOD_TN
    cp = jnp.pad(c, ((0, 8 - bsz), (0, 0)))
    out = pl.pallas_call(
        _mod_kernel,
        grid=(n // tn,),
        in_specs=[pl.BlockSpec((8, d), lambda j: (0, 0)),
                  pl.BlockSpec((d, tn), lambda j: (0, j)),
                  pl.BlockSpec((1, tn), lambda j: (0, j))],
        out_specs=pl.BlockSpec((8, tn), lambda j: (0, j)),
        out_shape=jax.ShapeDtypeStruct((8, n), F32),
        compiler_params=_params(1, 40),
        name="adaln_mod",
    )(cp, ada_w, ada_b.reshape(1, n))
    return out[:bsz].reshape(bsz, 6, d)


def _inproj_kernel(x_ref, mod_ref, g_ref, w_ref, o_ref, misc_ref, *, sub):
    tm = x_ref.shape[0]
    nblk = w_ref.shape[1] // IN_BLOCK
    for s in range(tm // sub):
        rows = pl.ds(s * sub, sub)
        y = _rms(x_ref[rows, :]) * g_ref[...]
        h = (y * (1.0 + mod_ref[0, 1:2, :]) + mod_ref[0, 0:1, :]).astype(BF16)
        for j in range(nblk):
            cols = slice(j * IN_BLOCK, (j + 1) * IN_BLOCK)
            acc = jnp.dot(h, w_ref[:, cols], preferred_element_type=F32)
            o_ref[rows, cols] = acc.astype(BF16)
            if j == nblk - 1:
                misc_ref[rows, :] = acc[:, MISC_OFF:]


def _in_projection(x2, mod, g, w_in_p, seq):
    t, d = x2.shape
    tm = INPROJ_TM
    per_b = seq // tm
    return pl.pallas_call(
        functools.partial(_inproj_kernel, sub=INPROJ_SUB),
        grid=(t // tm,),
        in_specs=[pl.BlockSpec((tm, d), lambda i: (i, 0)),
                  pl.BlockSpec((1, 6, d), lambda i: (i // per_b, 0, 0)),
                  pl.BlockSpec((1, d), lambda i: (0, 0)),
                  pl.BlockSpec((d, IN_PAD), lambda i: (0, 0), pipeline_mode=pl.Buffered(1))],
        out_specs=[pl.BlockSpec((tm, IN_PAD), lambda i: (i, 0)),
                   pl.BlockSpec((tm, MISC_W), lambda i: (i, 0))],
        out_shape=[jax.ShapeDtypeStruct((t, IN_PAD), BF16),
                   jax.ShapeDtypeStruct((t, MISC_W), F32)],
        compiler_params=_params(1, 48),
        name="norm_inproj",
    )(x2, mod, g.reshape(1, d), w_in_p)


def _cumsum_rows(x):
    n = x.shape[0]
    row = lax.broadcasted_iota(jnp.int32, x.shape, 0)
    sh = 1
    while sh < n:
        x = x + jnp.where(row >= sh, pltpu.roll(x, sh, axis=0), 0.0)
        sh *= 2
    return x


def _cummax_rows(x):
    n = x.shape[0]
    row = lax.broadcasted_iota(jnp.int32, x.shape, 0)
    sh = 1
    while sh < n:
        x = jnp.maximum(x, jnp.where(row >= sh, pltpu.roll(x, sh, axis=0), -jnp.inf))
        sh *= 2
    return x


def _cumsum_lanes(x):
    n = x.shape[1]
    col = lax.broadcasted_iota(jnp.int32, x.shape, 1)
    sh = 1
    while sh < n:
        x = x + jnp.where(col >= sh, pltpu.roll(x, sh, axis=1), 0.0)
        sh *= 2
    return x


def _mlstm_kernel(u_ref, v_ref, o_ref, gt_ref, cw_ref, cb_ref, wq_ref, wkt_ref, gb_ref, og_ref,
                  out_ref, xs_ref, ct_ref, n_ref, m_ref, *, ts, chunk):
    i = pl.program_id(0)
    nbatch = u_ref.shape[0]

    @pl.when(i == 0)
    def _():
        for b in range(nbatch):
            xs_ref[b, 0:8, :] = jnp.zeros((8, MLSTM_W), F32)
        ct_ref[...] = jnp.zeros(ct_ref.shape, F32)
        n_ref[...] = jnp.zeros(n_ref.shape, F32)
        m_ref[...] = jnp.zeros(m_ref.shape, F32)

    row = lax.broadcasted_iota(jnp.int32, (chunk, chunk), 0)
    col = lax.broadcasted_iota(jnp.int32, (chunk, chunk), 1)
    causal = col <= row

    for c in range(ts // chunk):
        for b in range(nbatch):
            _mlstm_chunk(u_ref.at[b], v_ref.at[b], o_ref.at[b], gt_ref.at[b], cw_ref, cb_ref,
                         wq_ref, wkt_ref, gb_ref, og_ref, out_ref.at[b], xs_ref.at[b],
                         ct_ref.at[b], n_ref.at[b], m_ref.at[b], c, chunk, causal)
    for b in range(nbatch):
        xs_ref[b, 0:8, :] = xs_ref[b, ts:ts + 8, :]


def _mlstm_chunk(u_ref, v_ref, o_ref, gt_ref, cw_ref, cb_ref, wq_ref, wkt_ref, gb_ref, og_ref,
                 out_ref, xs_ref, ct_ref, n_ref, m_ref, c, chunk, causal):
    nh, dh = MLSTM_HEADS, MLSTM_DH
    gb = gb_ref[...]
    kw = cw_ref.shape[0]
    ones = jnp.ones((chunk, LANES), BF16)
    r0 = c * chunk
    xc = u_ref[r0:r0 + chunk, :].astype(F32)
    xs_ref[8 + r0:8 + r0 + chunk, :] = xc
    acc = cb_ref[...] + cw_ref[kw - 1:kw, :] * xc
    for j in range(kw - 1):
        acc = acc + cw_ref[j:j + 1, :] * xs_ref[pl.ds(8 + r0 - (kw - 1) + j, chunk), :]
    su = _silu(acc)

    gc = gt_ref[r0:r0 + chunk, MISC_W - LANES:] + gb
    bc = _cumsum_rows(_log_sigmoid(gc))
    gt = gc.T[0:8, :]
    bt = _cumsum_lanes(_log_sigmoid(gt))
    mdc = _cummax_rows(gc - pltpu.roll(bc, LANES - nh, axis=1))
    for h in range(nh):
        c0 = h * dh
        b_c = bc[:, nh + h:nh + h + 1]
        li_r = gt[h:h + 1, :]
        b_r = bt[nh + h:nh + h + 1, :]
        g_tot = b_r[:, chunk - 1:chunk]
        z_r = li_r - b_r
        zmax = jnp.max(z_r, axis=-1, keepdims=True)
        m_prev = m_ref[h:h + 1, 0:1]

        ub = su[:, c0:c0 + dh].astype(BF16)
        qb = jnp.dot(ub, wq_ref[h], preferred_element_type=F32).astype(BF16)
        kt = jnp.dot(wkt_ref[h], ub.T,
                     preferred_element_type=F32) * (dh ** -0.5)
        vc = v_ref[r0:r0 + chunk, c0:c0 + dh]

        mx_c = jnp.maximum(m_prev, mdc[:, h:h + 1])
        m_t = b_c + mx_c
        sb = (jnp.dot(qb, kt.astype(BF16), preferred_element_type=F32)
              * jnp.exp(jnp.where(causal, z_r - mx_c, -jnp.inf))).astype(BF16)
        inter = jnp.exp(m_prev - mx_c)
        ct = ct_ref[h]
        nrep = n_ref[h]
        num = (jnp.dot(sb, vc, preferred_element_type=F32)
               + inter * jnp.dot(qb, ct.astype(BF16), preferred_element_type=F32))
        den = (jnp.dot(sb, ones, preferred_element_type=F32)
               + inter * jnp.dot(qb, nrep.astype(BF16), preferred_element_type=F32))
        lim = jnp.maximum(jnp.abs(den), jnp.exp(-m_t))
        hh = num / jnp.concatenate([lim] * (dh // LANES), axis=1)

        mm = jnp.maximum(m_prev, zmax)
        s_prev = jnp.exp(m_prev - mm)
        s_loc = jnp.exp(zmax - mm)
        kwt = (kt * (jnp.exp(z_r - zmax) * s_loc)).astype(BF16)
        ct_ref[h] = s_prev * ct + jnp.dot(kwt, vc, preferred_element_type=F32)
        n_ref[h] = s_prev * nrep + jnp.dot(kwt, ones, preferred_element_type=F32)
        m_ref[h:h + 1, :] = jnp.broadcast_to(g_tot + mm, (1, LANES))

        og = o_ref[r0:r0 + chunk, c0:c0 + dh].astype(F32)
        hh = hh / (1.0 + jnp.exp(-og))
        out_ref[r0:r0 + chunk, c0:c0 + dh] = (
            _rms(hh) * og_ref[:, c0:c0 + dh]).astype(BF16)


def _mlstm(proj, misc, conv_w, conv_b, wq, wkt, gate_b, out_g, bsz, seq):
    ts, chunk = MLSTM_TS, MLSTM_CHUNK
    t = bsz * seq
    kern = functools.partial(_mlstm_kernel, ts=ts, chunk=chunk)
    proj3 = proj.reshape(bsz, seq, proj.shape[1])
    misc3 = misc.reshape(bsz, seq, MISC_W)
    out = pl.pallas_call(
        kern,
        grid=(seq // ts,),
        in_specs=[pl.BlockSpec((bsz, ts, MLSTM_W), lambda i: (0, i, 0)),
                  pl.BlockSpec((bsz, ts, MLSTM_W), lambda i: (0, i, 1)),
                  pl.BlockSpec((bsz, ts, MLSTM_W), lambda i: (0, i, 2)),
                  pl.BlockSpec((bsz, ts, MISC_W), lambda i: (0, i, 0)),
                  pl.BlockSpec(conv_w.shape, lambda i: (0, 0)),
                  pl.BlockSpec((1, MLSTM_W), lambda i: (0, 0)),
                  pl.BlockSpec(wq.shape, lambda i: (0, 0, 0)),
                  pl.BlockSpec(wkt.shape, lambda i: (0, 0, 0)),
                  pl.BlockSpec((1, LANES), lambda i: (0, 0)),
                  pl.BlockSpec((1, MLSTM_W), lambda i: (0, 0))],
        out_specs=pl.BlockSpec((bsz, ts, MLSTM_W), lambda i: (0, i, 0)),
        out_shape=jax.ShapeDtypeStruct((bsz, seq, MLSTM_W), BF16),
        scratch_shapes=[pltpu.VMEM((bsz, ts + 8, MLSTM_W), F32),
                        pltpu.VMEM((bsz, MLSTM_HEADS, MLSTM_DH, MLSTM_DH), F32),
                        pltpu.VMEM((bsz, MLSTM_HEADS, MLSTM_DH, LANES), F32),
                        pltpu.VMEM((bsz, 8, LANES), F32)],
        compiler_params=_params(1, 48),
        name="mlstm",
    )(proj3, proj3, proj3, misc3, conv_w, conv_b.reshape(1, MLSTM_W), wq, wkt, gate_b,
      out_g.reshape(1, MLSTM_W))
    return out.reshape(t, MLSTM_W)


MLA_VA = MLA_V + 16


def _qkv_kernel(cq_ref, ckv_ref, misc_ref, pos_ref, fr_ref, qg_ref, kvg_ref, wq_ref, wkv_ref,
                qt_ref, k_ref, vt_ref, cqn, ckvn, cos_s, sin_s, kr_s, *, scale, tk):
    h = pl.program_id(1)

    @pl.when(h == 0)
    def _():
        cqn[...] = (_rms(cq_ref[...].astype(F32)) * qg_ref[...]).astype(BF16)
        ckvn[...] = (_rms(ckv_ref[...].astype(F32)) * kvg_ref[...]).astype(BF16)
        ang = fr_ref[...] * pos_ref[0].astype(F32)
        reps = LANES // ang.shape[0]
        cs = jnp.concatenate([jnp.cos(ang)] * reps, axis=0).T
        sn = jnp.concatenate([jnp.sin(ang)] * reps, axis=0).T
        cos_s[...] = cs
        sin_s[...] = sn
        y = misc_ref[:, 0:LANES]
        kr_s[...] = y * cs + pltpu.roll(y, MLA_ROPE, axis=1) * sn

    for hh in range(wq_ref.shape[0]):
        mq = jnp.dot(cqn[...], wq_ref[hh], preferred_element_type=F32)
        qt_ref[0, hh, 0:MLA_NOPE, :] = (mq[:, 0:MLA_NOPE] * scale).T.astype(BF16)
        y = mq[:, MLA_NOPE:]
        r = (y * cos_s[...] + pltpu.roll(y, MLA_ROPE, axis=1) * sin_s[...]) * scale
        qt_ref[0, hh, MLA_NOPE:MLA_QK, :] = r.T[0:MLA_ROPE, :].astype(BF16)

        mkv = jnp.dot(ckvn[...], wkv_ref[hh], preferred_element_type=F32)
        k_ref[0, hh, :, 0:MLA_NOPE] = mkv[:, 0:MLA_NOPE].astype(BF16)
        k_ref[0, hh, :, MLA_NOPE:MLA_QK] = kr_s[:, 0:MLA_ROPE].astype(BF16)
        vt = mkv[:, MLA_NOPE:].T.astype(BF16)
        for j in range(vt.shape[1] // tk):
            vt_ref[0, hh, j, 0:MLA_V, :] = vt[:, j * tk:(j + 1) * tk]
            vt_ref[0, hh, j, MLA_V:, :] = jnp.ones((MLA_VA - MLA_V, tk), BF16)


def _mla_qkv(proj, misc, pos, freqs, q_g, kv_g, wq_h, wkv_h, bsz, seq):
    tm, tk = QKV_TM, ATT_TK
    hps = MLA_HEADS
    t = bsz * seq
    nsb = seq // tm
    cq_blk = (3 * MLSTM_W) // Q_LORA
    ckv_blk = (3 * MLSTM_W + Q_LORA) // KV_LORA
    kern = functools.partial(_qkv_kernel, scale=MLA_QK ** -0.5 * float(np.log2(np.e)), tk=tk)
    return pl.pallas_call(
        kern,
        grid=(t // tm, MLA_HEADS // hps),
        in_specs=[pl.BlockSpec((tm, Q_LORA), lambda i, h: (i, cq_blk)),
                  pl.BlockSpec((tm, KV_LORA), lambda i, h: (i, ckv_blk)),
                  pl.BlockSpec((tm, MISC_W), lambda i, h: (i, 0)),
                  pl.BlockSpec((1, 1, tm), lambda i, h: (i, 0, 0)),
                  pl.BlockSpec(freqs.shape, lambda i, h: (0, 0)),
                  pl.BlockSpec((1, Q_LORA), lambda i, h: (0, 0)),
                  pl.BlockSpec((1, KV_LORA), lambda i, h: (0, 0)),
                  pl.BlockSpec((hps, Q_LORA, 2 * LANES), lambda i, h: (h, 0, 0)),
                  pl.BlockSpec((hps, KV_LORA, 2 * LANES), lambda i, h: (h, 0, 0))],
        out_specs=[pl.BlockSpec((1, hps, MLA_QK, tm), lambda i, h: (i // nsb, h, 0, i % nsb)),
                   pl.BlockSpec((1, hps, tm, MLA_QK), lambda i, h: (i // nsb, h, i % nsb, 0)),
                   pl.BlockSpec((1, hps, tm // tk, MLA_VA, tk),
                                lambda i, h: (i // nsb, h, i % nsb, 0, 0))],
        out_shape=[jax.ShapeDtypeStruct((bsz, MLA_HEADS, MLA_QK, seq), BF16),
                   jax.ShapeDtypeStruct((bsz, MLA_HEADS, seq, MLA_QK), BF16),
                   jax.ShapeDtypeStruct((bsz, MLA_HEADS, seq // tk, MLA_VA, tk), BF16)],
        scratch_shapes=[pltpu.VMEM((tm, Q_LORA), BF16),
                        pltpu.VMEM((tm, KV_LORA), BF16),
                        pltpu.VMEM((tm, LANES), F32),
                        pltpu.VMEM((tm, LANES), F32),
                        pltpu.VMEM((tm, LANES), F32)],
        compiler_params=_params(2, 40),
        name="mla_qkv",
    )(proj, proj, misc, pos, freqs, q_g.reshape(1, Q_LORA), kv_g.reshape(1, KV_LORA),
      wq_h, wkv_h)


def _flash_kernel(qt_ref, k_ref, vt_ref, g_ref, wa_ref, wb_ref, o_ref, wab_ref, wbb_ref,
                  m_ref, acc_ref, bias_ref, *, tq, tk, nq, wa_slabs, wb_slabs):
    qi = pl.program_id(2)

    step = (pl.program_id(0) * pl.num_programs(1) + pl.program_id(1)) * nq + qi

    @pl.when(step < wa_slabs)
    def _():
        wab_ref[...] = wa_ref[...].astype(BF16)

    @pl.when(step < wb_slabs)
    def _():
        wbb_ref[...] = wb_ref[...].astype(BF16)

    m_ref[...] = jnp.full(m_ref.shape, -jnp.inf, F32)
    acc_ref[...] = jnp.zeros(acc_ref.shape, F32)
    @pl.when((pl.program_id(0) == 0) & (pl.program_id(1) == 0) & (qi == 0))
    def _():
        key = lax.broadcasted_iota(jnp.int32, (tk, tq), 0)
        qry = lax.broadcasted_iota(jnp.int32, (tk, tq), 1)
        bias_ref[...] = jnp.where(key <= qry, 0.0, -jnp.inf).astype(F32)

    hp = qt_ref.shape[1]

    def scores(hh, kj, c0, masked):
        k = k_ref[0, hh, kj * tk:(kj + 1) * tk, :]
        s = jnp.dot(k, qt_ref[0, hh, :, c0:], preferred_element_type=F32)
        if masked:
            s = s + bias_ref[:, 0:tq - c0]
        return s

    def update(hh, s, kj, c0):
        m_prev = m_ref[hh, :, c0:]
        m_new = jnp.maximum(m_prev, jnp.max(s, axis=0, keepdims=True))
        alpha = jnp.exp2(m_prev - m_new)
        p = jnp.exp2(s - m_new)
        acc_ref[hh, :, c0:] = alpha * acc_ref[hh, :, c0:] + jnp.dot(
            vt_ref[0, hh, kj], p.astype(BF16), preferred_element_type=F32)
        m_ref[hh, :, c0:] = m_new

    per_q = tq // tk

    def run(q):
        def place(b):
            d = b - q * per_q
            return (0, False) if d < 0 else (d * tk, True)

        nblk = (q + 1) * per_q
        s = [scores(hh, 0, *place(0)) for hh in range(hp)]
        for b in range(nblk):
            s_next = ([scores(hh, b + 1, *place(b + 1)) for hh in range(hp)]
                      if b + 1 < nblk else None)
            for hh in range(hp):
                update(hh, s[hh], b, place(b)[0])
            s = s_next

    for q in range(nq):
        pl.when(qi == q)(functools.partial(run, q))

    for hh in range(hp):
        o = acc_ref[hh, 0:MLA_V, :] / acc_ref[hh, MLA_V:MLA_V + 1, :]
        o = o * lax.rsqrt(jnp.mean(o * o, axis=0, keepdims=True) + EPS) * g_ref[hh]
        o_ref[:, hh * MLA_V:(hh + 1) * MLA_V] = o.T.astype(BF16)


def _cast_slab_rows(rows, n_steps):
    bf16_rows = 16
    for slab in range(bf16_rows, rows + 1, bf16_rows):
        if rows % slab == 0 and rows // slab <= n_steps:
            return slab
    raise ValueError("no slab size fits")


def _mla_attention(qt, k, vt, out_g, wa, wb):
    bsz, nh, seq, _ = k.shape
    tq, tk = ATT_TQ, ATT_TK
    nq = seq // tq
    hp = ATT_HP
    ng = nh // hp
    n_steps = bsz * ng * nq
    sa = _cast_slab_rows(wa.shape[0], n_steps)
    sb = _cast_slab_rows(wb.shape[0], n_steps)
    na, nb = wa.shape[0] // sa, wb.shape[0] // sb
    kern = functools.partial(_flash_kernel, tq=tq, tk=tk, nq=nq, wa_slabs=na, wb_slabs=nb)
    step = lambda b, h, i: (b * ng + h) * nq + i
    wa_map = lambda b, h, i: (jnp.minimum(step(b, h, i), na - 1), 0)
    wb_map = lambda b, h, i: (jnp.minimum(step(b, h, i), nb - 1), 0)
    return pl.pallas_call(
        kern,
        grid=(bsz, ng, nq),
        in_specs=[pl.BlockSpec((1, hp, MLA_QK, tq), lambda b, h, i: (b, h, 0, i)),
                  pl.BlockSpec((1, hp, seq, MLA_QK), lambda b, h, i: (b, h, 0, 0)),
                  pl.BlockSpec((1, hp, seq // tk, MLA_VA, tk), lambda b, h, i: (b, h, 0, 0, 0)),
                  pl.BlockSpec((hp, MLA_V, 1), lambda b, h, i: (h, 0, 0)),
                  pl.BlockSpec((sa, wa.shape[1]), wa_map),
                  pl.BlockSpec((sb, wb.shape[1]), wb_map)],
        out_specs=[pl.BlockSpec((tq, hp * MLA_V), lambda b, h, i: (b * nq + i, h)),
                   pl.BlockSpec((sa, wa.shape[1]), wa_map),
                   pl.BlockSpec((sb, wb.shape[1]), wb_map)],
        out_shape=[jax.ShapeDtypeStruct((bsz * seq, nh * MLA_V), BF16),
                   jax.ShapeDtypeStruct(wa.shape, BF16),
                   jax.ShapeDtypeStruct(wb.shape, BF16)],
        scratch_shapes=[pltpu.VMEM((hp, 1, tq), F32),
                        pltpu.VMEM((hp, MLA_VA, tq), F32),
                        pltpu.VMEM((tk, tq), F32)],
        compiler_params=_params(3, 48),
        name="mla_attention",
    )(qt, k, vt, out_g.reshape(nh, MLA_V, 1), wa, wb)


def _outproj_kernel(hm_ref, ha_ref, w_ref, x_ref, mod_ref, g_ref, x1_ref, h2_ref, *, sub):
    km = hm_ref.shape[1]
    for s in range(hm_ref.shape[0] // sub):
        rows = pl.ds(s * sub, sub)
        mix = (jnp.dot(hm_ref[rows, :], w_ref[0:km, :], preferred_element_type=F32)
               + jnp.dot(ha_ref[rows, :], w_ref[km:, :], preferred_element_type=F32))
        x1 = x_ref[rows, :] + mod_ref[0, 2:3, :] * mix
        x1_ref[rows, :] = x1
        y = _rms(x1) * g_ref[...]
        h2_ref[rows, :] = (y * (1.0 + mod_ref[0, 4:5, :]) + mod_ref[0, 3:4, :]).astype(BF16)


def _out_projection(hm, ha, w_out, x2, mod, g, seq):
    t, d = x2.shape
    tm = OUTPROJ_TM
    per_b = seq // tm
    return pl.pallas_call(
        functools.partial(_outproj_kernel, sub=OUTPROJ_SUB),
        grid=(t // tm,),
        in_specs=[pl.BlockSpec((tm, hm.shape[1]), lambda i: (i, 0)),
                  pl.BlockSpec((tm, ha.shape[1]), lambda i: (i, 0)),
                  pl.BlockSpec(w_out.shape, lambda i: (0, 0), pipeline_mode=pl.Buffered(1)),
                  pl.BlockSpec((tm, d), lambda i: (i, 0)),
                  pl.BlockSpec((1, 6, d), lambda i: (i // per_b, 0, 0)),
                  pl.BlockSpec((1, d), lambda i: (0, 0))],
        out_specs=[pl.BlockSpec((tm, d), lambda i: (i, 0)),
                   pl.BlockSpec((tm, d), lambda i: (i, 0))],
        out_shape=[jax.ShapeDtypeStruct((t, d), F32),
                   jax.ShapeDtypeStruct((t, d), BF16)],
        compiler_params=_params(1, 48),
        name="out_proj",
    )(hm, ha, w_out, x2, mod, g.reshape(1, d))


def _ffn_up_kernel(h_ref, wv_ref, wg_ref, cwv_ref, cwg_ref, cbv_ref, cbg_ref, o_ref,
                   xv_ref, xg_ref, wvb_ref, wgb_ref, *, tm, blocks_per_seq):
    i = pl.program_id(1)

    @pl.when(i % blocks_per_seq == 0)
    def _():
        xv_ref[0:8, :] = jnp.zeros((8, xv_ref.shape[1]), F32)
        xg_ref[0:8, :] = jnp.zeros((8, xg_ref.shape[1]), F32)

    @pl.when(i == 0)
    def _():
        wvb_ref[...] = wv_ref[...].astype(BF16)
        wgb_ref[...] = wg_ref[...].astype(BF16)

    h = h_ref[...]

    def conv(w_ref, cw_ref, cb_ref, xs_ref):
        kw = cw_ref.shape[0]
        up = jnp.dot(h, w_ref[...], preferred_element_type=F32)
        xs_ref[8:8 + tm, :] = up
        y = cb_ref[...] + cw_ref[kw - 1:kw, :] * up
        for j in range(kw - 1):
            y = y + cw_ref[j:j + 1, :] * xs_ref[pl.ds(8 - (kw - 1) + j, tm), :]
        xs_ref[0:8, :] = xs_ref[tm:tm + 8, :]
        return y

    val = conv(wvb_ref, cwv_ref, cbv_ref, xv_ref)
    gate = conv(wgb_ref, cwg_ref, cbg_ref, xg_ref)
    o_ref[...] = (_silu(gate) * val).astype(BF16)


def _ffn_up(h2, w_up, conv_w, conv_b, seq):
    t, d = h2.shape
    f = w_up.shape[1] // 2
    tm, tf = FFN_UP_TM, FFN_UP_TF
    nf = f // tf
    kw = conv_w.shape[0]
    kern = functools.partial(_ffn_up_kernel, tm=tm, blocks_per_seq=seq // tm)
    cb = conv_b.reshape(1, 2 * f)
    return pl.pallas_call(
        kern,
        grid=(nf, t // tm),
        in_specs=[pl.BlockSpec((tm, d), lambda j, i: (i, 0)),
                  pl.BlockSpec((d, tf), lambda j, i: (0, j)),
                  pl.BlockSpec((d, tf), lambda j, i: (0, nf + j)),
                  pl.BlockSpec((kw, tf), lambda j, i: (0, j)),
                  pl.BlockSpec((kw, tf), lambda j, i: (0, nf + j)),
                  pl.BlockSpec((1, tf), lambda j, i: (0, j)),
                  pl.BlockSpec((1, tf), lambda j, i: (0, nf + j))],
        out_specs=pl.BlockSpec((tm, tf), lambda j, i: (i, j)),
        out_shape=jax.ShapeDtypeStruct((t, f), BF16),
        scratch_shapes=[pltpu.VMEM((tm + 8, tf), F32),
                        pltpu.VMEM((tm + 8, tf), F32),
                        pltpu.VMEM((d, tf), BF16),
                        pltpu.VMEM((d, tf), BF16)],
        compiler_params=_params(2, 56),
        name="ffn_up",
    )(h2, w_up, w_up, conv_w, conv_w, cb, cb)


def _ffn_down_kernel(a_ref, w_ref, x1_ref, mod_ref, g_ref, o_ref, *, sub):
    for s in range(a_ref.shape[0] // sub):
        rows = pl.ds(s * sub, sub)
        y = jnp.dot(a_ref[rows, :], w_ref[...], preferred_element_type=F32)
        x2 = x1_ref[rows, :] + mod_ref[0, 5:6, :] * y
        o_ref[rows, :] = _rms(x2) * g_ref[...]


def _ffn_down(act, w_down, x1, mod, g, seq):
    t, f = act.shape
    d = w_down.shape[1]
    tm = FFN_DOWN_TM
    per_b = seq // tm
    return pl.pallas_call(
        functools.partial(_ffn_down_kernel, sub=FFN_DOWN_SUB),
        grid=(t // tm,),
        in_specs=[pl.BlockSpec((tm, f), lambda i: (i, 0)),
                  pl.BlockSpec((f, d), lambda i: (0, 0), pipeline_mode=pl.Buffered(1)),
                  pl.BlockSpec((tm, d), lambda i: (i, 0)),
                  pl.BlockSpec((1, 6, d), lambda i: (i // per_b, 0, 0)),
                  pl.BlockSpec((1, d), lambda i: (0, 0))],
        out_specs=pl.BlockSpec((tm, d), lambda i: (i, 0)),
        out_shape=jax.ShapeDtypeStruct((t, d), F32),
        compiler_params=_params(1, 58),
        name="ffn_down",
    )(act, w_down, x1, mod, g.reshape(1, d))


def _rot_cols(w):
    half = w.shape[-1] // 2
    return jnp.concatenate([-w[..., half:], w[..., :half]], axis=-1)


def _win_prep_kernel(w_ref, tail_ref, o_ref, *, n_main):
    j = pl.program_id(0)
    gates = 2 * MLSTM_HEADS
    half = MLA_ROPE // 2

    @pl.when(j < n_main)
    def _():
        o_ref[...] = w_ref[...].T.astype(BF16)

    @pl.when(j == n_main)
    def _():
        o_ref[...] = tail_ref[gates:gates + Q_LORA, :].T.astype(BF16)

    @pl.when(j == n_main + 1)
    def _():
        r0 = gates + Q_LORA
        kr0 = r0 + KV_LORA
        blk = o_ref.shape[1]
        used = KV_LORA + 2 * MLA_ROPE + gates
        o_ref[...] = jnp.concatenate(
            [tail_ref[r0:kr0, :],
             tail_ref[kr0:kr0 + MLA_ROPE, :],
             -tail_ref[kr0 + half:kr0 + MLA_ROPE, :],
             tail_ref[kr0:kr0 + half, :],
             tail_ref[0:gates, :],
             jnp.zeros((blk - used, o_ref.shape[0]), F32)], axis=0).T.astype(BF16)


def _prep_w_in(w_in):
    d = w_in.shape[0]
    wt = w_in.T
    n_head = 3 * MLSTM_W
    blk = Q_LORA
    n_main = n_head // blk
    return pl.pallas_call(
        functools.partial(_win_prep_kernel, n_main=n_main),
        grid=(IN_PAD // blk,),
        in_specs=[pl.BlockSpec((blk, d), lambda j: (jnp.minimum(j, n_main - 1), 0)),
                  pl.BlockSpec((IN_PAD - n_head, d), lambda j: (n_head // (IN_PAD - n_head), 0))],
        out_specs=pl.BlockSpec((d, blk), lambda j: (0, j)),
        out_shape=jax.ShapeDtypeStruct((d, IN_PAD), BF16),
        compiler_params=_params(1, 40),
        name="w_in_prep",
    )(wt, wt)


def kernel(x, c, positions, ada_w, ada_b, attn_norm_g, w_in, mlstm_conv_w, mlstm_conv_b, mlstm_wq, mlstm_wk, mlstm_igate_b, mlstm_fgate_b, mla_q_norm_g, mla_w_uq, mla_kv_norm_g, mla_w_ukv, mlstm_out_g, mla_out_g, w_out, ffn_norm_g, ffn_w_up, ffn_conv_w, ffn_conv_b, ffn_w_down, final_norm_g):
    bsz, seq, d = x.shape
    t = bsz * seq
    depth = ada_w.shape[0]
    assert depth == 1, "the final RMSNorm is fused into the single layer's down-projection"
    xr = x.reshape(t, d)
    pos = positions.reshape(t // QKV_TM, 1, QKV_TM)
    half = MLA_ROPE // 2
    freqs = (ROPE_THETA ** (-jnp.arange(half, dtype=F32) / half)).reshape(half, 1)

    for l in range(depth):
        mod = _modulation(c, ada_w[l], ada_b[l])

        proj, misc = _in_projection(xr, mod, attn_norm_g[l], _prep_w_in(w_in[l]), seq)

        wq_m = mlstm_wq[l].astype(BF16)
        wkt_m = mlstm_wk[l].transpose(0, 2, 1).astype(BF16)
        gate_b = jnp.concatenate(
            [mlstm_igate_b[l], mlstm_fgate_b[l],
             jnp.zeros((LANES - 2 * MLSTM_HEADS,), F32)]).reshape(1, LANES)
        hm = _mlstm(proj, misc, mlstm_conv_w[l], mlstm_conv_b[l], wq_m, wkt_m, gate_b,
                    mlstm_out_g[l], bsz, seq)

        wq = mla_w_uq[l].reshape(Q_LORA, MLA_HEADS, MLA_QK)
        wq_r = wq[..., MLA_NOPE:]
        wq_h = jnp.concatenate([wq[..., :MLA_NOPE], wq_r, _rot_cols(wq_r)], axis=-1)
        wq_h = wq_h.transpose(1, 0, 2).astype(BF16)
        wkv_h = mla_w_ukv[l].reshape(KV_LORA, MLA_HEADS, MLA_NOPE + MLA_V)
        wkv_h = wkv_h.transpose(1, 0, 2).astype(BF16)
        qt, k, vt = _mla_qkv(proj, misc, pos, freqs, mla_q_norm_g[l], mla_kv_norm_g[l],
                             wq_h, wkv_h, bsz, seq)
        ha, w_down_b, w_out_b = _mla_attention(qt, k, vt, mla_out_g[l], ffn_w_down[l], w_out[l])

        x1, h2 = _out_projection(hm, ha, w_out_b, xr, mod, ffn_norm_g[l], seq)

        act = _ffn_up(h2, ffn_w_up[l], ffn_conv_w[l], ffn_conv_b[l], seq)
        xr = _ffn_down(act, w_down_b, x1, mod, final_norm_g, seq)
    return xr.reshape(bsz, seq, d)
```

```python
import functools

import jax
import jax.numpy as jnp
import numpy as np
from jax import lax
from jax.experimental import pallas as pl
from jax.experimental.pallas import tpu as pltpu

F32 = jnp.float32
BF16 = jnp.bfloat16

EPS = 1e-6
ROPE_THETA = 10000.0
MLSTM_HEADS = 4
MLSTM_DH = 256
MLSTM_W = MLSTM_HEADS * MLSTM_DH
MLA_HEADS = 8
MLA_NOPE = 128
MLA_ROPE = 64
MLA_V = 128
MLA_QK = MLA_NOPE + MLA_ROPE
Q_LORA = 512
KV_LORA = 256
LANES = 128
MIB = 1024 * 1024

IN_PAD = 4096
IN_BLOCK = 1024
MISC_W = 256
MISC_OFF = IN_BLOCK - MISC_W

MOD_TN = 1024
INPROJ_TM, INPROJ_SUB = 512, 256
MLSTM_TS, MLSTM_CHUNK = 512, 512
QKV_TM = 1024
ATT_TQ = 1024
ATT_TK = 256
ATT_HP = 2
OUTPROJ_TM, OUTPROJ_SUB = 512, 256
FFN_UP_TM, FFN_UP_TF = 1024, 512
FFN_DOWN_TM, FFN_DOWN_SUB = 512, 256


def _params(n_axes, vmem_mib):
    return pltpu.CompilerParams(
        dimension_semantics=("arbitrary",) * n_axes,
        vmem_limit_bytes=vmem_mib * MIB)


def _rms(x):
    return x * lax.rsqrt(jnp.mean(x * x, axis=-1, keepdims=True) + EPS)


def _silu(x):
    return x / (1.0 + jnp.exp(-x))


def _log_sigmoid(x):
    return jnp.minimum(x, 0.0) - jnp.log1p(jnp.exp(-jnp.abs(x)))


def _mod_kernel(c_ref, w_ref, b_ref, o_ref):
    ca = _silu(c_ref[...]).astype(BF16)
    o_ref[...] = jnp.dot(ca, w_ref[...].astype(BF16),
                         preferred_element_type=F32) + b_ref[...]


def _modulation(c, ada_w, ada_b):
    bsz, d = c.shape
    n = ada_w.shape[1]
    tn = MOD_TN
    cp = jnp.pad(c, ((0, 8 - bsz), (0, 0)))
    out = pl.pallas_call(
        _mod_kernel,
        grid=(n // tn,),
        in_specs=[pl.BlockSpec((8, d), lambda j: (0, 0)),
                  pl.BlockSpec((d, tn), lambda j: (0, j)),
                  pl.BlockSpec((1, tn), lambda j: (0, j))],
        out_specs=pl.BlockSpec((8, tn), lambda j: (0, j)),
        out_shape=jax.ShapeDtypeStruct((8, n), F32),
        compiler_params=_params(1, 40),
        name="adaln_mod",
    )(cp, ada_w, ada_b.reshape(1, n))
    return out[:bsz].reshape(bsz, 6, d)


def _inproj_kernel(x_ref, mod_ref, g_ref, w_ref, o_ref, misc_ref, *, sub):
    tm = x_ref.shape[0]
    nblk = w_ref.shape[1] // IN_BLOCK
    for s in range(tm // sub):
        rows = pl.ds(s * sub, sub)
        y = _rms(x_ref[rows, :]) * g_ref[...]
        h = (y * (1.0 + mod_ref[0, 1:2, :]) + mod_ref[0, 0:1, :]).astype(BF16)
        for j in range(nblk):
            cols = slice(j * IN_BLOCK, (j + 1) * IN_BLOCK)
            acc = jnp.dot(h, w_ref[:, cols], preferred_element_type=F32)
            o_ref[rows, cols] = acc.astype(BF16)
            if j == nblk - 1:
                misc_ref[rows, :] = acc[:, MISC_OFF:]


def _in_projection(x2, mod, g, w_in_p, seq):
    t, d = x2.shape
    tm = INPROJ_TM
    per_b = seq // tm
    return pl.pallas_call(
        functools.partial(_inproj_kernel, sub=INPROJ_SUB),
        grid=(t // tm,),
        in_specs=[pl.BlockSpec((tm, d), lambda i: (i, 0)),
                  pl.BlockSpec((1, 6, d), lambda i: (i // per_b, 0, 0)),
                  pl.BlockSpec((1, d), lambda i: (0, 0)),
                  pl.BlockSpec((d, IN_PAD), lambda i: (0, 0), pipeline_mode=pl.Buffered(1))],
        out_specs=[pl.BlockSpec((tm, IN_PAD), lambda i: (i, 0)),
                   pl.BlockSpec((tm, MISC_W), lambda i: (i, 0))],
        out_shape=[jax.ShapeDtypeStruct((t, IN_PAD), BF16),
                   jax.ShapeDtypeStruct((t, MISC_W), F32)],
        compiler_params=_params(1, 48),
        name="norm_inproj",
    )(x2, mod, g.reshape(1, d), w_in_p)


def _cumsum_rows(x):
    n = x.shape[0]
    row = lax.broadcasted_iota(jnp.int32, x.shape, 0)
    sh = 1
    while sh < n:
        x = x + jnp.where(row >= sh, pltpu.roll(x, sh, axis=0), 0.0)
        sh *= 2
    return x


def _cummax_rows(x):
    n = x.shape[0]
    row = lax.broadcasted_iota(jnp.int32, x.shape, 0)
    sh = 1
    while sh < n:
        x = jnp.maximum(x, jnp.where(row >= sh, pltpu.roll(x, sh, axis=0), -jnp.inf))
        sh *= 2
    return x


def _cumsum_lanes(x):
    n = x.shape[1]
    col = lax.broadcasted_iota(jnp.int32, x.shape, 1)
    sh = 1
    while sh < n:
        x = x + jnp.where(col >= sh, pltpu.roll(x, sh, axis=1), 0.0)
        sh *= 2
    return x


def _mlstm_kernel(u_ref, v_ref, o_ref, gt_ref, cw_ref, cb_ref, wq_ref, wkt_ref, gb_ref, og_ref,
                  out_ref, xs_ref, ct_ref, n_ref, m_ref, *, ts, chunk):
    i = pl.program_id(0)
    nbatch = u_ref.shape[0]

    @pl.when(i == 0)
    def _():
        for b in range(nbatch):
            xs_ref[b, 0:8, :] = jnp.zeros((8, MLSTM_W), F32)
        ct_ref[...] = jnp.zeros(ct_ref.shape, F32)
        n_ref[...] = jnp.zeros(n_ref.shape, F32)
        m_ref[...] = jnp.zeros(m_ref.shape, F32)

    row = lax.broadcasted_iota(jnp.int32, (chunk, chunk), 0)
    col = lax.broadcasted_iota(jnp.int32, (chunk, chunk), 1)
    causal = col <= row

    for c in range(ts // chunk):
        for b in range(nbatch):
            _mlstm_chunk(u_ref.at[b], v_ref.at[b], o_ref.at[b], gt_ref.at[b], cw_ref, cb_ref,
                         wq_ref, wkt_ref, gb_ref, og_ref, out_ref.at[b], xs_ref.at[b],
                         ct_ref.at[b], n_ref.at[b], m_ref.at[b], c, chunk, causal)
    for b in range(nbatch):
        xs_ref[b, 0:8, :] = xs_ref[b, ts:ts + 8, :]


def _mlstm_chunk(u_ref, v_ref, o_ref, gt_ref, cw_ref, cb_ref, wq_ref, wkt_ref, gb_ref, og_ref,
                 out_ref, xs_ref, ct_ref, n_ref, m_ref, c, chunk, causal):
    nh, dh = MLSTM_HEADS, MLSTM_DH
    gb = gb_ref[...]
    kw = cw_ref.shape[0]
    ones = jnp.ones((chunk, LANES), BF16)
    r0 = c * chunk
    xc = u_ref[r0:r0 + chunk, :].astype(F32)
    xs_ref[8 + r0:8 + r0 + chunk, :] = xc
    acc = cb_ref[...] + cw_ref[kw - 1:kw, :] * xc
    for j in range(kw - 1):
        acc = acc + cw_ref[j:j + 1, :] * xs_ref[pl.ds(8 + r0 - (kw - 1) + j, chunk), :]
    su = _silu(acc)

    gc = gt_ref[r0:r0 + chunk, MISC_W - LANES:] + gb
    bc = _cumsum_rows(_log_sigmoid(gc))
    gt = gc.T[0:8, :]
    bt = _cumsum_lanes(_log_sigmoid(gt))
    mdc = _cummax_rows(gc - pltpu.roll(bc, LANES - nh, axis=1))
    for h in range(nh):
        c0 = h * dh
        b_c = bc[:, nh + h:nh + h + 1]
        li_r = gt[h:h + 1, :]
        b_r = bt[nh + h:nh + h + 1, :]
        g_tot = b_r[:, chunk - 1:chunk]
        z_r = li_r - b_r
        zmax = jnp.max(z_r, axis=-1, keepdims=True)
        m_prev = m_ref[h:h + 1, 0:1]

        ub = su[:, c0:c0 + dh].astype(BF16)
        qb = jnp.dot(ub, wq_ref[h], preferred_element_type=F32).astype(BF16)
        kt = jnp.dot(wkt_ref[h], ub.T,
                     preferred_element_type=F32) * (dh ** -0.5)
        vc = v_ref[r0:r0 + chunk, c0:c0 + dh]

        mx_c = jnp.maximum(m_prev, mdc[:, h:h + 1])
        m_t = b_c + mx_c
        sb = (jnp.dot(qb, kt.astype(BF16), preferred_element_type=F32)
              * jnp.exp(jnp.where(causal, z_r - mx_c, -jnp.inf))).astype(BF16)
        inter = jnp.exp(m_prev - mx_c)
        ct = ct_ref[h]
        nrep = n_ref[h]
        num = (jnp.dot(sb, vc, preferred_element_type=F32)
               + inter * jnp.dot(qb, ct.astype(BF16), preferred_element_type=F32))
        den = (jnp.dot(sb, ones, preferred_element_type=F32)
               + inter * jnp.dot(qb, nrep.astype(BF16), preferred_element_type=F32))
        lim = jnp.maximum(jnp.abs(den), jnp.exp(-m_t))
        hh = num / jnp.concatenate([lim] * (dh // LANES), axis=1)

        mm = jnp.maximum(m_prev, zmax)
        s_prev = jnp.exp(m_prev - mm)
        s_loc = jnp.exp(zmax - mm)
        kwt = (kt * (jnp.exp(z_r - zmax) * s_loc)).astype(BF16)
        ct_ref[h] = s_prev * ct + jnp.dot(kwt, vc, preferred_element_type=F32)
        n_ref[h] = s_prev * nrep + jnp.dot(kwt, ones, preferred_element_type=F32)
        m_ref[h:h + 1, :] = jnp.broadcast_to(g_tot + mm, (1, LANES))

        og = o_ref[r0:r0 + chunk, c0:c0 + dh].astype(F32)
        hh = hh / (1.0 + jnp.exp(-og))
        out_ref[r0:r0 + chunk, c0:c0 + dh] = (
            _rms(hh) * og_ref[:, c0:c0 + dh]).astype(BF16)


def _mlstm(proj, misc, conv_w, conv_b, wq, wkt, gate_b, out_g, bsz, seq):
    ts, chunk = MLSTM_TS, MLSTM_CHUNK
    t = bsz * seq
    kern = functools.partial(_mlstm_kernel, ts=ts, chunk=chunk)
    proj3 = proj.reshape(bsz, seq, proj.shape[1])
    misc3 = misc.reshape(bsz, seq, MISC_W)
    out = pl.pallas_call(
        kern,
        grid=(seq // ts,),
        in_specs=[pl.BlockSpec((bsz, ts, MLSTM_W), lambda i: (0, i, 0)),
                  pl.BlockSpec((bsz, ts, MLSTM_W), lambda i: (0, i, 1)),
                  pl.BlockSpec((bsz, ts, MLSTM_W), lambda i: (0, i, 2)),
                  pl.BlockSpec((bsz, ts, MISC_W), lambda i: (0, i, 0)),
                  pl.BlockSpec(conv_w.shape, lambda i: (0, 0)),
                  pl.BlockSpec((1, MLSTM_W), lambda i: (0, 0)),
                  pl.BlockSpec(wq.shape, lambda i: (0, 0, 0)),
                  pl.BlockSpec(wkt.shape, lambda i: (0, 0, 0)),
                  pl.BlockSpec((1, LANES), lambda i: (0, 0)),
                  pl.BlockSpec((1, MLSTM_W), lambda i: (0, 0))],
        out_specs=pl.BlockSpec((bsz, ts, MLSTM_W), lambda i: (0, i, 0)),
        out_shape=jax.ShapeDtypeStruct((bsz, seq, MLSTM_W), BF16),
        scratch_shapes=[pltpu.VMEM((bsz, ts + 8, MLSTM_W), F32),
                        pltpu.VMEM((bsz, MLSTM_HEADS, MLSTM_DH, MLSTM_DH), F32),
                        pltpu.VMEM((bsz, MLSTM_HEADS, MLSTM_DH, LANES), F32),
                        pltpu.VMEM((bsz, 8, LANES), F32)],
        compiler_params=_params(1, 48),
        name="mlstm",
    )(proj3, proj3, proj3, misc3, conv_w, conv_b.reshape(1, MLSTM_W), wq, wkt, gate_b,
      out_g.reshape(1, MLSTM_W))
    return out.reshape(t, MLSTM_W)


MLA_VA = MLA_V + 16


def _qkv_kernel(cq_ref, ckv_ref, misc_ref, pos_ref, fr_ref, qg_ref, kvg_ref, wq_ref, wkv_ref,
                qt_ref, k_ref, vt_ref, cqn, ckvn, cos_s, sin_s, kr_s, *, scale, tk):
    h = pl.program_id(1)

    @pl.when(h == 0)
    def _():
        cqn[...] = (_rms(cq_ref[...].astype(F32)) * qg_ref[...]).astype(BF16)
        ckvn[...] = (_rms(ckv_ref[...].astype(F32)) * kvg_ref[...]).astype(BF16)
        ang = fr_ref[...] * pos_ref[0].astype(F32)
        reps = LANES // ang.shape[0]
        cs = jnp.concatenate([jnp.cos(ang)] * reps, axis=0).T
        sn = jnp.concatenate([jnp.sin(ang)] * reps, axis=0).T
        cos_s[...] = cs
        sin_s[...] = sn
        y = misc_ref[:, 0:LANES]
        kr_s[...] = y * cs + pltpu.roll(y, MLA_ROPE, axis=1) * sn

    for hh in range(wq_ref.shape[0]):
        mq = jnp.dot(cqn[...], wq_ref[hh], preferred_element_type=F32)
        qt_ref[0, hh, 0:MLA_NOPE, :] = (mq[:, 0:MLA_NOPE] * scale).T.astype(BF16)
        y = mq[:, MLA_NOPE:]
        r = (y * cos_s[...] + pltpu.roll(y, MLA_ROPE, axis=1) * sin_s[...]) * scale
        qt_ref[0, hh, MLA_NOPE:MLA_QK, :] = r.T[0:MLA_ROPE, :].astype(BF16)

        mkv = jnp.dot(ckvn[...], wkv_ref[hh], preferred_element_type=F32)
        k_ref[0, hh, :, 0:MLA_NOPE] = mkv[:, 0:MLA_NOPE].astype(BF16)
        k_ref[0, hh, :, MLA_NOPE:MLA_QK] = kr_s[:, 0:MLA_ROPE].astype(BF16)
        vt = mkv[:, MLA_NOPE:].T.astype(BF16)
        for j in range(vt.shape[1] // tk):
            vt_ref[0, hh, j, 0:MLA_V, :] = vt[:, j * tk:(j + 1) * tk]
            vt_ref[0, hh, j, MLA_V:, :] = jnp.ones((MLA_VA - MLA_V, tk), BF16)


def _mla_qkv(proj, misc, pos, freqs, q_g, kv_g, wq_h, wkv_h, bsz, seq):
    tm, tk = QKV_TM, ATT_TK
    hps = MLA_HEADS
    t = bsz * seq
    nsb = seq // tm
    cq_blk = (3 * MLSTM_W) // Q_LORA
    ckv_blk = (3 * MLSTM_W + Q_LORA) // KV_LORA
    kern = functools.partial(_qkv_kernel, scale=MLA_QK ** -0.5 * float(np.log2(np.e)), tk=tk)
    return pl.pallas_call(
        kern,
        grid=(t // tm, MLA_HEADS // hps),
        in_specs=[pl.BlockSpec((tm, Q_LORA), lambda i, h: (i, cq_blk)),
                  pl.BlockSpec((tm, KV_LORA), lambda i, h: (i, ckv_blk)),
                  pl.BlockSpec((tm, MISC_W), lambda i, h: (i, 0)),
                  pl.BlockSpec((1, 1, tm), lambda i, h: (i, 0, 0)),
                  pl.BlockSpec(freqs.shape, lambda i, h: (0, 0)),
                  pl.BlockSpec((1, Q_LORA), lambda i, h: (0, 0)),
                  pl.BlockSpec((1, KV_LORA), lambda i, h: (0, 0)),
                  pl.BlockSpec((hps, Q_LORA, 2 * LANES), lambda i, h: (h, 0, 0)),
                  pl.BlockSpec((hps, KV_LORA, 2 * LANES), lambda i, h: (h, 0, 0))],
        out_specs=[pl.BlockSpec((1, hps, MLA_QK, tm), lambda i, h: (i // nsb, h, 0, i % nsb)),
                   pl.BlockSpec((1, hps, tm, MLA_QK), lambda i, h: (i // nsb, h, i % nsb, 0)),
                   pl.BlockSpec((1, hps, tm // tk, MLA_VA, tk),
                                lambda i, h: (i // nsb, h, i % nsb, 0, 0))],
        out_shape=[jax.ShapeDtypeStruct((bsz, MLA_HEADS, MLA_QK, seq), BF16),
                   jax.ShapeDtypeStruct((bsz, MLA_HEADS, seq, MLA_QK), BF16),
                   jax.ShapeDtypeStruct((bsz, MLA_HEADS, seq // tk, MLA_VA, tk), BF16)],
        scratch_shapes=[pltpu.VMEM((tm, Q_LORA), BF16),
                        pltpu.VMEM((tm, KV_LORA), BF16),
                        pltpu.VMEM((tm, LANES), F32),
                        pltpu.VMEM((tm, LANES), F32),
                        pltpu.VMEM((tm, LANES), F32)],
        compiler_params=_params(2, 40),
        name="mla_qkv",
    )(proj, proj, misc, pos, freqs, q_g.reshape(1, Q_LORA), kv_g.reshape(1, KV_LORA),
      wq_h, wkv_h)


def _flash_kernel(qt_ref, k_ref, vt_ref, g_ref, wa_ref, wb_ref, o_ref, wab_ref, wbb_ref,
                  m_ref, acc_ref, bias_ref, *, tq, tk, nq, wa_slabs, wb_slabs):
    qi = pl.program_id(2)

    step = (pl.program_id(0) * pl.num_programs(1) + pl.program_id(1)) * nq + qi

    @pl.when(step < wa_slabs)
    def _():
        wab_ref[...] = wa_ref[...].astype(BF16)

    @pl.when(step < wb_slabs)
    def _():
        wbb_ref[...] = wb_ref[...].astype(BF16)

    @pl.when((pl.program_id(0) == 0) & (pl.program_id(1) == 0) & (qi == 0))
    def _():
        key = lax.broadcasted_iota(jnp.int32, (tk, tq), 0)
        qry = lax.broadcasted_iota(jnp.int32, (tk, tq), 1)
        bias_ref[...] = jnp.where(key <= qry, 0.0, -jnp.inf).astype(F32)

    hp = qt_ref.shape[1]

    def scores(hh, kj, c0, masked):
        k = k_ref[0, hh, kj * tk:(kj + 1) * tk, :]
        s = jnp.dot(k, qt_ref[0, hh, :, c0:], preferred_element_type=F32)
        if masked:
            s = s + bias_ref[:, 0:tq - c0]
        return s

    def update(hh, s, kj, c0):
        if kj == 0:
            assert c0 == 0
            m_new = jnp.max(s, axis=0, keepdims=True)
            acc_ref[hh] = jnp.dot(vt_ref[0, hh, kj], jnp.exp2(s - m_new).astype(BF16),
                                  preferred_element_type=F32)
        else:
            m_prev = m_ref[hh, :, c0:]
            m_new = jnp.maximum(m_prev, jnp.max(s, axis=0, keepdims=True))
            alpha = jnp.exp2(m_prev - m_new)
            p = jnp.exp2(s - m_new)
            acc_ref[hh, :, c0:] = alpha * acc_ref[hh, :, c0:] + jnp.dot(
                vt_ref[0, hh, kj], p.astype(BF16), preferred_element_type=F32)
        m_ref[hh, :, c0:] = m_new

    per_q = tq // tk

    def run(q):
        def place(b):
            d = b - q * per_q
            return (0, False) if d < 0 else (d * tk, True)

        nblk = (q + 1) * per_q
        s = [scores(hh, 0, *place(0)) for hh in range(hp)]
        for b in range(nblk):
            s_next = ([scores(hh, b + 1, *place(b + 1)) for hh in range(hp)]
                      if b + 1 < nblk else None)
            for hh in range(hp):
                update(hh, s[hh], b, place(b)[0])
            s = s_next

    for q in range(nq):
        pl.when(qi == q)(functools.partial(run, q))

    for hh in range(hp):
        o = acc_ref[hh, 0:MLA_V, :] / acc_ref[hh, MLA_V:MLA_V + 1, :]
        o = o * lax.rsqrt(jnp.mean(o * o, axis=0, keepdims=True) + EPS) * g_ref[hh]
        o_ref[:, hh * MLA_V:(hh + 1) * MLA_V] = o.T.astype(BF16)


def _cast_slab_rows(rows, n_steps):
    bf16_rows = 16
    for slab in range(bf16_rows, rows + 1, bf16_rows):
        if rows % slab == 0 and rows // slab <= n_steps:
            return slab
    raise ValueError("no slab size fits")


def _mla_attention(qt, k, vt, out_g, wa, wb):
    bsz, nh, seq, _ = k.shape
    tq, tk = ATT_TQ, ATT_TK
    nq = seq // tq
    hp = ATT_HP
    ng = nh // hp
    n_steps = bsz * ng * nq
    sa = _cast_slab_rows(wa.shape[0], n_steps)
    sb = _cast_slab_rows(wb.shape[0], n_steps)
    na, nb = wa.shape[0] // sa, wb.shape[0] // sb
    kern = functools.partial(_flash_kernel, tq=tq, tk=tk, nq=nq, wa_slabs=na, wb_slabs=nb)
    step = lambda b, h, i: (b * ng + h) * nq + i
    wa_map = lambda b, h, i: (jnp.minimum(step(b, h, i), na - 1), 0)
    wb_map = lambda b, h, i: (jnp.minimum(step(b, h, i), nb - 1), 0)
    return pl.pallas_call(
        kern,
        grid=(bsz, ng, nq),
        in_specs=[pl.BlockSpec((1, hp, MLA_QK, tq), lambda b, h, i: (b, h, 0, i)),
                  pl.BlockSpec((1, hp, seq, MLA_QK), lambda b, h, i: (b, h, 0, 0)),
                  pl.BlockSpec((1, hp, seq // tk, MLA_VA, tk), lambda b, h, i: (b, h, 0, 0, 0)),
                  pl.BlockSpec((hp, MLA_V, 1), lambda b, h, i: (h, 0, 0)),
                  pl.BlockSpec((sa, wa.shape[1]), wa_map),
                  pl.BlockSpec((sb, wb.shape[1]), wb_map)],
        out_specs=[pl.BlockSpec((tq, hp * MLA_V), lambda b, h, i: (b * nq + i, h)),
                   pl.BlockSpec((sa, wa.shape[1]), wa_map),
                   pl.BlockSpec((sb, wb.shape[1]), wb_map)],
        out_shape=[jax.ShapeDtypeStruct((bsz * seq, nh * MLA_V), BF16),
                   jax.ShapeDtypeStruct(wa.shape, BF16),
                   jax.ShapeDtypeStruct(wb.shape, BF16)],
        scratch_shapes=[pltpu.VMEM((hp, 1, tq), F32),
                        pltpu.VMEM((hp, MLA_VA, tq), F32),
                        pltpu.VMEM((tk, tq), F32)],
        compiler_params=_params(3, 48),
        name="mla_attention",
    )(qt, k, vt, out_g.reshape(nh, MLA_V, 1), wa, wb)


def _outproj_kernel(hm_ref, ha_ref, w_ref, x_ref, mod_ref, g_ref, x1_ref, h2_ref, *, sub):
    km = hm_ref.shape[1]
    for s in range(hm_ref.shape[0] // sub):
        rows = pl.ds(s * sub, sub)
        mix = (jnp.dot(hm_ref[rows, :], w_ref[0:km, :], preferred_element_type=F32)
               + jnp.dot(ha_ref[rows, :], w_ref[km:, :], preferred_element_type=F32))
        x1 = x_ref[rows, :] + mod_ref[0, 2:3, :] * mix
        x1_ref[rows, :] = x1
        y = _rms(x1) * g_ref[...]
        h2_ref[rows, :] = (y * (1.0 + mod_ref[0, 4:5, :]) + mod_ref[0, 3:4, :]).astype(BF16)


def _out_projection(hm, ha, w_out, x2, mod, g, seq):
    t, d = x2.shape
    tm = OUTPROJ_TM
    per_b = seq // tm
    return pl.pallas_call(
        functools.partial(_outproj_kernel, sub=OUTPROJ_SUB),
        grid=(t // tm,),
        in_specs=[pl.BlockSpec((tm, hm.shape[1]), lambda i: (i, 0)),
                  pl.BlockSpec((tm, ha.shape[1]), lambda i: (i, 0)),
                  pl.BlockSpec(w_out.shape, lambda i: (0, 0), pipeline_mode=pl.Buffered(1)),
                  pl.BlockSpec((tm, d), lambda i: (i, 0)),
                  pl.BlockSpec((1, 6, d), lambda i: (i // per_b, 0, 0)),
                  pl.BlockSpec((1, d), lambda i: (0, 0))],
        out_specs=[pl.BlockSpec((tm, d), lambda i: (i, 0)),
                   pl.BlockSpec((tm, d), lambda i: (i, 0))],
        out_shape=[jax.ShapeDtypeStruct((t, d), F32),
                   jax.ShapeDtypeStruct((t, d), BF16)],
        compiler_params=_params(1, 48),
        name="out_proj",
    )(hm, ha, w_out, x2, mod, g.reshape(1, d))


def _ffn_up_kernel(h_ref, wv_ref, wg_ref, cwv_ref, cwg_ref, cbv_ref, cbg_ref, o_ref,
                   xv_ref, xg_ref, wvb_ref, wgb_ref, *, tm, blocks_per_seq):
    i = pl.program_id(1)

    @pl.when(i % blocks_per_seq == 0)
    def _():
        xv_ref[0:8, :] = jnp.zeros((8, xv_ref.shape[1]), F32)
        xg_ref[0:8, :] = jnp.zeros((8, xg_ref.shape[1]), F32)

    @pl.when(i == 0)
    def _():
        wvb_ref[...] = wv_ref[...].astype(BF16)
        wgb_ref[...] = wg_ref[...].astype(BF16)

    h = h_ref[...]

    def conv(w_ref, cw_ref, cb_ref, xs_ref):
        kw = cw_ref.shape[0]
        up = jnp.dot(h, w_ref[...], preferred_element_type=F32)
        xs_ref[8:8 + tm, :] = up
        y = cb_ref[...] + cw_ref[kw - 1:kw, :] * up
        for j in range(kw - 1):
            y = y + cw_ref[j:j + 1, :] * xs_ref[pl.ds(8 - (kw - 1) + j, tm), :]
        xs_ref[0:8, :] = xs_ref[tm:tm + 8, :]
        return y

    val = conv(wvb_ref, cwv_ref, cbv_ref, xv_ref)
    gate = conv(wgb_ref, cwg_ref, cbg_ref, xg_ref)
    o_ref[...] = (_silu(gate) * val).astype(BF16)


def _ffn_up(h2, w_up, conv_w, conv_b, seq):
    t, d = h2.shape
    f = w_up.shape[1] // 2
    tm, tf = FFN_UP_TM, FFN_UP_TF
    nf = f // tf
    kw = conv_w.shape[0]
    kern = functools.partial(_ffn_up_kernel, tm=tm, blocks_per_seq=seq // tm)
    cb = conv_b.reshape(1, 2 * f)
    return pl.pallas_call(
        kern,
        grid=(nf, t // tm),
        in_specs=[pl.BlockSpec((tm, d), lambda j, i: (i, 0)),
                  pl.BlockSpec((d, tf), lambda j, i: (0, j)),
                  pl.BlockSpec((d, tf), lambda j, i: (0, nf + j)),
                  pl.BlockSpec((kw, tf), lambda j, i: (0, j)),
                  pl.BlockSpec((kw, tf), lambda j, i: (0, nf + j)),
                  pl.BlockSpec((1, tf), lambda j, i: (0, j)),
                  pl.BlockSpec((1, tf), lambda j, i: (0, nf + j))],
        out_specs=pl.BlockSpec((tm, tf), lambda j, i: (i, j)),
        out_shape=jax.ShapeDtypeStruct((t, f), BF16),
        scratch_shapes=[pltpu.VMEM((tm + 8, tf), F32),
                        pltpu.VMEM((tm + 8, tf), F32),
                        pltpu.VMEM((d, tf), BF16),
                        pltpu.VMEM((d, tf), BF16)],
        compiler_params=_params(2, 56),
        name="ffn_up",
    )(h2, w_up, w_up, conv_w, conv_w, cb, cb)


def _ffn_down_kernel(a_ref, w_ref, x1_ref, mod_ref, g_ref, o_ref, *, sub):
    for s in range(a_ref.shape[0] // sub):
        rows = pl.ds(s * sub, sub)
        y = jnp.dot(a_ref[rows, :], w_ref[...], preferred_element_type=F32)
        x2 = x1_ref[rows, :] + mod_ref[0, 5:6, :] * y
        o_ref[rows, :] = _rms(x2) * g_ref[...]


def _ffn_down(act, w_down, x1, mod, g, seq):
    t, f = act.shape
    d = w_down.shape[1]
    tm = FFN_DOWN_TM
    per_b = seq // tm
    return pl.pallas_call(
        functools.partial(_ffn_down_kernel, sub=FFN_DOWN_SUB),
        grid=(t // tm,),
        in_specs=[pl.BlockSpec((tm, f), lambda i: (i, 0)),
                  pl.BlockSpec((f, d), lambda i: (0, 0), pipeline_mode=pl.Buffered(1)),
                  pl.BlockSpec((tm, d), lambda i: (i, 0)),
                  pl.BlockSpec((1, 6, d), lambda i: (i // per_b, 0, 0)),
                  pl.BlockSpec((1, d), lambda i: (0, 0))],
        out_specs=pl.BlockSpec((tm, d), lambda i: (i, 0)),
        out_shape=jax.ShapeDtypeStruct((t, d), F32),
        compiler_params=_params(1, 58),
        name="ffn_down",
    )(act, w_down, x1, mod, g.reshape(1, d))


def _rot_cols(w):
    half = w.shape[-1] // 2
    return jnp.concatenate([-w[..., half:], w[..., :half]], axis=-1)


def _win_prep_kernel(w_ref, tail_ref, o_ref, *, n_main):
    j = pl.program_id(0)
    gates = 2 * MLSTM_HEADS
    half = MLA_ROPE // 2

    @pl.when(j < n_main)
    def _():
        o_ref[...] = w_ref[...].T.astype(BF16)

    @pl.when(j == n_main)
    def _():
        o_ref[...] = tail_ref[gates:gates + Q_LORA, :].T.astype(BF16)

    @pl.when(j == n_main + 1)
    def _():
        r0 = gates + Q_LORA
        kr0 = r0 + KV_LORA
        blk = o_ref.shape[1]
        used = KV_LORA + 2 * MLA_ROPE + gates
        o_ref[...] = jnp.concatenate(
            [tail_ref[r0:kr0, :],
             tail_ref[kr0:kr0 + MLA_ROPE, :],
             -tail_ref[kr0 + half:kr0 + MLA_ROPE, :],
             tail_ref[kr0:kr0 + half, :],
             tail_ref[0:gates, :],
             jnp.zeros((blk - used, o_ref.shape[0]), F32)], axis=0).T.astype(BF16)


def _prep_w_in(w_in):
    d = w_in.shape[0]
    wt = w_in.T
    n_head = 3 * MLSTM_W
    blk = Q_LORA
    n_main = n_head // blk
    return pl.pallas_call(
        functools.partial(_win_prep_kernel, n_main=n_main),
        grid=(IN_PAD // blk,),
        in_specs=[pl.BlockSpec((blk, d), lambda j: (jnp.minimum(j, n_main - 1), 0)),
                  pl.BlockSpec((IN_PAD - n_head, d), lambda j: (n_head // (IN_PAD - n_head), 0))],
        out_specs=pl.BlockSpec((d, blk), lambda j: (0, j)),
        out_shape=jax.ShapeDtypeStruct((d, IN_PAD), BF16),
        compiler_params=_params(1, 40),
        name="w_in_prep",
    )(wt, wt)


def kernel(x, c, positions, ada_w, ada_b, attn_norm_g, w_in, mlstm_conv_w, mlstm_conv_b, mlstm_wq, mlstm_wk, mlstm_igate_b, mlstm_fgate_b, mla_q_norm_g, mla_w_uq, mla_kv_norm_g, mla_w_ukv, mlstm_out_g, mla_out_g, w_out, ffn_norm_g, ffn_w_up, ffn_conv_w, ffn_conv_b, ffn_w_down, final_norm_g):
    bsz, seq, d = x.shape
    t = bsz * seq
    depth = ada_w.shape[0]
    assert depth == 1, "the final RMSNorm is fused into the single layer's down-projection"
    xr = x.reshape(t, d)
    pos = positions.reshape(t // QKV_TM, 1, QKV_TM)
    half = MLA_ROPE // 2
    freqs = (ROPE_THETA ** (-jnp.arange(half, dtype=F32) / half)).reshape(half, 1)

    for l in range(depth):
        mod = _modulation(c, ada_w[l], ada_b[l])

        proj, misc = _in_projection(xr, mod, attn_norm_g[l], _prep_w_in(w_in[l]), seq)

        wq_m = mlstm_wq[l].astype(BF16)
        wkt_m = mlstm_wk[l].transpose(0, 2, 1).astype(BF16)
        gate_b = jnp.concatenate(
            [mlstm_igate_b[l], mlstm_fgate_b[l],
             jnp.zeros((LANES - 2 * MLSTM_HEADS,), F32)]).reshape(1, LANES)
        hm = _mlstm(proj, misc, mlstm_conv_w[l], mlstm_conv_b[l], wq_m, wkt_m, gate_b,
                    mlstm_out_g[l], bsz, seq)

        wq = mla_w_uq[l].reshape(Q_LORA, MLA_HEADS, MLA_QK)
        wq_r = wq[..., MLA_NOPE:]
        wq_h = jnp.concatenate([wq[..., :MLA_NOPE], wq_r, _rot_cols(wq_r)], axis=-1)
        wq_h = wq_h.transpose(1, 0, 2).astype(BF16)
        wkv_h = mla_w_ukv[l].reshape(KV_LORA, MLA_HEADS, MLA_NOPE + MLA_V)
        wkv_h = wkv_h.transpose(1, 0, 2).astype(BF16)
        qt, k, vt = _mla_qkv(proj, misc, pos, freqs, mla_q_norm_g[l], mla_kv_norm_g[l],
                             wq_h, wkv_h, bsz, seq)
        ha, w_down_b, w_out_b = _mla_attention(qt, k, vt, mla_out_g[l], ffn_w_down[l], w_out[l])

        x1, h2 = _out_projection(hm, ha, w_out_b, xr, mod, ffn_norm_g[l], seq)

        act = _ffn_up(h2, ffn_w_up[l], ffn_conv_w[l], ffn_conv_b[l], seq)
        xr = _ffn_down(act, w_down_b, x1, mod, final_norm_g, seq)
    return xr.reshape(bsz, seq, d)
```

```python
import functools

import jax
import jax.numpy as jnp
import numpy as np
from jax import lax
from jax.experimental import pallas as pl
from jax.experimental.pallas import tpu as pltpu

F32 = jnp.float32
BF16 = jnp.bfloat16

EPS = 1e-6
ROPE_THETA = 10000.0
MLSTM_HEADS = 4
MLSTM_DH = 256
MLSTM_W = MLSTM_HEADS * MLSTM_DH
MLA_HEADS = 8
MLA_NOPE = 128
MLA_ROPE = 64
MLA_V = 128
MLA_QK = MLA_NOPE + MLA_ROPE
Q_LORA = 512
KV_LORA = 256
LANES = 128
MIB = 1024 * 1024

IN_PAD = 4096
IN_BLOCK = 1024
MISC_W = 256
MISC_OFF = IN_BLOCK - MISC_W

MOD_TN = 1024
INPROJ_TM, INPROJ_SUB = 512, 256
MLSTM_TS, MLSTM_CHUNK = 512, 512
QKV_TM = 1024
ATT_TQ = 1024
ATT_TK = 256
ATT_HP = 2
OUTPROJ_TM, OUTPROJ_SUB = 512, 512
FFN_UP_TM, FFN_UP_TF = 1024, 512
FFN_DOWN_TM, FFN_DOWN_SUB = 512, 512


def _params(n_axes, vmem_mib):
    return pltpu.CompilerParams(
        dimension_semantics=("arbitrary",) * n_axes,
        vmem_limit_bytes=vmem_mib * MIB)


def _rms(x):
    return x * lax.rsqrt(jnp.mean(x * x, axis=-1, keepdims=True) + EPS)


def _silu(x):
    return x / (1.0 + jnp.exp(-x))


def _log_sigmoid(x):
    return jnp.minimum(x, 0.0) - jnp.log1p(jnp.exp(-jnp.abs(x)))


def _mod_kernel(c_ref, w_ref, b_ref, o_ref):
    ca = _silu(c_ref[...]).astype(BF16)
    o_ref[...] = jnp.dot(ca, w_ref[...].astype(BF16),
                         preferred_element_type=F32) + b_ref[...]


def _modulation(c, ada_w, ada_b):
    bsz, d = c.shape
    n = ada_w.shape[1]
    tn = MOD_TN
    cp = jnp.pad(c, ((0, 8 - bsz), (0, 0)))
    out = pl.pallas_call(
        _mod_kernel,
        grid=(n // tn,),
        in_specs=[pl.BlockSpec((8, d), lambda j: (0, 0)),
                  pl.BlockSpec((d, tn), lambda j: (0, j)),
                  pl.BlockSpec((1, tn), lambda j: (0, j))],
        out_specs=pl.BlockSpec((8, tn), lambda j: (0, j)),
        out_shape=jax.ShapeDtypeStruct((8, n), F32),
        compiler_params=_params(1, 40),
        name="adaln_mod",
    )(cp, ada_w, ada_b.reshape(1, n))
    return out[:bsz].reshape(bsz, 6, d)


def _inproj_kernel(x_ref, mod_ref, g_ref, w_ref, o_ref, misc_ref, *, sub):
    tm = x_ref.shape[0]
    nblk = w_ref.shape[1] // IN_BLOCK
    for s in range(tm // sub):
        rows = pl.ds(s * sub, sub)
        y = _rms(x_ref[rows, :]) * g_ref[...]
        h = (y * (1.0 + mod_ref[0, 1:2, :]) + mod_ref[0, 0:1, :]).astype(BF16)
        for j in range(nblk):
            cols = slice(j * IN_BLOCK, (j + 1) * IN_BLOCK)
            acc = jnp.dot(h, w_ref[:, cols], preferred_element_type=F32)
            o_ref[rows, cols] = acc.astype(BF16)
            if j == nblk - 1:
                misc_ref[rows, :] = acc[:, MISC_OFF:]


def _in_projection(x2, mod, g, w_in_p, seq):
    t, d = x2.shape
    tm = INPROJ_TM
    per_b = seq // tm
    return pl.pallas_call(
        functools.partial(_inproj_kernel, sub=INPROJ_SUB),
        grid=(t // tm,),
        in_specs=[pl.BlockSpec((tm, d), lambda i: (i, 0)),
                  pl.BlockSpec((1, 6, d), lambda i: (i // per_b, 0, 0)),
                  pl.BlockSpec((1, d), lambda i: (0, 0)),
                  pl.BlockSpec((d, IN_PAD), lambda i: (0, 0), pipeline_mode=pl.Buffered(1))],
        out_specs=[pl.BlockSpec((tm, IN_PAD), lambda i: (i, 0)),
                   pl.BlockSpec((tm, MISC_W), lambda i: (i, 0))],
        out_shape=[jax.ShapeDtypeStruct((t, IN_PAD), BF16),
                   jax.ShapeDtypeStruct((t, MISC_W), F32)],
        compiler_params=_params(1, 48),
        name="norm_inproj",
    )(x2, mod, g.reshape(1, d), w_in_p)


def _cumsum_rows(x):
    n = x.shape[0]
    row = lax.broadcasted_iota(jnp.int32, x.shape, 0)
    sh = 1
    while sh < n:
        x = x + jnp.where(row >= sh, pltpu.roll(x, sh, axis=0), 0.0)
        sh *= 2
    return x


def _cummax_rows(x):
    n = x.shape[0]
    row = lax.broadcasted_iota(jnp.int32, x.shape, 0)
    sh = 1
    while sh < n:
        x = jnp.maximum(x, jnp.where(row >= sh, pltpu.roll(x, sh, axis=0), -jnp.inf))
        sh *= 2
    return x


def _cumsum_lanes(x):
    n = x.shape[1]
    col = lax.broadcasted_iota(jnp.int32, x.shape, 1)
    sh = 1
    while sh < n:
        x = x + jnp.where(col >= sh, pltpu.roll(x, sh, axis=1), 0.0)
        sh *= 2
    return x


def _mlstm_kernel(u_ref, v_ref, o_ref, gt_ref, cw_ref, cb_ref, wq_ref, wkt_ref, gb_ref, og_ref,
                  out_ref, xs_ref, ct_ref, n_ref, m_ref, *, ts, chunk):
    i = pl.program_id(0)
    nbatch = u_ref.shape[0]

    @pl.when(i == 0)
    def _():
        for b in range(nbatch):
            xs_ref[b, 0:8, :] = jnp.zeros((8, MLSTM_W), F32)
        ct_ref[...] = jnp.zeros(ct_ref.shape, F32)
        n_ref[...] = jnp.zeros(n_ref.shape, F32)
        m_ref[...] = jnp.zeros(m_ref.shape, F32)

    row = lax.broadcasted_iota(jnp.int32, (chunk, chunk), 0)
    col = lax.broadcasted_iota(jnp.int32, (chunk, chunk), 1)
    causal = col <= row

    for c in range(ts // chunk):
        for b in range(nbatch):
            _mlstm_chunk(u_ref.at[b], v_ref.at[b], o_ref.at[b], gt_ref.at[b], cw_ref, cb_ref,
                         wq_ref, wkt_ref, gb_ref, og_ref, out_ref.at[b], xs_ref.at[b],
                         ct_ref.at[b], n_ref.at[b], m_ref.at[b], c, chunk, causal)
    for b in range(nbatch):
        xs_ref[b, 0:8, :] = xs_ref[b, ts:ts + 8, :]


def _mlstm_chunk(u_ref, v_ref, o_ref, gt_ref, cw_ref, cb_ref, wq_ref, wkt_ref, gb_ref, og_ref,
                 out_ref, xs_ref, ct_ref, n_ref, m_ref, c, chunk, causal):
    nh, dh = MLSTM_HEADS, MLSTM_DH
    gb = gb_ref[...]
    kw = cw_ref.shape[0]
    ones = jnp.ones((chunk, LANES), BF16)
    r0 = c * chunk
    xc = u_ref[r0:r0 + chunk, :].astype(F32)
    xs_ref[8 + r0:8 + r0 + chunk, :] = xc
    acc = cb_ref[...] + cw_ref[kw - 1:kw, :] * xc
    for j in range(kw - 1):
        acc = acc + cw_ref[j:j + 1, :] * xs_ref[pl.ds(8 + r0 - (kw - 1) + j, chunk), :]
    su = _silu(acc)

    gc = gt_ref[r0:r0 + chunk, MISC_W - LANES:] + gb
    bc = _cumsum_rows(_log_sigmoid(gc))
    gt = gc.T[0:8, :]
    bt = _cumsum_lanes(_log_sigmoid(gt))
    mdc = _cummax_rows(gc - pltpu.roll(bc, LANES - nh, axis=1))
    for h in range(nh):
        c0 = h * dh
        b_c = bc[:, nh + h:nh + h + 1]
        li_r = gt[h:h + 1, :]
        b_r = bt[nh + h:nh + h + 1, :]
        g_tot = b_r[:, chunk - 1:chunk]
        z_r = li_r - b_r
        zmax = jnp.max(z_r, axis=-1, keepdims=True)
        m_prev = m_ref[h:h + 1, 0:1]

        ub = su[:, c0:c0 + dh].astype(BF16)
        qb = jnp.dot(ub, wq_ref[h], preferred_element_type=F32).astype(BF16)
        kt = jnp.dot(wkt_ref[h], ub.T,
                     preferred_element_type=F32) * (dh ** -0.5)
        vc = v_ref[r0:r0 + chunk, c0:c0 + dh]

        mx_c = jnp.maximum(m_prev, mdc[:, h:h + 1])
        m_t = b_c + mx_c
        sb = (jnp.dot(qb, kt.astype(BF16), preferred_element_type=F32)
              * jnp.exp(jnp.where(causal, z_r - mx_c, -jnp.inf))).astype(BF16)
        inter = jnp.exp(m_prev - mx_c)
        ct = ct_ref[h]
        nrep = n_ref[h]
        num = (jnp.dot(sb, vc, preferred_element_type=F32)
               + inter * jnp.dot(qb, ct.astype(BF16), preferred_element_type=F32))
        den = (jnp.dot(sb, ones, preferred_element_type=F32)
               + inter * jnp.dot(qb, nrep.astype(BF16), preferred_element_type=F32))
        lim = jnp.maximum(jnp.abs(den), jnp.exp(-m_t))
        hh = num / jnp.concatenate([lim] * (dh // LANES), axis=1)

        mm = jnp.maximum(m_prev, zmax)
        s_prev = jnp.exp(m_prev - mm)
        s_loc = jnp.exp(zmax - mm)
        kwt = (kt * (jnp.exp(z_r - zmax) * s_loc)).astype(BF16)
        ct_ref[h] = s_prev * ct + jnp.dot(kwt, vc, preferred_element_type=F32)
        n_ref[h] = s_prev * nrep + jnp.dot(kwt, ones, preferred_element_type=F32)
        m_ref[h:h + 1, :] = jnp.broadcast_to(g_tot + mm, (1, LANES))

        og = o_ref[r0:r0 + chunk, c0:c0 + dh].astype(F32)
        hh = hh / (1.0 + jnp.exp(-og))
        out_ref[r0:r0 + chunk, c0:c0 + dh] = (
            _rms(hh) * og_ref[:, c0:c0 + dh]).astype(BF16)


def _mlstm(proj, misc, conv_w, conv_b, wq, wkt, gate_b, out_g, bsz, seq):
    ts, chunk = MLSTM_TS, MLSTM_CHUNK
    t = bsz * seq
    kern = functools.partial(_mlstm_kernel, ts=ts, chunk=chunk)
    proj3 = proj.reshape(bsz, seq, proj.shape[1])
    misc3 = misc.reshape(bsz, seq, MISC_W)
    out = pl.pallas_call(
        kern,
        grid=(seq // ts,),
        in_specs=[pl.BlockSpec((bsz, ts, MLSTM_W), lambda i: (0, i, 0)),
                  pl.BlockSpec((bsz, ts, MLSTM_W), lambda i: (0, i, 1)),
                  pl.BlockSpec((bsz, ts, MLSTM_W), lambda i: (0, i, 2)),
                  pl.BlockSpec((bsz, ts, MISC_W), lambda i: (0, i, 0)),
                  pl.BlockSpec(conv_w.shape, lambda i: (0, 0)),
                  pl.BlockSpec((1, MLSTM_W), lambda i: (0, 0)),
                  pl.BlockSpec(wq.shape, lambda i: (0, 0, 0)),
                  pl.BlockSpec(wkt.shape, lambda i: (0, 0, 0)),
                  pl.BlockSpec((1, LANES), lambda i: (0, 0)),
                  pl.BlockSpec((1, MLSTM_W), lambda i: (0, 0))],
        out_specs=pl.BlockSpec((bsz, ts, MLSTM_W), lambda i: (0, i, 0)),
        out_shape=jax.ShapeDtypeStruct((bsz, seq, MLSTM_W), BF16),
        scratch_shapes=[pltpu.VMEM((bsz, ts + 8, MLSTM_W), F32),
                        pltpu.VMEM((bsz, MLSTM_HEADS, MLSTM_DH, MLSTM_DH), F32),
                        pltpu.VMEM((bsz, MLSTM_HEADS, MLSTM_DH, LANES), F32),
                        pltpu.VMEM((bsz, 8, LANES), F32)],
        compiler_params=_params(1, 48),
        name="mlstm",
    )(proj3, proj3, proj3, misc3, conv_w, conv_b.reshape(1, MLSTM_W), wq, wkt, gate_b,
      out_g.reshape(1, MLSTM_W))
    return out.reshape(t, MLSTM_W)


MLA_VA = MLA_V + 16


def _qkv_kernel(cq_ref, ckv_ref, misc_ref, pos_ref, fr_ref, qg_ref, kvg_ref, wq_ref, wkv_ref,
                qt_ref, k_ref, vt_ref, cqn, ckvn, cos_s, sin_s, kr_s, *, scale, tk):
    h = pl.program_id(1)

    @pl.when(h == 0)
    def _():
        cqn[...] = (_rms(cq_ref[...].astype(F32)) * qg_ref[...]).astype(BF16)
        ckvn[...] = (_rms(ckv_ref[...].astype(F32)) * kvg_ref[...]).astype(BF16)
        ang = fr_ref[...] * pos_ref[0].astype(F32)
        reps = LANES // ang.shape[0]
        cs = jnp.concatenate([jnp.cos(ang)] * reps, axis=0).T
        sn = jnp.concatenate([jnp.sin(ang)] * reps, axis=0).T
        cos_s[...] = cs
        sin_s[...] = sn
        y = misc_ref[:, 0:LANES]
        kr_s[...] = y * cs + pltpu.roll(y, MLA_ROPE, axis=1) * sn

    for hh in range(wq_ref.shape[0]):
        mq = jnp.dot(cqn[...], wq_ref[hh], preferred_element_type=F32)
        qt_ref[0, hh, 0:MLA_NOPE, :] = (mq[:, 0:MLA_NOPE] * scale).T.astype(BF16)
        y = mq[:, MLA_NOPE:]
        r = (y * cos_s[...] + pltpu.roll(y, MLA_ROPE, axis=1) * sin_s[...]) * scale
        qt_ref[0, hh, MLA_NOPE:MLA_QK, :] = r.T[0:MLA_ROPE, :].astype(BF16)

        mkv = jnp.dot(ckvn[...], wkv_ref[hh], preferred_element_type=F32)
        k_ref[0, hh, :, 0:MLA_NOPE] = mkv[:, 0:MLA_NOPE].astype(BF16)
        k_ref[0, hh, :, MLA_NOPE:MLA_QK] = kr_s[:, 0:MLA_ROPE].astype(BF16)
        vt = mkv[:, MLA_NOPE:].T.astype(BF16)
        for j in range(vt.shape[1] // tk):
            vt_ref[0, hh, j, 0:MLA_V, :] = vt[:, j * tk:(j + 1) * tk]
            vt_ref[0, hh, j, MLA_V:, :] = jnp.ones((MLA_VA - MLA_V, tk), BF16)


def _mla_qkv(proj, misc, pos, freqs, q_g, kv_g, wq_h, wkv_h, bsz, seq):
    tm, tk = QKV_TM, ATT_TK
    hps = MLA_HEADS
    t = bsz * seq
    nsb = seq // tm
    cq_blk = (3 * MLSTM_W) // Q_LORA
    ckv_blk = (3 * MLSTM_W + Q_LORA) // KV_LORA
    kern = functools.partial(_qkv_kernel, scale=MLA_QK ** -0.5 * float(np.log2(np.e)), tk=tk)
    return pl.pallas_call(
        kern,
        grid=(t // tm, MLA_HEADS // hps),
        in_specs=[pl.BlockSpec((tm, Q_LORA), lambda i, h: (i, cq_blk)),
                  pl.BlockSpec((tm, KV_LORA), lambda i, h: (i, ckv_blk)),
                  pl.BlockSpec((tm, MISC_W), lambda i, h: (i, 0)),
                  pl.BlockSpec((1, 1, tm), lambda i, h: (i, 0, 0)),
                  pl.BlockSpec(freqs.shape, lambda i, h: (0, 0)),
                  pl.BlockSpec((1, Q_LORA), lambda i, h: (0, 0)),
                  pl.BlockSpec((1, KV_LORA), lambda i, h: (0, 0)),
                  pl.BlockSpec((hps, Q_LORA, 2 * LANES), lambda i, h: (h, 0, 0)),
                  pl.BlockSpec((hps, KV_LORA, 2 * LANES), lambda i, h: (h, 0, 0))],
        out_specs=[pl.BlockSpec((1, hps, MLA_QK, tm), lambda i, h: (i // nsb, h, 0, i % nsb)),
                   pl.BlockSpec((1, hps, tm, MLA_QK), lambda i, h: (i // nsb, h, i % nsb, 0)),
                   pl.BlockSpec((1, hps, tm // tk, MLA_VA, tk),
                                lambda i, h: (i // nsb, h, i % nsb, 0, 0))],
        out_shape=[jax.ShapeDtypeStruct((bsz, MLA_HEADS, MLA_QK, seq), BF16),
                   jax.ShapeDtypeStruct((bsz, MLA_HEADS, seq, MLA_QK), BF16),
                   jax.ShapeDtypeStruct((bsz, MLA_HEADS, seq // tk, MLA_VA, tk), BF16)],
        scratch_shapes=[pltpu.VMEM((tm, Q_LORA), BF16),
                        pltpu.VMEM((tm, KV_LORA), BF16),
                        pltpu.VMEM((tm, LANES), F32),
                        pltpu.VMEM((tm, LANES), F32),
                        pltpu.VMEM((tm, LANES), F32)],
        compiler_params=_params(2, 40),
        name="mla_qkv",
    )(proj, proj, misc, pos, freqs, q_g.reshape(1, Q_LORA), kv_g.reshape(1, KV_LORA),
      wq_h, wkv_h)


def _flash_kernel(qt_ref, k_ref, vt_ref, g_ref, wa_ref, wb_ref, o_ref, wab_ref, wbb_ref,
                  m_ref, acc_ref, bias_ref, *, tq, tk, nq, wa_slabs, wb_slabs):
    qi = pl.program_id(2)

    step = (pl.program_id(0) * pl.num_programs(1) + pl.program_id(1)) * nq + qi

    @pl.when(step < wa_slabs)
    def _():
        wab_ref[...] = wa_ref[...].astype(BF16)

    @pl.when(step < wb_slabs)
    def _():
        wbb_ref[...] = wb_ref[...].astype(BF16)

    @pl.when((pl.program_id(0) == 0) & (pl.program_id(1) == 0) & (qi == 0))
    def _():
        key = lax.broadcasted_iota(jnp.int32, (tk, tq), 0)
        qry = lax.broadcasted_iota(jnp.int32, (tk, tq), 1)
        bias_ref[...] = jnp.where(key <= qry, 0.0, -jnp.inf).astype(F32)

    hp = qt_ref.shape[1]

    def scores(hh, kj, c0, masked):
        k = k_ref[0, hh, kj * tk:(kj + 1) * tk, :]
        s = jnp.dot(k, qt_ref[0, hh, :, c0:], preferred_element_type=F32)
        if masked:
            s = s + bias_ref[:, 0:tq - c0]
        return s

    def update(hh, s, kj, c0):
        if kj == 0:
            assert c0 == 0
            m_new = jnp.max(s, axis=0, keepdims=True)
            acc_ref[hh] = jnp.dot(vt_ref[0, hh, kj], jnp.exp2(s - m_new).astype(BF16),
                                  preferred_element_type=F32)
        else:
            m_prev = m_ref[hh, :, c0:]
            m_new = jnp.maximum(m_prev, jnp.max(s, axis=0, keepdims=True))
            alpha = jnp.exp2(m_prev - m_new)
            p = jnp.exp2(s - m_new)
            acc_ref[hh, :, c0:] = alpha * acc_ref[hh, :, c0:] + jnp.dot(
                vt_ref[0, hh, kj], p.astype(BF16), preferred_element_type=F32)
        m_ref[hh, :, c0:] = m_new

    per_q = tq // tk

    def run(q):
        def place(b):
            d = b - q * per_q
            return (0, False) if d < 0 else (d * tk, True)

        nblk = (q + 1) * per_q
        s = [scores(hh, 0, *place(0)) for hh in range(hp)]
        for b in range(nblk):
            s_next = ([scores(hh, b + 1, *place(b + 1)) for hh in range(hp)]
                      if b + 1 < nblk else None)
            for hh in range(hp):
                update(hh, s[hh], b, place(b)[0])
            s = s_next

    for q in range(nq):
        pl.when(qi == q)(functools.partial(run, q))

    for hh in range(hp):
        o = acc_ref[hh, 0:MLA_V, :] / acc_ref[hh, MLA_V:MLA_V + 1, :]
        o = o * lax.rsqrt(jnp.mean(o * o, axis=0, keepdims=True) + EPS) * g_ref[hh]
        o_ref[:, hh * MLA_V:(hh + 1) * MLA_V] = o.T.astype(BF16)


def _cast_slab_rows(rows, n_steps):
    bf16_rows = 16
    for slab in range(bf16_rows, rows + 1, bf16_rows):
        if rows % slab == 0 and rows // slab <= n_steps:
            return slab
    raise ValueError("no slab size fits")


def _mla_attention(qt, k, vt, out_g, wa, wb):
    bsz, nh, seq, _ = k.shape
    tq, tk = ATT_TQ, ATT_TK
    nq = seq // tq
    hp = ATT_HP
    ng = nh // hp
    n_steps = bsz * ng * nq
    sa = _cast_slab_rows(wa.shape[0], n_steps)
    sb = _cast_slab_rows(wb.shape[0], n_steps)
    na, nb = wa.shape[0] // sa, wb.shape[0] // sb
    kern = functools.partial(_flash_kernel, tq=tq, tk=tk, nq=nq, wa_slabs=na, wb_slabs=nb)
    step = lambda b, h, i: (b * ng + h) * nq + i
    wa_map = lambda b, h, i: (jnp.minimum(step(b, h, i), na - 1), 0)
    wb_map = lambda b, h, i: (jnp.minimum(step(b, h, i), nb - 1), 0)
    return pl.pallas_call(
        kern,
        grid=(bsz, ng, nq),
        in_specs=[pl.BlockSpec((1, hp, MLA_QK, tq), lambda b, h, i: (b, h, 0, i)),
                  pl.BlockSpec((1, hp, seq, MLA_QK), lambda b, h, i: (b, h, 0, 0)),
                  pl.BlockSpec((1, hp, seq // tk, MLA_VA, tk), lambda b, h, i: (b, h, 0, 0, 0)),
                  pl.BlockSpec((hp, MLA_V, 1), lambda b, h, i: (h, 0, 0)),
                  pl.BlockSpec((sa, wa.shape[1]), wa_map),
                  pl.BlockSpec((sb, wb.shape[1]), wb_map)],
        out_specs=[pl.BlockSpec((tq, hp * MLA_V), lambda b, h, i: (b * nq + i, h)),
                   pl.BlockSpec((sa, wa.shape[1]), wa_map),
                   pl.BlockSpec((sb, wb.shape[1]), wb_map)],
        out_shape=[jax.ShapeDtypeStruct((bsz * seq, nh * MLA_V), BF16),
                   jax.ShapeDtypeStruct(wa.shape, BF16),
                   jax.ShapeDtypeStruct(wb.shape, BF16)],
        scratch_shapes=[pltpu.VMEM((hp, 1, tq), F32),
                        pltpu.VMEM((hp, MLA_VA, tq), F32),
                        pltpu.VMEM((tk, tq), F32)],
        compiler_params=_params(3, 48),
        name="mla_attention",
    )(qt, k, vt, out_g.reshape(nh, MLA_V, 1), wa, wb)


def _outproj_kernel(hm_ref, ha_ref, w_ref, x_ref, mod_ref, g_ref, x1_ref, h2_ref, *, sub):
    km = hm_ref.shape[1]
    for s in range(hm_ref.shape[0] // sub):
        rows = pl.ds(s * sub, sub)
        mix = (jnp.dot(hm_ref[rows, :], w_ref[0:km, :], preferred_element_type=F32)
               + jnp.dot(ha_ref[rows, :], w_ref[km:, :], preferred_element_type=F32))
        x1 = x_ref[rows, :] + mod_ref[0, 2:3, :] * mix
        x1_ref[rows, :] = x1
        y = _rms(x1) * g_ref[...]
        h2_ref[rows, :] = (y * (1.0 + mod_ref[0, 4:5, :]) + mod_ref[0, 3:4, :]).astype(BF16)


def _out_projection(hm, ha, w_out, x2, mod, g, seq):
    t, d = x2.shape
    tm = OUTPROJ_TM
    per_b = seq // tm
    return pl.pallas_call(
        functools.partial(_outproj_kernel, sub=OUTPROJ_SUB),
        grid=(t // tm,),
        in_specs=[pl.BlockSpec((tm, hm.shape[1]), lambda i: (i, 0)),
                  pl.BlockSpec((tm, ha.shape[1]), lambda i: (i, 0)),
                  pl.BlockSpec(w_out.shape, lambda i: (0, 0), pipeline_mode=pl.Buffered(1)),
                  pl.BlockSpec((tm, d), lambda i: (i, 0)),
                  pl.BlockSpec((1, 6, d), lambda i: (i // per_b, 0, 0)),
                  pl.BlockSpec((1, d), lambda i: (0, 0))],
        out_specs=[pl.BlockSpec((tm, d), lambda i: (i, 0)),
                   pl.BlockSpec((tm, d), lambda i: (i, 0))],
        out_shape=[jax.ShapeDtypeStruct((t, d), F32),
                   jax.ShapeDtypeStruct((t, d), BF16)],
        compiler_params=_params(1, 48),
        name="out_proj",
    )(hm, ha, w_out, x2, mod, g.reshape(1, d))


def _ffn_up_kernel(h_ref, wv_ref, wg_ref, cwv_ref, cwg_ref, cbv_ref, cbg_ref, o_ref,
                   xv_ref, xg_ref, wvb_ref, wgb_ref, *, tm, blocks_per_seq):
    i = pl.program_id(1)

    @pl.when(i % blocks_per_seq == 0)
    def _():
        xv_ref[0:8, :] = jnp.zeros((8, xv_ref.shape[1]), F32)
        xg_ref[0:8, :] = jnp.zeros((8, xg_ref.shape[1]), F32)

    @pl.when(i == 0)
    def _():
        wvb_ref[...] = wv_ref[...].astype(BF16)
        wgb_ref[...] = wg_ref[...].astype(BF16)

    h = h_ref[...]

    def conv(w_ref, cw_ref, cb_ref, xs_ref):
        kw = cw_ref.shape[0]
        up = jnp.dot(h, w_ref[...], preferred_element_type=F32)
        xs_ref[8:8 + tm, :] = up
        y = cb_ref[...] + cw_ref[kw - 1:kw, :] * up
        for j in range(kw - 1):
            y = y + cw_ref[j:j + 1, :] * xs_ref[pl.ds(8 - (kw - 1) + j, tm), :]
        xs_ref[0:8, :] = xs_ref[tm:tm + 8, :]
        return y

    val = conv(wvb_ref, cwv_ref, cbv_ref, xv_ref)
    gate = conv(wgb_ref, cwg_ref, cbg_ref, xg_ref)
    o_ref[...] = (_silu(gate) * val).astype(BF16)


def _ffn_up(h2, w_up, conv_w, conv_b, seq):
    t, d = h2.shape
    f = w_up.shape[1] // 2
    tm, tf = FFN_UP_TM, FFN_UP_TF
    nf = f // tf
    kw = conv_w.shape[0]
    kern = functools.partial(_ffn_up_kernel, tm=tm, blocks_per_seq=seq // tm)
    cb = conv_b.reshape(1, 2 * f)
    return pl.pallas_call(
        kern,
        grid=(nf, t // tm),
        in_specs=[pl.BlockSpec((tm, d), lambda j, i: (i, 0)),
                  pl.BlockSpec((d, tf), lambda j, i: (0, j)),
                  pl.BlockSpec((d, tf), lambda j, i: (0, nf + j)),
                  pl.BlockSpec((kw, tf), lambda j, i: (0, j)),
                  pl.BlockSpec((kw, tf), lambda j, i: (0, nf + j)),
                  pl.BlockSpec((1, tf), lambda j, i: (0, j)),
                  pl.BlockSpec((1, tf), lambda j, i: (0, nf + j))],
        out_specs=pl.BlockSpec((tm, tf), lambda j, i: (i, j)),
        out_shape=jax.ShapeDtypeStruct((t, f), BF16),
        scratch_shapes=[pltpu.VMEM((tm + 8, tf), F32),
                        pltpu.VMEM((tm + 8, tf), F32),
                        pltpu.VMEM((d, tf), BF16),
                        pltpu.VMEM((d, tf), BF16)],
        compiler_params=_params(2, 56),
        name="ffn_up",
    )(h2, w_up, w_up, conv_w, conv_w, cb, cb)


def _ffn_down_kernel(a_ref, w_ref, x1_ref, mod_ref, g_ref, o_ref, *, sub):
    for s in range(a_ref.shape[0] // sub):
        rows = pl.ds(s * sub, sub)
        y = jnp.dot(a_ref[rows, :], w_ref[...], preferred_element_type=F32)
        x2 = x1_ref[rows, :] + mod_ref[0, 5:6, :] * y
        o_ref[rows, :] = _rms(x2) * g_ref[...]


def _ffn_down(act, w_down, x1, mod, g, seq):
    t, f = act.shape
    d = w_down.shape[1]
    tm = FFN_DOWN_TM
    per_b = seq // tm
    return pl.pallas_call(
        functools.partial(_ffn_down_kernel, sub=FFN_DOWN_SUB),
        grid=(t // tm,),
        in_specs=[pl.BlockSpec((tm, f), lambda i: (i, 0)),
                  pl.BlockSpec((f, d), lambda i: (0, 0), pipeline_mode=pl.Buffered(1)),
                  pl.BlockSpec((tm, d), lambda i: (i, 0)),
                  pl.BlockSpec((1, 6, d), lambda i: (i // per_b, 0, 0)),
                  pl.BlockSpec((1, d), lambda i: (0, 0))],
        out_specs=pl.BlockSpec((tm, d), lambda i: (i, 0)),
        out_shape=jax.ShapeDtypeStruct((t, d), F32),
        compiler_params=_params(1, 58),
        name="ffn_down",
    )(act, w_down, x1, mod, g.reshape(1, d))


def _rot_cols(w):
    half = w.shape[-1] // 2
    return jnp.concatenate([-w[..., half:], w[..., :half]], axis=-1)


def _win_prep_kernel(w_ref, tail_ref, o_ref, *, n_main):
    j = pl.program_id(0)
    gates = 2 * MLSTM_HEADS
    half = MLA_ROPE // 2

    @pl.when(j < n_main)
    def _():
        o_ref[...] = w_ref[...].T.astype(BF16)

    @pl.when(j == n_main)
    def _():
        o_ref[...] = tail_ref[gates:gates + Q_LORA, :].T.astype(BF16)

    @pl.when(j == n_main + 1)
    def _():
        r0 = gates + Q_LORA
        kr0 = r0 + KV_LORA
        blk = o_ref.shape[1]
        used = KV_LORA + 2 * MLA_ROPE + gates
        o_ref[...] = jnp.concatenate(
            [tail_ref[r0:kr0, :],
             tail_ref[kr0:kr0 + MLA_ROPE, :],
             -tail_ref[kr0 + half:kr0 + MLA_ROPE, :],
             tail_ref[kr0:kr0 + half, :],
             tail_ref[0:gates, :],
             jnp.zeros((blk - used, o_ref.shape[0]), F32)], axis=0).T.astype(BF16)


def _prep_w_in(w_in):
    d = w_in.shape[0]
    wt = w_in.T
    n_head = 3 * MLSTM_W
    blk = Q_LORA
    n_main = n_head // blk
    return pl.pallas_call(
        functools.partial(_win_prep_kernel, n_main=n_main),
        grid=(IN_PAD // blk,),
        in_specs=[pl.BlockSpec((blk, d), lambda j: (jnp.minimum(j, n_main - 1), 0)),
                  pl.BlockSpec((IN_PAD - n_head, d), lambda j: (n_head // (IN_PAD - n_head), 0))],
        out_specs=pl.BlockSpec((d, blk), lambda j: (0, j)),
        out_shape=jax.ShapeDtypeStruct((d, IN_PAD), BF16),
        compiler_params=_params(1, 40),
        name="w_in_prep",
    )(wt, wt)


def kernel(x, c, positions, ada_w, ada_b, attn_norm_g, w_in, mlstm_conv_w, mlstm_conv_b, mlstm_wq, mlstm_wk, mlstm_igate_b, mlstm_fgate_b, mla_q_norm_g, mla_w_uq, mla_kv_norm_g, mla_w_ukv, mlstm_out_g, mla_out_g, w_out, ffn_norm_g, ffn_w_up, ffn_conv_w, ffn_conv_b, ffn_w_down, final_norm_g):
    bsz, seq, d = x.shape
    t = bsz * seq
    depth = ada_w.shape[0]
    assert depth == 1, "the final RMSNorm is fused into the single layer's down-projection"
    xr = x.reshape(t, d)
    pos = positions.reshape(t // QKV_TM, 1, QKV_TM)
    half = MLA_ROPE // 2
    freqs = (ROPE_THETA ** (-jnp.arange(half, dtype=F32) / half)).reshape(half, 1)

    for l in range(depth):
        mod = _modulation(c, ada_w[l], ada_b[l])

        proj, misc = _in_projection(xr, mod, attn_norm_g[l], _prep_w_in(w_in[l]), seq)

        wq_m = mlstm_wq[l].astype(BF16)
        wkt_m = mlstm_wk[l].transpose(0, 2, 1).astype(BF16)
        gate_b = jnp.concatenate(
            [mlstm_igate_b[l], mlstm_fgate_b[l],
             jnp.zeros((LANES - 2 * MLSTM_HEADS,), F32)]).reshape(1, LANES)
        hm = _mlstm(proj, misc, mlstm_conv_w[l], mlstm_conv_b[l], wq_m, wkt_m, gate_b,
                    mlstm_out_g[l], bsz, seq)

        wq = mla_w_uq[l].reshape(Q_LORA, MLA_HEADS, MLA_QK)
        wq_r = wq[..., MLA_NOPE:]
        wq_h = jnp.concatenate([wq[..., :MLA_NOPE], wq_r, _rot_cols(wq_r)], axis=-1)
        wq_h = wq_h.transpose(1, 0, 2).astype(BF16)
        wkv_h = mla_w_ukv[l].reshape(KV_LORA, MLA_HEADS, MLA_NOPE + MLA_V)
        wkv_h = wkv_h.transpose(1, 0, 2).astype(BF16)
        qt, k, vt = _mla_qkv(proj, misc, pos, freqs, mla_q_norm_g[l], mla_kv_norm_g[l],
                             wq_h, wkv_h, bsz, seq)
        ha, w_down_b, w_out_b = _mla_attention(qt, k, vt, mla_out_g[l], ffn_w_down[l], w_out[l])

        x1, h2 = _out_projection(hm, ha, w_out_b, xr, mod, ffn_norm_g[l], seq)

        act = _ffn_up(h2, ffn_w_up[l], ffn_conv_w[l], ffn_conv_b[l], seq)
        xr = _ffn_down(act, w_down_b, x1, mod, final_norm_g, seq)
    return xr.reshape(bsz, seq, d)
```

```python
import functools

import jax
import jax.numpy as jnp
import numpy as np
from jax import lax
from jax.experimental import pallas as pl
from jax.experimental.pallas import tpu as pltpu

F32 = jnp.float32
BF16 = jnp.bfloat16

EPS = 1e-6
ROPE_THETA = 10000.0
MLSTM_HEADS = 4
MLSTM_DH = 256
MLSTM_W = MLSTM_HEADS * MLSTM_DH
MLA_HEADS = 8
MLA_NOPE = 128
MLA_ROPE = 64
MLA_V = 128
MLA_QK = MLA_NOPE + MLA_ROPE
Q_LORA = 512
KV_LORA = 256
LANES = 128
MIB = 1024 * 1024

IN_PAD = 4096
IN_BLOCK = 1024
MISC_W = 256
MISC_OFF = IN_BLOCK - MISC_W

MOD_TN = 1024
INPROJ_TM, INPROJ_SUB = 512, 256
MLSTM_TS, MLSTM_CHUNK = 512, 512
QKV_TM = 1024
ATT_TQ = 1024
ATT_TK = 256
ATT_HP = 2
OUTPROJ_TM, OUTPROJ_SUB = 512, 512
FFN_UP_TM, FFN_UP_TF = 1024, 512
FFN_DOWN_TM, FFN_DOWN_SUB = 512, 512


def _params(n_axes, vmem_mib):
    return pltpu.CompilerParams(
        dimension_semantics=("arbitrary",) * n_axes,
        vmem_limit_bytes=vmem_mib * MIB)


def _rms(x):
    return x * lax.rsqrt(jnp.mean(x * x, axis=-1, keepdims=True) + EPS)


def _silu(x):
    return x / (1.0 + jnp.exp(-x))


def _log_sigmoid(x):
    return jnp.minimum(x, 0.0) - jnp.log1p(jnp.exp(-jnp.abs(x)))


def _mod_kernel(c_ref, w_ref, b_ref, o_ref):
    ca = _silu(c_ref[...]).astype(BF16)
    o_ref[...] = jnp.dot(ca, w_ref[...].astype(BF16),
                         preferred_element_type=F32) + b_ref[...]


def _modulation(c, ada_w, ada_b):
    bsz, d = c.shape
    n = ada_w.shape[1]
    tn = MOD_TN
    cp = jnp.pad(c, ((0, 8 - bsz), (0, 0)))
    out = pl.pallas_call(
        _mod_kernel,
        grid=(n // tn,),
        in_specs=[pl.BlockSpec((8, d), lambda j: (0, 0)),
                  pl.BlockSpec((d, tn), lambda j: (0, j)),
                  pl.BlockSpec((1, tn), lambda j: (0, j))],
        out_specs=pl.BlockSpec((8, tn), lambda j: (0, j)),
        out_shape=jax.ShapeDtypeStruct((8, n), F32),
        compiler_params=_params(1, 40),
        name="adaln_mod",
    )(cp, ada_w, ada_b.reshape(1, n))
    return out[:bsz].reshape(bsz, 6, d)


def _inproj_kernel(x_ref, mod_ref, g_ref, w_ref, o_ref, misc_ref, *, sub):
    tm = x_ref.shape[0]
    nblk = w_ref.shape[1] // IN_BLOCK
    for s in range(tm // sub):
        rows = pl.ds(s * sub, sub)
        y = _rms(x_ref[rows, :]) * g_ref[...]
        h = (y * (1.0 + mod_ref[0, 1:2, :]) + mod_ref[0, 0:1, :]).astype(BF16)
        for j in range(nblk):
            cols = slice(j * IN_BLOCK, (j + 1) * IN_BLOCK)
            acc = jnp.dot(h, w_ref[:, cols], preferred_element_type=F32)
            o_ref[rows, cols] = acc.astype(BF16)
            if j == nblk - 1:
                misc_ref[rows, :] = acc[:, MISC_OFF:]


def _in_projection(x2, mod, g, w_in_p, seq):
    t, d = x2.shape
    tm = INPROJ_TM
    per_b = seq // tm
    return pl.pallas_call(
        functools.partial(_inproj_kernel, sub=INPROJ_SUB),
        grid=(t // tm,),
        in_specs=[pl.BlockSpec((tm, d), lambda i: (i, 0)),
                  pl.BlockSpec((1, 6, d), lambda i: (i // per_b, 0, 0)),
                  pl.BlockSpec((1, d), lambda i: (0, 0)),
                  pl.BlockSpec((d, IN_PAD), lambda i: (0, 0), pipeline_mode=pl.Buffered(1))],
        out_specs=[pl.BlockSpec((tm, IN_PAD), lambda i: (i, 0)),
                   pl.BlockSpec((tm, MISC_W), lambda i: (i, 0))],
        out_shape=[jax.ShapeDtypeStruct((t, IN_PAD), BF16),
                   jax.ShapeDtypeStruct((t, MISC_W), F32)],
        compiler_params=_params(1, 48),
        name="norm_inproj",
    )(x2, mod, g.reshape(1, d), w_in_p)


def _cumsum_rows(x):
    n = x.shape[0]
    row = lax.broadcasted_iota(jnp.int32, x.shape, 0)
    sh = 1
    while sh < n:
        x = x + jnp.where(row >= sh, pltpu.roll(x, sh, axis=0), 0.0)
        sh *= 2
    return x


def _cummax_rows(x):
    n = x.shape[0]
    row = lax.broadcasted_iota(jnp.int32, x.shape, 0)
    sh = 1
    while sh < n:
        x = jnp.maximum(x, jnp.where(row >= sh, pltpu.roll(x, sh, axis=0), -jnp.inf))
        sh *= 2
    return x


def _cumsum_lanes(x):
    n = x.shape[1]
    col = lax.broadcasted_iota(jnp.int32, x.shape, 1)
    sh = 1
    while sh < n:
        x = x + jnp.where(col >= sh, pltpu.roll(x, sh, axis=1), 0.0)
        sh *= 2
    return x


def _mlstm_kernel(u_ref, v_ref, o_ref, gt_ref, cw_ref, cb_ref, wq_ref, wkt_ref, gb_ref, og_ref,
                  out_ref, xs_ref, ct_ref, n_ref, m_ref, *, ts, chunk):
    i = pl.program_id(0)
    nbatch = u_ref.shape[0]

    @pl.when(i == 0)
    def _():
        for b in range(nbatch):
            xs_ref[b, 0:8, :] = jnp.zeros((8, MLSTM_W), F32)
        ct_ref[...] = jnp.zeros(ct_ref.shape, F32)
        n_ref[...] = jnp.zeros(n_ref.shape, F32)
        m_ref[...] = jnp.zeros(m_ref.shape, F32)

    row = lax.broadcasted_iota(jnp.int32, (chunk, chunk), 0)
    col = lax.broadcasted_iota(jnp.int32, (chunk, chunk), 1)
    causal = col <= row

    for c in range(ts // chunk):
        for b in range(nbatch):
            _mlstm_chunk(u_ref.at[b], v_ref.at[b], o_ref.at[b], gt_ref.at[b], cw_ref, cb_ref,
                         wq_ref, wkt_ref, gb_ref, og_ref, out_ref.at[b], xs_ref.at[b],
                         ct_ref.at[b], n_ref.at[b], m_ref.at[b], c, chunk, causal)
    for b in range(nbatch):
        xs_ref[b, 0:8, :] = xs_ref[b, ts:ts + 8, :]


def _mlstm_chunk(u_ref, v_ref, o_ref, gt_ref, cw_ref, cb_ref, wq_ref, wkt_ref, gb_ref, og_ref,
                 out_ref, xs_ref, ct_ref, n_ref, m_ref, c, chunk, causal):
    nh, dh = MLSTM_HEADS, MLSTM_DH
    gb = gb_ref[...]
    kw = cw_ref.shape[0]
    ones = jnp.ones((chunk, LANES), BF16)
    r0 = c * chunk
    xc = u_ref[r0:r0 + chunk, :].astype(F32)
    xs_ref[8 + r0:8 + r0 + chunk, :] = xc
    acc = cb_ref[...] + cw_ref[kw - 1:kw, :] * xc
    for j in range(kw - 1):
        acc = acc + cw_ref[j:j + 1, :] * xs_ref[pl.ds(8 + r0 - (kw - 1) + j, chunk), :]
    su = _silu(acc)

    gc = gt_ref[r0:r0 + chunk, MISC_W - LANES:] + gb
    bc = _cumsum_rows(_log_sigmoid(gc))
    gt = gc.T[0:8, :]
    bt = _cumsum_lanes(_log_sigmoid(gt))
    mdc = _cummax_rows(gc - pltpu.roll(bc, LANES - nh, axis=1))
    for h in range(nh):
        c0 = h * dh
        b_c = bc[:, nh + h:nh + h + 1]
        li_r = gt[h:h + 1, :]
        b_r = bt[nh + h:nh + h + 1, :]
        g_tot = b_r[:, chunk - 1:chunk]
        z_r = li_r - b_r
        zmax = jnp.max(z_r, axis=-1, keepdims=True)
        m_prev = m_ref[h:h + 1, 0:1]

        ub = su[:, c0:c0 + dh].astype(BF16)
        qb = jnp.dot(ub, wq_ref[h], preferred_element_type=F32).astype(BF16)
        kt = jnp.dot(wkt_ref[h], ub.T,
                     preferred_element_type=F32) * (dh ** -0.5)
        vc = v_ref[r0:r0 + chunk, c0:c0 + dh]

        mx_c = jnp.maximum(m_prev, mdc[:, h:h + 1])
        m_t = b_c + mx_c
        sb = (jnp.dot(qb, kt.astype(BF16), preferred_element_type=F32)
              * jnp.exp(jnp.where(causal, z_r - mx_c, -jnp.inf))).astype(BF16)
        inter = jnp.exp(m_prev - mx_c)
        ct = ct_ref[h]
        nrep = n_ref[h]
        num = (jnp.dot(sb, vc, preferred_element_type=F32)
               + inter * jnp.dot(qb, ct.astype(BF16), preferred_element_type=F32))
        den = (jnp.dot(sb, ones, preferred_element_type=F32)
               + inter * jnp.dot(qb, nrep.astype(BF16), preferred_element_type=F32))
        lim = jnp.maximum(jnp.abs(den), jnp.exp(-m_t))
        hh = num / jnp.concatenate([lim] * (dh // LANES), axis=1)

        mm = jnp.maximum(m_prev, zmax)
        s_prev = jnp.exp(m_prev - mm)
        s_loc = jnp.exp(zmax - mm)
        kwt = (kt * (jnp.exp(z_r - zmax) * s_loc)).astype(BF16)
        ct_ref[h] = s_prev * ct + jnp.dot(kwt, vc, preferred_element_type=F32)
        n_ref[h] = s_prev * nrep + jnp.dot(kwt, ones, preferred_element_type=F32)
        m_ref[h:h + 1, :] = jnp.broadcast_to(g_tot + mm, (1, LANES))

        og = o_ref[r0:r0 + chunk, c0:c0 + dh].astype(F32)
        hh = hh / (1.0 + jnp.exp(-og))
        out_ref[r0:r0 + chunk, c0:c0 + dh] = (
            _rms(hh) * og_ref[:, c0:c0 + dh]).astype(BF16)


def _mlstm(proj, misc, conv_w, conv_b, wq, wkt, gate_b, out_g, bsz, seq):
    ts, chunk = MLSTM_TS, MLSTM_CHUNK
    t = bsz * seq
    kern = functools.partial(_mlstm_kernel, ts=ts, chunk=chunk)
    proj3 = proj.reshape(bsz, seq, proj.shape[1])
    misc3 = misc.reshape(bsz, seq, MISC_W)
    out = pl.pallas_call(
        kern,
        grid=(seq // ts,),
        in_specs=[pl.BlockSpec((bsz, ts, MLSTM_W), lambda i: (0, i, 0)),
                  pl.BlockSpec((bsz, ts, MLSTM_W), lambda i: (0, i, 1)),
                  pl.BlockSpec((bsz, ts, MLSTM_W), lambda i: (0, i, 2)),
                  pl.BlockSpec((bsz, ts, MISC_W), lambda i: (0, i, 0)),
                  pl.BlockSpec(conv_w.shape, lambda i: (0, 0)),
                  pl.BlockSpec((1, MLSTM_W), lambda i: (0, 0)),
                  pl.BlockSpec(wq.shape, lambda i: (0, 0, 0)),
                  pl.BlockSpec(wkt.shape, lambda i: (0, 0, 0)),
                  pl.BlockSpec((1, LANES), lambda i: (0, 0)),
                  pl.BlockSpec((1, MLSTM_W), lambda i: (0, 0))],
        out_specs=pl.BlockSpec((bsz, ts, MLSTM_W), lambda i: (0, i, 0)),
        out_shape=jax.ShapeDtypeStruct((bsz, seq, MLSTM_W), BF16),
        scratch_shapes=[pltpu.VMEM((bsz, ts + 8, MLSTM_W), F32),
                        pltpu.VMEM((bsz, MLSTM_HEADS, MLSTM_DH, MLSTM_DH), F32),
                        pltpu.VMEM((bsz, MLSTM_HEADS, MLSTM_DH, LANES), F32),
                        pltpu.VMEM((bsz, 8, LANES), F32)],
        compiler_params=_params(1, 48),
        name="mlstm",
    )(proj3, proj3, proj3, misc3, conv_w, conv_b.reshape(1, MLSTM_W), wq, wkt, gate_b,
      out_g.reshape(1, MLSTM_W))
    return out.reshape(t, MLSTM_W)


MLA_VA = MLA_V + 16


def _qkv_kernel(cq_ref, ckv_ref, misc_ref, pos_ref, fr_ref, qg_ref, kvg_ref, wq_ref, wkv_ref,
                qt_ref, k_ref, vt_ref, cqn, ckvn, cos_s, sin_s, kr_s, *, scale, tk):
    cqn[...] = (_rms(cq_ref[...].astype(F32)) * qg_ref[...]).astype(BF16)
    ckvn[...] = (_rms(ckv_ref[...].astype(F32)) * kvg_ref[...]).astype(BF16)
    ang = fr_ref[...] * pos_ref[0].astype(F32)
    reps = LANES // ang.shape[0]
    cs = jnp.concatenate([jnp.cos(ang)] * reps, axis=0).T
    sn = jnp.concatenate([jnp.sin(ang)] * reps, axis=0).T
    cos_s[...] = cs
    sin_s[...] = sn
    y = misc_ref[:, 0:LANES]
    kr_s[...] = y * cs + pltpu.roll(y, MLA_ROPE, axis=1) * sn

    for hh in range(wq_ref.shape[0]):
        mq = jnp.dot(cqn[...], wq_ref[hh], preferred_element_type=F32)
        qt_ref[0, hh, 0:MLA_NOPE, :] = (mq[:, 0:MLA_NOPE] * scale).astype(BF16).T
        y = mq[:, MLA_NOPE:]
        r = (y * cos_s[...] + pltpu.roll(y, MLA_ROPE, axis=1) * sin_s[...]) * scale
        qt_ref[0, hh, MLA_NOPE:MLA_QK, :] = r.astype(BF16).T[0:MLA_ROPE, :]

        mkv = jnp.dot(ckvn[...], wkv_ref[hh], preferred_element_type=F32)
        k_ref[0, hh, :, 0:MLA_NOPE] = mkv[:, 0:MLA_NOPE].astype(BF16)
        k_ref[0, hh, :, MLA_NOPE:MLA_QK] = kr_s[:, 0:MLA_ROPE].astype(BF16)
        vt = mkv[:, MLA_NOPE:].astype(BF16).T
        for j in range(vt.shape[1] // tk):
            vt_ref[0, hh, j, 0:MLA_V, :] = vt[:, j * tk:(j + 1) * tk]
            vt_ref[0, hh, j, MLA_V:, :] = jnp.ones((MLA_VA - MLA_V, tk), BF16)


def _mla_qkv(proj, misc, pos, freqs, q_g, kv_g, wq_h, wkv_h, bsz, seq):
    tm, tk = QKV_TM, ATT_TK
    hps = MLA_HEADS
    t = bsz * seq
    nsb = seq // tm
    cq_blk = (3 * MLSTM_W) // Q_LORA
    ckv_blk = (3 * MLSTM_W + Q_LORA) // KV_LORA
    kern = functools.partial(_qkv_kernel, scale=MLA_QK ** -0.5 * float(np.log2(np.e)), tk=tk)
    return pl.pallas_call(
        kern,
        grid=(t // tm, MLA_HEADS // hps),
        in_specs=[pl.BlockSpec((tm, Q_LORA), lambda i, h: (i, cq_blk)),
                  pl.BlockSpec((tm, KV_LORA), lambda i, h: (i, ckv_blk)),
                  pl.BlockSpec((tm, MISC_W), lambda i, h: (i, 0)),
                  pl.BlockSpec((1, 1, tm), lambda i, h: (i, 0, 0)),
                  pl.BlockSpec(freqs.shape, lambda i, h: (0, 0)),
                  pl.BlockSpec((1, Q_LORA), lambda i, h: (0, 0)),
                  pl.BlockSpec((1, KV_LORA), lambda i, h: (0, 0)),
                  pl.BlockSpec((hps, Q_LORA, 2 * LANES), lambda i, h: (h, 0, 0)),
                  pl.BlockSpec((hps, KV_LORA, 2 * LANES), lambda i, h: (h, 0, 0))],
        out_specs=[pl.BlockSpec((1, hps, MLA_QK, tm), lambda i, h: (i // nsb, h, 0, i % nsb)),
                   pl.BlockSpec((1, hps, tm, MLA_QK), lambda i, h: (i // nsb, h, i % nsb, 0)),
                   pl.BlockSpec((1, hps, tm // tk, MLA_VA, tk),
                                lambda i, h: (i // nsb, h, i % nsb, 0, 0))],
        out_shape=[jax.ShapeDtypeStruct((bsz, MLA_HEADS, MLA_QK, seq), BF16),
                   jax.ShapeDtypeStruct((bsz, MLA_HEADS, seq, MLA_QK), BF16),
                   jax.ShapeDtypeStruct((bsz, MLA_HEADS, seq // tk, MLA_VA, tk), BF16)],
        scratch_shapes=[pltpu.VMEM((tm, Q_LORA), BF16),
                        pltpu.VMEM((tm, KV_LORA), BF16),
                        pltpu.VMEM((tm, LANES), F32),
                        pltpu.VMEM((tm, LANES), F32),
                        pltpu.VMEM((tm, LANES), F32)],
        compiler_params=_params(2, 40),
        name="mla_qkv",
    )(proj, proj, misc, pos, freqs, q_g.reshape(1, Q_LORA), kv_g.reshape(1, KV_LORA),
      wq_h, wkv_h)


def _flash_kernel(qt_ref, k_ref, vt_ref, g_ref, wa_ref, wb_ref, o_ref, wab_ref, wbb_ref,
                  m_ref, acc_ref, bias_ref, *, tq, tk, nq, wa_slabs, wb_slabs):
    qi = pl.program_id(2)

    step = (pl.program_id(0) * pl.num_programs(1) + pl.program_id(1)) * nq + qi

    @pl.when(step < wa_slabs)
    def _():
        wab_ref[...] = wa_ref[...].astype(BF16)

    @pl.when(step < wb_slabs)
    def _():
        wbb_ref[...] = wb_ref[...].astype(BF16)

    @pl.when((pl.program_id(0) == 0) & (pl.program_id(1) == 0) & (qi == 0))
    def _():
        key = lax.broadcasted_iota(jnp.int32, (tk, tq), 0)
        qry = lax.broadcasted_iota(jnp.int32, (tk, tq), 1)
        bias_ref[...] = jnp.where(key <= qry, 0.0, -jnp.inf).astype(F32)

    hp = qt_ref.shape[1]

    def scores(hh, kj, c0, masked):
        k = k_ref[0, hh, kj * tk:(kj + 1) * tk, :]
        s = jnp.dot(k, qt_ref[0, hh, :, c0:], preferred_element_type=F32)
        if masked:
            s = s + bias_ref[:, 0:tq - c0]
        return s

    def update(hh, s, kj, c0):
        if kj == 0:
            assert c0 == 0
            m_new = jnp.max(s, axis=0, keepdims=True)
            acc_ref[hh] = jnp.dot(vt_ref[0, hh, kj], jnp.exp2(s - m_new).astype(BF16),
                                  preferred_element_type=F32)
        else:
            m_prev = m_ref[hh, :, c0:]
            m_new = jnp.maximum(m_prev, jnp.max(s, axis=0, keepdims=True))
            alpha = jnp.exp2(m_prev - m_new)
            p = jnp.exp2(s - m_new)
            acc_ref[hh, :, c0:] = alpha * acc_ref[hh, :, c0:] + jnp.dot(
                vt_ref[0, hh, kj], p.astype(BF16), preferred_element_type=F32)
        m_ref[hh, :, c0:] = m_new

    per_q = tq // tk

    def run(q):
        def place(b):
            d = b - q * per_q
            return (0, False) if d < 0 else (d * tk, True)

        nblk = (q + 1) * per_q
        s = [scores(hh, 0, *place(0)) for hh in range(hp)]
        for b in range(nblk):
            s_next = ([scores(hh, b + 1, *place(b + 1)) for hh in range(hp)]
                      if b + 1 < nblk else None)
            for hh in range(hp):
                update(hh, s[hh], b, place(b)[0])
            s = s_next

    for q in range(nq):
        pl.when(qi == q)(functools.partial(run, q))

    for hh in range(hp):
        o = acc_ref[hh, 0:MLA_V, :] / acc_ref[hh, MLA_V:MLA_V + 1, :]
        o = o * lax.rsqrt(jnp.mean(o * o, axis=0, keepdims=True) + EPS) * g_ref[hh]
        o_ref[:, hh * MLA_V:(hh + 1) * MLA_V] = o.astype(BF16).T


def _cast_slab_rows(rows, n_steps):
    bf16_rows = 16
    for slab in range(bf16_rows, rows + 1, bf16_rows):
        if rows % slab == 0 and rows // slab <= n_steps:
            return slab
    raise ValueError("no slab size fits")


def _mla_attention(qt, k, vt, out_g, wa, wb):
    bsz, nh, seq, _ = k.shape
    tq, tk = ATT_TQ, ATT_TK
    nq = seq // tq
    hp = ATT_HP
    ng = nh // hp
    n_steps = bsz * ng * nq
    sa = _cast_slab_rows(wa.shape[0], n_steps)
    sb = _cast_slab_rows(wb.shape[0], n_steps)
    na, nb = wa.shape[0] // sa, wb.shape[0] // sb
    kern = functools.partial(_flash_kernel, tq=tq, tk=tk, nq=nq, wa_slabs=na, wb_slabs=nb)
    step = lambda b, h, i: (b * ng + h) * nq + i
    wa_map = lambda b, h, i: (jnp.minimum(step(b, h, i), na - 1), 0)
    wb_map = lambda b, h, i: (jnp.minimum(step(b, h, i), nb - 1), 0)
    return pl.pallas_call(
        kern,
        grid=(bsz, ng, nq),
        in_specs=[pl.BlockSpec((1, hp, MLA_QK, tq), lambda b, h, i: (b, h, 0, i)),
                  pl.BlockSpec((1, hp, seq, MLA_QK), lambda b, h, i: (b, h, 0, 0)),
                  pl.BlockSpec((1, hp, seq // tk, MLA_VA, tk), lambda b, h, i: (b, h, 0, 0, 0)),
                  pl.BlockSpec((hp, MLA_V, 1), lambda b, h, i: (h, 0, 0)),
                  pl.BlockSpec((sa, wa.shape[1]), wa_map),
                  pl.BlockSpec((sb, wb.shape[1]), wb_map)],
        out_specs=[pl.BlockSpec((tq, hp * MLA_V), lambda b, h, i: (b * nq + i, h)),
                   pl.BlockSpec((sa, wa.shape[1]), wa_map),
                   pl.BlockSpec((sb, wb.shape[1]), wb_map)],
        out_shape=[jax.ShapeDtypeStruct((bsz * seq, nh * MLA_V), BF16),
                   jax.ShapeDtypeStruct(wa.shape, BF16),
                   jax.ShapeDtypeStruct(wb.shape, BF16)],
        scratch_shapes=[pltpu.VMEM((hp, 1, tq), F32),
                        pltpu.VMEM((hp, MLA_VA, tq), F32),
                        pltpu.VMEM((tk, tq), F32)],
        compiler_params=_params(3, 48),
        name="mla_attention",
    )(qt, k, vt, out_g.reshape(nh, MLA_V, 1), wa, wb)


def _outproj_kernel(hm_ref, ha_ref, w_ref, x_ref, mod_ref, g_ref, x1_ref, h2_ref, *, sub):
    km = hm_ref.shape[1]
    for s in range(hm_ref.shape[0] // sub):
        rows = pl.ds(s * sub, sub)
        mix = (jnp.dot(hm_ref[rows, :], w_ref[0:km, :], preferred_element_type=F32)
               + jnp.dot(ha_ref[rows, :], w_ref[km:, :], preferred_element_type=F32))
        x1 = x_ref[rows, :] + mod_ref[0, 2:3, :] * mix
        x1_ref[rows, :] = x1
        y = _rms(x1) * g_ref[...]
        h2_ref[rows, :] = (y * (1.0 + mod_ref[0, 4:5, :]) + mod_ref[0, 3:4, :]).astype(BF16)


def _out_projection(hm, ha, w_out, x2, mod, g, seq):
    t, d = x2.shape
    tm = OUTPROJ_TM
    per_b = seq // tm
    return pl.pallas_call(
        functools.partial(_outproj_kernel, sub=OUTPROJ_SUB),
        grid=(t // tm,),
        in_specs=[pl.BlockSpec((tm, hm.shape[1]), lambda i: (i, 0)),
                  pl.BlockSpec((tm, ha.shape[1]), lambda i: (i, 0)),
                  pl.BlockSpec(w_out.shape, lambda i: (0, 0), pipeline_mode=pl.Buffered(1)),
                  pl.BlockSpec((tm, d), lambda i: (i, 0)),
                  pl.BlockSpec((1, 6, d), lambda i: (i // per_b, 0, 0)),
                  pl.BlockSpec((1, d), lambda i: (0, 0))],
        out_specs=[pl.BlockSpec((tm, d), lambda i: (i, 0)),
                   pl.BlockSpec((tm, d), lambda i: (i, 0))],
        out_shape=[jax.ShapeDtypeStruct((t, d), F32),
                   jax.ShapeDtypeStruct((t, d), BF16)],
        compiler_params=_params(1, 48),
        name="out_proj",
    )(hm, ha, w_out, x2, mod, g.reshape(1, d))


def _ffn_up_kernel(h_ref, wv_ref, wg_ref, cwv_ref, cwg_ref, cbv_ref, cbg_ref, o_ref,
                   xv_ref, xg_ref, wvb_ref, wgb_ref, *, tm, blocks_per_seq):
    i = pl.program_id(1)

    @pl.when(i % blocks_per_seq == 0)
    def _():
        xv_ref[0:8, :] = jnp.zeros((8, xv_ref.shape[1]), F32)
        xg_ref[0:8, :] = jnp.zeros((8, xg_ref.shape[1]), F32)

    @pl.when(i == 0)
    def _():
        wvb_ref[...] = wv_ref[...].astype(BF16)
        wgb_ref[...] = wg_ref[...].astype(BF16)

    h = h_ref[...]

    def conv(w_ref, cw_ref, cb_ref, xs_ref):
        kw = cw_ref.shape[0]
        up = jnp.dot(h, w_ref[...], preferred_element_type=F32)
        xs_ref[8:8 + tm, :] = up
        y = cb_ref[...] + cw_ref[kw - 1:kw, :] * up
        for j in range(kw - 1):
            y = y + cw_ref[j:j + 1, :] * xs_ref[pl.ds(8 - (kw - 1) + j, tm), :]
        xs_ref[0:8, :] = xs_ref[tm:tm + 8, :]
        return y

    val = conv(wvb_ref, cwv_ref, cbv_ref, xv_ref)
    gate = conv(wgb_ref, cwg_ref, cbg_ref, xg_ref)
    o_ref[...] = (_silu(gate) * val).astype(BF16)


def _ffn_up(h2, w_up, conv_w, conv_b, seq):
    t, d = h2.shape
    f = w_up.shape[1] // 2
    tm, tf = FFN_UP_TM, FFN_UP_TF
    nf = f // tf
    kw = conv_w.shape[0]
    kern = functools.partial(_ffn_up_kernel, tm=tm, blocks_per_seq=seq // tm)
    cb = conv_b.reshape(1, 2 * f)
    return pl.pallas_call(
        kern,
        grid=(nf, t // tm),
        in_specs=[pl.BlockSpec((tm, d), lambda j, i: (i, 0)),
                  pl.BlockSpec((d, tf), lambda j, i: (0, j)),
                  pl.BlockSpec((d, tf), lambda j, i: (0, nf + j)),
                  pl.BlockSpec((kw, tf), lambda j, i: (0, j)),
                  pl.BlockSpec((kw, tf), lambda j, i: (0, nf + j)),
                  pl.BlockSpec((1, tf), lambda j, i: (0, j)),
                  pl.BlockSpec((1, tf), lambda j, i: (0, nf + j))],
        out_specs=pl.BlockSpec((tm, tf), lambda j, i: (i, j)),
        out_shape=jax.ShapeDtypeStruct((t, f), BF16),
        scratch_shapes=[pltpu.VMEM((tm + 8, tf), F32),
                        pltpu.VMEM((tm + 8, tf), F32),
                        pltpu.VMEM((d, tf), BF16),
                        pltpu.VMEM((d, tf), BF16)],
        compiler_params=_params(2, 56),
        name="ffn_up",
    )(h2, w_up, w_up, conv_w, conv_w, cb, cb)


def _ffn_down_kernel(a_ref, w_ref, x1_ref, mod_ref, g_ref, o_ref, *, sub):
    for s in range(a_ref.shape[0] // sub):
        rows = pl.ds(s * sub, sub)
        y = jnp.dot(a_ref[rows, :], w_ref[...], preferred_element_type=F32)
        x2 = x1_ref[rows, :] + mod_ref[0, 5:6, :] * y
        o_ref[rows, :] = _rms(x2) * g_ref[...]


def _ffn_down(act, w_down, x1, mod, g, seq):
    t, f = act.shape
    d = w_down.shape[1]
    tm = FFN_DOWN_TM
    per_b = seq // tm
    return pl.pallas_call(
        functools.partial(_ffn_down_kernel, sub=FFN_DOWN_SUB),
        grid=(t // tm,),
        in_specs=[pl.BlockSpec((tm, f), lambda i: (i, 0)),
                  pl.BlockSpec((f, d), lambda i: (0, 0), pipeline_mode=pl.Buffered(1)),
                  pl.BlockSpec((tm, d), lambda i: (i, 0)),
                  pl.BlockSpec((1, 6, d), lambda i: (i // per_b, 0, 0)),
                  pl.BlockSpec((1, d), lambda i: (0, 0))],
        out_specs=pl.BlockSpec((tm, d), lambda i: (i, 0)),
        out_shape=jax.ShapeDtypeStruct((t, d), F32),
        compiler_params=_params(1, 58),
        name="ffn_down",
    )(act, w_down, x1, mod, g.reshape(1, d))


def _rot_cols(w):
    half = w.shape[-1] // 2
    return jnp.concatenate([-w[..., half:], w[..., :half]], axis=-1)


def _win_prep_kernel(w_ref, tail_ref, o_ref, *, n_main):
    j = pl.program_id(0)
    gates = 2 * MLSTM_HEADS
    half = MLA_ROPE // 2

    @pl.when(j < n_main)
    def _():
        o_ref[...] = w_ref[...].T.astype(BF16)

    @pl.when(j == n_main)
    def _():
        o_ref[...] = tail_ref[gates:gates + Q_LORA, :].T.astype(BF16)

    @pl.when(j == n_main + 1)
    def _():
        r0 = gates + Q_LORA
        kr0 = r0 + KV_LORA
        blk = o_ref.shape[1]
        used = KV_LORA + 2 * MLA_ROPE + gates
        o_ref[...] = jnp.concatenate(
            [tail_ref[r0:kr0, :],
             tail_ref[kr0:kr0 + MLA_ROPE, :],
             -tail_ref[kr0 + half:kr0 + MLA_ROPE, :],
             tail_ref[kr0:kr0 + half, :],
             tail_ref[0:gates, :],
             jnp.zeros((blk - used, o_ref.shape[0]), F32)], axis=0).T.astype(BF16)


def _prep_w_in(w_in):
    d = w_in.shape[0]
    wt = w_in.T
    n_head = 3 * MLSTM_W
    blk = Q_LORA
    n_main = n_head // blk
    return pl.pallas_call(
        functools.partial(_win_prep_kernel, n_main=n_main),
        grid=(IN_PAD // blk,),
        in_specs=[pl.BlockSpec((blk, d), lambda j: (jnp.minimum(j, n_main - 1), 0)),
                  pl.BlockSpec((IN_PAD - n_head, d), lambda j: (n_head // (IN_PAD - n_head), 0))],
        out_specs=pl.BlockSpec((d, blk), lambda j: (0, j)),
        out_shape=jax.ShapeDtypeStruct((d, IN_PAD), BF16),
        compiler_params=_params(1, 40),
        name="w_in_prep",
    )(wt, wt)


def kernel(x, c, positions, ada_w, ada_b, attn_norm_g, w_in, mlstm_conv_w, mlstm_conv_b, mlstm_wq, mlstm_wk, mlstm_igate_b, mlstm_fgate_b, mla_q_norm_g, mla_w_uq, mla_kv_norm_g, mla_w_ukv, mlstm_out_g, mla_out_g, w_out, ffn_norm_g, ffn_w_up, ffn_conv_w, ffn_conv_b, ffn_w_down, final_norm_g):
    bsz, seq, d = x.shape
    t = bsz * seq
    depth = ada_w.shape[0]
    assert depth == 1, "the final RMSNorm is fused into the single layer's down-projection"
    xr = x.reshape(t, d)
    pos = positions.reshape(t // QKV_TM, 1, QKV_TM)
    half = MLA_ROPE // 2
    freqs = (ROPE_THETA ** (-jnp.arange(half, dtype=F32) / half)).reshape(half, 1)

    for l in range(depth):
        mod = _modulation(c, ada_w[l], ada_b[l])

        proj, misc = _in_projection(xr, mod, attn_norm_g[l], _prep_w_in(w_in[l]), seq)

        wq_m = mlstm_wq[l].astype(BF16)
        wkt_m = mlstm_wk[l].transpose(0, 2, 1).astype(BF16)
        gate_b = jnp.concatenate(
            [mlstm_igate_b[l], mlstm_fgate_b[l],
             jnp.zeros((LANES - 2 * MLSTM_HEADS,), F32)]).reshape(1, LANES)
        hm = _mlstm(proj, misc, mlstm_conv_w[l], mlstm_conv_b[l], wq_m, wkt_m, gate_b,
                    mlstm_out_g[l], bsz, seq)

        wq = mla_w_uq[l].reshape(Q_LORA, MLA_HEADS, MLA_QK)
        wq_r = wq[..., MLA_NOPE:]
        wq_h = jnp.concatenate([wq[..., :MLA_NOPE], wq_r, _rot_cols(wq_r)], axis=-1)
        wq_h = wq_h.transpose(1, 0, 2).astype(BF16)
        wkv_h = mla_w_ukv[l].reshape(KV_LORA, MLA_HEADS, MLA_NOPE + MLA_V)
        wkv_h = wkv_h.transpose(1, 0, 2).astype(BF16)
        qt, k, vt = _mla_qkv(proj, misc, pos, freqs, mla_q_norm_g[l], mla_kv_norm_g[l],
                             wq_h, wkv_h, bsz, seq)
        ha, w_down_b, w_out_b = _mla_attention(qt, k, vt, mla_out_g[l], ffn_w_down[l], w_out[l])

        x1, h2 = _out_projection(hm, ha, w_out_b, xr, mod, ffn_norm_g[l], seq)

        act = _ffn_up(h2, ffn_w_up[l], ffn_conv_w[l], ffn_conv_b[l], seq)
        xr = _ffn_down(act, w_down_b, x1, mod, final_norm_g, seq)
    return xr.reshape(bsz, seq, d)
```

```python
import functools

import jax
import jax.numpy as jnp
import numpy as np
from jax import lax
from jax.experimental import pallas as pl
from jax.experimental.pallas import tpu as pltpu

F32 = jnp.float32
BF16 = jnp.bfloat16

EPS = 1e-6
ROPE_THETA = 10000.0
MLSTM_HEADS = 4
MLSTM_DH = 256
MLSTM_W = MLSTM_HEADS * MLSTM_DH
MLA_HEADS = 8
MLA_NOPE = 128
MLA_ROPE = 64
MLA_V = 128
MLA_QK = MLA_NOPE + MLA_ROPE
Q_LORA = 512
KV_LORA = 256
LANES = 128
MIB = 1024 * 1024

IN_PAD = 4096
IN_BLOCK = 1024
MISC_W = 256
MISC_OFF = IN_BLOCK - MISC_W

MOD_TN = 1024
INPROJ_TM, INPROJ_SUB = 512, 256
MLSTM_TS, MLSTM_CHUNK = 512, 512
QKV_TM = 1024
ATT_TQ = 1024
ATT_TK = 256
ATT_HP = 2
OUTPROJ_TM, OUTPROJ_SUB = 512, 512
FFN_UP_TM, FFN_UP_TF = 1024, 512
FFN_DOWN_TM, FFN_DOWN_SUB = 512, 512


def _params(n_axes, vmem_mib):
    return pltpu.CompilerParams(
        dimension_semantics=("arbitrary",) * n_axes,
        vmem_limit_bytes=vmem_mib * MIB)


def _rms(x):
    return x * lax.rsqrt(jnp.mean(x * x, axis=-1, keepdims=True) + EPS)


def _silu(x):
    return x / (1.0 + jnp.exp(-x))


def _log_sigmoid(x):
    return jnp.minimum(x, 0.0) - jnp.log1p(jnp.exp(-jnp.abs(x)))


def _mod_kernel(c_ref, w_ref, b_ref, o_ref):
    ca = _silu(c_ref[...]).astype(BF16)
    o_ref[...] = jnp.dot(ca, w_ref[...].astype(BF16),
                         preferred_element_type=F32) + b_ref[...]


def _modulation(c, ada_w, ada_b):
    bsz, d = c.shape
    n = ada_w.shape[1]
    tn = MOD_TN
    cp = jnp.pad(c, ((0, 8 - bsz), (0, 0)))
    out = pl.pallas_call(
        _mod_kernel,
        grid=(n // tn,),
        in_specs=[pl.BlockSpec((8, d), lambda j: (0, 0)),
                  pl.BlockSpec((d, tn), lambda j: (0, j)),
                  pl.BlockSpec((1, tn), lambda j: (0, j))],
        out_specs=pl.BlockSpec((8, tn), lambda j: (0, j)),
        out_shape=jax.ShapeDtypeStruct((8, n), F32),
        compiler_params=_params(1, 40),
        name="adaln_mod",
    )(cp, ada_w, ada_b.reshape(1, n))
    return out[:bsz].reshape(bsz, 6, d)


def _inproj_kernel(x_ref, mod_ref, g_ref, w_ref, o_ref, misc_ref, *, sub):
    tm = x_ref.shape[0]
    nblk = w_ref.shape[1] // IN_BLOCK
    for s in range(tm // sub):
        rows = pl.ds(s * sub, sub)
        y = _rms(x_ref[rows, :]) * g_ref[...]
        h = (y * (1.0 + mod_ref[0, 1:2, :]) + mod_ref[0, 0:1, :]).astype(BF16)
        for j in range(nblk):
            cols = slice(j * IN_BLOCK, (j + 1) * IN_BLOCK)
            acc = jnp.dot(h, w_ref[:, cols], preferred_element_type=F32)
            o_ref[rows, cols] = acc.astype(BF16)
            if j == nblk - 1:
                misc_ref[rows, :] = acc[:, MISC_OFF:]


def _in_projection(x2, mod, g, w_in_p, seq):
    t, d = x2.shape
    tm = INPROJ_TM
    per_b = seq // tm
    return pl.pallas_call(
        functools.partial(_inproj_kernel, sub=INPROJ_SUB),
        grid=(t // tm,),
        in_specs=[pl.BlockSpec((tm, d), lambda i: (i, 0)),
                  pl.BlockSpec((1, 6, d), lambda i: (i // per_b, 0, 0)),
                  pl.BlockSpec((1, d), lambda i: (0, 0)),
                  pl.BlockSpec((d, IN_PAD), lambda i: (0, 0), pipeline_mode=pl.Buffered(1))],
        out_specs=[pl.BlockSpec((tm, IN_PAD), lambda i: (i, 0)),
                   pl.BlockSpec((tm, MISC_W), lambda i: (i, 0))],
        out_shape=[jax.ShapeDtypeStruct((t, IN_PAD), BF16),
                   jax.ShapeDtypeStruct((t, MISC_W), F32)],
        compiler_params=_params(1, 48),
        name="norm_inproj",
    )(x2, mod, g.reshape(1, d), w_in_p)


def _cumsum_rows(x):
    n = x.shape[0]
    row = lax.broadcasted_iota(jnp.int32, x.shape, 0)
    sh = 1
    while sh < n:
        x = x + jnp.where(row >= sh, pltpu.roll(x, sh, axis=0), 0.0)
        sh *= 2
    return x


def _cummax_rows(x):
    n = x.shape[0]
    row = lax.broadcasted_iota(jnp.int32, x.shape, 0)
    sh = 1
    while sh < n:
        x = jnp.maximum(x, jnp.where(row >= sh, pltpu.roll(x, sh, axis=0), -jnp.inf))
        sh *= 2
    return x


def _cumsum_lanes(x):
    n = x.shape[1]
    col = lax.broadcasted_iota(jnp.int32, x.shape, 1)
    sh = 1
    while sh < n:
        x = x + jnp.where(col >= sh, pltpu.roll(x, sh, axis=1), 0.0)
        sh *= 2
    return x


def _mlstm_kernel(u_ref, v_ref, o_ref, gt_ref, cw_ref, cb_ref, wqf_ref, wkf_ref, gb_ref, og_ref,
                  out_ref, xs_ref, ct_ref, n_ref, m_ref, wq_ref, wkt_ref, *, ts, chunk):
    i = pl.program_id(0)
    nbatch = u_ref.shape[0]

    @pl.when(i == 0)
    def _():
        for h in range(wqf_ref.shape[0]):
            wq_ref[h] = wqf_ref[h].astype(BF16)
            wkt_ref[h] = wkf_ref[h].astype(BF16).T
        for b in range(nbatch):
            xs_ref[b, 0:8, :] = jnp.zeros((8, MLSTM_W), F32)
        ct_ref[...] = jnp.zeros(ct_ref.shape, F32)
        n_ref[...] = jnp.zeros(n_ref.shape, F32)
        m_ref[...] = jnp.zeros(m_ref.shape, F32)

    row = lax.broadcasted_iota(jnp.int32, (chunk, chunk), 0)
    col = lax.broadcasted_iota(jnp.int32, (chunk, chunk), 1)
    causal = col <= row

    for c in range(ts // chunk):
        for b in range(nbatch):
            _mlstm_chunk(u_ref.at[b], v_ref.at[b], o_ref.at[b], gt_ref.at[b], cw_ref, cb_ref,
                         wq_ref, wkt_ref, gb_ref, og_ref, out_ref.at[b], xs_ref.at[b],
                         ct_ref.at[b], n_ref.at[b], m_ref.at[b], c, chunk, causal)
    for b in range(nbatch):
        xs_ref[b, 0:8, :] = xs_ref[b, ts:ts + 8, :]


def _mlstm_chunk(u_ref, v_ref, o_ref, gt_ref, cw_ref, cb_ref, wq_ref, wkt_ref, gb_ref, og_ref,
                 out_ref, xs_ref, ct_ref, n_ref, m_ref, c, chunk, causal):
    nh, dh = MLSTM_HEADS, MLSTM_DH
    gb = gb_ref[...]
    kw = cw_ref.shape[0]
    ones = jnp.ones((chunk, LANES), BF16)
    r0 = c * chunk
    xc = u_ref[r0:r0 + chunk, :].astype(F32)
    xs_ref[8 + r0:8 + r0 + chunk, :] = xc
    acc = cb_ref[...] + cw_ref[kw - 1:kw, :] * xc
    for j in range(kw - 1):
        acc = acc + cw_ref[j:j + 1, :] * xs_ref[pl.ds(8 + r0 - (kw - 1) + j, chunk), :]
    su = _silu(acc)

    gc = gt_ref[r0:r0 + chunk, MISC_W - LANES:] + gb
    bc = _cumsum_rows(_log_sigmoid(gc))
    gt = gc.T[0:8, :]
    bt = _cumsum_lanes(_log_sigmoid(gt))
    mdc = _cummax_rows(gc - pltpu.roll(bc, LANES - nh, axis=1))
    for h in range(nh):
        c0 = h * dh
        b_c = bc[:, nh + h:nh + h + 1]
        li_r = gt[h:h + 1, :]
        b_r = bt[nh + h:nh + h + 1, :]
        g_tot = b_r[:, chunk - 1:chunk]
        z_r = li_r - b_r
        zmax = jnp.max(z_r, axis=-1, keepdims=True)
        m_prev = m_ref[h:h + 1, 0:1]

        ub = su[:, c0:c0 + dh].astype(BF16)
        qb = jnp.dot(ub, wq_ref[h], preferred_element_type=F32).astype(BF16)
        kt = jnp.dot(wkt_ref[h], ub.T,
                     preferred_element_type=F32) * (dh ** -0.5)
        vc = v_ref[r0:r0 + chunk, c0:c0 + dh]

        mx_c = jnp.maximum(m_prev, mdc[:, h:h + 1])
        m_t = b_c + mx_c
        sb = (jnp.dot(qb, kt.astype(BF16), preferred_element_type=F32)
              * jnp.exp(jnp.where(causal, z_r - mx_c, -jnp.inf))).astype(BF16)
        inter = jnp.exp(m_prev - mx_c)
        ct = ct_ref[h]
        nrep = n_ref[h]
        num = (jnp.dot(sb, vc, preferred_element_type=F32)
               + inter * jnp.dot(qb, ct.astype(BF16), preferred_element_type=F32))
        den = (jnp.dot(sb, ones, preferred_element_type=F32)
               + inter * jnp.dot(qb, nrep.astype(BF16), preferred_element_type=F32))
        lim = jnp.maximum(jnp.abs(den), jnp.exp(-m_t))
        hh = num / jnp.concatenate([lim] * (dh // LANES), axis=1)

        mm = jnp.maximum(m_prev, zmax)
        s_prev = jnp.exp(m_prev - mm)
        s_loc = jnp.exp(zmax - mm)
        kwt = (kt * (jnp.exp(z_r - zmax) * s_loc)).astype(BF16)
        ct_ref[h] = s_prev * ct + jnp.dot(kwt, vc, preferred_element_type=F32)
        n_ref[h] = s_prev * nrep + jnp.dot(kwt, ones, preferred_element_type=F32)
        m_ref[h:h + 1, :] = jnp.broadcast_to(g_tot + mm, (1, LANES))

        og = o_ref[r0:r0 + chunk, c0:c0 + dh].astype(F32)
        hh = hh / (1.0 + jnp.exp(-og))
        out_ref[r0:r0 + chunk, c0:c0 + dh] = (
            _rms(hh) * og_ref[:, c0:c0 + dh]).astype(BF16)


def _mlstm(proj, misc, conv_w, conv_b, wq, wk, gate_b, out_g, bsz, seq):
    ts, chunk = MLSTM_TS, MLSTM_CHUNK
    t = bsz * seq
    kern = functools.partial(_mlstm_kernel, ts=ts, chunk=chunk)
    proj3 = proj.reshape(bsz, seq, proj.shape[1])
    misc3 = misc.reshape(bsz, seq, MISC_W)
    out = pl.pallas_call(
        kern,
        grid=(seq // ts,),
        in_specs=[pl.BlockSpec((bsz, ts, MLSTM_W), lambda i: (0, i, 0)),
                  pl.BlockSpec((bsz, ts, MLSTM_W), lambda i: (0, i, 1)),
                  pl.BlockSpec((bsz, ts, MLSTM_W), lambda i: (0, i, 2)),
                  pl.BlockSpec((bsz, ts, MISC_W), lambda i: (0, i, 0)),
                  pl.BlockSpec(conv_w.shape, lambda i: (0, 0)),
                  pl.BlockSpec((1, MLSTM_W), lambda i: (0, 0)),
                  pl.BlockSpec(wq.shape, lambda i: (0, 0, 0)),
                  pl.BlockSpec(wk.shape, lambda i: (0, 0, 0)),
                  pl.BlockSpec((1, LANES), lambda i: (0, 0)),
                  pl.BlockSpec((1, MLSTM_W), lambda i: (0, 0))],
        out_specs=pl.BlockSpec((bsz, ts, MLSTM_W), lambda i: (0, i, 0)),
        out_shape=jax.ShapeDtypeStruct((bsz, seq, MLSTM_W), BF16),
        scratch_shapes=[pltpu.VMEM((bsz, ts + 8, MLSTM_W), F32),
                        pltpu.VMEM((bsz, MLSTM_HEADS, MLSTM_DH, MLSTM_DH), F32),
                        pltpu.VMEM((bsz, MLSTM_HEADS, MLSTM_DH, LANES), F32),
                        pltpu.VMEM((bsz, 8, LANES), F32),
                        pltpu.VMEM(wq.shape, BF16),
                        pltpu.VMEM(wk.shape, BF16)],
        compiler_params=_params(1, 48),
        name="mlstm",
    )(proj3, proj3, proj3, misc3, conv_w, conv_b.reshape(1, MLSTM_W), wq, wk, gate_b,
      out_g.reshape(1, MLSTM_W))
    return out.reshape(t, MLSTM_W)


MLA_VA = MLA_V + 16


def _qkv_kernel(cq_ref, ckv_ref, misc_ref, pos_ref, fr_ref, qg_ref, kvg_ref, wq_ref, wkv_ref,
                qt_ref, k_ref, vt_ref, cqn, ckvn, cos_s, sin_s, kr_s, *, scale, tk):
    cqn[...] = (_rms(cq_ref[...].astype(F32)) * qg_ref[...]).astype(BF16)
    ckvn[...] = (_rms(ckv_ref[...].astype(F32)) * kvg_ref[...]).astype(BF16)
    ang = fr_ref[...] * pos_ref[0].astype(F32)
    reps = LANES // ang.shape[0]
    cs = jnp.concatenate([jnp.cos(ang)] * reps, axis=0).T
    sn = jnp.concatenate([jnp.sin(ang)] * reps, axis=0).T
    cos_s[...] = cs
    sin_s[...] = sn
    y = misc_ref[:, 0:LANES]
    kr_s[...] = y * cs + pltpu.roll(y, MLA_ROPE, axis=1) * sn

    for hh in range(wq_ref.shape[0]):
        mq = jnp.dot(cqn[...], wq_ref[hh], preferred_element_type=F32)
        qt_ref[0, hh, 0:MLA_NOPE, :] = (mq[:, 0:MLA_NOPE] * scale).astype(BF16).T
        y = mq[:, MLA_NOPE:]
        r = (y * cos_s[...] + pltpu.roll(y, MLA_ROPE, axis=1) * sin_s[...]) * scale
        qt_ref[0, hh, MLA_NOPE:MLA_QK, :] = r.astype(BF16).T[0:MLA_ROPE, :]

        w_kv = wkv_ref[:, hh * 2 * LANES:(hh + 1) * 2 * LANES].astype(BF16)
        mkv = jnp.dot(ckvn[...], w_kv, preferred_element_type=F32)
        k_ref[0, hh, :, 0:MLA_NOPE] = mkv[:, 0:MLA_NOPE].astype(BF16)
        k_ref[0, hh, :, MLA_NOPE:MLA_QK] = kr_s[:, 0:MLA_ROPE].astype(BF16)
        vt = mkv[:, MLA_NOPE:].astype(BF16).T
        for j in range(vt.shape[1] // tk):
            vt_ref[0, hh, j, 0:MLA_V, :] = vt[:, j * tk:(j + 1) * tk]
            vt_ref[0, hh, j, MLA_V:, :] = jnp.ones((MLA_VA - MLA_V, tk), BF16)


def _mla_qkv(proj, misc, pos, freqs, q_g, kv_g, wq_h, wkv, bsz, seq):
    tm, tk = QKV_TM, ATT_TK
    hps = MLA_HEADS
    t = bsz * seq
    nsb = seq // tm
    cq_blk = (3 * MLSTM_W) // Q_LORA
    ckv_blk = (3 * MLSTM_W + Q_LORA) // KV_LORA
    kern = functools.partial(_qkv_kernel, scale=MLA_QK ** -0.5 * float(np.log2(np.e)), tk=tk)
    return pl.pallas_call(
        kern,
        grid=(t // tm, MLA_HEADS // hps),
        in_specs=[pl.BlockSpec((tm, Q_LORA), lambda i, h: (i, cq_blk)),
                  pl.BlockSpec((tm, KV_LORA), lambda i, h: (i, ckv_blk)),
                  pl.BlockSpec((tm, MISC_W), lambda i, h: (i, 0)),
                  pl.BlockSpec((1, 1, tm), lambda i, h: (i, 0, 0)),
                  pl.BlockSpec(freqs.shape, lambda i, h: (0, 0)),
                  pl.BlockSpec((1, Q_LORA), lambda i, h: (0, 0)),
                  pl.BlockSpec((1, KV_LORA), lambda i, h: (0, 0)),
                  pl.BlockSpec((hps, Q_LORA, 2 * LANES), lambda i, h: (h, 0, 0)),
                  pl.BlockSpec(wkv.shape, lambda i, h: (0, 0))],
        out_specs=[pl.BlockSpec((1, hps, MLA_QK, tm), lambda i, h: (i // nsb, h, 0, i % nsb)),
                   pl.BlockSpec((1, hps, tm, MLA_QK), lambda i, h: (i // nsb, h, i % nsb, 0)),
                   pl.BlockSpec((1, hps, tm // tk, MLA_VA, tk),
                                lambda i, h: (i // nsb, h, i % nsb, 0, 0))],
        out_shape=[jax.ShapeDtypeStruct((bsz, MLA_HEADS, MLA_QK, seq), BF16),
                   jax.ShapeDtypeStruct((bsz, MLA_HEADS, seq, MLA_QK), BF16),
                   jax.ShapeDtypeStruct((bsz, MLA_HEADS, seq // tk, MLA_VA, tk), BF16)],
        scratch_shapes=[pltpu.VMEM((tm, Q_LORA), BF16),
                        pltpu.VMEM((tm, KV_LORA), BF16),
                        pltpu.VMEM((tm, LANES), F32),
                        pltpu.VMEM((tm, LANES), F32),
                        pltpu.VMEM((tm, LANES), F32)],
        compiler_params=_params(2, 40),
        name="mla_qkv",
    )(proj, proj, misc, pos, freqs, q_g.reshape(1, Q_LORA), kv_g.reshape(1, KV_LORA),
      wq_h, wkv)


def _flash_kernel(qt_ref, k_ref, vt_ref, g_ref, wa_ref, wb_ref, o_ref, wab_ref, wbb_ref,
                  m_ref, acc_ref, bias_ref, *, tq, tk, nq, wa_slabs, wb_slabs):
    qi = pl.program_id(2)

    step = (pl.program_id(0) * pl.num_programs(1) + pl.program_id(1)) * nq + qi

    @pl.when(step < wa_slabs)
    def _():
        wab_ref[...] = wa_ref[...].astype(BF16)

    @pl.when(step < wb_slabs)
    def _():
        wbb_ref[...] = wb_ref[...].astype(BF16)

    @pl.when((pl.program_id(0) == 0) & (pl.program_id(1) == 0) & (qi == 0))
    def _():
        key = lax.broadcasted_iota(jnp.int32, (tk, tq), 0)
        qry = lax.broadcasted_iota(jnp.int32, (tk, tq), 1)
        bias_ref[...] = jnp.where(key <= qry, 0.0, -jnp.inf).astype(F32)

    hp = qt_ref.shape[1]

    def scores(hh, kj, c0, masked):
        k = k_ref[0, hh, kj * tk:(kj + 1) * tk, :]
        s = jnp.dot(k, qt_ref[0, hh, :, c0:], preferred_element_type=F32)
        if masked:
            s = s + bias_ref[:, 0:tq - c0]
        return s

    def update(hh, s, kj, c0):
        if kj == 0:
            assert c0 == 0
            m_new = jnp.max(s, axis=0, keepdims=True)
            acc_ref[hh] = jnp.dot(vt_ref[0, hh, kj], jnp.exp2(s - m_new).astype(BF16),
                                  preferred_element_type=F32)
        else:
            m_prev = m_ref[hh, :, c0:]
            m_new = jnp.maximum(m_prev, jnp.max(s, axis=0, keepdims=True))
            alpha = jnp.exp2(m_prev - m_new)
            p = jnp.exp2(s - m_new)
            acc_ref[hh, :, c0:] = alpha * acc_ref[hh, :, c0:] + jnp.dot(
                vt_ref[0, hh, kj], p.astype(BF16), preferred_element_type=F32)
        m_ref[hh, :, c0:] = m_new

    per_q = tq // tk

    def run(q):
        def place(b):
            d = b - q * per_q
            return (0, False) if d < 0 else (d * tk, True)

        nblk = (q + 1) * per_q
        s = [scores(hh, 0, *place(0)) for hh in range(hp)]
        for b in range(nblk):
            s_next = ([scores(hh, b + 1, *place(b + 1)) for hh in range(hp)]
                      if b + 1 < nblk else None)
            for hh in range(hp):
                update(hh, s[hh], b, place(b)[0])
            s = s_next

    for q in range(nq):
        pl.when(qi == q)(functools.partial(run, q))

    for hh in range(hp):
        o = acc_ref[hh, 0:MLA_V, :] / acc_ref[hh, MLA_V:MLA_V + 1, :]
        o = o * lax.rsqrt(jnp.mean(o * o, axis=0, keepdims=True) + EPS) * g_ref[hh]
        o_ref[:, hh * MLA_V:(hh + 1) * MLA_V] = o.astype(BF16).T


def _cast_slab_rows(rows, n_steps):
    bf16_rows = 16
    for slab in range(bf16_rows, rows + 1, bf16_rows):
        if rows % slab == 0 and rows // slab <= n_steps:
            return slab
    raise ValueError("no slab size fits")


def _mla_attention(qt, k, vt, out_g, wa, wb):
    bsz, nh, seq, _ = k.shape
    tq, tk = ATT_TQ, ATT_TK
    nq = seq // tq
    hp = ATT_HP
    ng = nh // hp
    n_steps = bsz * ng * nq
    sa = _cast_slab_rows(wa.shape[0], n_steps)
    sb = _cast_slab_rows(wb.shape[0], n_steps)
    na, nb = wa.shape[0] // sa, wb.shape[0] // sb
    kern = functools.partial(_flash_kernel, tq=tq, tk=tk, nq=nq, wa_slabs=na, wb_slabs=nb)
    step = lambda b, h, i: (b * ng + h) * nq + i
    wa_map = lambda b, h, i: (jnp.minimum(step(b, h, i), na - 1), 0)
    wb_map = lambda b, h, i: (jnp.minimum(step(b, h, i), nb - 1), 0)
    return pl.pallas_call(
        kern,
        grid=(bsz, ng, nq),
        in_specs=[pl.BlockSpec((1, hp, MLA_QK, tq), lambda b, h, i: (b, h, 0, i)),
                  pl.BlockSpec((1, hp, seq, MLA_QK), lambda b, h, i: (b, h, 0, 0)),
                  pl.BlockSpec((1, hp, seq // tk, MLA_VA, tk), lambda b, h, i: (b, h, 0, 0, 0)),
                  pl.BlockSpec((hp, MLA_V, 1), lambda b, h, i: (h, 0, 0)),
                  pl.BlockSpec((sa, wa.shape[1]), wa_map),
                  pl.BlockSpec((sb, wb.shape[1]), wb_map)],
        out_specs=[pl.BlockSpec((tq, hp * MLA_V), lambda b, h, i: (b * nq + i, h)),
                   pl.BlockSpec((sa, wa.shape[1]), wa_map),
                   pl.BlockSpec((sb, wb.shape[1]), wb_map)],
        out_shape=[jax.ShapeDtypeStruct((bsz * seq, nh * MLA_V), BF16),
                   jax.ShapeDtypeStruct(wa.shape, BF16),
                   jax.ShapeDtypeStruct(wb.shape, BF16)],
        scratch_shapes=[pltpu.VMEM((hp, 1, tq), F32),
                        pltpu.VMEM((hp, MLA_VA, tq), F32),
                        pltpu.VMEM((tk, tq), F32)],
        compiler_params=_params(3, 48),
        name="mla_attention",
    )(qt, k, vt, out_g.reshape(nh, MLA_V, 1), wa, wb)


def _outproj_kernel(hm_ref, ha_ref, w_ref, x_ref, mod_ref, g_ref, x1_ref, h2_ref, *, sub):
    km = hm_ref.shape[1]
    for s in range(hm_ref.shape[0] // sub):
        rows = pl.ds(s * sub, sub)
        mix = (jnp.dot(hm_ref[rows, :], w_ref[0:km, :], preferred_element_type=F32)
               + jnp.dot(ha_ref[rows, :], w_ref[km:, :], preferred_element_type=F32))
        x1 = x_ref[rows, :] + mod_ref[0, 2:3, :] * mix
        x1_ref[rows, :] = x1
        y = _rms(x1) * g_ref[...]
        h2_ref[rows, :] = (y * (1.0 + mod_ref[0, 4:5, :]) + mod_ref[0, 3:4, :]).astype(BF16)


def _out_projection(hm, ha, w_out, x2, mod, g, seq):
    t, d = x2.shape
    tm = OUTPROJ_TM
    per_b = seq // tm
    return pl.pallas_call(
        functools.partial(_outproj_kernel, sub=OUTPROJ_SUB),
        grid=(t // tm,),
        in_specs=[pl.BlockSpec((tm, hm.shape[1]), lambda i: (i, 0)),
                  pl.BlockSpec((tm, ha.shape[1]), lambda i: (i, 0)),
                  pl.BlockSpec(w_out.shape, lambda i: (0, 0), pipeline_mode=pl.Buffered(1)),
                  pl.BlockSpec((tm, d), lambda i: (i, 0)),
                  pl.BlockSpec((1, 6, d), lambda i: (i // per_b, 0, 0)),
                  pl.BlockSpec((1, d), lambda i: (0, 0))],
        out_specs=[pl.BlockSpec((tm, d), lambda i: (i, 0)),
                   pl.BlockSpec((tm, d), lambda i: (i, 0))],
        out_shape=[jax.ShapeDtypeStruct((t, d), F32),
                   jax.ShapeDtypeStruct((t, d), BF16)],
        compiler_params=_params(1, 48),
        name="out_proj",
    )(hm, ha, w_out, x2, mod, g.reshape(1, d))


def _ffn_up_kernel(h_ref, wv_ref, wg_ref, cwv_ref, cwg_ref, cbv_ref, cbg_ref, o_ref,
                   xv_ref, xg_ref, wvb_ref, wgb_ref, *, tm, blocks_per_seq):
    i = pl.program_id(1)

    @pl.when(i % blocks_per_seq == 0)
    def _():
        xv_ref[0:8, :] = jnp.zeros((8, xv_ref.shape[1]), F32)
        xg_ref[0:8, :] = jnp.zeros((8, xg_ref.shape[1]), F32)

    @pl.when(i == 0)
    def _():
        wvb_ref[...] = wv_ref[...].astype(BF16)
        wgb_ref[...] = wg_ref[...].astype(BF16)

    h = h_ref[...]

    def conv(w_ref, cw_ref, cb_ref, xs_ref):
        kw = cw_ref.shape[0]
        up = jnp.dot(h, w_ref[...], preferred_element_type=F32)
        xs_ref[8:8 + tm, :] = up
        y = cb_ref[...] + cw_ref[kw - 1:kw, :] * up
        for j in range(kw - 1):
            y = y + cw_ref[j:j + 1, :] * xs_ref[pl.ds(8 - (kw - 1) + j, tm), :]
        xs_ref[0:8, :] = xs_ref[tm:tm + 8, :]
        return y

    val = conv(wvb_ref, cwv_ref, cbv_ref, xv_ref)
    gate = conv(wgb_ref, cwg_ref, cbg_ref, xg_ref)
    o_ref[...] = (_silu(gate) * val).astype(BF16)


def _ffn_up(h2, w_up, conv_w, conv_b, seq):
    t, d = h2.shape
    f = w_up.shape[1] // 2
    tm, tf = FFN_UP_TM, FFN_UP_TF
    nf = f // tf
    kw = conv_w.shape[0]
    kern = functools.partial(_ffn_up_kernel, tm=tm, blocks_per_seq=seq // tm)
    cb = conv_b.reshape(1, 2 * f)
    return pl.pallas_call(
        kern,
        grid=(nf, t // tm),
        in_specs=[pl.BlockSpec((tm, d), lambda j, i: (i, 0)),
                  pl.BlockSpec((d, tf), lambda j, i: (0, j)),
                  pl.BlockSpec((d, tf), lambda j, i: (0, nf + j)),
                  pl.BlockSpec((kw, tf), lambda j, i: (0, j)),
                  pl.BlockSpec((kw, tf), lambda j, i: (0, nf + j)),
                  pl.BlockSpec((1, tf), lambda j, i: (0, j)),
                  pl.BlockSpec((1, tf), lambda j, i: (0, nf + j))],
        out_specs=pl.BlockSpec((tm, tf), lambda j, i: (i, j)),
        out_shape=jax.ShapeDtypeStruct((t, f), BF16),
        scratch_shapes=[pltpu.VMEM((tm + 8, tf), F32),
                        pltpu.VMEM((tm + 8, tf), F32),
                        pltpu.VMEM((d, tf), BF16),
                        pltpu.VMEM((d, tf), BF16)],
        compiler_params=_params(2, 56),
        name="ffn_up",
    )(h2, w_up, w_up, conv_w, conv_w, cb, cb)


def _ffn_down_kernel(a_ref, w_ref, x1_ref, mod_ref, g_ref, o_ref, *, sub):
    for s in range(a_ref.shape[0] // sub):
        rows = pl.ds(s * sub, sub)
        y = jnp.dot(a_ref[rows, :], w_ref[...], preferred_element_type=F32)
        x2 = x1_ref[rows, :] + mod_ref[0, 5:6, :] * y
        o_ref[rows, :] = _rms(x2) * g_ref[...]


def _ffn_down(act, w_down, x1, mod, g, seq):
    t, f = act.shape
    d = w_down.shape[1]
    tm = FFN_DOWN_TM
    per_b = seq // tm
    return pl.pallas_call(
        functools.partial(_ffn_down_kernel, sub=FFN_DOWN_SUB),
        grid=(t // tm,),
        in_specs=[pl.BlockSpec((tm, f), lambda i: (i, 0)),
                  pl.BlockSpec((f, d), lambda i: (0, 0), pipeline_mode=pl.Buffered(1)),
                  pl.BlockSpec((tm, d), lambda i: (i, 0)),
                  pl.BlockSpec((1, 6, d), lambda i: (i // per_b, 0, 0)),
                  pl.BlockSpec((1, d), lambda i: (0, 0))],
        out_specs=pl.BlockSpec((tm, d), lambda i: (i, 0)),
        out_shape=jax.ShapeDtypeStruct((t, d), F32),
        compiler_params=_params(1, 58),
        name="ffn_down",
    )(act, w_down, x1, mod, g.reshape(1, d))


def _rot_cols(w):
    half = w.shape[-1] // 2
    return jnp.concatenate([-w[..., half:], w[..., :half]], axis=-1)


def _win_prep_kernel(w_ref, tail_ref, o_ref, *, n_main):
    j = pl.program_id(0)
    gates = 2 * MLSTM_HEADS
    half = MLA_ROPE // 2

    @pl.when(j < n_main)
    def _():
        o_ref[...] = w_ref[...].T.astype(BF16)

    @pl.when(j == n_main)
    def _():
        o_ref[...] = tail_ref[gates:gates + Q_LORA, :].T.astype(BF16)

    @pl.when(j == n_main + 1)
    def _():
        r0 = gates + Q_LORA
        kr0 = r0 + KV_LORA
        blk = o_ref.shape[1]
        used = KV_LORA + 2 * MLA_ROPE + gates
        o_ref[...] = jnp.concatenate(
            [tail_ref[r0:kr0, :],
             tail_ref[kr0:kr0 + MLA_ROPE, :],
             -tail_ref[kr0 + half:kr0 + MLA_ROPE, :],
             tail_ref[kr0:kr0 + half, :],
             tail_ref[0:gates, :],
             jnp.zeros((blk - used, o_ref.shape[0]), F32)], axis=0).T.astype(BF16)


def _prep_w_in(w_in):
    d = w_in.shape[0]
    wt = w_in.T
    n_head = 3 * MLSTM_W
    blk = Q_LORA
    n_main = n_head // blk
    return pl.pallas_call(
        functools.partial(_win_prep_kernel, n_main=n_main),
        grid=(IN_PAD // blk,),
        in_specs=[pl.BlockSpec((blk, d), lambda j: (jnp.minimum(j, n_main - 1), 0)),
                  pl.BlockSpec((IN_PAD - n_head, d), lambda j: (n_head // (IN_PAD - n_head), 0))],
        out_specs=pl.BlockSpec((d, blk), lambda j: (0, j)),
        out_shape=jax.ShapeDtypeStruct((d, IN_PAD), BF16),
        compiler_params=_params(1, 40),
        name="w_in_prep",
    )(wt, wt)


def kernel(x, c, positions, ada_w, ada_b, attn_norm_g, w_in, mlstm_conv_w, mlstm_conv_b, mlstm_wq, mlstm_wk, mlstm_igate_b, mlstm_fgate_b, mla_q_norm_g, mla_w_uq, mla_kv_norm_g, mla_w_ukv, mlstm_out_g, mla_out_g, w_out, ffn_norm_g, ffn_w_up, ffn_conv_w, ffn_conv_b, ffn_w_down, final_norm_g):
    bsz, seq, d = x.shape
    t = bsz * seq
    depth = ada_w.shape[0]
    assert depth == 1, "the final RMSNorm is fused into the single layer's down-projection"
    xr = x.reshape(t, d)
    pos = positions.reshape(t // QKV_TM, 1, QKV_TM)
    half = MLA_ROPE // 2
    freqs = (ROPE_THETA ** (-jnp.arange(half, dtype=F32) / half)).reshape(half, 1)

    for l in range(depth):
        mod = _modulation(c, ada_w[l], ada_b[l])

        proj, misc = _in_projection(xr, mod, attn_norm_g[l], _prep_w_in(w_in[l]), seq)

        gate_b = jnp.concatenate(
            [mlstm_igate_b[l], mlstm_fgate_b[l],
             jnp.zeros((LANES - 2 * MLSTM_HEADS,), F32)]).reshape(1, LANES)
        hm = _mlstm(proj, misc, mlstm_conv_w[l], mlstm_conv_b[l], mlstm_wq[l], mlstm_wk[l], gate_b,
                    mlstm_out_g[l], bsz, seq)

        wq = mla_w_uq[l].reshape(Q_LORA, MLA_HEADS, MLA_QK)
        wq_r = wq[..., MLA_NOPE:]
        wq_h = jnp.concatenate([wq[..., :MLA_NOPE], wq_r, _rot_cols(wq_r)], axis=-1)
        wq_h = wq_h.transpose(1, 0, 2).astype(BF16)
        qt, k, vt = _mla_qkv(proj, misc, pos, freqs, mla_q_norm_g[l], mla_kv_norm_g[l],
                             wq_h, mla_w_ukv[l], bsz, seq)
        ha, w_down_b, w_out_b = _mla_attention(qt, k, vt, mla_out_g[l], ffn_w_down[l], w_out[l])

        x1, h2 = _out_projection(hm, ha, w_out_b, xr, mod, ffn_norm_g[l], seq)

        act = _ffn_up(h2, ffn_w_up[l], ffn_conv_w[l], ffn_conv_b[l], seq)
        xr = _ffn_down(act, w_down_b, x1, mod, final_norm_g, seq)
    return xr.reshape(bsz, seq, d)
```

```python
import functools

import jax
import jax.numpy as jnp
import numpy as np
from jax import lax
from jax.experimental import pallas as pl
from jax.experimental.pallas import tpu as pltpu

F32 = jnp.float32
BF16 = jnp.bfloat16

EPS = 1e-6
ROPE_THETA = 10000.0
MLSTM_HEADS = 4
MLSTM_DH = 256
MLSTM_W = MLSTM_HEADS * MLSTM_DH
MLA_HEADS = 8
MLA_NOPE = 128
MLA_ROPE = 64
MLA_V = 128
MLA_QK = MLA_NOPE + MLA_ROPE
Q_LORA = 512
KV_LORA = 256
LANES = 128
MIB = 1024 * 1024

IN_PAD = 4096
IN_BLOCK = 1024
MISC_W = 256
MISC_OFF = IN_BLOCK - MISC_W

MOD_TN = 1024
INPROJ_TM, INPROJ_SUB = 512, 256
MLSTM_TS, MLSTM_CHUNK = 512, 512
QKV_TM = 1024
ATT_TQ = 1024
ATT_TK = 256
ATT_HP = 2
OUTPROJ_TM, OUTPROJ_SUB = 512, 512
FFN_UP_TM, FFN_UP_TF = 1024, 512
FFN_DOWN_TM, FFN_DOWN_SUB = 512, 512
FFN_DOWN_WCHUNKS = 4


def _params(n_axes, vmem_mib):
    return pltpu.CompilerParams(
        dimension_semantics=("arbitrary",) * n_axes,
        vmem_limit_bytes=vmem_mib * MIB)


def _rms(x):
    return x * lax.rsqrt(jnp.mean(x * x, axis=-1, keepdims=True) + EPS)


def _silu(x):
    return x / (1.0 + jnp.exp(-x))


def _log_sigmoid(x):
    return jnp.minimum(x, 0.0) - jnp.log1p(jnp.exp(-jnp.abs(x)))


def _mod_kernel(c_ref, w_ref, b_ref, o_ref):
    ca = _silu(c_ref[...]).astype(BF16)
    o_ref[...] = jnp.dot(ca, w_ref[...].astype(BF16),
                         preferred_element_type=F32) + b_ref[...]


def _modulation(c, ada_w, ada_b):
    bsz, d = c.shape
    n = ada_w.shape[1]
    tn = MOD_TN
    cp = jnp.pad(c, ((0, 8 - bsz), (0, 0)))
    out = pl.pallas_call(
        _mod_kernel,
        grid=(n // tn,),
        in_specs=[pl.BlockSpec((8, d), lambda j: (0, 0)),
                  pl.BlockSpec((d, tn), lambda j: (0, j)),
                  pl.BlockSpec((1, tn), lambda j: (0, j))],
        out_specs=pl.BlockSpec((8, tn), lambda j: (0, j)),
        out_shape=jax.ShapeDtypeStruct((8, n), F32),
        compiler_params=_params(1, 40),
        name="adaln_mod",
    )(cp, ada_w, ada_b.reshape(1, n))
    return out[:bsz].reshape(bsz, 6, d)


def _inproj_kernel(x_ref, mod_ref, g_ref, w_ref, o_ref, misc_ref, *, sub):
    tm = x_ref.shape[0]
    nblk = w_ref.shape[1] // IN_BLOCK
    for s in range(tm // sub):
        rows = pl.ds(s * sub, sub)
        y = _rms(x_ref[rows, :]) * g_ref[...]
        h = (y * (1.0 + mod_ref[0, 1:2, :]) + mod_ref[0, 0:1, :]).astype(BF16)
        for j in range(nblk):
            cols = slice(j * IN_BLOCK, (j + 1) * IN_BLOCK)
            acc = jnp.dot(h, w_ref[:, cols], preferred_element_type=F32)
            o_ref[rows, cols] = acc.astype(BF16)
            if j == nblk - 1:
                misc_ref[rows, :] = acc[:, MISC_OFF:]


def _in_projection(x2, mod, g, w_in_p, seq):
    t, d = x2.shape
    tm = INPROJ_TM
    per_b = seq // tm
    return pl.pallas_call(
        functools.partial(_inproj_kernel, sub=INPROJ_SUB),
        grid=(t // tm,),
        in_specs=[pl.BlockSpec((tm, d), lambda i: (i, 0)),
                  pl.BlockSpec((1, 6, d), lambda i: (i // per_b, 0, 0)),
                  pl.BlockSpec((1, d), lambda i: (0, 0)),
                  pl.BlockSpec((d, IN_PAD), lambda i: (0, 0), pipeline_mode=pl.Buffered(1))],
        out_specs=[pl.BlockSpec((tm, IN_PAD), lambda i: (i, 0)),
                   pl.BlockSpec((tm, MISC_W), lambda i: (i, 0))],
        out_shape=[jax.ShapeDtypeStruct((t, IN_PAD), BF16),
                   jax.ShapeDtypeStruct((t, MISC_W), F32)],
        compiler_params=_params(1, 48),
        name="norm_inproj",
    )(x2, mod, g.reshape(1, d), w_in_p)


def _cumsum_rows(x):
    n = x.shape[0]
    row = lax.broadcasted_iota(jnp.int32, x.shape, 0)
    sh = 1
    while sh < n:
        x = x + jnp.where(row >= sh, pltpu.roll(x, sh, axis=0), 0.0)
        sh *= 2
    return x


def _cummax_rows(x):
    n = x.shape[0]
    row = lax.broadcasted_iota(jnp.int32, x.shape, 0)
    sh = 1
    while sh < n:
        x = jnp.maximum(x, jnp.where(row >= sh, pltpu.roll(x, sh, axis=0), -jnp.inf))
        sh *= 2
    return x


def _cumsum_lanes(x):
    n = x.shape[1]
    col = lax.broadcasted_iota(jnp.int32, x.shape, 1)
    sh = 1
    while sh < n:
        x = x + jnp.where(col >= sh, pltpu.roll(x, sh, axis=1), 0.0)
        sh *= 2
    return x


def _mlstm_kernel(u_ref, v_ref, o_ref, gt_ref, cw_ref, cb_ref, wqf_ref, wkf_ref, gb_ref, og_ref,
                  out_ref, xs_ref, ct_ref, n_ref, m_ref, wq_ref, wkt_ref, *, ts, chunk):
    i = pl.program_id(0)
    nbatch = u_ref.shape[0]

    @pl.when(i == 0)
    def _():
        for h in range(wqf_ref.shape[0]):
            wq_ref[h] = wqf_ref[h].astype(BF16)
            wkt_ref[h] = wkf_ref[h].astype(BF16).T
        for b in range(nbatch):
            xs_ref[b, 0:8, :] = jnp.zeros((8, MLSTM_W), F32)
        ct_ref[...] = jnp.zeros(ct_ref.shape, F32)
        n_ref[...] = jnp.zeros(n_ref.shape, F32)
        m_ref[...] = jnp.zeros(m_ref.shape, F32)

    row = lax.broadcasted_iota(jnp.int32, (chunk, chunk), 0)
    col = lax.broadcasted_iota(jnp.int32, (chunk, chunk), 1)
    causal = col <= row

    for c in range(ts // chunk):
        for b in range(nbatch):
            _mlstm_chunk(u_ref.at[b], v_ref.at[b], o_ref.at[b], gt_ref.at[b], cw_ref, cb_ref,
                         wq_ref, wkt_ref, gb_ref, og_ref, out_ref.at[b], xs_ref.at[b],
                         ct_ref.at[b], n_ref.at[b], m_ref.at[b], c, chunk, causal)
    for b in range(nbatch):
        xs_ref[b, 0:8, :] = xs_ref[b, ts:ts + 8, :]


def _mlstm_chunk(u_ref, v_ref, o_ref, gt_ref, cw_ref, cb_ref, wq_ref, wkt_ref, gb_ref, og_ref,
                 out_ref, xs_ref, ct_ref, n_ref, m_ref, c, chunk, causal):
    nh, dh = MLSTM_HEADS, MLSTM_DH
    gb = gb_ref[...]
    kw = cw_ref.shape[0]
    ones = jnp.ones((chunk, LANES), BF16)
    r0 = c * chunk
    xc = u_ref[r0:r0 + chunk, :].astype(F32)
    xs_ref[8 + r0:8 + r0 + chunk, :] = xc
    acc = cb_ref[...] + cw_ref[kw - 1:kw, :] * xc
    for j in range(kw - 1):
        acc = acc + cw_ref[j:j + 1, :] * xs_ref[pl.ds(8 + r0 - (kw - 1) + j, chunk), :]
    su = _silu(acc)

    gc = gt_ref[r0:r0 + chunk, MISC_W - LANES:] + gb
    bc = _cumsum_rows(_log_sigmoid(gc))
    gt = gc.T[0:8, :]
    bt = _cumsum_lanes(_log_sigmoid(gt))
    mdc = _cummax_rows(gc - pltpu.roll(bc, LANES - nh, axis=1))
    for h in range(nh):
        c0 = h * dh
        b_c = bc[:, nh + h:nh + h + 1]
        li_r = gt[h:h + 1, :]
        b_r = bt[nh + h:nh + h + 1, :]
        g_tot = b_r[:, chunk - 1:chunk]
        z_r = li_r - b_r
        zmax = jnp.max(z_r, axis=-1, keepdims=True)
        m_prev = m_ref[h:h + 1, 0:1]

        ub = su[:, c0:c0 + dh].astype(BF16)
        qb = jnp.dot(ub, wq_ref[h], preferred_element_type=F32).astype(BF16)
        kt = jnp.dot(wkt_ref[h], ub.T,
                     preferred_element_type=F32) * (dh ** -0.5)
        vc = v_ref[r0:r0 + chunk, c0:c0 + dh]

        mx_c = jnp.maximum(m_prev, mdc[:, h:h + 1])
        m_t = b_c + mx_c
        sb = (jnp.dot(qb, kt.astype(BF16), preferred_element_type=F32)
              * jnp.exp(jnp.where(causal, z_r - mx_c, -jnp.inf))).astype(BF16)
        inter = jnp.exp(m_prev - mx_c)
        ct = ct_ref[h]
        nrep = n_ref[h]
        num = (jnp.dot(sb, vc, preferred_element_type=F32)
               + inter * jnp.dot(qb, ct.astype(BF16), preferred_element_type=F32))
        den = (jnp.dot(sb, ones, preferred_element_type=F32)
               + inter * jnp.dot(qb, nrep.astype(BF16), preferred_element_type=F32))
        lim = jnp.maximum(jnp.abs(den), jnp.exp(-m_t))
        hh = num / jnp.concatenate([lim] * (dh // LANES), axis=1)

        mm = jnp.maximum(m_prev, zmax)
        s_prev = jnp.exp(m_prev - mm)
        s_loc = jnp.exp(zmax - mm)
        kwt = (kt * (jnp.exp(z_r - zmax) * s_loc)).astype(BF16)
        ct_ref[h] = s_prev * ct + jnp.dot(kwt, vc, preferred_element_type=F32)
        n_ref[h] = s_prev * nrep + jnp.dot(kwt, ones, preferred_element_type=F32)
        m_ref[h:h + 1, :] = jnp.broadcast_to(g_tot + mm, (1, LANES))

        og = o_ref[r0:r0 + chunk, c0:c0 + dh].astype(F32)
        hh = hh / (1.0 + jnp.exp(-og))
        out_ref[r0:r0 + chunk, c0:c0 + dh] = (
            _rms(hh) * og_ref[:, c0:c0 + dh]).astype(BF16)


def _mlstm(proj, misc, conv_w, conv_b, wq, wk, gate_b, out_g, bsz, seq):
    ts, chunk = MLSTM_TS, MLSTM_CHUNK
    t = bsz * seq
    kern = functools.partial(_mlstm_kernel, ts=ts, chunk=chunk)
    proj3 = proj.reshape(bsz, seq, proj.shape[1])
    misc3 = misc.reshape(bsz, seq, MISC_W)
    out = pl.pallas_call(
        kern,
        grid=(seq // ts,),
        in_specs=[pl.BlockSpec((bsz, ts, MLSTM_W), lambda i: (0, i, 0)),
                  pl.BlockSpec((bsz, ts, MLSTM_W), lambda i: (0, i, 1)),
                  pl.BlockSpec((bsz, ts, MLSTM_W), lambda i: (0, i, 2)),
                  pl.BlockSpec((bsz, ts, MISC_W), lambda i: (0, i, 0)),
                  pl.BlockSpec(conv_w.shape, lambda i: (0, 0)),
                  pl.BlockSpec((1, MLSTM_W), lambda i: (0, 0)),
                  pl.BlockSpec(wq.shape, lambda i: (0, 0, 0)),
                  pl.BlockSpec(wk.shape, lambda i: (0, 0, 0)),
                  pl.BlockSpec((1, LANES), lambda i: (0, 0)),
                  pl.BlockSpec((1, MLSTM_W), lambda i: (0, 0))],
        out_specs=pl.BlockSpec((bsz, ts, MLSTM_W), lambda i: (0, i, 0)),
        out_shape=jax.ShapeDtypeStruct((bsz, seq, MLSTM_W), BF16),
        scratch_shapes=[pltpu.VMEM((bsz, ts + 8, MLSTM_W), F32),
                        pltpu.VMEM((bsz, MLSTM_HEADS, MLSTM_DH, MLSTM_DH), F32),
                        pltpu.VMEM((bsz, MLSTM_HEADS, MLSTM_DH, LANES), F32),
                        pltpu.VMEM((bsz, 8, LANES), F32),
                        pltpu.VMEM(wq.shape, BF16),
                        pltpu.VMEM(wk.shape, BF16)],
        compiler_params=_params(1, 48),
        name="mlstm",
    )(proj3, proj3, proj3, misc3, conv_w, conv_b.reshape(1, MLSTM_W), wq, wk, gate_b,
      out_g.reshape(1, MLSTM_W))
    return out.reshape(t, MLSTM_W)


MLA_VA = MLA_V + 16


def _qkv_kernel(cq_ref, ckv_ref, misc_ref, pos_ref, fr_ref, qg_ref, kvg_ref, wq_ref, wkv_ref,
                qt_ref, k_ref, vt_ref, cqn, ckvn, cos_s, sin_s, kr_s, *, scale, tk):
    cqn[...] = (_rms(cq_ref[...].astype(F32)) * qg_ref[...]).astype(BF16)
    ckvn[...] = (_rms(ckv_ref[...].astype(F32)) * kvg_ref[...]).astype(BF16)
    ang = fr_ref[...] * pos_ref[0].astype(F32)
    reps = LANES // ang.shape[0]
    cs = jnp.concatenate([jnp.cos(ang)] * reps, axis=0).T
    sn = jnp.concatenate([jnp.sin(ang)] * reps, axis=0).T
    cos_s[...] = cs
    sin_s[...] = sn
    y = misc_ref[:, 0:LANES]
    kr_s[...] = y * cs + pltpu.roll(y, MLA_ROPE, axis=1) * sn

    for hh in range(wq_ref.shape[0]):
        mq = jnp.dot(cqn[...], wq_ref[hh], preferred_element_type=F32)
        qt_ref[0, hh, 0:MLA_NOPE, :] = (mq[:, 0:MLA_NOPE] * scale).astype(BF16).T
        y = mq[:, MLA_NOPE:]
        r = (y * cos_s[...] + pltpu.roll(y, MLA_ROPE, axis=1) * sin_s[...]) * scale
        qt_ref[0, hh, MLA_NOPE:MLA_QK, :] = r.astype(BF16).T[0:MLA_ROPE, :]

        w_kv = wkv_ref[:, hh * 2 * LANES:(hh + 1) * 2 * LANES].astype(BF16)
        mkv = jnp.dot(ckvn[...], w_kv, preferred_element_type=F32)
        k_ref[0, hh, :, 0:MLA_NOPE] = mkv[:, 0:MLA_NOPE].astype(BF16)
        k_ref[0, hh, :, MLA_NOPE:MLA_QK] = kr_s[:, 0:MLA_ROPE].astype(BF16)
        vt = mkv[:, MLA_NOPE:].astype(BF16).T
        for j in range(vt.shape[1] // tk):
            vt_ref[0, hh, j, 0:MLA_V, :] = vt[:, j * tk:(j + 1) * tk]
            vt_ref[0, hh, j, MLA_V:, :] = jnp.ones((MLA_VA - MLA_V, tk), BF16)


def _mla_qkv(proj, misc, pos, freqs, q_g, kv_g, wq_h, wkv, bsz, seq):
    tm, tk = QKV_TM, ATT_TK
    hps = MLA_HEADS
    t = bsz * seq
    nsb = seq // tm
    cq_blk = (3 * MLSTM_W) // Q_LORA
    ckv_blk = (3 * MLSTM_W + Q_LORA) // KV_LORA
    kern = functools.partial(_qkv_kernel, scale=MLA_QK ** -0.5 * float(np.log2(np.e)), tk=tk)
    return pl.pallas_call(
        kern,
        grid=(t // tm, MLA_HEADS // hps),
        in_specs=[pl.BlockSpec((tm, Q_LORA), lambda i, h: (i, cq_blk)),
                  pl.BlockSpec((tm, KV_LORA), lambda i, h: (i, ckv_blk)),
                  pl.BlockSpec((tm, MISC_W), lambda i, h: (i, 0)),
                  pl.BlockSpec((1, 1, tm), lambda i, h: (i, 0, 0)),
                  pl.BlockSpec(freqs.shape, lambda i, h: (0, 0)),
                  pl.BlockSpec((1, Q_LORA), lambda i, h: (0, 0)),
                  pl.BlockSpec((1, KV_LORA), lambda i, h: (0, 0)),
                  pl.BlockSpec((hps, Q_LORA, 2 * LANES), lambda i, h: (h, 0, 0)),
                  pl.BlockSpec(wkv.shape, lambda i, h: (0, 0))],
        out_specs=[pl.BlockSpec((1, hps, MLA_QK, tm), lambda i, h: (i // nsb, h, 0, i % nsb)),
                   pl.BlockSpec((1, hps, tm, MLA_QK), lambda i, h: (i // nsb, h, i % nsb, 0)),
                   pl.BlockSpec((1, hps, tm // tk, MLA_VA, tk),
                                lambda i, h: (i // nsb, h, i % nsb, 0, 0))],
        out_shape=[jax.ShapeDtypeStruct((bsz, MLA_HEADS, MLA_QK, seq), BF16),
                   jax.ShapeDtypeStruct((bsz, MLA_HEADS, seq, MLA_QK), BF16),
                   jax.ShapeDtypeStruct((bsz, MLA_HEADS, seq // tk, MLA_VA, tk), BF16)],
        scratch_shapes=[pltpu.VMEM((tm, Q_LORA), BF16),
                        pltpu.VMEM((tm, KV_LORA), BF16),
                        pltpu.VMEM((tm, LANES), F32),
                        pltpu.VMEM((tm, LANES), F32),
                        pltpu.VMEM((tm, LANES), F32)],
        compiler_params=_params(2, 40),
        name="mla_qkv",
    )(proj, proj, misc, pos, freqs, q_g.reshape(1, Q_LORA), kv_g.reshape(1, KV_LORA),
      wq_h, wkv)


def _flash_kernel(qt_ref, k_ref, vt_ref, g_ref, wa_ref, wb_ref, o_ref, wab_ref, wbb_ref,
                  m_ref, acc_ref, bias_ref, *, tq, tk, nq, wa_slabs, wb_slabs):
    qi = pl.program_id(2)

    step = (pl.program_id(0) * pl.num_programs(1) + pl.program_id(1)) * nq + qi

    @pl.when(step < wa_slabs)
    def _():
        wab_ref[...] = wa_ref[...].astype(BF16)

    @pl.when(step < wb_slabs)
    def _():
        wbb_ref[...] = wb_ref[...].astype(BF16)

    @pl.when((pl.program_id(0) == 0) & (pl.program_id(1) == 0) & (qi == 0))
    def _():
        key = lax.broadcasted_iota(jnp.int32, (tk, tq), 0)
        qry = lax.broadcasted_iota(jnp.int32, (tk, tq), 1)
        bias_ref[...] = jnp.where(key <= qry, 0.0, -jnp.inf).astype(F32)

    hp = qt_ref.shape[1]

    def scores(hh, kj, c0, masked):
        k = k_ref[0, hh, kj * tk:(kj + 1) * tk, :]
        s = jnp.dot(k, qt_ref[0, hh, :, c0:], preferred_element_type=F32)
        if masked:
            s = s + bias_ref[:, 0:tq - c0]
        return s

    def update(hh, s, kj, c0):
        if kj == 0:
            assert c0 == 0
            m_new = jnp.max(s, axis=0, keepdims=True)
            acc_ref[hh] = jnp.dot(vt_ref[0, hh, kj], jnp.exp2(s - m_new).astype(BF16),
                                  preferred_element_type=F32)
        else:
            m_prev = m_ref[hh, :, c0:]
            m_new = jnp.maximum(m_prev, jnp.max(s, axis=0, keepdims=True))
            alpha = jnp.exp2(m_prev - m_new)
            p = jnp.exp2(s - m_new)
            acc_ref[hh, :, c0:] = alpha * acc_ref[hh, :, c0:] + jnp.dot(
                vt_ref[0, hh, kj], p.astype(BF16), preferred_element_type=F32)
        m_ref[hh, :, c0:] = m_new

    per_q = tq // tk

    def run(q):
        def place(b):
            d = b - q * per_q
            return (0, False) if d < 0 else (d * tk, True)

        nblk = (q + 1) * per_q
        s = [scores(hh, 0, *place(0)) for hh in range(hp)]
        for b in range(nblk):
            s_next = ([scores(hh, b + 1, *place(b + 1)) for hh in range(hp)]
                      if b + 1 < nblk else None)
            for hh in range(hp):
                update(hh, s[hh], b, place(b)[0])
            s = s_next

    for q in range(nq):
        pl.when(qi == q)(functools.partial(run, q))

    for hh in range(hp):
        o = acc_ref[hh, 0:MLA_V, :] / acc_ref[hh, MLA_V:MLA_V + 1, :]
        o = o * lax.rsqrt(jnp.mean(o * o, axis=0, keepdims=True) + EPS) * g_ref[hh]
        o_ref[:, hh * MLA_V:(hh + 1) * MLA_V] = o.astype(BF16).T


def _cast_slab_rows(rows, n_steps):
    bf16_rows = 16
    for slab in range(bf16_rows, rows + 1, bf16_rows):
        if rows % slab == 0 and rows // slab <= n_steps:
            return slab
    raise ValueError("no slab size fits")


def _mla_attention(qt, k, vt, out_g, wa, wb):
    bsz, nh, seq, _ = k.shape
    tq, tk = ATT_TQ, ATT_TK
    nq = seq // tq
    hp = ATT_HP
    ng = nh // hp
    n_steps = bsz * ng * nq
    sa = _cast_slab_rows(wa.shape[0], n_steps)
    sb = _cast_slab_rows(wb.shape[0], n_steps)
    na, nb = wa.shape[0] // sa, wb.shape[0] // sb
    kern = functools.partial(_flash_kernel, tq=tq, tk=tk, nq=nq, wa_slabs=na, wb_slabs=nb)
    step = lambda b, h, i: (b * ng + h) * nq + i
    wa_map = lambda b, h, i: (jnp.minimum(step(b, h, i), na - 1), 0)
    wb_map = lambda b, h, i: (jnp.minimum(step(b, h, i), nb - 1), 0)
    return pl.pallas_call(
        kern,
        grid=(bsz, ng, nq),
        in_specs=[pl.BlockSpec((1, hp, MLA_QK, tq), lambda b, h, i: (b, h, 0, i)),
                  pl.BlockSpec((1, hp, seq, MLA_QK), lambda b, h, i: (b, h, 0, 0)),
                  pl.BlockSpec((1, hp, seq // tk, MLA_VA, tk), lambda b, h, i: (b, h, 0, 0, 0)),
                  pl.BlockSpec((hp, MLA_V, 1), lambda b, h, i: (h, 0, 0)),
                  pl.BlockSpec((sa, wa.shape[1]), wa_map),
                  pl.BlockSpec((sb, wb.shape[1]), wb_map)],
        out_specs=[pl.BlockSpec((tq, hp * MLA_V), lambda b, h, i: (b * nq + i, h)),
                   pl.BlockSpec((sa, wa.shape[1]), wa_map),
                   pl.BlockSpec((sb, wb.shape[1]), wb_map)],
        out_shape=[jax.ShapeDtypeStruct((bsz * seq, nh * MLA_V), BF16),
                   jax.ShapeDtypeStruct(wa.shape, BF16),
                   jax.ShapeDtypeStruct(wb.shape, BF16)],
        scratch_shapes=[pltpu.VMEM((hp, 1, tq), F32),
                        pltpu.VMEM((hp, MLA_VA, tq), F32),
                        pltpu.VMEM((tk, tq), F32)],
        compiler_params=_params(3, 48),
        name="mla_attention",
    )(qt, k, vt, out_g.reshape(nh, MLA_V, 1), wa, wb)


def _outproj_kernel(hm_ref, ha_ref, w_ref, x_ref, mod_ref, g_ref, x1_ref, h2_ref, *, sub):
    km = hm_ref.shape[1]
    for s in range(hm_ref.shape[0] // sub):
        rows = pl.ds(s * sub, sub)
        mix = (jnp.dot(hm_ref[rows, :], w_ref[0:km, :], preferred_element_type=F32)
               + jnp.dot(ha_ref[rows, :], w_ref[km:, :], preferred_element_type=F32))
        x1 = x_ref[rows, :] + mod_ref[0, 2:3, :] * mix
        x1_ref[rows, :] = x1
        y = _rms(x1) * g_ref[...]
        h2_ref[rows, :] = (y * (1.0 + mod_ref[0, 4:5, :]) + mod_ref[0, 3:4, :]).astype(BF16)


def _out_projection(hm, ha, w_out, x2, mod, g, seq):
    t, d = x2.shape
    tm = OUTPROJ_TM
    per_b = seq // tm
    return pl.pallas_call(
        functools.partial(_outproj_kernel, sub=OUTPROJ_SUB),
        grid=(t // tm,),
        in_specs=[pl.BlockSpec((tm, hm.shape[1]), lambda i: (i, 0)),
                  pl.BlockSpec((tm, ha.shape[1]), lambda i: (i, 0)),
                  pl.BlockSpec(w_out.shape, lambda i: (0, 0), pipeline_mode=pl.Buffered(1)),
                  pl.BlockSpec((tm, d), lambda i: (i, 0)),
                  pl.BlockSpec((1, 6, d), lambda i: (i // per_b, 0, 0)),
                  pl.BlockSpec((1, d), lambda i: (0, 0))],
        out_specs=[pl.BlockSpec((tm, d), lambda i: (i, 0)),
                   pl.BlockSpec((tm, d), lambda i: (i, 0))],
        out_shape=[jax.ShapeDtypeStruct((t, d), F32),
                   jax.ShapeDtypeStruct((t, d), BF16)],
        compiler_params=_params(1, 48),
        name="out_proj",
    )(hm, ha, w_out, x2, mod, g.reshape(1, d))


def _ffn_up_kernel(h_ref, wv_ref, wg_ref, cwv_ref, cwg_ref, cbv_ref, cbg_ref, o_ref,
                   xv_ref, xg_ref, wvb_ref, wgb_ref, *, tm, blocks_per_seq):
    i = pl.program_id(1)

    @pl.when(i % blocks_per_seq == 0)
    def _():
        xv_ref[0:8, :] = jnp.zeros((8, xv_ref.shape[1]), F32)
        xg_ref[0:8, :] = jnp.zeros((8, xg_ref.shape[1]), F32)

    @pl.when(i == 0)
    def _():
        wvb_ref[...] = wv_ref[...].astype(BF16)
        wgb_ref[...] = wg_ref[...].astype(BF16)

    h = h_ref[...]

    def conv(w_ref, cw_ref, cb_ref, xs_ref):
        kw = cw_ref.shape[0]
        up = jnp.dot(h, w_ref[...], preferred_element_type=F32)
        xs_ref[8:8 + tm, :] = up
        y = cb_ref[...] + cw_ref[kw - 1:kw, :] * up
        for j in range(kw - 1):
            y = y + cw_ref[j:j + 1, :] * xs_ref[pl.ds(8 - (kw - 1) + j, tm), :]
        xs_ref[0:8, :] = xs_ref[tm:tm + 8, :]
        return y

    val = conv(wvb_ref, cwv_ref, cbv_ref, xv_ref)
    gate = conv(wgb_ref, cwg_ref, cbg_ref, xg_ref)
    o_ref[...] = (_silu(gate) * val).astype(BF16)


def _ffn_up(h2, w_up, conv_w, conv_b, seq):
    t, d = h2.shape
    f = w_up.shape[1] // 2
    tm, tf = FFN_UP_TM, FFN_UP_TF
    nf = f // tf
    kw = conv_w.shape[0]
    kern = functools.partial(_ffn_up_kernel, tm=tm, blocks_per_seq=seq // tm)
    cb = conv_b.reshape(1, 2 * f)
    return pl.pallas_call(
        kern,
        grid=(nf, t // tm),
        in_specs=[pl.BlockSpec((tm, d), lambda j, i: (i, 0)),
                  pl.BlockSpec((d, tf), lambda j, i: (0, j)),
                  pl.BlockSpec((d, tf), lambda j, i: (0, nf + j)),
                  pl.BlockSpec((kw, tf), lambda j, i: (0, j)),
                  pl.BlockSpec((kw, tf), lambda j, i: (0, nf + j)),
                  pl.BlockSpec((1, tf), lambda j, i: (0, j)),
                  pl.BlockSpec((1, tf), lambda j, i: (0, nf + j))],
        out_specs=pl.BlockSpec((tm, tf), lambda j, i: (i, j)),
        out_shape=jax.ShapeDtypeStruct((t, f), BF16),
        scratch_shapes=[pltpu.VMEM((tm + 8, tf), F32),
                        pltpu.VMEM((tm + 8, tf), F32),
                        pltpu.VMEM((d, tf), BF16),
                        pltpu.VMEM((d, tf), BF16)],
        compiler_params=_params(2, 56),
        name="ffn_up",
    )(h2, w_up, w_up, conv_w, conv_w, cb, cb)


def _ffn_down_kernel(a_ref, w_hbm, x1_ref, mod_ref, g_ref, o_ref, w_ref, sem, *, sub, nchunk):
    i = pl.program_id(0)
    kc = w_ref.shape[0] // nchunk

    def chunk_copy(c):
        rows = pl.ds(c * kc, kc)
        return pltpu.make_async_copy(w_hbm.at[rows, :], w_ref.at[rows, :], sem.at[c])

    def finish(rows, y):
        x2 = x1_ref[rows, :] + mod_ref[0, 5:6, :] * y
        o_ref[rows, :] = _rms(x2) * g_ref[...]

    @pl.when(i == 0)
    def _():
        for c in range(nchunk):
            chunk_copy(c).start()
        sub0 = sub // 2
        for s in range(a_ref.shape[0] // sub0):
            rows = pl.ds(s * sub0, sub0)
            y = None
            for c in range(nchunk):
                if s == 0:
                    chunk_copy(c).wait()
                part = jnp.dot(a_ref[rows, c * kc:(c + 1) * kc], w_ref[c * kc:(c + 1) * kc, :],
                               preferred_element_type=F32)
                y = part if y is None else y + part
            finish(rows, y)

    @pl.when(i > 0)
    def _():
        for s in range(a_ref.shape[0] // sub):
            rows = pl.ds(s * sub, sub)
            finish(rows, jnp.dot(a_ref[rows, :], w_ref[...], preferred_element_type=F32))


def _ffn_down(act, w_down, x1, mod, g, seq):
    t, f = act.shape
    d = w_down.shape[1]
    tm = FFN_DOWN_TM
    per_b = seq // tm
    return pl.pallas_call(
        functools.partial(_ffn_down_kernel, sub=FFN_DOWN_SUB, nchunk=FFN_DOWN_WCHUNKS),
        grid=(t // tm,),
        in_specs=[pl.BlockSpec((tm, f), lambda i: (i, 0)),
                  pl.BlockSpec(memory_space=pl.ANY),
                  pl.BlockSpec((tm, d), lambda i: (i, 0)),
                  pl.BlockSpec((1, 6, d), lambda i: (i // per_b, 0, 0)),
                  pl.BlockSpec((1, d), lambda i: (0, 0))],
        out_specs=pl.BlockSpec((tm, d), lambda i: (i, 0)),
        out_shape=jax.ShapeDtypeStruct((t, d), F32),
        scratch_shapes=[pltpu.VMEM((f, d), BF16),
                        pltpu.SemaphoreType.DMA((FFN_DOWN_WCHUNKS,))],
        compiler_params=_params(1, 58),
        name="ffn_down",
    )(act, w_down, x1, mod, g.reshape(1, d))


def _rot_cols(w):
    half = w.shape[-1] // 2
    return jnp.concatenate([-w[..., half:], w[..., :half]], axis=-1)


def _win_prep_kernel(w_ref, tail_ref, o_ref, *, n_main):
    j = pl.program_id(0)
    gates = 2 * MLSTM_HEADS
    half = MLA_ROPE // 2

    @pl.when(j < n_main)
    def _():
        o_ref[...] = w_ref[...].T.astype(BF16)

    @pl.when(j == n_main)
    def _():
        o_ref[...] = tail_ref[gates:gates + Q_LORA, :].T.astype(BF16)

    @pl.when(j == n_main + 1)
    def _():
        r0 = gates + Q_LORA
        kr0 = r0 + KV_LORA
        blk = o_ref.shape[1]
        used = KV_LORA + 2 * MLA_ROPE + gates
        o_ref[...] = jnp.concatenate(
            [tail_ref[r0:kr0, :],
             tail_ref[kr0:kr0 + MLA_ROPE, :],
             -tail_ref[kr0 + half:kr0 + MLA_ROPE, :],
             tail_ref[kr0:kr0 + half, :],
             tail_ref[0:gates, :],
             jnp.zeros((blk - used, o_ref.shape[0]), F32)], axis=0).T.astype(BF16)


def _prep_w_in(w_in):
    d = w_in.shape[0]
    wt = w_in.T
    n_head = 3 * MLSTM_W
    blk = Q_LORA
    n_main = n_head // blk
    return pl.pallas_call(
        functools.partial(_win_prep_kernel, n_main=n_main),
        grid=(IN_PAD // blk,),
        in_specs=[pl.BlockSpec((blk, d), lambda j: (jnp.minimum(j, n_main - 1), 0)),
                  pl.BlockSpec((IN_PAD - n_head, d), lambda j: (n_head // (IN_PAD - n_head), 0))],
        out_specs=pl.BlockSpec((d, blk), lambda j: (0, j)),
        out_shape=jax.ShapeDtypeStruct((d, IN_PAD), BF16),
        compiler_params=_params(1, 40),
        name="w_in_prep",
    )(wt, wt)


def kernel(x, c, positions, ada_w, ada_b, attn_norm_g, w_in, mlstm_conv_w, mlstm_conv_b, mlstm_wq, mlstm_wk, mlstm_igate_b, mlstm_fgate_b, mla_q_norm_g, mla_w_uq, mla_kv_norm_g, mla_w_ukv, mlstm_out_g, mla_out_g, w_out, ffn_norm_g, ffn_w_up, ffn_conv_w, ffn_conv_b, ffn_w_down, final_norm_g):
    bsz, seq, d = x.shape
    t = bsz * seq
    depth = ada_w.shape[0]
    assert depth == 1, "the final RMSNorm is fused into the single layer's down-projection"
    xr = x.reshape(t, d)
    pos = positions.reshape(t // QKV_TM, 1, QKV_TM)
    half = MLA_ROPE // 2
    freqs = (ROPE_THETA ** (-jnp.arange(half, dtype=F32) / half)).reshape(half, 1)

    for l in range(depth):
        mod = _modulation(c, ada_w[l], ada_b[l])

        proj, misc = _in_projection(xr, mod, attn_norm_g[l], _prep_w_in(w_in[l]), seq)

        gate_b = jnp.concatenate(
            [mlstm_igate_b[l], mlstm_fgate_b[l],
             jnp.zeros((LANES - 2 * MLSTM_HEADS,), F32)]).reshape(1, LANES)
        hm = _mlstm(proj, misc, mlstm_conv_w[l], mlstm_conv_b[l], mlstm_wq[l], mlstm_wk[l], gate_b,
                    mlstm_out_g[l], bsz, seq)

        wq = mla_w_uq[l].reshape(Q_LORA, MLA_HEADS, MLA_QK)
        wq_r = wq[..., MLA_NOPE:]
        wq_h = jnp.concatenate([wq[..., :MLA_NOPE], wq_r, _rot_cols(wq_r)], axis=-1)
        wq_h = wq_h.transpose(1, 0, 2).astype(BF16)
        qt, k, vt = _mla_qkv(proj, misc, pos, freqs, mla_q_norm_g[l], mla_kv_norm_g[l],
                             wq_h, mla_w_ukv[l], bsz, seq)
        ha, w_down_b, w_out_b = _mla_attention(qt, k, vt, mla_out_g[l], ffn_w_down[l], w_out[l])

        x1, h2 = _out_projection(hm, ha, w_out_b, xr, mod, ffn_norm_g[l], seq)

        act = _ffn_up(h2, ffn_w_up[l], ffn_conv_w[l], ffn_conv_b[l], seq)
        xr = _ffn_down(act, w_down_b, x1, mod, final_norm_g, seq)
    return xr.reshape(bsz, seq, d)
```
